```python
import jax, jax.numpy as jnp
from jax import lax
import numpy as np

D_MODEL = 1024
BATCH = 16
SEQ = 256
DEPTH = 2
DEC_BATCH = 2
DEC_SEQ = 1024
PAST_LEN = 256

GRID_W = 64
N_EVEN_LAYERS = (DEPTH + 1) // 2
N_ODD_LAYERS = DEPTH // 2
N_SUB = 3
FFN_RESIDUAL = 0.5
EPS = 1e-6
NEG_INF = -1e30

ATTN_HEADS = 8
ATTN_KV_HEADS = 2
ATTN_GROUP = ATTN_HEADS // ATTN_KV_HEADS
HEAD_DIM = 64
WINDOW = 128
ATTN_BLOCK = 128
ROPE_BASE = 10000.0

GLA_HEADS = 4
GLA_DK = 64
GLA_DV = 128
GLA_RANK = 16
GLA_TAU = 16.0
GLA_CHUNK = 32

A_Q = ATTN_HEADS * HEAD_DIM
A_KV = ATTN_KV_HEADS * HEAD_DIM
B_QK = GLA_HEADS * GLA_DK
B_V = GLA_HEADS * GLA_DV
EVEN_SPLITS = (A_Q, A_KV, A_KV, B_QK, B_QK, B_V, B_V, GLA_RANK, GLA_RANK)
EVEN_SPLIT_IDX = tuple(np.cumsum(EVEN_SPLITS)[:-1].tolist())
EVEN_IN = sum(EVEN_SPLITS)
EVEN_OUT = A_Q + B_V

CHUNK = 128
CMLP_WIDTH = D_MODEL
CMLP_GROUPS = 4

D_FF = 2816

kernel_name = 'hybrid_diffusion_prefix_step'


def rmsnorm(x, g):
    xf = x.astype(jnp.float32)
    y = xf * lax.rsqrt(jnp.mean(xf * xf, axis=-1, keepdims=True) + EPS)
    return (y * g.astype(jnp.float32)).astype(x.dtype)


def adaln(cond, w, b):
    return (jax.nn.silu(cond) @ w + b).reshape(-1, N_SUB, 3, D_MODEL)


def sub_in(x, mod, i, g_pre):
    return rmsnorm(x, g_pre) * (1 + mod[:, i, 1][:, None]) + mod[:, i, 0][:, None]


def sub_out(x, y, mod, i, g_post, coef):
    return x + coef * mod[:, i, 2][:, None] * rmsnorm(y, g_post)


def macaron_ffn(x, mod, i, g_pre, g_post, wg, wu, wd):
    h = sub_in(x, mod, i, g_pre)
    y = (jax.nn.silu(h @ wg) * (h @ wu)) @ wd
    return sub_out(x, y, mod, i, g_post, FFN_RESIDUAL)


def axial_rope(x):
    n = x.shape[1]
    rows = n // GRID_W
    row = jnp.repeat(jnp.arange(rows), GRID_W)
    col = jnp.tile(jnp.arange(GRID_W), rows)
    half = HEAD_DIM // 2
    nf = half // 2
    inv = ROPE_BASE ** (-jnp.arange(nf, dtype=jnp.float32) * 2.0 / half)
    ang = jnp.stack([row[:, None] * inv, col[:, None] * inv], axis=1)
    cos = jnp.cos(ang)[None, :, None].astype(x.dtype)
    sin = jnp.sin(ang)[None, :, None].astype(x.dtype)
    xr = x.reshape(x.shape[:-1] + (2, 2, nf))
    x1, x2 = xr[..., 0, :], xr[..., 1, :]
    out = jnp.stack([x1 * cos - x2 * sin, x2 * cos + x1 * sin], axis=-2)
    return out.reshape(x.shape)


def context_attention(q, k, v, sink):
    B, P = q.shape[:2]
    s = jnp.einsum('bpkgd,bskd->bkgps', q, k).astype(jnp.float32)
    sk = jnp.broadcast_to(sink.astype(jnp.float32)[None, :, :, None, None], s.shape[:-1] + (1,))
    p = jax.nn.softmax(jnp.concatenate([s, sk], axis=-1), axis=-1)[..., :-1].astype(v.dtype)
    o = jnp.einsum('bkgps,bskd->bpkgd', p, v)
    return o.reshape(B, P, A_Q)


def banded_attention(q, k, v, k_ctx, v_ctx, sink):
    B, L = q.shape[:2]
    nb = L // ATTN_BLOCK
    P = k_ctx.shape[1]
    qb = q.reshape(B, nb, ATTN_BLOCK, ATTN_KV_HEADS, ATTN_GROUP, HEAD_DIM)

    def windows(t):
        tp = jnp.pad(t, ((0, 0), (ATTN_BLOCK, ATTN_BLOCK), (0, 0), (0, 0)))
        tp = tp.reshape(B, nb + 2, ATTN_BLOCK, ATTN_KV_HEADS, HEAD_DIM)
        return jnp.concatenate([tp[:, :-2], tp[:, 1:-1], tp[:, 2:]], axis=2)

    kw, vw = windows(k), windows(v)
    qi = jnp.arange(nb)[:, None] * ATTN_BLOCK + jnp.arange(ATTN_BLOCK)[None, :]
    kj = (jnp.arange(nb)[:, None] - 1) * ATTN_BLOCK + jnp.arange(3 * ATTN_BLOCK)[None, :]
    mask = ((jnp.abs(qi[:, :, None] - kj[:, None, :]) <= WINDOW)
            & (kj[:, None, :] >= 0) & (kj[:, None, :] < L))
    s_loc = jnp.einsum('bnqkgd,bnskd->bkgnqs', qb, kw).astype(jnp.float32)
    s_loc = jnp.where(mask, s_loc, NEG_INF)
    s_ctx = jnp.einsum('bnqkgd,bpkd->bkgnqp', qb, k_ctx).astype(jnp.float32)
    sk = jnp.broadcast_to(sink.astype(jnp.float32)[None, :, :, None, None, None], s_loc.shape[:-1] + (1,))
    p = jax.nn.softmax(jnp.concatenate([s_loc, s_ctx, sk], axis=-1), axis=-1)
    n_loc = 3 * ATTN_BLOCK
    p_loc = p[..., :n_loc].astype(v.dtype)
    p_ctx = p[..., n_loc:n_loc + P].astype(v.dtype)
    o = (jnp.einsum('bkgnqs,bnskd->bnqkgd', p_loc, vw)
         + jnp.einsum('bkgnqp,bpkd->bnqkgd', p_ctx, v_ctx))
    return o.reshape(B, L, A_Q)


def gla_chunked(q, k, v, logd, s0):
    B, L, H, K = q.shape
    V = v.shape[-1]
    C = GLA_CHUNK
    n = L // C
    f32 = jnp.float32
    qc = q.astype(f32).reshape(B, n, C, H, K)
    kc = k.astype(f32).reshape(B, n, C, H, K)
    vc = v.astype(f32).reshape(B, n, C, H, V)
    b = jnp.cumsum(logd.astype(f32).reshape(B, n, C, H, K), axis=2)
    b_last = b[:, :, -1]
    causal = jnp.tril(jnp.ones((C, C), dtype=bool))
    diff = b[:, :, :, None] - b[:, :, None, :]
    dec = jnp.exp(jnp.where(causal[:, :, None, None], diff, -jnp.inf))
    a = jnp.einsum('bnihk,bnjhk,bnijhk->bnhij', qc, kc, dec)
    o_intra = jnp.einsum('bnhij,bnjhv->bnihv', a, vc)
    u = jnp.einsum('bnjhk,bnjhv->bnhkv', kc * jnp.exp(b_last[:, :, None] - b), vc)

    def step(s, xs):
        d, uu = xs
        return d[..., None] * s + uu, s

    s_fin, s_prev = lax.scan(step, s0.astype(f32),
                             (jnp.moveaxis(jnp.exp(b_last), 1, 0), jnp.moveaxis(u, 1, 0)))
    s_prev = jnp.moveaxis(s_prev, 0, 1)
    o_inter = jnp.einsum('bnihk,bnhkv->bnihv', qc * jnp.exp(b), s_prev)
    o = (o_intra + o_inter).reshape(B, L, H, V).astype(v.dtype)
    return o, s_fin


def bi_gla(q, k, v, ld_f, ld_b, s_f0, s_b0):
    o_f, s_f = gla_chunked(q, k, v, ld_f, s_f0)
    o_b, s_b = gla_chunked(jnp.flip(q, 1), jnp.flip(k, 1), jnp.flip(v, 1), jnp.flip(ld_b, 1), s_b0)
    return o_f + jnp.flip(o_b, 1), s_f, s_b


def even_project(h, w_in, wa_f, ba_f, wa_b, ba_b):
    B, L, _ = h.shape
    qa, ka, va, qb, kb, vb, gb, lrf, lrb = jnp.split(h @ w_in, EVEN_SPLIT_IDX, axis=-1)
    qa = qa.reshape(B, L, ATTN_HEADS, HEAD_DIM)
    ka = ka.reshape(B, L, ATTN_KV_HEADS, HEAD_DIM)
    va = va.reshape(B, L, ATTN_KV_HEADS, HEAD_DIM)
    qb = qb.reshape(B, L, GLA_HEADS, GLA_DK) * (GLA_DK ** -0.5)
    kb = kb.reshape(B, L, GLA_HEADS, GLA_DK)
    vb = vb.reshape(B, L, GLA_HEADS, GLA_DV)
    ld_f = jax.nn.log_sigmoid((lrf @ wa_f + ba_f).astype(jnp.float32)).reshape(B, L, GLA_HEADS, GLA_DK) / GLA_TAU
    ld_b = jax.nn.log_sigmoid((lrb @ wa_b + ba_b).astype(jnp.float32)).reshape(B, L, GLA_HEADS, GLA_DK) / GLA_TAU
    return qa, ka, va, qb, kb, vb, gb, ld_f, ld_b


def even_output(attn_o, gla_o, gb, gla_g, w_out):
    B, L = attn_o.shape[:2]
    g_o = rmsnorm(gla_o, gla_g.reshape(GLA_HEADS, GLA_DV)).reshape(B, L, B_V) * jax.nn.silu(gb)
    return jnp.concatenate([attn_o, g_o], axis=-1) @ w_out


def even_mixer_context(h, w_in, w_out, sink, wa_f, ba_f, wa_b, ba_b, gla_g):
    B, P, _ = h.shape
    qa, ka, va, qb, kb, vb, gb, ld_f, ld_b = even_project(h, w_in, wa_f, ba_f, wa_b, ba_b)
    qa = (qa * (HEAD_DIM ** -0.5)).reshape(B, P, ATTN_KV_HEADS, ATTN_GROUP, HEAD_DIM)
    attn = context_attention(qa, ka, va, sink.reshape(ATTN_KV_HEADS, ATTN_GROUP))
    zero = jnp.zeros((B, GLA_HEADS, GLA_DK, GLA_DV), jnp.float32)
    g_o, s_f, s_b = bi_gla(qb, kb, vb, ld_f, ld_b, zero, zero)
    return even_output(attn, g_o, gb, gla_g, w_out), ka, va, s_f, s_b


def even_mixer_latent(h, k_ctx, v_ctx, s_f0, s_b0, w_in, w_out, sink, wa_f, ba_f, wa_b, ba_b, gla_g):
    B, L, _ = h.shape
    qa, ka, va, qb, kb, vb, gb, ld_f, ld_b = even_project(h, w_in, wa_f, ba_f, wa_b, ba_b)
    qa = (axial_rope(qa) * (HEAD_DIM ** -0.5)).reshape(B, L, ATTN_KV_HEADS, ATTN_GROUP, HEAD_DIM)
    ka = axial_rope(ka)
    attn = banded_attention(qa, ka, va, k_ctx, v_ctx, sink.reshape(ATTN_KV_HEADS, ATTN_GROUP))
    g_o, _, _ = bi_gla(qb, kb, vb, ld_f, ld_b, s_f0, s_b0)
    return even_output(attn, g_o, gb, gla_g, w_out)


def chunk_mlp(h, w_in, v_gain, v_bias, w_s, b_s, w_out):
    B, L, _ = h.shape
    n = L // CHUNK
    z = jax.nn.gelu(h @ w_in, approximate=False)
    u, v = jnp.split(z, 2, axis=-1)
    vf = v.astype(jnp.float32)
    mu = jnp.mean(vf, axis=-1, keepdims=True)
    var = jnp.mean(jnp.square(vf - mu), axis=-1, keepdims=True)
    v = ((vf - mu) * lax.rsqrt(var + EPS) * v_gain.astype(jnp.float32) + v_bias.astype(jnp.float32)).astype(h.dtype)
    v = v.reshape(B, n, CHUNK, CMLP_GROUPS, CMLP_WIDTH // CMLP_GROUPS)
    mixed = jnp.einsum('gts,bnsgc->bntgc', w_s, v) + b_s.T[:, :, None]
    return (u * mixed.reshape(B, L, CMLP_WIDTH)) @ w_out


def setup_inputs(seed: int = 0) -> dict:
    key = jax.random.key(seed)
    ks = iter(jax.random.split(key, 32))

    def nrm(shape, scale=1.0):
        return jax.random.normal(next(ks), shape, jnp.float32) * scale

    return {
        'x_prompt': nrm((BATCH, SEQ, D_MODEL)),
        'x_sample': nrm((DEC_BATCH, DEC_SEQ, D_MODEL)),
        'cache_k': nrm((DEC_BATCH, N_EVEN_LAYERS, PAST_LEN, ATTN_KV_HEADS, HEAD_DIM)),
        'cache_v': nrm((DEC_BATCH, N_EVEN_LAYERS, PAST_LEN, ATTN_KV_HEADS, HEAD_DIM)),
        'state_gla_fwd': nrm((DEC_BATCH, N_EVEN_LAYERS, GLA_HEADS, GLA_DK, GLA_DV), 2.0),
        'state_gla_bwd': nrm((DEC_BATCH, N_EVEN_LAYERS, GLA_HEADS, GLA_DK, GLA_DV), 2.0),
        'c': nrm((DEC_BATCH, D_MODEL)),
        'c_ctx': nrm((D_MODEL,)),
        'w_mod': nrm((DEPTH, D_MODEL, N_SUB * 3 * D_MODEL), 0.5 * D_MODEL ** -0.5),
        'b_mod': nrm((DEPTH, N_SUB * 3 * D_MODEL), 0.01),
        'norm_pre': 1.0 + nrm((DEPTH, N_SUB, D_MODEL), 0.05),
        'norm_post': 1.0 + nrm((DEPTH, N_SUB, D_MODEL), 0.05),
        'ffn_w_gate': nrm((DEPTH, 2, D_MODEL, D_FF), D_MODEL ** -0.5),
        'ffn_w_up': nrm((DEPTH, 2, D_MODEL, D_FF), D_MODEL ** -0.5),
        'ffn_w_down': nrm((DEPTH, 2, D_FF, D_MODEL), D_FF ** -0.5),
        'ev_w_in': nrm((N_EVEN_LAYERS, D_MODEL, EVEN_IN), D_MODEL ** -0.5),
        'ev_w_out': nrm((N_EVEN_LAYERS, EVEN_OUT, D_MODEL), EVEN_OUT ** -0.5),
        'ev_sink': nrm((N_EVEN_LAYERS, ATTN_HEADS)),
        'gla_wa_f': nrm((N_EVEN_LAYERS, GLA_RANK, B_QK), GLA_RANK ** -0.5),
        'gla_ba_f': nrm((N_EVEN_LAYERS, B_QK), 0.01),
        'gla_wa_b': nrm((N_EVEN_LAYERS, GLA_RANK, B_QK), GLA_RANK ** -0.5),
        'gla_ba_b': nrm((N_EVEN_LAYERS, B_QK), 0.01),
        'gla_norm': 1.0 + nrm((N_EVEN_LAYERS, B_V), 0.05),
        'cm_w_in': nrm((N_ODD_LAYERS, D_MODEL, 2 * CMLP_WIDTH), D_MODEL ** -0.5),
        'cm_v_gain': 1.0 + nrm((N_ODD_LAYERS, CMLP_WIDTH), 0.05),
        'cm_v_bias': nrm((N_ODD_LAYERS, CMLP_WIDTH), 0.01),
        'cm_w_s': nrm((N_ODD_LAYERS, CMLP_GROUPS, CHUNK, CHUNK), CHUNK ** -0.5),
        'cm_b_s': 1.0 + nrm((N_ODD_LAYERS, CMLP_GROUPS, CHUNK), 0.05),
        'cm_w_out': nrm((N_ODD_LAYERS, CMLP_WIDTH, D_MODEL), CMLP_WIDTH ** -0.5),
    }


def reference(x_prompt, x_sample, cache_k, cache_v, state_gla_fwd, state_gla_bwd, c, c_ctx,
              w_mod, b_mod, norm_pre, norm_post, ffn_w_gate, ffn_w_up, ffn_w_down,
              ev_w_in, ev_w_out, ev_sink, gla_wa_f, gla_ba_f, gla_wa_b, gla_ba_b, gla_norm,
              cm_w_in, cm_v_gain, cm_v_bias, cm_w_s, cm_b_s, cm_w_out):
    xp, xs = x_prompt, x_sample
    new_k, new_v, new_sf, new_sb = [], [], [], []
    for layer in range(DEPTH):
        mp = adaln(c_ctx[None, :], w_mod[layer], b_mod[layer])
        ms = adaln(c, w_mod[layer], b_mod[layer])
        g_pre, g_post = norm_pre[layer], norm_post[layer]
        xp = macaron_ffn(xp, mp, 0, g_pre[0], g_post[0], ffn_w_gate[layer, 0], ffn_w_up[layer, 0], ffn_w_down[layer, 0])
        xs = macaron_ffn(xs, ms, 0, g_pre[0], g_post[0], ffn_w_gate[layer, 0], ffn_w_up[layer, 0], ffn_w_down[layer, 0])
        hp = sub_in(xp, mp, 1, g_pre[1])
        hs = sub_in(xs, ms, 1, g_pre[1])
        if layer % 2 == 0:
            e = layer // 2
            ev = (ev_w_in[e], ev_w_out[e], ev_sink[e], gla_wa_f[e], gla_ba_f[e], gla_wa_b[e], gla_ba_b[e], gla_norm[e])
            yp, k_c, v_c, s_f, s_b = even_mixer_context(hp, *ev)
            ys = even_mixer_latent(hs, cache_k[:, e], cache_v[:, e], state_gla_fwd[:, e], state_gla_bwd[:, e], *ev)
            new_k.append(k_c)
            new_v.append(v_c)
            new_sf.append(s_f.astype(x_prompt.dtype))
            new_sb.append(s_b.astype(x_prompt.dtype))
        else:
            o = layer // 2
            cm = (cm_w_in[o], cm_v_gain[o], cm_v_bias[o], cm_w_s[o], cm_b_s[o], cm_w_out[o])
            yp = chunk_mlp(hp, *cm)
            ys = chunk_mlp(hs, *cm)
        xp = sub_out(xp, yp, mp, 1, g_post[1], 1.0)
        xs = sub_out(xs, ys, ms, 1, g_post[1], 1.0)
        xp = macaron_ffn(xp, mp, 2, g_pre[2], g_post[2], ffn_w_gate[layer, 1], ffn_w_up[layer, 1], ffn_w_down[layer, 1])
        xs = macaron_ffn(xs, ms, 2, g_pre[2], g_post[2], ffn_w_gate[layer, 1], ffn_w_up[layer, 1], ffn_w_down[layer, 1])
    new_cache_k = jnp.stack(new_k, axis=1)
    new_cache_v = jnp.stack(new_v, axis=1)
    new_state_gla_fwd = jnp.stack(new_sf, axis=1)
    new_state_gla_bwd = jnp.stack(new_sb, axis=1)
    return (xp, xs, new_cache_k, new_cache_v, new_state_gla_fwd, new_state_gla_bwd)
```

```python
import functools
import math

import jax
import jax.numpy as jnp
from jax import lax
from jax.experimental import pallas as pl
from jax.experimental.pallas import tpu as pltpu

F32 = jnp.float32
BF16 = jnp.bfloat16

EPS = 1e-6
NEG_INF = -1e30
FFN_RESIDUAL = 0.5
N_SUB = 3

ATTN_HEADS = 8
ATTN_KV_HEADS = 2
HEAD_DIM = 64
ATTN_BLOCK = 128
WINDOW = 128
GRID_W = 64
GRID_W_LOG2 = 6
ROPE_BASE = 10000.0
GLA_HEADS = 4
GLA_DK = 64
GLA_DV = 128
GLA_RANK = 16
GLA_TAU = 16.0
CHUNK = 128
CMLP_GROUPS = 4

A_Q = ATTN_HEADS * HEAD_DIM
A_KV = ATTN_KV_HEADS * HEAD_DIM
B_QK = GLA_HEADS * GLA_DK
B_V = GLA_HEADS * GLA_DV
EV_MAIN = A_Q + 2 * A_KV + 2 * B_QK + 2 * B_V
EV_OFFS = (0, A_Q, A_Q + A_KV, A_Q + 2 * A_KV, A_Q + 2 * A_KV + B_QK,
           A_Q + 2 * A_KV + 2 * B_QK, A_Q + 2 * A_KV + 2 * B_QK + B_V, EV_MAIN)

LANES = 128
LANES_LOG2 = 7
SUBLANES = 8
VMEM_LIMIT_BYTES = 56 * 1024 * 1024

TOKEN_TILE = 512
FFN_CHUNK = 256
GLA_TILE = 128
GLA_SUB = 8


def _group_index_map(tile_rows, n_prompt_tok, seq_tok):
    def group(i):
        return jnp.maximum(i * tile_rows - (n_prompt_tok - seq_tok), 0) // seq_tok
    return group


def _rms(x, g):
    ms = jnp.mean(x * x, axis=-1, keepdims=True)
    return x * lax.rsqrt(ms + EPS) * g


def _sub_in(x, mod_ref, gpre_ref, sub):
    shift = mod_ref[3 * sub + 0:3 * sub + 1, :]
    scale = mod_ref[3 * sub + 1:3 * sub + 2, :]
    return _rms(x, gpre_ref[sub:sub + 1, :]) * (1.0 + scale) + shift


def _sub_out(x, y, mod_ref, gpost_ref, sub, coef):
    gate = mod_ref[3 * sub + 2:3 * sub + 3, :]
    return x + (coef * gate) * _rms(y, gpost_ref[sub:sub + 1, :])


def _silu(x):
    return x * jax.nn.sigmoid(x)


def _mod_kernel(cond_ref, w_ref, b_ref, o_ref):
    s = _silu(cond_ref[...]).astype(BF16)
    o_ref[0] = jnp.dot(s, w_ref[0].astype(BF16), preferred_element_type=F32) + b_ref[0]


def _adaln_mods(cond, w_mod, b_mod):
    depth, d, n = w_mod.shape
    tn = 1024
    rows = cond.shape[0]
    return pl.pallas_call(
        _mod_kernel,
        grid=(depth, n // tn),
        in_specs=[
            pl.BlockSpec((rows, d), lambda l, j: (0, 0)),
            pl.BlockSpec((1, d, tn), lambda l, j: (l, 0, j)),
            pl.BlockSpec((1, 1, tn), lambda l, j: (l, 0, j)),
        ],
        out_specs=pl.BlockSpec((1, rows, tn), lambda l, j: (l, 0, j)),
        out_shape=jax.ShapeDtypeStruct((depth, rows, n), F32),
        compiler_params=pltpu.CompilerParams(dimension_semantics=("parallel", "parallel")),
        name="adaln_mods",
    )(cond, w_mod, b_mod.reshape(depth, 1, n))


def _ffn_kernel(x_ref, mod_ref, gpre_ref, gpost_ref, wg_hbm, wu_hbm, wd_hbm, o_ref,
                wg_bf, wu_bf, wd_bf, st_g, st_u, st_d, sem, *, sub, layer, half, n_chunks):
    fc = FFN_CHUNK

    @pl.when(pl.program_id(0) == 0)
    def _stage_weights():
        def chunk_copies(c, slot):
            cols = pl.ds(c * fc, fc)
            return (
                pltpu.make_async_copy(wg_hbm.at[layer, half, :, cols], st_g.at[slot], sem.at[0, slot]),
                pltpu.make_async_copy(wu_hbm.at[layer, half, :, cols], st_u.at[slot], sem.at[1, slot]),
                pltpu.make_async_copy(wd_hbm.at[layer, half, cols, :], st_d.at[slot], sem.at[2, slot]),
            )

        for cp in chunk_copies(0, 0):
            cp.start()
        for c in range(n_chunks):
            slot = c % 2
            if c + 1 < n_chunks:
                for cp in chunk_copies(c + 1, 1 - slot):
                    cp.start()
            for cp in chunk_copies(c, slot):
                cp.wait()
            wg_bf[:, c * fc:(c + 1) * fc] = st_g[slot].astype(BF16)
            wu_bf[:, c * fc:(c + 1) * fc] = st_u[slot].astype(BF16)
            wd_bf[c * fc:(c + 1) * fc, :] = st_d[slot].astype(BF16)

    x = x_ref[...]
    h = _sub_in(x, mod_ref, gpre_ref, sub).astype(BF16)
    y = jnp.zeros(x.shape, F32)
    for c in range(n_chunks):
        g = jnp.dot(h, wg_bf[:, c * fc:(c + 1) * fc], preferred_element_type=F32)
        u = jnp.dot(h, wu_bf[:, c * fc:(c + 1) * fc], preferred_element_type=F32)
        a = (_silu(g) * u).astype(BF16)
        y = y + jnp.dot(a, wd_bf[c * fc:(c + 1) * fc, :], preferred_element_type=F32)
    o_ref[...] = _sub_out(x, y, mod_ref, gpost_ref, sub, FFN_RESIDUAL)


def _ffn(x, mods_l, gpre_l, gpost_l, wg, wu, wd, *, sub, layer, half, group):
    n_tok, d = x.shape
    d_ff = wg.shape[-1]
    tm = TOKEN_TILE
    fc = FFN_CHUNK
    n_chunks = d_ff // fc
    assert n_chunks * fc == d_ff and n_tok % tm == 0
    kern = functools.partial(_ffn_kernel, sub=sub, layer=layer, half=half, n_chunks=n_chunks)
    return pl.pallas_call(
        kern,
        grid=(n_tok // tm,),
        in_specs=[
            pl.BlockSpec((tm, d), lambda i: (i, 0)),
            pl.BlockSpec((None, 3 * N_SUB, d), lambda i: (group(i), 0, 0)),
            pl.BlockSpec((N_SUB, d), lambda i: (0, 0)),
            pl.BlockSpec((N_SUB, d), lambda i: (0, 0)),
            pl.BlockSpec(memory_space=pl.ANY),
            pl.BlockSpec(memory_space=pl.ANY),
            pl.BlockSpec(memory_space=pl.ANY),
        ],
        out_specs=pl.BlockSpec((tm, d), lambda i: (i, 0)),
        out_shape=jax.ShapeDtypeStruct((n_tok, d), F32),
        scratch_shapes=[
            pltpu.VMEM((d, d_ff), BF16),
            pltpu.VMEM((d, d_ff), BF16),
            pltpu.VMEM((d_ff, d), BF16),
            pltpu.VMEM((2, d, fc), F32),
            pltpu.VMEM((2, d, fc), F32),
            pltpu.VMEM((2, fc, d), F32),
            pltpu.SemaphoreType.DMA((3, 2)),
        ],
        compiler_params=pltpu.CompilerParams(
            dimension_semantics=("arbitrary",), vmem_limit_bytes=VMEM_LIMIT_BYTES),
        name=f"ffn_l{layer}h{half}",
    )(x, mods_l, gpre_l, gpost_l, wg, wu, wd)


def _log_sigmoid(x):
    return jnp.minimum(x, 0.0) - jnp.log1p(jnp.exp(-jnp.abs(x)))


def _evin_kernel(x_ref, mod_ref, gpre_ref, w_ref, wa_ref, ba_ref,
                 qa_ref, ka_ref, va_ref, qb_ref, kb_ref, vb_ref, gb_ref, ldf_ref, ldb_ref,
                 w_bf, wlr_bf):
    @pl.when(pl.program_id(0) == 0)
    def _cast_weights():
        w_bf[...] = w_ref[0, :, 0:EV_MAIN].astype(BF16)
        wlr_bf[...] = jnp.zeros(wlr_bf.shape, BF16)
        wlr_bf[:, 0:2 * GLA_RANK] = w_ref[0, :, EV_MAIN:EV_MAIN + 2 * GLA_RANK].astype(BF16)

    h = _sub_in(x_ref[...], mod_ref, gpre_ref, 1).astype(BF16)

    def proj(lo, hi):
        return jnp.dot(h, w_bf[:, lo:hi], preferred_element_type=F32)

    o = EV_OFFS
    qa_ref[...] = proj(o[0], o[1]) * (HEAD_DIM ** -0.5)
    ka_ref[...] = proj(o[1], o[2])
    va_ref[...] = proj(o[2], o[3])
    qb_ref[...] = proj(o[3], o[4]) * (GLA_DK ** -0.5)
    kb_ref[...] = proj(o[4], o[5])
    vb_ref[...] = proj(o[5], o[6])
    gb_ref[...] = proj(o[6], o[7])
    lr = jnp.dot(h, wlr_bf[...], preferred_element_type=F32).astype(BF16)
    logits = jnp.dot(lr, wa_ref[...].astype(BF16), preferred_element_type=F32) + ba_ref[...]
    ld = _log_sigmoid(logits) * (1.0 / GLA_TAU)
    ldf_ref[...] = ld[:, 0:B_QK]
    ldb_ref[...] = ld[:, B_QK:2 * B_QK]


def _even_in(x, mods_l, gpre_l, w_in, wa_bd, ba_cat, *, group):
    n_tok, d = x.shape
    tm = TOKEN_TILE
    ev_in = w_in.shape[-1]
    widths = (A_Q, A_KV, A_KV, B_QK, B_QK, B_V, B_V, B_QK, B_QK)
    return pl.pallas_call(
        _evin_kernel,
        grid=(n_tok // tm,),
        in_specs=[
            pl.BlockSpec((tm, d), lambda i: (i, 0)),
            pl.BlockSpec((None, 3 * N_SUB, d), lambda i: (group(i), 0, 0)),
            pl.BlockSpec((N_SUB, d), lambda i: (0, 0)),
            pl.BlockSpec((1, d, ev_in), lambda i: (0, 0, 0), pipeline_mode=pl.Buffered(1)),
            pl.BlockSpec(wa_bd.shape, lambda i: (0, 0)),
            pl.BlockSpec(ba_cat.shape, lambda i: (0, 0)),
        ],
        out_specs=[pl.BlockSpec((tm, w), lambda i: (i, 0)) for w in widths],
        out_shape=[jax.ShapeDtypeStruct((n_tok, w), F32) for w in widths],
        scratch_shapes=[pltpu.VMEM((d, EV_MAIN), BF16), pltpu.VMEM((d, LANES), BF16)],
        compiler_params=pltpu.CompilerParams(
            dimension_semantics=("arbitrary",), vmem_limit_bytes=VMEM_LIMIT_BYTES),
        name="even_in",
    )(x, mods_l, gpre_l, w_in, wa_bd, ba_cat)


def _lane_lt(shape, n):
    return lax.broadcasted_iota(jnp.int32, shape, len(shape) - 1) < n


def _dup_kv_head(x, g):
    sw = pltpu.roll(x, HEAD_DIM, 1)
    lo = _lane_lt(x.shape, HEAD_DIM)
    return jnp.where(lo, x, sw) if g == 0 else jnp.where(lo, sw, x)


def _dot_nt(a, b):
    return lax.dot_general(a, b, (((1,), (1,)), ((), ())), preferred_element_type=F32)


def _attn_ctx_kernel(sink_ref, q_ref, k_ref, v_ref, o_ref):
    k = k_ref[...]
    v = v_ref[...]
    k2 = [_dup_kv_head(k, g).astype(BF16) for g in range(ATTN_KV_HEADS)]
    v2 = [_dup_kv_head(v, g).astype(BF16) for g in range(ATTN_KV_HEADS)]
    group = ATTN_HEADS // ATTN_KV_HEADS
    for m in range(ATTN_HEADS // 2):
        qm = q_ref[:, m * LANES:(m + 1) * LANES]
        lo = _lane_lt(qm.shape, HEAD_DIM)
        outs = []
        for par in range(2):
            hd = 2 * m + par
            g = hd // group
            qh = jnp.where(lo if par == 0 else ~lo, qm, 0.0).astype(BF16)
            s = _dot_nt(qh, k2[g])
            sink = sink_ref[hd]
            mx = jnp.maximum(jnp.max(s, axis=-1, keepdims=True), sink)
            p = jnp.exp(s - mx)
            den = jnp.sum(p, axis=-1, keepdims=True) + jnp.exp(sink - mx)
            outs.append(jnp.dot(p.astype(BF16), v2[g], preferred_element_type=F32) / den)
        o_ref[:, m * LANES:(m + 1) * LANES] = jnp.where(lo, outs[0], outs[1])


def _attn_context(sink, qa, ka, va, *, n_seq, seq):
    return pl.pallas_call(
        _attn_ctx_kernel,
        grid=(n_seq,),
        in_specs=[
            pl.BlockSpec(memory_space=pltpu.SMEM),
            pl.BlockSpec((seq, A_Q), lambda b: (b, 0)),
            pl.BlockSpec((seq, A_KV), lambda b: (b, 0)),
            pl.BlockSpec((seq, A_KV), lambda b: (b, 0)),
        ],
        out_specs=pl.BlockSpec((seq, A_Q), lambda b: (b, 0)),
        out_shape=jax.ShapeDtypeStruct((n_seq * seq, A_Q), F32),
        compiler_params=pltpu.CompilerParams(dimension_semantics=("parallel",)),
        name="attn_context",
    )(sink, qa, ka, va)


def _rope_tables(n_tok):
    shape = (n_tok, LANES)
    t = lax.broadcasted_iota(jnp.int32, shape, 0)
    lane = lax.broadcasted_iota(jnp.int32, shape, 1)
    half = HEAD_DIM // 2
    nf = half // 2
    within = lane & (HEAD_DIM - 1)
    is_col = within >= half
    second = (within & (half - 1)) >= nf
    f = (within & (nf - 1)).astype(F32)
    inv = jnp.exp(f * (-2.0 / half * math.log(ROPE_BASE)))
    pos = jnp.where(is_col, t & (GRID_W - 1), t >> GRID_W_LOG2).astype(F32)
    ang = pos * inv
    cos = jnp.cos(ang)
    sin = jnp.sin(ang)
    return cos, jnp.where(second, sin, -sin), second


def _rope(x, cos, sin_signed, second):
    nf = HEAD_DIM // 4
    up = pltpu.roll(x, nf, 1)
    dn = pltpu.roll(x, LANES - nf, 1)
    return x * cos + jnp.where(second, up, dn) * sin_signed


def _attn_lat_kernel(sink_ref, q_ref, k_ref, v_ref, kc_ref, vc_ref, o_ref, qr_ref, kp_ref, vp_ref,
                     *, seq):
    nb = seq // ATTN_BLOCK
    blk = ATTN_BLOCK
    cos, sin_signed, second = _rope_tables(seq)
    for m in range(A_Q // LANES):
        qr_ref[:, m * LANES:(m + 1) * LANES] = _rope(
            q_ref[:, m * LANES:(m + 1) * LANES], cos, sin_signed, second).astype(BF16)
    kr = _rope(k_ref[...], cos, sin_signed, second)
    v = v_ref[...]
    zpad = jnp.zeros((blk, LANES), BF16)
    kc2, vc2 = [], []
    for g in range(ATTN_KV_HEADS):
        kp_ref[g, 0:blk, :] = zpad
        kp_ref[g, blk + seq:2 * blk + seq, :] = zpad
        vp_ref[g, 0:blk, :] = zpad
        vp_ref[g, blk + seq:2 * blk + seq, :] = zpad
        kp_ref[g, blk:blk + seq, :] = _dup_kv_head(kr, g).astype(BF16)
        vp_ref[g, blk:blk + seq, :] = _dup_kv_head(v, g).astype(BF16)
        kc2.append(_dup_kv_head(kc_ref[0], g).astype(BF16))
        vc2.append(_dup_kv_head(vc_ref[0], g).astype(BF16))
    group = ATTN_HEADS // ATTN_KV_HEADS
    r = lax.broadcasted_iota(jnp.int32, (blk, 3 * blk), 0)
    c = lax.broadcasted_iota(jnp.int32, (blk, 3 * blk), 1)
    band = (c >= r) & (c <= r + 2 * WINDOW)

    def block_body(i, carry):
        row0 = pl.multiple_of(i * blk, blk)
        kj = (i - 1) * blk + c
        mask = band & (kj >= 0) & (kj < seq)
        for m in range(ATTN_HEADS // 2):
            qm = qr_ref[pl.ds(row0, blk), m * LANES:(m + 1) * LANES]
            lo = _lane_lt(qm.shape, HEAD_DIM)
            outs = []
            for par in range(2):
                hd = 2 * m + par
                g = hd // group
                qh = jnp.where(lo if par == 0 else ~lo, qm, jnp.zeros_like(qm))
                kw = kp_ref[g, pl.ds(row0, 3 * blk), :]
                vw = vp_ref[g, pl.ds(row0, 3 * blk), :]
                s_loc = jnp.where(mask, _dot_nt(qh, kw), NEG_INF)
                s_ctx = _dot_nt(qh, kc2[g])
                sink = sink_ref[hd]
                mx = jnp.maximum(jnp.maximum(jnp.max(s_loc, axis=-1, keepdims=True),
                                             jnp.max(s_ctx, axis=-1, keepdims=True)), sink)
                p_loc = jnp.exp(s_loc - mx)
                p_ctx = jnp.exp(s_ctx - mx)
                den = (jnp.sum(p_loc, axis=-1, keepdims=True) + jnp.sum(p_ctx, axis=-1, keepdims=True)
                       + jnp.exp(sink - mx))
                o = (jnp.dot(p_loc.astype(BF16), vw, preferred_element_type=F32)
                     + jnp.dot(p_ctx.astype(BF16), vc2[g], preferred_element_type=F32))
                outs.append(o / den)
            o_ref[pl.ds(row0, blk), m * LANES:(m + 1) * LANES] = jnp.where(lo, outs[0], outs[1])
        return carry

    lax.fori_loop(0, nb, block_body, 0)


def _attn_latent(sink, qa, ka, va, k_ctx, v_ctx, *, n_seq, seq, row_block0):
    past = k_ctx.shape[1]
    kern = functools.partial(_attn_lat_kernel, seq=seq)
    return pl.pallas_call(
        kern,
        grid=(n_seq,),
        in_specs=[
            pl.BlockSpec(memory_space=pltpu.SMEM),
            pl.BlockSpec((seq, A_Q), lambda b: (row_block0 + b, 0)),
            pl.BlockSpec((seq, A_KV), lambda b: (row_block0 + b, 0)),
            pl.BlockSpec((seq, A_KV), lambda b: (row_block0 + b, 0)),
            pl.BlockSpec((1, past, A_KV), lambda b: (b, 0, 0)),
            pl.BlockSpec((1, past, A_KV), lambda b: (b, 0, 0)),
        ],
        out_specs=pl.BlockSpec((seq, A_Q), lambda b: (b, 0)),
        out_shape=jax.ShapeDtypeStruct((n_seq * seq, A_Q), F32),
        scratch_shapes=[
            pltpu.VMEM((seq, A_Q), BF16),
            pltpu.VMEM((ATTN_KV_HEADS, seq + 2 * ATTN_BLOCK, LANES), BF16),
            pltpu.VMEM((ATTN_KV_HEADS, seq + 2 * ATTN_BLOCK, LANES), BF16),
        ],
        compiler_params=pltpu.CompilerParams(dimension_semantics=("parallel",)),
        name="attn_latent",
    )(sink, qa, ka, va, k_ctx, v_ctx)


def _same_block(a, b, size):
    return (a & -size) == (b & -size)


def _split3(x):
    hi = x.astype(BF16)
    r1 = x - hi.astype(F32)
    mid = r1.astype(BF16)
    lo = (r1 - mid.astype(F32)).astype(BF16)
    return hi, mid, lo


def _gla_tile(q, k, v, ld, st_ref, sel_ref, *, rev):
    t = GLA_TILE
    n_pair = GLA_HEADS // 2
    row = lax.broadcasted_iota(jnp.int32, (t, t), 0)
    col = lax.broadcasted_iota(jnp.int32, (t, t), 1)
    tri = (row <= col) if rev else (row >= col)
    tri = jnp.where(tri, 1.0, 0.0).astype(BF16)
    hi, mid, lo = _split3(ld)
    b = (jnp.dot(tri, hi, preferred_element_type=F32) + jnp.dot(tri, mid, preferred_element_type=F32)
         + jnp.dot(tri, lo, preferred_element_type=F32))
    b_tot = b[0:1, :] if rev else b[t - 1:t, :]
    qs = (q * jnp.exp(b)).astype(BF16)
    kd = (k * jnp.exp(b_tot - b)).astype(BF16)

    rowv = lax.broadcasted_iota(jnp.int32, (t, B_QK), 0)
    levels = []
    half = GLA_SUB
    while 2 * half <= t:
        blk = 2 * half
        pieces = []
        for bs in range(0, t, blk):
            r = bs + half if rev else bs + half - 1
            pieces.append(jnp.broadcast_to(b[r:r + 1, :], (blk, B_QK)))
        ref = jnp.concatenate(pieces, axis=0) if len(pieces) > 1 else pieces[0]
        second = (rowv & (blk - 1)) >= half
        is_q = ~second if rev else second
        ql = (q * jnp.exp(jnp.where(is_q, b - ref, -jnp.inf))).astype(BF16)
        kl = (k * jnp.exp(jnp.where(is_q, -jnp.inf, ref - b))).astype(BF16)
        levels.append((blk, ql, kl))
        half = blk

    rr = lax.broadcasted_iota(jnp.int32, (t, 2 * t), 0)
    ss = lax.broadcasted_iota(jnp.int32, (t, 2 * t), 1) & (t - 1)
    sub_rows = lax.broadcasted_iota(jnp.int32, (GLA_SUB, LANES), 0)
    lane_lo = _lane_lt((t, LANES), GLA_DK)
    lane_lo2 = _lane_lt((2 * t, LANES), GLA_DK)
    par_rows = lax.broadcasted_iota(jnp.int32, (2 * t, LANES), 0) < t
    head_sel = par_rows == lane_lo2

    outs = []
    for p in range(n_pair):
        pl_ = slice(p * LANES, (p + 1) * LANES)
        qp, kp, bp = q[:, pl_], k[:, pl_], b[:, pl_]
        rows = []
        for sb in range(t // GLA_SUB):
            r0 = sb * GLA_SUB
            q_sub = qp[r0:r0 + GLA_SUB, :]
            b_sub = bp[r0:r0 + GLA_SUB, :]
            cols = []
            for j in range(GLA_SUB):
                kj = kp[r0 + j:r0 + j + 1, :]
                bj = bp[r0 + j:r0 + j + 1, :]
                valid = (sub_rows <= j) if rev else (sub_rows >= j)
                e = jnp.exp(jnp.where(valid, b_sub - bj, -jnp.inf))
                cols.append(q_sub * kj * e)
            rows.append(jnp.concatenate(cols, axis=1))
        e_p = jnp.concatenate(rows, axis=0).astype(BF16)
        scores = jnp.dot(e_p, sel_ref[...], preferred_element_type=F32)
        scores = jnp.where(_same_block(rr, ss, GLA_SUB), scores, 0.0)
        for blk, ql, kl in levels:
            klp = kl[:, pl_]
            kstack = jnp.concatenate([klp, klp], axis=0)
            kstack = jnp.where(head_sel, kstack, jnp.zeros_like(kstack))
            s_l = _dot_nt(ql[:, pl_], kstack)
            scores = scores + jnp.where(_same_block(rr, ss, blk), s_l, 0.0)
        vp = v[:, p * 2 * GLA_DV:(p + 1) * 2 * GLA_DV]
        v_lo = _lane_lt(vp.shape, GLA_DV)
        vbd = jnp.concatenate([jnp.where(v_lo, vp, 0.0), jnp.where(v_lo, 0.0, vp)], axis=0).astype(BF16)
        st = st_ref[p]
        o_p = (jnp.dot(scores.astype(BF16), vbd, preferred_element_type=F32)
               + _dot_nt(qs[:, pl_], st.astype(BF16)))
        outs.append(o_p)
        upd = jnp.dot(vp.T.astype(BF16), kd[:, pl_], preferred_element_type=F32)
        st_rows_lo = lax.broadcasted_iota(jnp.int32, upd.shape, 0) < GLA_DV
        bd = st_rows_lo == _lane_lt(upd.shape, GLA_DK)
        st_ref[p] = st * jnp.exp(b_tot[:, pl_]) + jnp.where(bd, upd, 0.0)
    return jnp.concatenate(outs, axis=1)


def _gla_kernel(*refs, seq, has_init, want_state):
    q_ref, k_ref, v_ref, ldf_ref, ldb_ref = refs[:5]
    pos = 5
    if has_init:
        s0f_ref, s0b_ref = refs[pos:pos + 2]
        pos += 2
    o_ref = refs[pos]
    pos += 1
    if want_state:
        sf_ref, sb_ref = refs[pos:pos + 2]
        pos += 2
    st_ref, sel_ref = refs[pos:pos + 2]
    t = GLA_TILE
    nt = seq // t
    n_pair = GLA_HEADS // 2

    kk = lax.broadcasted_iota(jnp.int32, (GLA_SUB * LANES, 2 * t), 0)
    nn = lax.broadcasted_iota(jnp.int32, (GLA_SUB * LANES, 2 * t), 1)
    sel = ((nn & (GLA_SUB - 1)) == (kk >> LANES_LOG2)) & (((kk & (LANES - 1)) >= GLA_DK) == (nn >= t))
    sel_ref[...] = jnp.where(sel, 1.0, 0.0).astype(BF16)

    def load_state(s0_ref):
        for p in range(n_pair):
            if s0_ref is None:
                st_ref[p] = jnp.zeros(st_ref.shape[1:], F32)
            else:
                x = jnp.concatenate([s0_ref[0, 2 * p], s0_ref[0, 2 * p + 1]], axis=0)
                xt = x.T
                lo = _lane_lt(xt.shape, GLA_DK)
                st_ref[p] = jnp.concatenate([jnp.where(lo, xt, 0.0), jnp.where(lo, 0.0, xt)], axis=0)

    def store_state(out_ref):
        for p in range(n_pair):
            st = st_ref[p]
            lo = _lane_lt((GLA_DV, LANES), GLA_DK)
            z = jnp.where(lo, st[0:GLA_DV, :], st[GLA_DV:2 * GLA_DV, :]).T
            out_ref[0, 2 * p] = z[0:GLA_DK, :]
            out_ref[0, 2 * p + 1] = z[GLA_DK:2 * GLA_DK, :]

    for rev in (False, True):
        ld_ref = ldb_ref if rev else ldf_ref
        load_state((s0b_ref if rev else s0f_ref) if has_init else None)

        def tile_body(ti, carry, rev=rev, ld_ref=ld_ref):
            tile = (nt - 1 - ti) if rev else ti
            r0 = pl.multiple_of(tile * t, t)
            rows = pl.ds(r0, t)
            o = _gla_tile(q_ref[rows, :], k_ref[rows, :], v_ref[rows, :], ld_ref[rows, :],
                          st_ref, sel_ref, rev=rev)
            if rev:
                o_ref[rows, :] = o_ref[rows, :] + o
            else:
                o_ref[rows, :] = o
            return carry

        lax.fori_loop(0, nt, tile_body, 0)
        if want_state:
            store_state(sb_ref if rev else sf_ref)


def _gla(qb, kb, vb, ldf, ldb, s0f, s0b, *, n_seq, seq, row_block0, want_state):
    has_init = s0f is not None
    kern = functools.partial(_gla_kernel, seq=seq, has_init=has_init, want_state=want_state)
    tok = lambda w: pl.BlockSpec((seq, w), lambda b: (row_block0 + b, 0))
    in_specs = [tok(B_QK), tok(B_QK), tok(B_V), tok(B_QK), tok(B_QK)]
    args = [qb, kb, vb, ldf, ldb]
    state_spec = pl.BlockSpec((1, GLA_HEADS, GLA_DK, GLA_DV), lambda b: (b, 0, 0, 0))
    if has_init:
        in_specs += [state_spec, state_spec]
        args += [s0f, s0b]
    out_specs = [pl.BlockSpec((seq, B_V), lambda b: (b, 0))]
    out_shape = [jax.ShapeDtypeStruct((n_seq * seq, B_V), F32)]
    if want_state:
        out_specs += [state_spec, state_spec]
        out_shape += [jax.ShapeDtypeStruct((n_seq, GLA_HEADS, GLA_DK, GLA_DV), F32)] * 2
    return pl.pallas_call(
        kern,
        grid=(n_seq,),
        in_specs=in_specs,
        out_specs=out_specs,
        out_shape=out_shape,
        scratch_shapes=[
            pltpu.VMEM((GLA_HEADS // 2, 2 * GLA_DV, LANES), F32),
            pltpu.VMEM((GLA_SUB * LANES, 2 * GLA_TILE), BF16),
        ],
        compiler_params=pltpu.CompilerParams(
            dimension_semantics=("parallel",), vmem_limit_bytes=VMEM_LIMIT_BYTES),
        name="gla_state" if want_state else "gla_latent",
    )(*args)


def _evout_kernel(x_ref, mod_ref, gpost_ref, attn_ref, gla_ref, gb_ref, gn_ref, w_ref, o_ref, w_bf):
    @pl.when(pl.program_id(0) == 0)
    def _cast_weights():
        w_bf[...] = w_ref[0].astype(BF16)

    y = jnp.dot(attn_ref[...].astype(BF16), w_bf[0:A_Q, :], preferred_element_type=F32)
    for hd in range(GLA_HEADS):
        sl = slice(hd * GLA_DV, (hd + 1) * GLA_DV)
        g = _rms(gla_ref[:, sl], gn_ref[:, sl]) * _silu(gb_ref[:, sl])
        y = y + jnp.dot(g.astype(BF16), w_bf[A_Q + hd * GLA_DV:A_Q + (hd + 1) * GLA_DV, :],
                        preferred_element_type=F32)
    o_ref[...] = _sub_out(x_ref[...], y, mod_ref, gpost_ref, 1, 1.0)


def _even_out(x, mods_l, gpost_l, attn_o, gla_o, gb, gla_norm, w_out, *, group):
    n_tok, d = x.shape
    tm = TOKEN_TILE
    ev_out = w_out.shape[1]
    return pl.pallas_call(
        _evout_kernel,
        grid=(n_tok // tm,),
        in_specs=[
            pl.BlockSpec((tm, d), lambda i: (i, 0)),
            pl.BlockSpec((None, 3 * N_SUB, d), lambda i: (group(i), 0, 0)),
            pl.BlockSpec((N_SUB, d), lambda i: (0, 0)),
            pl.BlockSpec((tm, A_Q), lambda i: (i, 0)),
            pl.BlockSpec((tm, B_V), lambda i: (i, 0)),
            pl.BlockSpec((tm, B_V), lambda i: (i, 0)),
            pl.BlockSpec((1, B_V), lambda i: (0, 0)),
            pl.BlockSpec((1, ev_out, d), lambda i: (0, 0, 0), pipeline_mode=pl.Buffered(1)),
        ],
        out_specs=pl.BlockSpec((tm, d), lambda i: (i, 0)),
        out_shape=jax.ShapeDtypeStruct((n_tok, d), F32),
        scratch_shapes=[pltpu.VMEM((ev_out, d), BF16)],
        compiler_params=pltpu.CompilerParams(
            dimension_semantics=("arbitrary",), vmem_limit_bytes=VMEM_LIMIT_BYTES),
        name="even_out",
    )(x, mods_l, gpost_l, attn_o, gla_o, gb, gla_norm, w_out)


def _gelu(x):
    return 0.5 * x * (1.0 + lax.erf(x * (2.0 ** -0.5)))


def _cm_kernel(x_ref, mod_ref, gpre_ref, gpost_ref, win_ref, vg_ref, vb_ref, ws_ref, bs_ref, wout_ref,
               o_ref, win_bf, wout_bf):
    @pl.when(pl.program_id(0) == 0)
    def _cast_weights():
        win_bf[...] = win_ref[0].astype(BF16)
        wout_bf[...] = wout_ref[0].astype(BF16)

    x = x_ref[...]
    width = wout_bf.shape[0]
    gw = width // CMLP_GROUPS
    h = _sub_in(x, mod_ref, gpre_ref, 1).astype(BF16)
    u = _gelu(jnp.dot(h, win_bf[:, 0:width], preferred_element_type=F32))
    v = _gelu(jnp.dot(h, win_bf[:, width:2 * width], preferred_element_type=F32))
    mu = jnp.mean(v, axis=-1, keepdims=True)
    vc = v - mu
    var = jnp.mean(vc * vc, axis=-1, keepdims=True)
    vn = (vc * lax.rsqrt(var + EPS) * vg_ref[...] + vb_ref[...]).astype(BF16)
    ws = [ws_ref[0, g].astype(BF16) for g in range(CMLP_GROUPS)]
    rows = []
    for c in range(x.shape[0] // CHUNK):
        cols = []
        for g in range(CMLP_GROUPS):
            blk = vn[c * CHUNK:(c + 1) * CHUNK, g * gw:(g + 1) * gw]
            cols.append(jnp.dot(ws[g], blk, preferred_element_type=F32) + bs_ref[0, :, g:g + 1])
        rows.append(jnp.concatenate(cols, axis=1))
    mixed = jnp.concatenate(rows, axis=0)
    y = jnp.dot((u * mixed).astype(BF16), wout_bf[...], preferred_element_type=F32)
    o_ref[...] = _sub_out(x, y, mod_ref, gpost_ref, 1, 1.0)


def _chunk_mlp(x, mods_l, gpre_l, gpost_l, w_in, v_gain, v_bias, w_s, b_s_t, w_out, *, group):
    n_tok, d = x.shape
    tm = TOKEN_TILE
    width = w_out.shape[1]
    return pl.pallas_call(
        _cm_kernel,
        grid=(n_tok // tm,),
        in_specs=[
            pl.BlockSpec((tm, d), lambda i: (i, 0)),
            pl.BlockSpec((None, 3 * N_SUB, d), lambda i: (group(i), 0, 0)),
            pl.BlockSpec((N_SUB, d), lambda i: (0, 0)),
            pl.BlockSpec((N_SUB, d), lambda i: (0, 0)),
            pl.BlockSpec((1, d, 2 * width), lambda i: (0, 0, 0), pipeline_mode=pl.Buffered(1)),
            pl.BlockSpec((1, width), lambda i: (0, 0)),
            pl.BlockSpec((1, width), lambda i: (0, 0)),
            pl.BlockSpec((1, CMLP_GROUPS, CHUNK, CHUNK), lambda i: (0, 0, 0, 0)),
            pl.BlockSpec((1, CHUNK, CMLP_GROUPS), lambda i: (0, 0, 0)),
            pl.BlockSpec((1, width, d), lambda i: (0, 0, 0), pipeline_mode=pl.Buffered(1)),
        ],
        out_specs=pl.BlockSpec((tm, d), lambda i: (i, 0)),
        out_shape=jax.ShapeDtypeStruct((n_tok, d), F32),
        scratch_shapes=[pltpu.VMEM((d, 2 * width), BF16), pltpu.VMEM((width, d), BF16)],
        compiler_params=pltpu.CompilerParams(
            dimension_semantics=("arbitrary",), vmem_limit_bytes=VMEM_LIMIT_BYTES),
        name="chunk_mlp",
    )(x, mods_l, gpre_l, gpost_l, w_in, v_gain, v_bias, w_s, b_s_t, w_out)


def kernel(x_prompt, x_sample, cache_k, cache_v, state_gla_fwd, state_gla_bwd, c, c_ctx, w_mod, b_mod, norm_pre, norm_post, ffn_w_gate, ffn_w_up, ffn_w_down, ev_w_in, ev_w_out, ev_sink, gla_wa_f, gla_ba_f, gla_wa_b, gla_ba_b, gla_norm, cm_w_in, cm_v_gain, cm_v_bias, cm_w_s, cm_b_s, cm_w_out):
    batch, seq, d = x_prompt.shape
    dec_batch, dec_seq, _ = x_sample.shape
    depth = w_mod.shape[0]
    n_prompt_tok = batch * seq
    n_sample_tok = dec_batch * dec_seq
    assert n_prompt_tok % dec_seq == 0 and dec_seq % TOKEN_TILE == 0
    group = _group_index_map(TOKEN_TILE, n_prompt_tok, dec_seq)

    x = jnp.concatenate([x_prompt.reshape(n_prompt_tok, d), x_sample.reshape(n_sample_tok, d)], axis=0)

    n_cond = 1 + dec_batch
    cond_rows = -(-n_cond // SUBLANES) * SUBLANES
    cond = jnp.concatenate([c_ctx[None, :], c, jnp.zeros((cond_rows - n_cond, d), F32)], axis=0)
    mods = _adaln_mods(cond, w_mod, b_mod).reshape(depth, cond_rows, 3 * N_SUB, d)

    new_k, new_v, new_sf, new_sb = [], [], [], []
    for layer in range(depth):
        mods_l, gpre_l, gpost_l = mods[layer], norm_pre[layer], norm_post[layer]
        x = _ffn(x, mods_l, gpre_l, gpost_l, ffn_w_gate, ffn_w_up, ffn_w_down,
                 sub=0, layer=layer, half=0, group=group)
        if layer % 2 == 0:
            e = layer // 2
            rank = gla_wa_f.shape[1]
            wa_bd = jnp.zeros((LANES, 2 * B_QK), F32)
            wa_bd = wa_bd.at[0:rank, 0:B_QK].set(gla_wa_f[e]).at[rank:2 * rank, B_QK:].set(gla_wa_b[e])
            ba_cat = jnp.concatenate([gla_ba_f[e], gla_ba_b[e]])[None, :]
            qa, ka, va, qb, kb, vb, gb, ldf, ldb = _even_in(
                x, mods_l, gpre_l, ev_w_in[e:e + 1], wa_bd, ba_cat, group=group)
            sink = ev_sink[e]
            attn_p = _attn_context(sink, qa, ka, va, n_seq=batch, seq=seq)
            k_ctx = cache_k[:, e].reshape(dec_batch, -1, A_KV)
            v_ctx = cache_v[:, e].reshape(dec_batch, -1, A_KV)
            attn_s = _attn_latent(sink, qa, ka, va, k_ctx, v_ctx, n_seq=dec_batch, seq=dec_seq,
                                  row_block0=n_prompt_tok // dec_seq)
            gla_p, s_f, s_b = _gla(qb, kb, vb, ldf, ldb, None, None, n_seq=batch, seq=seq,
                                   row_block0=0, want_state=True)
            (gla_s,) = _gla(qb, kb, vb, ldf, ldb, state_gla_fwd[:, e], state_gla_bwd[:, e],
                            n_seq=dec_batch, seq=dec_seq, row_block0=n_prompt_tok // dec_seq,
                            want_state=False)
            attn_o = jnp.concatenate([attn_p, attn_s], axis=0)
            gla_o = jnp.concatenate([gla_p, gla_s], axis=0)
            x = _even_out(x, mods_l, gpost_l, attn_o, gla_o, gb, gla_norm[e][None, :],
                          ev_w_out[e:e + 1], group=group)
            new_k.append(ka[:n_prompt_tok].reshape(batch, seq, ATTN_KV_HEADS, HEAD_DIM))
            new_v.append(va[:n_prompt_tok].reshape(batch, seq, ATTN_KV_HEADS, HEAD_DIM))
            new_sf.append(s_f)
            new_sb.append(s_b)
        else:
            o = layer // 2
            x = _chunk_mlp(x, mods_l, gpre_l, gpost_l, cm_w_in[o:o + 1], cm_v_gain[o][None, :],
                           cm_v_bias[o][None, :], cm_w_s[o:o + 1],
                           jnp.swapaxes(cm_b_s[o], 0, 1)[None], cm_w_out[o:o + 1], group=group)
        x = _ffn(x, mods_l, gpre_l, gpost_l, ffn_w_gate, ffn_w_up, ffn_w_down,
                 sub=2, layer=layer, half=1, group=group)

    y_prompt = x[:n_prompt_tok].reshape(batch, seq, d)
    y_sample = x[n_prompt_tok:].reshape(dec_batch, dec_seq, d)
    return (y_prompt, y_sample, jnp.stack(new_k, axis=1), jnp.stack(new_v, axis=1),
            jnp.stack(new_sf, axis=1), jnp.stack(new_sb, axis=1))
```

```python
import functools
import math

import jax
import jax.numpy as jnp
from jax import lax
from jax.experimental import pallas as pl
from jax.experimental.pallas import tpu as pltpu

F32 = jnp.float32
BF16 = jnp.bfloat16

EPS = 1e-6
NEG_INF = -1e30
FFN_RESIDUAL = 0.5
N_SUB = 3

ATTN_HEADS = 8
ATTN_KV_HEADS = 2
HEAD_DIM = 64
ATTN_BLOCK = 128
WINDOW = 128
GRID_W = 64
GRID_W_LOG2 = 6
ROPE_BASE = 10000.0
GLA_HEADS = 4
GLA_DK = 64
GLA_DV = 128
GLA_RANK = 16
GLA_TAU = 16.0
CHUNK = 128
CMLP_GROUPS = 4

A_Q = ATTN_HEADS * HEAD_DIM
A_KV = ATTN_KV_HEADS * HEAD_DIM
B_QK = GLA_HEADS * GLA_DK
B_V = GLA_HEADS * GLA_DV
EV_MAIN = A_Q + 2 * A_KV + 2 * B_QK + 2 * B_V
EV_OFFS = (0, A_Q, A_Q + A_KV, A_Q + 2 * A_KV, A_Q + 2 * A_KV + B_QK,
           A_Q + 2 * A_KV + 2 * B_QK, A_Q + 2 * A_KV + 2 * B_QK + B_V, EV_MAIN)

LANES = 128
LANES_LOG2 = 7
SUBLANES = 8
VMEM_LIMIT_BYTES = 56 * 1024 * 1024

TOKEN_TILE = 512
FFN_CHUNK = 256
GLA_TILE = 128
GLA_SUB = 8


def _group_index_map(tile_rows, n_prompt_tok, seq_tok):
    def group(i):
        return jnp.maximum(i * tile_rows - (n_prompt_tok - seq_tok), 0) // seq_tok
    return group


def _rms(x, g):
    ms = jnp.mean(x * x, axis=-1, keepdims=True)
    return x * lax.rsqrt(ms + EPS) * g


def _sub_in(x, mod_ref, gpre_ref, sub):
    shift = mod_ref[3 * sub + 0:3 * sub + 1, :]
    scale = mod_ref[3 * sub + 1:3 * sub + 2, :]
    return _rms(x, gpre_ref[sub:sub + 1, :]) * (1.0 + scale) + shift


def _sub_out(x, y, mod_ref, gpost_ref, sub, coef):
    gate = mod_ref[3 * sub + 2:3 * sub + 3, :]
    return x + (coef * gate) * _rms(y, gpost_ref[sub:sub + 1, :])


def _silu(x):
    return x * jax.nn.sigmoid(x)


def _mod_kernel(cond_ref, w_ref, b_ref, o_ref):
    s = _silu(cond_ref[...]).astype(BF16)
    o_ref[0] = jnp.dot(s, w_ref[0].astype(BF16), preferred_element_type=F32) + b_ref[0]


def _adaln_mods(cond, w_mod, b_mod):
    depth, d, n = w_mod.shape
    tn = 1024
    rows = cond.shape[0]
    return pl.pallas_call(
        _mod_kernel,
        grid=(depth, n // tn),
        in_specs=[
            pl.BlockSpec((rows, d), lambda l, j: (0, 0)),
            pl.BlockSpec((1, d, tn), lambda l, j: (l, 0, j)),
            pl.BlockSpec((1, 1, tn), lambda l, j: (l, 0, j)),
        ],
        out_specs=pl.BlockSpec((1, rows, tn), lambda l, j: (l, 0, j)),
        out_shape=jax.ShapeDtypeStruct((depth, rows, n), F32),
        compiler_params=pltpu.CompilerParams(dimension_semantics=("parallel", "parallel")),
        name="adaln_mods",
    )(cond, w_mod, b_mod.reshape(depth, 1, n))


def _tile_specs(tm, width, np_tiles):
    return [pl.BlockSpec((tm, width), lambda i: (jnp.minimum(i, np_tiles - 1), 0)),
            pl.BlockSpec((tm, width), lambda i: (jnp.maximum(i - np_tiles, 0), 0))]


def _pick(refs, is_prompt):
    if len(refs) == 1:
        return refs[0][...]
    return jnp.where(is_prompt, refs[0][...], refs[1][...])


def _ffn_kernel(*refs, sub, layer, half, n_chunks, n_x, n_o, np_tiles):
    x_refs = refs[:n_x]
    mod_ref, gpre_ref, gpost_ref, wg_hbm, wu_hbm, wd_hbm = refs[n_x:n_x + 6]
    o_refs = refs[n_x + 6:n_x + 6 + n_o]
    wg_bf, wu_bf, wd_bf, st_g, st_u, st_d, sem = refs[n_x + 6 + n_o:]
    fc = FFN_CHUNK
    step = pl.program_id(0)
    is_prompt = step < np_tiles

    def chunk_copies(c, slot):
        cols = pl.ds(c * fc, fc)
        return (
            pltpu.make_async_copy(wg_hbm.at[layer, half, :, cols], st_g.at[slot], sem.at[0, slot]),
            pltpu.make_async_copy(wu_hbm.at[layer, half, :, cols], st_u.at[slot], sem.at[1, slot]),
            pltpu.make_async_copy(wd_hbm.at[layer, half, cols, :], st_d.at[slot], sem.at[2, slot]),
        )

    def run(stage_weights):
        if stage_weights:
            for cp in chunk_copies(0, 0):
                cp.start()
        x = _pick(x_refs, is_prompt)
        h = _sub_in(x, mod_ref, gpre_ref, sub).astype(BF16)
        y = jnp.zeros(x.shape, F32)
        for c in range(n_chunks):
            sl = slice(c * fc, (c + 1) * fc)
            if stage_weights:
                slot = c % 2
                if c + 1 < n_chunks:
                    for cp in chunk_copies(c + 1, 1 - slot):
                        cp.start()
                for cp in chunk_copies(c, slot):
                    cp.wait()
                wg_bf[:, sl] = st_g[slot].astype(BF16)
                wu_bf[:, sl] = st_u[slot].astype(BF16)
                wd_bf[sl, :] = st_d[slot].astype(BF16)
            g = jnp.dot(h, wg_bf[:, sl], preferred_element_type=F32)
            u = jnp.dot(h, wu_bf[:, sl], preferred_element_type=F32)
            a = (_silu(g) * u).astype(BF16)
            y = y + jnp.dot(a, wd_bf[sl, :], preferred_element_type=F32)
        out = _sub_out(x, y, mod_ref, gpost_ref, sub, FFN_RESIDUAL)
        if n_o == 1:
            o_refs[0][...] = out
        else:
            @pl.when(is_prompt)
            def _store_prompt():
                o_refs[0][...] = out

            @pl.when(jnp.logical_not(is_prompt))
            def _store_latent():
                o_refs[1][...] = out

    @pl.when(step == 0)
    def _first_tile():
        run(True)

    @pl.when(step > 0)
    def _other_tiles():
        run(False)


def _ffn(xs, mods_l, gpre_l, gpost_l, wg, wu, wd, *, sub, layer, half, group, np_tiles, split_out):
    d = xs[0].shape[-1]
    n_tok = sum(x.shape[0] for x in xs)
    d_ff = wg.shape[-1]
    tm = TOKEN_TILE
    fc = FFN_CHUNK
    n_chunks = d_ff // fc
    assert n_chunks * fc == d_ff and n_tok % tm == 0
    n_x, n_o = len(xs), (2 if split_out else 1)
    kern = functools.partial(_ffn_kernel, sub=sub, layer=layer, half=half, n_chunks=n_chunks,
                             n_x=n_x, n_o=n_o, np_tiles=np_tiles)
    whole = pl.BlockSpec((tm, d), lambda i: (i, 0))
    x_specs = _tile_specs(tm, d, np_tiles) if n_x == 2 else [whole]
    n_p = np_tiles * tm
    if split_out:
        out_specs = _tile_specs(tm, d, np_tiles)
        out_shape = [jax.ShapeDtypeStruct((n_p, d), F32), jax.ShapeDtypeStruct((n_tok - n_p, d), F32)]
    else:
        out_specs = [whole]
        out_shape = [jax.ShapeDtypeStruct((n_tok, d), F32)]
    return pl.pallas_call(
        kern,
        grid=(n_tok // tm,),
        in_specs=x_specs + [
            pl.BlockSpec((None, 3 * N_SUB, d), lambda i: (group(i), 0, 0)),
            pl.BlockSpec((N_SUB, d), lambda i: (0, 0)),
            pl.BlockSpec((N_SUB, d), lambda i: (0, 0)),
            pl.BlockSpec(memory_space=pl.ANY),
            pl.BlockSpec(memory_space=pl.ANY),
            pl.BlockSpec(memory_space=pl.ANY),
        ],
        out_specs=out_specs,
        out_shape=out_shape,
        scratch_shapes=[
            pltpu.VMEM((d, d_ff), BF16),
            pltpu.VMEM((d, d_ff), BF16),
            pltpu.VMEM((d_ff, d), BF16),
            pltpu.VMEM((2, d, fc), F32),
            pltpu.VMEM((2, d, fc), F32),
            pltpu.VMEM((2, fc, d), F32),
            pltpu.SemaphoreType.DMA((3, 2)),
        ],
        compiler_params=pltpu.CompilerParams(
            dimension_semantics=("arbitrary",), vmem_limit_bytes=VMEM_LIMIT_BYTES),
        name=f"ffn_l{layer}h{half}",
    )(*xs, mods_l, gpre_l, gpost_l, wg, wu, wd)


def _log_sigmoid(x):
    return jnp.minimum(x, 0.0) - jnp.log1p(jnp.exp(-jnp.abs(x)))


def _evin_kernel(x_ref, mod_ref, gpre_ref, w_ref, wa_ref, ba_ref,
                 qa_ref, ka_ref, va_ref, qb_ref, kb_ref, vb_ref, gb_ref, ldf_ref, ldb_ref,
                 w_bf, wlr_bf):
    @pl.when(pl.program_id(0) == 0)
    def _cast_weights():
        w_bf[...] = w_ref[0, 0:EV_MAIN, :].astype(BF16)
        wlr_bf[...] = jnp.zeros(wlr_bf.shape, BF16)
        wlr_bf[0:2 * GLA_RANK, :] = w_ref[0, EV_MAIN:EV_MAIN + 2 * GLA_RANK, :].astype(BF16)

    h = _sub_in(x_ref[...], mod_ref, gpre_ref, 1).astype(BF16)

    def proj(lo, hi):
        return _dot_nt(h, w_bf[lo:hi, :])

    o = EV_OFFS
    qa_ref[...] = (proj(o[0], o[1]) * (HEAD_DIM ** -0.5)).astype(qa_ref.dtype)
    ka_ref[...] = proj(o[1], o[2])
    va_ref[...] = proj(o[2], o[3])
    qb_ref[...] = proj(o[3], o[4]) * (GLA_DK ** -0.5)
    kb_ref[...] = proj(o[4], o[5])
    vb_ref[...] = proj(o[5], o[6]).astype(vb_ref.dtype)
    gb_ref[...] = proj(o[6], o[7]).astype(gb_ref.dtype)
    lr = _dot_nt(h, wlr_bf[...]).astype(BF16)
    logits = jnp.dot(lr, wa_ref[...].astype(BF16), preferred_element_type=F32) + ba_ref[...]
    ld = _log_sigmoid(logits) * (1.0 / GLA_TAU)
    ldf_ref[...] = ld[:, 0:B_QK]
    ldb_ref[...] = ld[:, B_QK:2 * B_QK]


def _even_in(x, mods_l, gpre_l, w_in_t, e, wa_bd, ba_cat, *, group):
    n_tok, d = x.shape
    tm = TOKEN_TILE
    ev_in = w_in_t.shape[1]
    widths = (A_Q, A_KV, A_KV, B_QK, B_QK, B_V, B_V, B_QK, B_QK)
    dtypes = (BF16, F32, F32, F32, F32, BF16, BF16, F32, F32)
    return pl.pallas_call(
        _evin_kernel,
        grid=(n_tok // tm,),
        in_specs=[
            pl.BlockSpec((tm, d), lambda i: (i, 0)),
            pl.BlockSpec((None, 3 * N_SUB, d), lambda i: (group(i), 0, 0)),
            pl.BlockSpec((N_SUB, d), lambda i: (0, 0)),
            pl.BlockSpec((1, ev_in, d), lambda i: (e, 0, 0), pipeline_mode=pl.Buffered(1)),
            pl.BlockSpec(wa_bd.shape, lambda i: (0, 0)),
            pl.BlockSpec(ba_cat.shape, lambda i: (0, 0)),
        ],
        out_specs=[pl.BlockSpec((tm, w), lambda i: (i, 0)) for w in widths],
        out_shape=[jax.ShapeDtypeStruct((n_tok, w), t) for w, t in zip(widths, dtypes)],
        scratch_shapes=[pltpu.VMEM((EV_MAIN, d), BF16), pltpu.VMEM((LANES, d), BF16)],
        compiler_params=pltpu.CompilerParams(
            dimension_semantics=("arbitrary",), vmem_limit_bytes=VMEM_LIMIT_BYTES),
        name="even_in",
    )(x, mods_l, gpre_l, w_in_t, wa_bd, ba_cat)


def _lane_lt(shape, n):
    return lax.broadcasted_iota(jnp.int32, shape, len(shape) - 1) < n


def _dup_kv_head(x, g):
    sw = pltpu.roll(x, HEAD_DIM, 1)
    lo = _lane_lt(x.shape, HEAD_DIM)
    return jnp.where(lo, x, sw) if g == 0 else jnp.where(lo, sw, x)


def _dot_nt(a, b):
    return lax.dot_general(a, b, (((1,), (1,)), ((), ())), preferred_element_type=F32)


def _attn_ctx_kernel(sink_ref, q_ref, k_ref, v_ref, o_ref, kt_ref, vt_ref):
    k = k_ref[...]
    v = v_ref[...]
    kt_ref[0] = k.T
    vt_ref[0] = v.T
    k2 =[_dup_kv_head(k, g).astype(BF16) for g in range(ATTN_KV_HEADS)]
    v2 = [_dup_kv_head(v, g).astype(BF16) for g in range(ATTN_KV_HEADS)]
    group = ATTN_HEADS // ATTN_KV_HEADS
    for m in range(ATTN_HEADS // 2):
        qm = q_ref[:, m * LANES:(m + 1) * LANES]
        lo = _lane_lt(qm.shape, HEAD_DIM)
        outs = []
        for par in range(2):
            hd = 2 * m + par
            g = hd // group
            qh = jnp.where(lo if par == 0 else ~lo, qm, 0.0).astype(BF16)
            s = _dot_nt(qh, k2[g])
            sink = sink_ref[hd]
            mx = jnp.maximum(jnp.max(s, axis=-1, keepdims=True), sink)
            p = jnp.exp(s - mx)
            den = jnp.sum(p, axis=-1, keepdims=True) + jnp.exp(sink - mx)
            outs.append(jnp.dot(p.astype(BF16), v2[g], preferred_element_type=F32) / den)
        o_ref[:, m * LANES:(m + 1) * LANES] = jnp.where(lo, outs[0], outs[1]).astype(o_ref.dtype)


def _attn_context(sink, qa, ka, va, *, n_seq, seq):
    cache_spec = pl.BlockSpec((1, A_KV, seq), lambda b: (b, 0, 0))
    return pl.pallas_call(
        _attn_ctx_kernel,
        grid=(n_seq,),
        in_specs=[
            pl.BlockSpec(memory_space=pltpu.SMEM),
            pl.BlockSpec((seq, A_Q), lambda b: (b, 0)),
            pl.BlockSpec((seq, A_KV), lambda b: (b, 0)),
            pl.BlockSpec((seq, A_KV), lambda b: (b, 0)),
        ],
        out_specs=[pl.BlockSpec((seq, A_Q), lambda b: (b, 0)), cache_spec, cache_spec],
        out_shape=[jax.ShapeDtypeStruct((n_seq * seq, A_Q), BF16),
                   jax.ShapeDtypeStruct((n_seq, A_KV, seq), F32),
                   jax.ShapeDtypeStruct((n_seq, A_KV, seq), F32)],
        compiler_params=pltpu.CompilerParams(dimension_semantics=("parallel",)),
        name="attn_context",
    )(sink, qa, ka, va)


def _rope_tables(n_tok):
    shape = (n_tok, LANES)
    t = lax.broadcasted_iota(jnp.int32, shape, 0)
    lane = lax.broadcasted_iota(jnp.int32, shape, 1)
    half = HEAD_DIM // 2
    nf = half // 2
    within = lane & (HEAD_DIM - 1)
    is_col = within >= half
    second = (within & (half - 1)) >= nf
    f = (within & (nf - 1)).astype(F32)
    inv = jnp.exp(f * (-2.0 / half * math.log(ROPE_BASE)))
    pos = jnp.where(is_col, t & (GRID_W - 1), t >> GRID_W_LOG2).astype(F32)
    ang = pos * inv
    cos = jnp.cos(ang)
    sin = jnp.sin(ang)
    return cos, jnp.where(second, sin, -sin), second


def _rope(x, cos, sin_signed, second):
    nf = HEAD_DIM // 4
    up = pltpu.roll(x, nf, 1)
    dn = pltpu.roll(x, LANES - nf, 1)
    return x * cos + jnp.where(second, up, dn) * sin_signed


def _attn_lat_kernel(sink_ref, q_ref, k_ref, v_ref, kc_ref, vc_ref, o_ref, qr_ref, kp_ref, vp_ref,
                     *, seq):
    nb = seq // ATTN_BLOCK
    blk = ATTN_BLOCK
    cos, sin_signed, second = _rope_tables(seq)
    for m in range(A_Q // LANES):
        qr_ref[:, m * LANES:(m + 1) * LANES] = _rope(
            q_ref[:, m * LANES:(m + 1) * LANES].astype(F32), cos, sin_signed, second).astype(BF16)
    kr = _rope(k_ref[...], cos, sin_signed, second)
    v = v_ref[...]
    zpad = jnp.zeros((blk, LANES), BF16)
    kc2, vc2 = [], []
    for g in range(ATTN_KV_HEADS):
        kp_ref[g, 0:blk, :] = zpad
        kp_ref[g, blk + seq:2 * blk + seq, :] = zpad
        vp_ref[g, 0:blk, :] = zpad
        vp_ref[g, blk + seq:2 * blk + seq, :] = zpad
        kp_ref[g, blk:blk + seq, :] = _dup_kv_head(kr, g).astype(BF16)
        vp_ref[g, blk:blk + seq, :] = _dup_kv_head(v, g).astype(BF16)
        kc2.append(_dup_kv_head(kc_ref[0], g).astype(BF16))
        vc2.append(_dup_kv_head(vc_ref[0], g).astype(BF16))
    group = ATTN_HEADS // ATTN_KV_HEADS
    r = lax.broadcasted_iota(jnp.int32, (blk, 3 * blk), 0)
    c = lax.broadcasted_iota(jnp.int32, (blk, 3 * blk), 1)
    band = (c >= r) & (c <= r + 2 * WINDOW)

    def block_body(i, carry):
        row0 = pl.multiple_of(i * blk, blk)
        kj = (i - 1) * blk + c
        mask = band & (kj >= 0) & (kj < seq)
        for m in range(ATTN_HEADS // 2):
            qm = qr_ref[pl.ds(row0, blk), m * LANES:(m + 1) * LANES]
            lo = _lane_lt(qm.shape, HEAD_DIM)
            outs = []
            for par in range(2):
                hd = 2 * m + par
                g = hd // group
                qh = jnp.where(lo if par == 0 else ~lo, qm, jnp.zeros_like(qm))
                kw = kp_ref[g, pl.ds(row0, 3 * blk), :]
                vw = vp_ref[g, pl.ds(row0, 3 * blk), :]
                s_loc = jnp.where(mask, _dot_nt(qh, kw), NEG_INF)
                s_ctx = _dot_nt(qh, kc2[g])
                sink = sink_ref[hd]
                mx = jnp.maximum(jnp.maximum(jnp.max(s_loc, axis=-1, keepdims=True),
                                             jnp.max(s_ctx, axis=-1, keepdims=True)), sink)
                p_loc = jnp.exp(s_loc - mx)
                p_ctx = jnp.exp(s_ctx - mx)
                den = (jnp.sum(p_loc, axis=-1, keepdims=True) + jnp.sum(p_ctx, axis=-1, keepdims=True)
                       + jnp.exp(sink - mx))
                o = (jnp.dot(p_loc.astype(BF16), vw, preferred_element_type=F32)
                     + jnp.dot(p_ctx.astype(BF16), vc2[g], preferred_element_type=F32))
                outs.append(o / den)
            o_ref[pl.ds(row0, blk), m * LANES:(m + 1) * LANES] = jnp.where(
                lo, outs[0], outs[1]).astype(o_ref.dtype)
        return carry

    lax.fori_loop(0, nb, block_body, 0)


def _attn_latent(sink, qa, ka, va, k_ctx, v_ctx, *, n_seq, seq, row_block0):
    past = k_ctx.shape[1]
    kern = functools.partial(_attn_lat_kernel, seq=seq)
    return pl.pallas_call(
        kern,
        grid=(n_seq,),
        in_specs=[
            pl.BlockSpec(memory_space=pltpu.SMEM),
            pl.BlockSpec((seq, A_Q), lambda b: (row_block0 + b, 0)),
            pl.BlockSpec((seq, A_KV), lambda b: (row_block0 + b, 0)),
            pl.BlockSpec((seq, A_KV), lambda b: (row_block0 + b, 0)),
            pl.BlockSpec((1, past, A_KV), lambda b: (b, 0, 0)),
            pl.BlockSpec((1, past, A_KV), lambda b: (b, 0, 0)),
        ],
        out_specs=pl.BlockSpec((seq, A_Q), lambda b: (b, 0)),
        out_shape=jax.ShapeDtypeStruct((n_seq * seq, A_Q), BF16),
        scratch_shapes=[
            pltpu.VMEM((seq, A_Q), BF16),
            pltpu.VMEM((ATTN_KV_HEADS, seq + 2 * ATTN_BLOCK, LANES), BF16),
            pltpu.VMEM((ATTN_KV_HEADS, seq + 2 * ATTN_BLOCK, LANES), BF16),
        ],
        compiler_params=pltpu.CompilerParams(dimension_semantics=("parallel",)),
        name="attn_latent",
    )(sink, qa, ka, va, k_ctx, v_ctx)


def _same_block(a, b, size):
    return (a & -size) == (b & -size)


def _split3(x):
    hi = x.astype(BF16)
    r1 = x - hi.astype(F32)
    mid = r1.astype(BF16)
    lo = (r1 - mid.astype(F32)).astype(BF16)
    return hi, mid, lo


def _gla_tile(q, k, v, ld, st_ref, sel_ref, *, rev):
    t = GLA_TILE
    n_pair = GLA_HEADS // 2
    row = lax.broadcasted_iota(jnp.int32, (t, t), 0)
    col = lax.broadcasted_iota(jnp.int32, (t, t), 1)
    tri = (row <= col) if rev else (row >= col)
    tri = jnp.where(tri, 1.0, 0.0).astype(BF16)
    hi, mid, lo = _split3(ld)
    b = (jnp.dot(tri, hi, preferred_element_type=F32) + jnp.dot(tri, mid, preferred_element_type=F32)
         + jnp.dot(tri, lo, preferred_element_type=F32))
    b_tot = b[0:1, :] if rev else b[t - 1:t, :]
    qs = (q * jnp.exp(b)).astype(BF16)
    kd = (k * jnp.exp(b_tot - b)).astype(BF16)

    rowv = lax.broadcasted_iota(jnp.int32, (t, B_QK), 0)
    levels = []
    half = GLA_SUB
    while 2 * half <= t:
        blk = 2 * half
        pieces = []
        for bs in range(0, t, blk):
            r = bs + half if rev else bs + half - 1
            pieces.append(jnp.broadcast_to(b[r:r + 1, :], (blk, B_QK)))
        ref = jnp.concatenate(pieces, axis=0) if len(pieces) > 1 else pieces[0]
        second = (rowv & (blk - 1)) >= half
        is_q = ~second if rev else second
        ql = (q * jnp.exp(jnp.where(is_q, b - ref, -jnp.inf))).astype(BF16)
        kl = (k * jnp.exp(jnp.where(is_q, -jnp.inf, ref - b))).astype(BF16)
        levels.append((blk, ql, kl))
        half = blk

    rr = lax.broadcasted_iota(jnp.int32, (t, 2 * t), 0)
    ss = lax.broadcasted_iota(jnp.int32, (t, 2 * t), 1) & (t - 1)
    sub_rows = lax.broadcasted_iota(jnp.int32, (GLA_SUB, LANES), 0)
    lane_lo = _lane_lt((t, LANES), GLA_DK)
    lane_lo2 = _lane_lt((2 * t, LANES), GLA_DK)
    par_rows = lax.broadcasted_iota(jnp.int32, (2 * t, LANES), 0) < t
    head_sel = par_rows == lane_lo2

    outs = []
    for p in range(n_pair):
        pl_ = slice(p * LANES, (p + 1) * LANES)
        qp, kp, bp = q[:, pl_], k[:, pl_], b[:, pl_]
        rows = []
        for sb in range(t // GLA_SUB):
            r0 = sb * GLA_SUB
            q_sub = qp[r0:r0 + GLA_SUB, :]
            b_sub = bp[r0:r0 + GLA_SUB, :]
            cols = []
            for j in range(GLA_SUB):
                kj = kp[r0 + j:r0 + j + 1, :]
                bj = bp[r0 + j:r0 + j + 1, :]
                valid = (sub_rows <= j) if rev else (sub_rows >= j)
                e = jnp.exp(jnp.where(valid, b_sub - bj, -jnp.inf))
                cols.append(q_sub * kj * e)
            rows.append(jnp.concatenate(cols, axis=1))
        e_p = jnp.concatenate(rows, axis=0).astype(BF16)
        scores = jnp.dot(e_p, sel_ref[...], preferred_element_type=F32)
        scores = jnp.where(_same_block(rr, ss, GLA_SUB), scores, 0.0)
        for blk, ql, kl in levels:
            klp = kl[:, pl_]
            kstack = jnp.concatenate([klp, klp], axis=0)
            kstack = jnp.where(head_sel, kstack, jnp.zeros_like(kstack))
            s_l = _dot_nt(ql[:, pl_], kstack)
            scores = scores + jnp.where(_same_block(rr, ss, blk), s_l, 0.0)
        vp = v[:, p * 2 * GLA_DV:(p + 1) * 2 * GLA_DV]
        v_lo = _lane_lt(vp.shape, GLA_DV)
        vbd = jnp.concatenate([jnp.where(v_lo, vp, 0.0), jnp.where(v_lo, 0.0, vp)], axis=0).astype(BF16)
        st = st_ref[p]
        o_p = (jnp.dot(scores.astype(BF16), vbd, preferred_element_type=F32)
               + _dot_nt(qs[:, pl_], st.astype(BF16)))
        outs.append(o_p)
        upd = jnp.dot(vp.T.astype(BF16), kd[:, pl_], preferred_element_type=F32)
        st_rows_lo = lax.broadcasted_iota(jnp.int32, upd.shape, 0) < GLA_DV
        bd = st_rows_lo == _lane_lt(upd.shape, GLA_DK)
        st_ref[p] = st * jnp.exp(b_tot[:, pl_]) + jnp.where(bd, upd, 0.0)
    return jnp.concatenate(outs, axis=1)


def _gla_kernel(*refs, seq, has_init, want_state):
    q_ref, k_ref, v_ref, ldf_ref, ldb_ref = refs[:5]
    pos = 5
    if has_init:
        s0f_ref, s0b_ref = refs[pos:pos + 2]
        pos += 2
    o_ref = refs[pos]
    pos += 1
    if want_state:
        sf_ref, sb_ref = refs[pos:pos + 2]
        pos += 2
    st_ref, sel_ref = refs[pos:pos + 2]
    t = GLA_TILE
    nt = seq // t
    n_pair = GLA_HEADS // 2

    kk = lax.broadcasted_iota(jnp.int32, (GLA_SUB * LANES, 2 * t), 0)
    nn = lax.broadcasted_iota(jnp.int32, (GLA_SUB * LANES, 2 * t), 1)
    sel = ((nn & (GLA_SUB - 1)) == (kk >> LANES_LOG2)) & (((kk & (LANES - 1)) >= GLA_DK) == (nn >= t))
    sel_ref[...] = jnp.where(sel, 1.0, 0.0).astype(BF16)

    def load_state(s0_ref):
        for p in range(n_pair):
            if s0_ref is None:
                st_ref[p] = jnp.zeros(st_ref.shape[1:], F32)
            else:
                x = jnp.concatenate([s0_ref[0, 2 * p], s0_ref[0, 2 * p + 1]], axis=0)
                xt = x.T
                lo = _lane_lt(xt.shape, GLA_DK)
                st_ref[p] = jnp.concatenate([jnp.where(lo, xt, 0.0), jnp.where(lo, 0.0, xt)], axis=0)

    def store_state(out_ref):
        for p in range(n_pair):
            st = st_ref[p]
            lo = _lane_lt((GLA_DV, LANES), GLA_DK)
            z = jnp.where(lo, st[0:GLA_DV, :], st[GLA_DV:2 * GLA_DV, :]).T
            out_ref[0, 2 * p] = z[0:GLA_DK, :]
            out_ref[0, 2 * p + 1] = z[GLA_DK:2 * GLA_DK, :]

    for rev in (False, True):
        ld_ref = ldb_ref if rev else ldf_ref
        load_state((s0b_ref if rev else s0f_ref) if has_init else None)

        def tile_body(ti, carry, rev=rev, ld_ref=ld_ref):
            tile = (nt - 1 - ti) if rev else ti
            r0 = pl.multiple_of(tile * t, t)
            rows = pl.ds(r0, t)
            o = _gla_tile(q_ref[rows, :], k_ref[rows, :], v_ref[rows, :].astype(F32), ld_ref[rows, :],
                          st_ref, sel_ref, rev=rev)
            if rev:
                o_ref[rows, :] = o_ref[rows, :] + o
            else:
                o_ref[rows, :] = o
            return carry

        lax.fori_loop(0, nt, tile_body, 0)
        if want_state:
            store_state(sb_ref if rev else sf_ref)


def _gla(qb, kb, vb, ldf, ldb, s0f, s0b, *, n_seq, seq, row_block0, want_state):
    has_init = s0f is not None
    kern = functools.partial(_gla_kernel, seq=seq, has_init=has_init, want_state=want_state)
    tok = lambda w: pl.BlockSpec((seq, w), lambda b: (row_block0 + b, 0))
    in_specs = [tok(B_QK), tok(B_QK), tok(B_V), tok(B_QK), tok(B_QK)]
    args = [qb, kb, vb, ldf, ldb]
    state_spec = pl.BlockSpec((1, GLA_HEADS, GLA_DK, GLA_DV), lambda b: (b, 0, 0, 0))
    if has_init:
        in_specs += [state_spec, state_spec]
        args += [s0f, s0b]
    out_specs = [pl.BlockSpec((seq, B_V), lambda b: (b, 0))]
    out_shape = [jax.ShapeDtypeStruct((n_seq * seq, B_V), F32)]
    if want_state:
        out_specs += [state_spec, state_spec]
        out_shape += [jax.ShapeDtypeStruct((n_seq, GLA_HEADS, GLA_DK, GLA_DV), F32)] * 2
    return pl.pallas_call(
        kern,
        grid=(n_seq,),
        in_specs=in_specs,
        out_specs=out_specs,
        out_shape=out_shape,
        scratch_shapes=[
            pltpu.VMEM((GLA_HEADS // 2, 2 * GLA_DV, LANES), F32),
            pltpu.VMEM((GLA_SUB * LANES, 2 * GLA_TILE), BF16),
        ],
        compiler_params=pltpu.CompilerParams(
            dimension_semantics=("parallel",), vmem_limit_bytes=VMEM_LIMIT_BYTES),
        name="gla_state" if want_state else "gla_latent",
    )(*args)


def _evout_kernel(x_ref, mod_ref, gpost_ref, attn_p_ref, attn_s_ref, gla_p_ref, gla_s_ref, gb_ref, gn_ref,
                  w_ref, o_ref, w_bf, *, np_tiles):
    @pl.when(pl.program_id(0) == 0)
    def _cast_weights():
        w_bf[...] = w_ref[0].astype(BF16)

    is_prompt = pl.program_id(0) < np_tiles
    attn = _pick((attn_p_ref, attn_s_ref), is_prompt)
    gla = _pick((gla_p_ref, gla_s_ref), is_prompt)
    y = jnp.dot(attn, w_bf[0:A_Q, :], preferred_element_type=F32)
    for hd in range(GLA_HEADS):
        sl = slice(hd * GLA_DV, (hd + 1) * GLA_DV)
        g = _rms(gla[:, sl], gn_ref[:, sl]) * _silu(gb_ref[:, sl].astype(F32))
        y = y + jnp.dot(g.astype(BF16), w_bf[A_Q + hd * GLA_DV:A_Q + (hd + 1) * GLA_DV, :],
                        preferred_element_type=F32)
    o_ref[...] = _sub_out(x_ref[...], y, mod_ref, gpost_ref, 1, 1.0)


def _even_out(x, mods_l, gpost_l, attn_ps, gla_ps, gb, gla_norm, w_out, e, *, group, np_tiles):
    n_tok, d = x.shape
    tm = TOKEN_TILE
    ev_out = w_out.shape[1]
    kern = functools.partial(_evout_kernel, np_tiles=np_tiles)
    return pl.pallas_call(
        kern,
        grid=(n_tok // tm,),
        in_specs=[
            pl.BlockSpec((tm, d), lambda i: (i, 0)),
            pl.BlockSpec((None, 3 * N_SUB, d), lambda i: (group(i), 0, 0)),
            pl.BlockSpec((N_SUB, d), lambda i: (0, 0)),
            *_tile_specs(tm, A_Q, np_tiles),
            *_tile_specs(tm, B_V, np_tiles),
            pl.BlockSpec((tm, B_V), lambda i: (i, 0)),
            pl.BlockSpec((1, B_V), lambda i: (0, 0)),
            pl.BlockSpec((1, ev_out, d), lambda i: (e, 0, 0), pipeline_mode=pl.Buffered(1)),
        ],
        out_specs=pl.BlockSpec((tm, d), lambda i: (i, 0)),
        out_shape=jax.ShapeDtypeStruct((n_tok, d), F32),
        scratch_shapes=[pltpu.VMEM((ev_out, d), BF16)],
        compiler_params=pltpu.CompilerParams(
            dimension_semantics=("arbitrary",), vmem_limit_bytes=VMEM_LIMIT_BYTES),
        name="even_out",
    )(x, mods_l, gpost_l, *attn_ps, *gla_ps, gb, gla_norm, w_out)


def _gelu(x):
    return 0.5 * x * (1.0 + lax.erf(x * (2.0 ** -0.5)))


def _cm_kernel(x_ref, mod_ref, gpre_ref, gpost_ref, win_ref, vg_ref, vb_ref, ws_ref, bs_ref, wout_ref,
               o_ref, win_bf, wout_bf):
    @pl.when(pl.program_id(0) == 0)
    def _cast_weights():
        win_bf[...] = win_ref[0].astype(BF16)
        wout_bf[...] = wout_ref[0].astype(BF16)

    x = x_ref[...]
    width = wout_bf.shape[0]
    gw = width // CMLP_GROUPS
    h = _sub_in(x, mod_ref, gpre_ref, 1).astype(BF16)
    u = _gelu(jnp.dot(h, win_bf[:, 0:width], preferred_element_type=F32))
    v = _gelu(jnp.dot(h, win_bf[:, width:2 * width], preferred_element_type=F32))
    mu = jnp.mean(v, axis=-1, keepdims=True)
    vc = v - mu
    var = jnp.mean(vc * vc, axis=-1, keepdims=True)
    vn = (vc * lax.rsqrt(var + EPS) * vg_ref[...] + vb_ref[...]).astype(BF16)
    ws = [ws_ref[0, g].astype(BF16) for g in range(CMLP_GROUPS)]
    rows = []
    for c in range(x.shape[0] // CHUNK):
        cols = []
        for g in range(CMLP_GROUPS):
            blk = vn[c * CHUNK:(c + 1) * CHUNK, g * gw:(g + 1) * gw]
            cols.append(jnp.dot(ws[g], blk, preferred_element_type=F32) + bs_ref[0, :, g:g + 1])
        rows.append(jnp.concatenate(cols, axis=1))
    mixed = jnp.concatenate(rows, axis=0)
    y = jnp.dot((u * mixed).astype(BF16), wout_bf[...], preferred_element_type=F32)
    o_ref[...] = _sub_out(x, y, mod_ref, gpost_ref, 1, 1.0)


def _chunk_mlp(x, mods_l, gpre_l, gpost_l, w_in, v_gain, v_bias, w_s, b_s_t, w_out, *, group):
    n_tok, d = x.shape
    tm = TOKEN_TILE
    width = w_out.shape[1]
    return pl.pallas_call(
        _cm_kernel,
        grid=(n_tok // tm,),
        in_specs=[
            pl.BlockSpec((tm, d), lambda i: (i, 0)),
            pl.BlockSpec((None, 3 * N_SUB, d), lambda i: (group(i), 0, 0)),
            pl.BlockSpec((N_SUB, d), lambda i: (0, 0)),
            pl.BlockSpec((N_SUB, d), lambda i: (0, 0)),
            pl.BlockSpec((1, d, 2 * width), lambda i: (0, 0, 0), pipeline_mode=pl.Buffered(1)),
            pl.BlockSpec((1, width), lambda i: (0, 0)),
            pl.BlockSpec((1, width), lambda i: (0, 0)),
            pl.BlockSpec((1, CMLP_GROUPS, CHUNK, CHUNK), lambda i: (0, 0, 0, 0)),
            pl.BlockSpec((1, CHUNK, CMLP_GROUPS), lambda i: (0, 0, 0)),
            pl.BlockSpec((1, width, d), lambda i: (0, 0, 0), pipeline_mode=pl.Buffered(1)),
        ],
        out_specs=pl.BlockSpec((tm, d), lambda i: (i, 0)),
        out_shape=jax.ShapeDtypeStruct((n_tok, d), F32),
        scratch_shapes=[pltpu.VMEM((d, 2 * width), BF16), pltpu.VMEM((width, d), BF16)],
        compiler_params=pltpu.CompilerParams(
            dimension_semantics=("arbitrary",), vmem_limit_bytes=VMEM_LIMIT_BYTES),
        name="chunk_mlp",
    )(x, mods_l, gpre_l, gpost_l, w_in, v_gain, v_bias, w_s, b_s_t, w_out)


def kernel(x_prompt, x_sample, cache_k, cache_v, state_gla_fwd, state_gla_bwd, c, c_ctx, w_mod, b_mod, norm_pre, norm_post, ffn_w_gate, ffn_w_up, ffn_w_down, ev_w_in, ev_w_out, ev_sink, gla_wa_f, gla_ba_f, gla_wa_b, gla_ba_b, gla_norm, cm_w_in, cm_v_gain, cm_v_bias, cm_w_s, cm_b_s, cm_w_out):
    batch, seq, d = x_prompt.shape
    dec_batch, dec_seq, _ = x_sample.shape
    depth = w_mod.shape[0]
    n_prompt_tok = batch * seq
    n_sample_tok = dec_batch * dec_seq
    assert n_prompt_tok % dec_seq == 0 and dec_seq % TOKEN_TILE == 0
    group = _group_index_map(TOKEN_TILE, n_prompt_tok, dec_seq)
    np_tiles = n_prompt_tok // TOKEN_TILE
    ffn = functools.partial(_ffn, group=group, np_tiles=np_tiles)

    xs = (x_prompt.reshape(n_prompt_tok, d), x_sample.reshape(n_sample_tok, d))
    ev_w_in_t = jnp.swapaxes(ev_w_in, 1, 2)

    n_cond = 1 + dec_batch
    cond_rows = -(-n_cond // SUBLANES) * SUBLANES
    cond = jnp.concatenate([c_ctx[None, :], c, jnp.zeros((cond_rows - n_cond, d), F32)], axis=0)
    mods = _adaln_mods(cond, w_mod, b_mod).reshape(depth, cond_rows, 3 * N_SUB, d)

    new_k, new_v, new_sf, new_sb = [], [], [], []
    for layer in range(depth):
        mods_l, gpre_l, gpost_l = mods[layer], norm_pre[layer], norm_post[layer]
        (x,) = ffn(xs, mods_l, gpre_l, gpost_l, ffn_w_gate, ffn_w_up, ffn_w_down,
                   sub=0, layer=layer, half=0, split_out=False)
        if layer % 2 == 0:
            e = layer // 2
            rank = gla_wa_f.shape[1]
            wa_bd = jnp.zeros((LANES, 2 * B_QK), F32)
            wa_bd = wa_bd.at[0:rank, 0:B_QK].set(gla_wa_f[e]).at[rank:2 * rank, B_QK:].set(gla_wa_b[e])
            ba_cat = jnp.concatenate([gla_ba_f[e], gla_ba_b[e]])[None, :]
            qa, ka, va, qb, kb, vb, gb, ldf, ldb = _even_in(
                x, mods_l, gpre_l, ev_w_in_t, e, wa_bd, ba_cat, group=group)
            sink = ev_sink[e]
            attn_p, k_t, v_t = _attn_context(sink, qa, ka, va, n_seq=batch, seq=seq)
            k_ctx = cache_k[:, e].reshape(dec_batch, -1, A_KV)
            v_ctx = cache_v[:, e].reshape(dec_batch, -1, A_KV)
            attn_s = _attn_latent(sink, qa, ka, va, k_ctx, v_ctx, n_seq=dec_batch, seq=dec_seq,
                                  row_block0=n_prompt_tok // dec_seq)
            gla_p, s_f, s_b = _gla(qb, kb, vb, ldf, ldb, None, None, n_seq=batch, seq=seq,
                                   row_block0=0, want_state=True)
            (gla_s,) = _gla(qb, kb, vb, ldf, ldb, state_gla_fwd[:, e], state_gla_bwd[:, e],
                            n_seq=dec_batch, seq=dec_seq, row_block0=n_prompt_tok // dec_seq,
                            want_state=False)
            x = _even_out(x, mods_l, gpost_l, (attn_p, attn_s), (gla_p, gla_s), gb, gla_norm[e][None, :],
                          ev_w_out, e, group=group, np_tiles=np_tiles)
            for cache_t, dst in ((k_t, new_k), (v_t, new_v)):
                dst.append(jnp.transpose(cache_t.reshape(batch, ATTN_KV_HEADS, HEAD_DIM, seq), (0, 3, 1, 2)))
            new_sf.append(s_f)
            new_sb.append(s_b)
        else:
            o = layer // 2
            x = _chunk_mlp(x, mods_l, gpre_l, gpost_l, cm_w_in[o:o + 1], cm_v_gain[o][None, :],
                           cm_v_bias[o][None, :], cm_w_s[o:o + 1],
                           jnp.swapaxes(cm_b_s[o], 0, 1)[None], cm_w_out[o:o + 1], group=group)
        xs = ffn((x,), mods_l, gpre_l, gpost_l, ffn_w_gate, ffn_w_up, ffn_w_down,
                 sub=2, layer=layer, half=1, split_out=(layer == depth - 1))

    y_prompt = xs[0].reshape(batch, seq, d)
    y_sample = xs[1].reshape(dec_batch, dec_seq, d)
    return (y_prompt, y_sample, jnp.stack(new_k, axis=1), jnp.stack(new_v, axis=1),
            jnp.stack(new_sf, axis=1), jnp.stack(new_sb, axis=1))
```

```python
import functools
import math

import jax
import jax.numpy as jnp
from jax import lax
from jax.experimental import pallas as pl
from jax.experimental.pallas import tpu as pltpu

F32 = jnp.float32
BF16 = jnp.bfloat16

EPS = 1e-6
NEG_INF = -1e30
FFN_RESIDUAL = 0.5
N_SUB = 3

ATTN_HEADS = 8
ATTN_KV_HEADS = 2
HEAD_DIM = 64
ATTN_BLOCK = 128
WINDOW = 128
GRID_W = 64
GRID_W_LOG2 = 6
ROPE_BASE = 10000.0
GLA_HEADS = 4
GLA_DK = 64
GLA_DV = 128
GLA_RANK = 16
GLA_TAU = 16.0
CHUNK = 128
CMLP_GROUPS = 4

A_Q = ATTN_HEADS * HEAD_DIM
A_KV = ATTN_KV_HEADS * HEAD_DIM
B_QK = GLA_HEADS * GLA_DK
B_V = GLA_HEADS * GLA_DV
EV_MAIN = A_Q + 2 * A_KV + 2 * B_QK + 2 * B_V
EV_OFFS = (0, A_Q, A_Q + A_KV, A_Q + 2 * A_KV, A_Q + 2 * A_KV + B_QK,
           A_Q + 2 * A_KV + 2 * B_QK, A_Q + 2 * A_KV + 2 * B_QK + B_V, EV_MAIN)

LANES = 128
LANES_LOG2 = 7
SUBLANES = 8
VMEM_LIMIT_BYTES = 56 * 1024 * 1024

TOKEN_TILE = 512
ADALN_COLS = 2304
FFN_CHUNK = 256
GLA_TILE = 128
GLA_SUB = 8


def _group_index_map(tile_rows, n_prompt_tok, seq_tok):
    def group(i):
        return jnp.maximum(i * tile_rows - (n_prompt_tok - seq_tok), 0) // seq_tok
    return group


def _rms(x, g):
    ms = jnp.mean(x * x, axis=-1, keepdims=True)
    return x * lax.rsqrt(ms + EPS) * g


def _sub_in(x, mod_ref, gpre_ref, sub):
    shift = mod_ref[3 * sub + 0:3 * sub + 1, :]
    scale = mod_ref[3 * sub + 1:3 * sub + 2, :]
    return _rms(x, gpre_ref[sub:sub + 1, :]) * (1.0 + scale) + shift


def _sub_out(x, y, mod_ref, gpost_ref, sub, coef):
    gate = mod_ref[3 * sub + 2:3 * sub + 3, :]
    return x + (coef * gate) * _rms(y, gpost_ref[sub:sub + 1, :])


def _silu(x):
    return x * jax.nn.sigmoid(x)


def _mod_kernel(cond_ref, w_ref, b_ref, o_ref):
    s = _silu(cond_ref[...]).astype(BF16)
    o_ref[0] = jnp.dot(s, w_ref[0].astype(BF16), preferred_element_type=F32) + b_ref[0]


def _adaln_mods(cond, w_mod, b_mod):
    depth, d, n = w_mod.shape
    tn = ADALN_COLS
    assert n % tn == 0
    rows = cond.shape[0]
    return pl.pallas_call(
        _mod_kernel,
        grid=(depth, n // tn),
        in_specs=[
            pl.BlockSpec((rows, d), lambda l, j: (0, 0)),
            pl.BlockSpec((1, d, tn), lambda l, j: (l, 0, j)),
            pl.BlockSpec((1, 1, tn), lambda l, j: (l, 0, j)),
        ],
        out_specs=pl.BlockSpec((1, rows, tn), lambda l, j: (l, 0, j)),
        out_shape=jax.ShapeDtypeStruct((depth, rows, n), F32),
        compiler_params=pltpu.CompilerParams(
            dimension_semantics=("parallel", "parallel"), vmem_limit_bytes=VMEM_LIMIT_BYTES),
        name="adaln_mods",
    )(cond, w_mod, b_mod.reshape(depth, 1, n))


def _tile_specs(tm, width, np_tiles):
    return [pl.BlockSpec((tm, width), lambda i: (jnp.minimum(i, np_tiles - 1), 0)),
            pl.BlockSpec((tm, width), lambda i: (jnp.maximum(i - np_tiles, 0), 0))]


def _pick(refs, is_prompt):
    if len(refs) == 1:
        return refs[0][...]
    return jnp.where(is_prompt, refs[0][...], refs[1][...])


def _ffn_kernel(*refs, sub, layer, half, n_chunks, n_x, n_o, np_tiles):
    x_refs = refs[:n_x]
    mod_ref, gpre_ref, gpost_ref, wg_hbm, wu_hbm, wd_hbm = refs[n_x:n_x + 6]
    o_refs = refs[n_x + 6:n_x + 6 + n_o]
    wg_bf, wu_bf, wd_bf, st_g, st_u, st_d, sem = refs[n_x + 6 + n_o:]
    fc = FFN_CHUNK
    step = pl.program_id(0)
    is_prompt = step < np_tiles

    def chunk_copies(c, slot):
        cols = pl.ds(c * fc, fc)
        return (
            pltpu.make_async_copy(wg_hbm.at[layer, half, :, cols], st_g.at[slot], sem.at[0, slot]),
            pltpu.make_async_copy(wu_hbm.at[layer, half, :, cols], st_u.at[slot], sem.at[1, slot]),
            pltpu.make_async_copy(wd_hbm.at[layer, half, cols, :], st_d.at[slot], sem.at[2, slot]),
        )

    def run(stage_weights):
        if stage_weights:
            for cp in chunk_copies(0, 0):
                cp.start()
        x = _pick(x_refs, is_prompt)
        h = _sub_in(x, mod_ref, gpre_ref, sub).astype(BF16)
        y = jnp.zeros(x.shape, F32)
        for c in range(n_chunks):
            sl = slice(c * fc, (c + 1) * fc)
            if stage_weights:
                slot = c % 2
                if c + 1 < n_chunks:
                    for cp in chunk_copies(c + 1, 1 - slot):
                        cp.start()
                for cp in chunk_copies(c, slot):
                    cp.wait()
                wg_bf[:, sl] = st_g[slot].astype(BF16)
                wu_bf[:, sl] = st_u[slot].astype(BF16)
                wd_bf[sl, :] = st_d[slot].astype(BF16)
            g = jnp.dot(h, wg_bf[:, sl], preferred_element_type=F32)
            u = jnp.dot(h, wu_bf[:, sl], preferred_element_type=F32)
            a = (_silu(g) * u).astype(BF16)
            y = y + jnp.dot(a, wd_bf[sl, :], preferred_element_type=F32)
        out = _sub_out(x, y, mod_ref, gpost_ref, sub, FFN_RESIDUAL)
        if n_o == 1:
            o_refs[0][...] = out
        else:
            @pl.when(is_prompt)
            def _store_prompt():
                o_refs[0][...] = out

            @pl.when(jnp.logical_not(is_prompt))
            def _store_latent():
                o_refs[1][...] = out

    @pl.when(step == 0)
    def _first_tile():
        run(True)

    @pl.when(step > 0)
    def _other_tiles():
        run(False)


def _ffn(xs, mods_l, gpre_l, gpost_l, wg, wu, wd, *, sub, layer, half, group, np_tiles, split_out):
    d = xs[0].shape[-1]
    n_tok = sum(x.shape[0] for x in xs)
    d_ff = wg.shape[-1]
    tm = TOKEN_TILE
    fc = FFN_CHUNK
    n_chunks = d_ff // fc
    assert n_chunks * fc == d_ff and n_tok % tm == 0
    n_x, n_o = len(xs), (2 if split_out else 1)
    kern = functools.partial(_ffn_kernel, sub=sub, layer=layer, half=half, n_chunks=n_chunks,
                             n_x=n_x, n_o=n_o, np_tiles=np_tiles)
    whole = pl.BlockSpec((tm, d), lambda i: (i, 0))
    x_specs = _tile_specs(tm, d, np_tiles) if n_x == 2 else [whole]
    n_p = np_tiles * tm
    if split_out:
        out_specs = _tile_specs(tm, d, np_tiles)
        out_shape = [jax.ShapeDtypeStruct((n_p, d), F32), jax.ShapeDtypeStruct((n_tok - n_p, d), F32)]
    else:
        out_specs = [whole]
        out_shape = [jax.ShapeDtypeStruct((n_tok, d), F32)]
    return pl.pallas_call(
        kern,
        grid=(n_tok // tm,),
        in_specs=x_specs + [
            pl.BlockSpec((None, 3 * N_SUB, d), lambda i: (group(i), 0, 0)),
            pl.BlockSpec((N_SUB, d), lambda i: (0, 0)),
            pl.BlockSpec((N_SUB, d), lambda i: (0, 0)),
            pl.BlockSpec(memory_space=pl.ANY),
            pl.BlockSpec(memory_space=pl.ANY),
            pl.BlockSpec(memory_space=pl.ANY),
        ],
        out_specs=out_specs,
        out_shape=out_shape,
        scratch_shapes=[
            pltpu.VMEM((d, d_ff), BF16),
            pltpu.VMEM((d, d_ff), BF16),
            pltpu.VMEM((d_ff, d), BF16),
            pltpu.VMEM((2, d, fc), F32),
            pltpu.VMEM((2, d, fc), F32),
            pltpu.VMEM((2, fc, d), F32),
            pltpu.SemaphoreType.DMA((3, 2)),
        ],
        compiler_params=pltpu.CompilerParams(
            dimension_semantics=("arbitrary",), vmem_limit_bytes=VMEM_LIMIT_BYTES),
        name=f"ffn_l{layer}h{half}",
    )(*xs, mods_l, gpre_l, gpost_l, wg, wu, wd)


def _log_sigmoid(x):
    return jnp.minimum(x, 0.0) - jnp.log1p(jnp.exp(-jnp.abs(x)))


def _evin_kernel(x_ref, mod_ref, gpre_ref, w_ref, wa_ref, ba_ref,
                 qa_ref, ka_ref, va_ref, qb_ref, kb_ref, vb_ref, gb_ref, ldf_ref, ldb_ref,
                 w_bf):
    @pl.when(pl.program_id(0) == 0)
    def _cast_weights():
        w_bf[0:EV_MAIN, :] = w_ref[0, 0:EV_MAIN, :].astype(BF16)
        w_bf[EV_MAIN:EV_MAIN + LANES, :] = jnp.zeros((LANES, w_bf.shape[1]), BF16)
        w_bf[EV_MAIN:EV_MAIN + 2 * GLA_RANK, :] = w_ref[0, EV_MAIN:EV_MAIN + 2 * GLA_RANK, :].astype(BF16)

    h = _sub_in(x_ref[...], mod_ref, gpre_ref, 1).astype(BF16)
    full = _dot_nt(h, w_bf[...])

    def proj(lo, hi):
        return full[:, lo:hi]

    o = EV_OFFS
    qa_ref[...] = (proj(o[0], o[1]) * (HEAD_DIM ** -0.5)).astype(qa_ref.dtype)
    ka_ref[...] = proj(o[1], o[2])
    va_ref[...] = proj(o[2], o[3])
    qb_ref[...] = proj(o[3], o[4]) * (GLA_DK ** -0.5)
    kb_ref[...] = proj(o[4], o[5])
    vb_ref[...] = proj(o[5], o[6]).astype(vb_ref.dtype)
    gb_ref[...] = proj(o[6], o[7]).astype(gb_ref.dtype)
    lr = proj(EV_MAIN, EV_MAIN + LANES).astype(BF16)
    logits = jnp.dot(lr, wa_ref[...].astype(BF16), preferred_element_type=F32) + ba_ref[...]
    ld = _log_sigmoid(logits) * (1.0 / GLA_TAU)
    ldf_ref[...] = ld[:, 0:B_QK]
    ldb_ref[...] = ld[:, B_QK:2 * B_QK]


def _even_in(x, mods_l, gpre_l, w_in_t, e, wa_bd, ba_cat, *, group):
    n_tok, d = x.shape
    tm = TOKEN_TILE
    ev_in = w_in_t.shape[1]
    widths = (A_Q, A_KV, A_KV, B_QK, B_QK, B_V, B_V, B_QK, B_QK)
    dtypes = (BF16, F32, F32, F32, F32, BF16, BF16, F32, F32)
    return pl.pallas_call(
        _evin_kernel,
        grid=(n_tok // tm,),
        in_specs=[
            pl.BlockSpec((tm, d), lambda i: (i, 0)),
            pl.BlockSpec((None, 3 * N_SUB, d), lambda i: (group(i), 0, 0)),
            pl.BlockSpec((N_SUB, d), lambda i: (0, 0)),
            pl.BlockSpec((1, ev_in, d), lambda i: (e, 0, 0), pipeline_mode=pl.Buffered(1)),
            pl.BlockSpec(wa_bd.shape, lambda i: (0, 0)),
            pl.BlockSpec(ba_cat.shape, lambda i: (0, 0)),
        ],
        out_specs=[pl.BlockSpec((tm, w), lambda i: (i, 0)) for w in widths],
        out_shape=[jax.ShapeDtypeStruct((n_tok, w), t) for w, t in zip(widths, dtypes)],
        scratch_shapes=[pltpu.VMEM((EV_MAIN + LANES, d), BF16)],
        compiler_params=pltpu.CompilerParams(
            dimension_semantics=("arbitrary",), vmem_limit_bytes=VMEM_LIMIT_BYTES),
        name="even_in",
    )(x, mods_l, gpre_l, w_in_t, wa_bd, ba_cat)


def _lane_lt(shape, n):
    return lax.broadcasted_iota(jnp.int32, shape, len(shape) - 1) < n


def _dup_kv_head(x, g):
    sw = pltpu.roll(x, HEAD_DIM, 1)
    lo = _lane_lt(x.shape, HEAD_DIM)
    return jnp.where(lo, x, sw) if g == 0 else jnp.where(lo, sw, x)


def _dot_nt(a, b):
    return lax.dot_general(a, b, (((1,), (1,)), ((), ())), preferred_element_type=F32)


ATTN_GROUP = ATTN_HEADS // ATTN_KV_HEADS


def _sink_column(sink_ref, g, t):
    rows = lax.broadcasted_iota(jnp.int32, (ATTN_GROUP * t, 1), 0)
    col = jnp.full((ATTN_GROUP * t, 1), sink_ref[ATTN_GROUP * g], F32)
    for j in range(1, ATTN_GROUP):
        col = jnp.where(rows >= j * t, sink_ref[ATTN_GROUP * g + j], col)
    return col


def _attn_group(q_pairs, sink_col, keys):
    t = q_pairs[0].shape[0]
    lo = _lane_lt((t, LANES), HEAD_DIM)
    zero = jnp.zeros((t, LANES), BF16)
    q4 = jnp.concatenate([jnp.where(lo, q_pairs[0], zero), jnp.where(lo, zero, q_pairs[0]),
                          jnp.where(lo, q_pairs[1], zero), jnp.where(lo, zero, q_pairs[1])], axis=0)
    scores = []
    mx = sink_col
    for k2, _, mask in keys:
        s = _dot_nt(q4, k2)
        if mask is not None:
            s = jnp.where(mask, s, NEG_INF)
        scores.append(s)
        mx = jnp.maximum(mx, jnp.max(s, axis=-1, keepdims=True))
    den = jnp.exp(sink_col - mx)
    o = None
    for s, (_, v2, _) in zip(scores, keys):
        p = jnp.exp(s - mx)
        den = den + jnp.sum(p, axis=-1, keepdims=True)
        pv = jnp.dot(p.astype(BF16), v2, preferred_element_type=F32)
        o = pv if o is None else o + pv
    o = o / den
    return (jnp.where(lo, o[0:t], o[t:2 * t]), jnp.where(lo, o[2 * t:3 * t], o[3 * t:4 * t]))


def _attn_ctx_kernel(sink_ref, q_ref, k_ref, v_ref, o_ref, kt_ref, vt_ref):
    k = k_ref[...]
    v = v_ref[...]
    t = k.shape[0]
    kt_ref[0] = k.T
    vt_ref[0] = v.T
    for g in range(ATTN_KV_HEADS):
        keys = [(_dup_kv_head(k, g).astype(BF16), _dup_kv_head(v, g).astype(BF16), None)]
        cols = (2 * g, 2 * g + 1)
        outs = _attn_group([q_ref[:, m * LANES:(m + 1) * LANES] for m in cols],
                           _sink_column(sink_ref, g, t), keys)
        for m, o in zip(cols, outs):
            o_ref[:, m * LANES:(m + 1) * LANES] = o.astype(o_ref.dtype)


def _attn_context(sink, qa, ka, va, *, n_seq, seq):
    cache_spec = pl.BlockSpec((1, A_KV, seq), lambda b: (b, 0, 0))
    return pl.pallas_call(
        _attn_ctx_kernel,
        grid=(n_seq,),
        in_specs=[
            pl.BlockSpec(memory_space=pltpu.SMEM),
            pl.BlockSpec((seq, A_Q), lambda b: (b, 0)),
            pl.BlockSpec((seq, A_KV), lambda b: (b, 0)),
            pl.BlockSpec((seq, A_KV), lambda b: (b, 0)),
        ],
        out_specs=[pl.BlockSpec((seq, A_Q), lambda b: (b, 0)), cache_spec, cache_spec],
        out_shape=[jax.ShapeDtypeStruct((n_seq * seq, A_Q), BF16),
                   jax.ShapeDtypeStruct((n_seq, A_KV, seq), F32),
                   jax.ShapeDtypeStruct((n_seq, A_KV, seq), F32)],
        compiler_params=pltpu.CompilerParams(dimension_semantics=("parallel",)),
        name="attn_context",
    )(sink, qa, ka, va)


def _rope_tables(n_tok):
    n_rows = n_tok // GRID_W
    assert n_rows <= GRID_W
    shape = (GRID_W, LANES)
    pos = lax.broadcasted_iota(jnp.int32, shape, 0).astype(F32)
    lane = lax.broadcasted_iota(jnp.int32, shape, 1)
    half = HEAD_DIM // 2
    nf = half // 2
    within = lane & (HEAD_DIM - 1)
    is_col = within >= half
    second = (within & (half - 1)) >= nf
    f = (within & (nf - 1)).astype(F32)
    inv = jnp.exp(f * (-2.0 / half * math.log(ROPE_BASE)))
    ang = pos * inv
    cos_t = jnp.cos(ang)
    sin_t = jnp.sin(ang)
    sin_t = jnp.where(second, sin_t, -sin_t)

    def per_token(tab):
        return jnp.concatenate(
            [jnp.where(is_col, tab, jnp.broadcast_to(tab[r:r + 1, :], shape)) for r in range(n_rows)], axis=0)

    return per_token(cos_t), per_token(sin_t)


def _rope(x, cos, sin_signed):
    nf = HEAD_DIM // 4
    lane = lax.broadcasted_iota(jnp.int32, x.shape, 1)
    second = (lane & (2 * nf - 1)) >= nf
    up = pltpu.roll(x, nf, 1)
    dn = pltpu.roll(x, LANES - nf, 1)
    return x * cos + jnp.where(second, up, dn) * sin_signed


def _attn_lat_kernel(sink_ref, q_ref, k_ref, v_ref, kc_ref, vc_ref, o_ref, qr_ref, kp_ref, vp_ref,
                     *, seq):
    nb = seq // ATTN_BLOCK
    blk = ATTN_BLOCK
    cos, sin_signed = _rope_tables(seq)
    for m in range(A_Q // LANES):
        qr_ref[:, m * LANES:(m + 1) * LANES] = _rope(
            q_ref[:, m * LANES:(m + 1) * LANES].astype(F32), cos, sin_signed).astype(BF16)
    kr = _rope(k_ref[...], cos, sin_signed)
    v = v_ref[...]
    zpad = jnp.zeros((blk, LANES), BF16)
    ctx, sink_cols = [], []
    for g in range(ATTN_KV_HEADS):
        kp_ref[g, 0:blk, :] = zpad
        kp_ref[g, blk + seq:2 * blk + seq, :] = zpad
        vp_ref[g, 0:blk, :] = zpad
        vp_ref[g, blk + seq:2 * blk + seq, :] = zpad
        kp_ref[g, blk:blk + seq, :] = _dup_kv_head(kr, g).astype(BF16)
        vp_ref[g, blk:blk + seq, :] = _dup_kv_head(v, g).astype(BF16)
        ctx.append((_dup_kv_head(kc_ref[0], g).astype(BF16), _dup_kv_head(vc_ref[0], g).astype(BF16), None))
        sink_cols.append(_sink_column(sink_ref, g, blk))
    r = lax.broadcasted_iota(jnp.int32, (ATTN_GROUP * blk, 3 * blk), 0) & (blk - 1)
    c = lax.broadcasted_iota(jnp.int32, (ATTN_GROUP * blk, 3 * blk), 1)
    band = (c >= r) & (c <= r + 2 * WINDOW)

    def block_body(i, carry):
        row0 = pl.multiple_of(i * blk, blk)
        kj = (i - 1) * blk + c
        mask = band & (kj >= 0) & (kj < seq)
        for g in range(ATTN_KV_HEADS):
            cols = (2 * g, 2 * g + 1)
            window = (kp_ref[g, pl.ds(row0, 3 * blk), :], vp_ref[g, pl.ds(row0, 3 * blk), :], mask)
            outs = _attn_group([qr_ref[pl.ds(row0, blk), m * LANES:(m + 1) * LANES] for m in cols],
                               sink_cols[g], [window, ctx[g]])
            for m, o in zip(cols, outs):
                o_ref[pl.ds(row0, blk), m * LANES:(m + 1) * LANES] = o.astype(o_ref.dtype)
        return carry

    lax.fori_loop(0, nb, block_body, 0)


def _attn_latent(sink, qa, ka, va, k_ctx, v_ctx, *, n_seq, seq, row_block0):
    past = k_ctx.shape[1]
    kern = functools.partial(_attn_lat_kernel, seq=seq)
    return pl.pallas_call(
        kern,
        grid=(n_seq,),
        in_specs=[
            pl.BlockSpec(memory_space=pltpu.SMEM),
            pl.BlockSpec((seq, A_Q), lambda b: (row_block0 + b, 0)),
            pl.BlockSpec((seq, A_KV), lambda b: (row_block0 + b, 0)),
            pl.BlockSpec((seq, A_KV), lambda b: (row_block0 + b, 0)),
            pl.BlockSpec((1, past, A_KV), lambda b: (b, 0, 0)),
            pl.BlockSpec((1, past, A_KV), lambda b: (b, 0, 0)),
        ],
        out_specs=pl.BlockSpec((seq, A_Q), lambda b: (b, 0)),
        out_shape=jax.ShapeDtypeStruct((n_seq * seq, A_Q), BF16),
        scratch_shapes=[
            pltpu.VMEM((seq, A_Q), BF16),
            pltpu.VMEM((ATTN_KV_HEADS, seq + 2 * ATTN_BLOCK, LANES), BF16),
            pltpu.VMEM((ATTN_KV_HEADS, seq + 2 * ATTN_BLOCK, LANES), BF16),
        ],
        compiler_params=pltpu.CompilerParams(dimension_semantics=("parallel",)),
        name="attn_latent",
    )(sink, qa, ka, va, k_ctx, v_ctx)


def _same_block(a, b, size):
    return (a & -size) == (b & -size)


def _split3(x):
    hi = x.astype(BF16)
    r1 = x - hi.astype(F32)
    mid = r1.astype(BF16)
    lo = (r1 - mid.astype(F32)).astype(BF16)
    return hi, mid, lo


def _gla_tile(q, k, v, ld, st_ref, sel_ref, *, rev):
    t = GLA_TILE
    n_pair = GLA_HEADS // 2
    row = lax.broadcasted_iota(jnp.int32, (t, t), 0)
    col = lax.broadcasted_iota(jnp.int32, (t, t), 1)
    tri = (row <= col) if rev else (row >= col)
    tri = jnp.where(tri, 1.0, 0.0).astype(BF16)
    hi, mid, lo = _split3(ld * math.log2(math.e))
    b = (jnp.dot(tri, hi, preferred_element_type=F32) + jnp.dot(tri, mid, preferred_element_type=F32)
         + jnp.dot(tri, lo, preferred_element_type=F32))
    b_tot = b[0:1, :] if rev else b[t - 1:t, :]
    qs = (q * jnp.exp2(b)).astype(BF16)
    kd = (k * jnp.exp2(b_tot - b)).astype(BF16)

    rowv = lax.broadcasted_iota(jnp.int32, (t, B_QK), 0)
    levels = []
    half = GLA_SUB
    while 2 * half <= t:
        blk = 2 * half
        pieces = []
        for bs in range(0, t, blk):
            r = bs + half if rev else bs + half - 1
            pieces.append(jnp.broadcast_to(b[r:r + 1, :], (blk, B_QK)))
        ref = jnp.concatenate(pieces, axis=0) if len(pieces) > 1 else pieces[0]
        second = (rowv & (blk - 1)) >= half
        is_q = ~second if rev else second
        ql = (q * jnp.exp2(jnp.where(is_q, b - ref, -jnp.inf))).astype(BF16)
        kl = (k * jnp.exp2(jnp.where(is_q, -jnp.inf, ref - b))).astype(BF16)
        levels.append((blk, ql, kl))
        half = blk

    rr = lax.broadcasted_iota(jnp.int32, (t, 2 * t), 0)
    ss = lax.broadcasted_iota(jnp.int32, (t, 2 * t), 1) & (t - 1)
    sub_rows = lax.broadcasted_iota(jnp.int32, (GLA_SUB, LANES), 0)
    lane_lo = _lane_lt((t, LANES), GLA_DK)
    lane_lo2 = _lane_lt((2 * t, LANES), GLA_DK)
    par_rows = lax.broadcasted_iota(jnp.int32, (2 * t, LANES), 0) < t
    head_sel = par_rows == lane_lo2

    outs = []
    for p in range(n_pair):
        pl_ = slice(p * LANES, (p + 1) * LANES)
        qp, kp, bp = q[:, pl_], k[:, pl_], b[:, pl_]
        rows = []
        for sb in range(t // GLA_SUB):
            r0 = sb * GLA_SUB
            q_sub = qp[r0:r0 + GLA_SUB, :]
            b_sub = bp[r0:r0 + GLA_SUB, :]
            cols = []
            for j in range(GLA_SUB):
                kj = kp[r0 + j:r0 + j + 1, :]
                bj = bp[r0 + j:r0 + j + 1, :]
                valid = (sub_rows <= j) if rev else (sub_rows >= j)
                e = jnp.exp2(jnp.where(valid, b_sub - bj, -jnp.inf))
                cols.append(q_sub * kj * e)
            rows.append(jnp.concatenate(cols, axis=1))
        e_p = jnp.concatenate(rows, axis=0).astype(BF16)
        scores = jnp.dot(e_p, sel_ref[...], preferred_element_type=F32)
        scores = jnp.where(_same_block(rr, ss, GLA_SUB), scores, 0.0)
        for blk, ql, kl in levels:
            klp = kl[:, pl_]
            kstack = jnp.concatenate([klp, klp], axis=0)
            kstack = jnp.where(head_sel, kstack, jnp.zeros_like(kstack))
            s_l = _dot_nt(ql[:, pl_], kstack)
            scores = scores + jnp.where(_same_block(rr, ss, blk), s_l, 0.0)
        vp = v[:, p * 2 * GLA_DV:(p + 1) * 2 * GLA_DV]
        v_lo = _lane_lt(vp.shape, GLA_DV)
        vbd = jnp.concatenate([jnp.where(v_lo, vp, 0.0), jnp.where(v_lo, 0.0, vp)], axis=0).astype(BF16)
        st = st_ref[p]
        o_p = (jnp.dot(scores.astype(BF16), vbd, preferred_element_type=F32)
               + _dot_nt(qs[:, pl_], st.astype(BF16)))
        outs.append(o_p)
        upd = jnp.dot(vp.T.astype(BF16), kd[:, pl_], preferred_element_type=F32)
        st_rows_lo = lax.broadcasted_iota(jnp.int32, upd.shape, 0) < GLA_DV
        bd = st_rows_lo == _lane_lt(upd.shape, GLA_DK)
        st_ref[p] = st * jnp.exp2(b_tot[:, pl_]) + jnp.where(bd, upd, 0.0)
    return jnp.concatenate(outs, axis=1)


def _gla_kernel(*refs, seq, has_init, want_state):
    q_ref, k_ref, v_ref, ldf_ref, ldb_ref = refs[:5]
    pos = 5
    if has_init:
        s0f_ref, s0b_ref = refs[pos:pos + 2]
        pos += 2
    o_ref = refs[pos]
    pos += 1
    if want_state:
        sf_ref, sb_ref = refs[pos:pos + 2]
        pos += 2
    st_ref, sel_ref = refs[pos:pos + 2]
    t = GLA_TILE
    nt = seq // t
    n_pair = GLA_HEADS // 2

    kk = lax.broadcasted_iota(jnp.int32, (GLA_SUB * LANES, 2 * t), 0)
    nn = lax.broadcasted_iota(jnp.int32, (GLA_SUB * LANES, 2 * t), 1)
    sel = ((nn & (GLA_SUB - 1)) == (kk >> LANES_LOG2)) & (((kk & (LANES - 1)) >= GLA_DK) == (nn >= t))
    sel_ref[...] = jnp.where(sel, 1.0, 0.0).astype(BF16)

    def load_state(st, s0_ref):
        for p in range(n_pair):
            if s0_ref is None:
                st[p] = jnp.zeros(st.shape[1:], F32)
            else:
                x = jnp.concatenate([s0_ref[0, 2 * p], s0_ref[0, 2 * p + 1]], axis=0)
                xt = x.T
                lo = _lane_lt(xt.shape, GLA_DK)
                st[p] = jnp.concatenate([jnp.where(lo, xt, 0.0), jnp.where(lo, 0.0, xt)], axis=0)

    def store_state(st, out_ref):
        for p in range(n_pair):
            s = st[p]
            lo = _lane_lt((GLA_DV, LANES), GLA_DK)
            z = jnp.where(lo, s[0:GLA_DV, :], s[GLA_DV:2 * GLA_DV, :]).T
            out_ref[0, 2 * p] = z[0:GLA_DK, :]
            out_ref[0, 2 * p + 1] = z[GLA_DK:2 * GLA_DK, :]

    dirs = ((False, ldf_ref, st_ref.at[0]), (True, ldb_ref, st_ref.at[1]))
    for rev, _, st in dirs:
        load_state(st, (s0b_ref if rev else s0f_ref) if has_init else None)
    o_ref[...] = jnp.zeros(o_ref.shape, o_ref.dtype)

    def tile_body(ti, carry):
        for rev, ld_ref, st in dirs:
            tile = (nt - 1 - ti) if rev else ti
            r0 = pl.multiple_of(tile * t, t)
            rows = pl.ds(r0, t)
            o = _gla_tile(q_ref[rows, :], k_ref[rows, :], v_ref[rows, :].astype(F32), ld_ref[rows, :],
                          st, sel_ref, rev=rev)
            o_ref[rows, :] = o_ref[rows, :] + o
        return carry

    lax.fori_loop(0, nt, tile_body, 0)
    if want_state:
        for rev, _, st in dirs:
            store_state(st, sb_ref if rev else sf_ref)


def _gla(qb, kb, vb, ldf, ldb, s0f, s0b, *, n_seq, seq, row_block0, want_state):
    has_init = s0f is not None
    kern = functools.partial(_gla_kernel, seq=seq, has_init=has_init, want_state=want_state)
    tok = lambda w: pl.BlockSpec((seq, w), lambda b: (row_block0 + b, 0))
    in_specs = [tok(B_QK), tok(B_QK), tok(B_V), tok(B_QK), tok(B_QK)]
    args = [qb, kb, vb, ldf, ldb]
    state_spec = pl.BlockSpec((1, GLA_HEADS, GLA_DK, GLA_DV), lambda b: (b, 0, 0, 0))
    if has_init:
        in_specs += [state_spec, state_spec]
        args += [s0f, s0b]
    out_specs = [pl.BlockSpec((seq, B_V), lambda b: (b, 0))]
    out_shape = [jax.ShapeDtypeStruct((n_seq * seq, B_V), F32)]
    if want_state:
        out_specs += [state_spec, state_spec]
        out_shape += [jax.ShapeDtypeStruct((n_seq, GLA_HEADS, GLA_DK, GLA_DV), F32)] * 2
    return pl.pallas_call(
        kern,
        grid=(n_seq,),
        in_specs=in_specs,
        out_specs=out_specs,
        out_shape=out_shape,
        scratch_shapes=[
            pltpu.VMEM((2, GLA_HEADS // 2, 2 * GLA_DV, LANES), F32),
            pltpu.VMEM((GLA_SUB * LANES, 2 * GLA_TILE), BF16),
        ],
        compiler_params=pltpu.CompilerParams(
            dimension_semantics=("parallel",), vmem_limit_bytes=VMEM_LIMIT_BYTES),
        name="gla_state" if want_state else "gla_latent",
    )(*args)


def _evout_kernel(x_ref, mod_ref, gpost_ref, attn_p_ref, attn_s_ref, gla_p_ref, gla_s_ref, gb_ref, gn_ref,
                  w_ref, o_ref, w_bf, *, np_tiles):
    @pl.when(pl.program_id(0) == 0)
    def _cast_weights():
        w_bf[...] = w_ref[0].astype(BF16)

    is_prompt = pl.program_id(0) < np_tiles
    attn = _pick((attn_p_ref, attn_s_ref), is_prompt)
    gla = _pick((gla_p_ref, gla_s_ref), is_prompt)
    parts = [attn]
    for hd in range(GLA_HEADS):
        sl = slice(hd * GLA_DV, (hd + 1) * GLA_DV)
        g = _rms(gla[:, sl], gn_ref[:, sl]) * _silu(gb_ref[:, sl].astype(F32))
        parts.append(g.astype(BF16))
    y = jnp.dot(jnp.concatenate(parts, axis=1), w_bf[...], preferred_element_type=F32)
    o_ref[...] = _sub_out(x_ref[...], y, mod_ref, gpost_ref, 1, 1.0)


def _even_out(x, mods_l, gpost_l, attn_ps, gla_ps, gb, gla_norm, w_out, e, *, group, np_tiles):
    n_tok, d = x.shape
    tm = TOKEN_TILE
    ev_out = w_out.shape[1]
    kern = functools.partial(_evout_kernel, np_tiles=np_tiles)
    return pl.pallas_call(
        kern,
        grid=(n_tok // tm,),
        in_specs=[
            pl.BlockSpec((tm, d), lambda i: (i, 0)),
            pl.BlockSpec((None, 3 * N_SUB, d), lambda i: (group(i), 0, 0)),
            pl.BlockSpec((N_SUB, d), lambda i: (0, 0)),
            *_tile_specs(tm, A_Q, np_tiles),
            *_tile_specs(tm, B_V, np_tiles),
            pl.BlockSpec((tm, B_V), lambda i: (i, 0)),
            pl.BlockSpec((1, B_V), lambda i: (0, 0)),
            pl.BlockSpec((1, ev_out, d), lambda i: (e, 0, 0), pipeline_mode=pl.Buffered(1)),
        ],
        out_specs=pl.BlockSpec((tm, d), lambda i: (i, 0)),
        out_shape=jax.ShapeDtypeStruct((n_tok, d), F32),
        scratch_shapes=[pltpu.VMEM((ev_out, d), BF16)],
        compiler_params=pltpu.CompilerParams(
            dimension_semantics=("arbitrary",), vmem_limit_bytes=VMEM_LIMIT_BYTES),
        name="even_out",
    )(x, mods_l, gpost_l, *attn_ps, *gla_ps, gb, gla_norm, w_out)


def _gelu(x):
    return 0.5 * x * (1.0 + lax.erf(x * (2.0 ** -0.5)))


def _cm_kernel(x_ref, mod_ref, gpre_ref, gpost_ref, win_ref, vg_ref, vb_ref, ws_ref, bs_ref, wout_ref,
               o_ref, win_bf, wout_bf):
    @pl.when(pl.program_id(0) == 0)
    def _cast_weights():
        win_bf[...] = win_ref[0].astype(BF16)
        wout_bf[...] = wout_ref[0].astype(BF16)

    x = x_ref[...]
    width = wout_bf.shape[0]
    gw = width // CMLP_GROUPS
    h = _sub_in(x, mod_ref, gpre_ref, 1).astype(BF16)
    u = _gelu(jnp.dot(h, win_bf[:, 0:width], preferred_element_type=F32))
    v = _gelu(jnp.dot(h, win_bf[:, width:2 * width], preferred_element_type=F32))
    mu = jnp.mean(v, axis=-1, keepdims=True)
    vc = v - mu
    var = jnp.mean(vc * vc, axis=-1, keepdims=True)
    vn = (vc * lax.rsqrt(var + EPS) * vg_ref[...] + vb_ref[...]).astype(BF16)
    ws = [ws_ref[0, g].astype(BF16) for g in range(CMLP_GROUPS)]
    rows = []
    for c in range(x.shape[0] // CHUNK):
        cols = []
        for g in range(CMLP_GROUPS):
            blk = vn[c * CHUNK:(c + 1) * CHUNK, g * gw:(g + 1) * gw]
            cols.append(jnp.dot(ws[g], blk, preferred_element_type=F32) + bs_ref[0, :, g:g + 1])
        rows.append(jnp.concatenate(cols, axis=1))
    mixed = jnp.concatenate(rows, axis=0)
    y = jnp.dot((u * mixed).astype(BF16), wout_bf[...], preferred_element_type=F32)
    o_ref[...] = _sub_out(x, y, mod_ref, gpost_ref, 1, 1.0)


def _chunk_mlp(x, mods_l, gpre_l, gpost_l, w_in, v_gain, v_bias, w_s, b_s_t, w_out, *, group):
    n_tok, d = x.shape
    tm = TOKEN_TILE
    width = w_out.shape[1]
    return pl.pallas_call(
        _cm_kernel,
        grid=(n_tok // tm,),
        in_specs=[
            pl.BlockSpec((tm, d), lambda i: (i, 0)),
            pl.BlockSpec((None, 3 * N_SUB, d), lambda i: (group(i), 0, 0)),
            pl.BlockSpec((N_SUB, d), lambda i: (0, 0)),
            pl.BlockSpec((N_SUB, d), lambda i: (0, 0)),
            pl.BlockSpec((1, d, 2 * width), lambda i: (0, 0, 0), pipeline_mode=pl.Buffered(1)),
            pl.BlockSpec((1, width), lambda i: (0, 0)),
            pl.BlockSpec((1, width), lambda i: (0, 0)),
            pl.BlockSpec((1, CMLP_GROUPS, CHUNK, CHUNK), lambda i: (0, 0, 0, 0)),
            pl.BlockSpec((1, CHUNK, CMLP_GROUPS), lambda i: (0, 0, 0)),
            pl.BlockSpec((1, width, d), lambda i: (0, 0, 0), pipeline_mode=pl.Buffered(1)),
        ],
        out_specs=pl.BlockSpec((tm, d), lambda i: (i, 0)),
        out_shape=jax.ShapeDtypeStruct((n_tok, d), F32),
        scratch_shapes=[pltpu.VMEM((d, 2 * width), BF16), pltpu.VMEM((width, d), BF16)],
        compiler_params=pltpu.CompilerParams(
            dimension_semantics=("arbitrary",), vmem_limit_bytes=VMEM_LIMIT_BYTES),
        name="chunk_mlp",
    )(x, mods_l, gpre_l, gpost_l, w_in, v_gain, v_bias, w_s, b_s_t, w_out)


def kernel(x_prompt, x_sample, cache_k, cache_v, state_gla_fwd, state_gla_bwd, c, c_ctx, w_mod, b_mod, norm_pre, norm_post, ffn_w_gate, ffn_w_up, ffn_w_down, ev_w_in, ev_w_out, ev_sink, gla_wa_f, gla_ba_f, gla_wa_b, gla_ba_b, gla_norm, cm_w_in, cm_v_gain, cm_v_bias, cm_w_s, cm_b_s, cm_w_out):
    batch, seq, d = x_prompt.shape
    dec_batch, dec_seq, _ = x_sample.shape
    depth = w_mod.shape[0]
    n_prompt_tok = batch * seq
    n_sample_tok = dec_batch * dec_seq
    assert n_prompt_tok % dec_seq == 0 and dec_seq % TOKEN_TILE == 0
    group = _group_index_map(TOKEN_TILE, n_prompt_tok, dec_seq)
    np_tiles = n_prompt_tok // TOKEN_TILE
    ffn = functools.partial(_ffn, group=group, np_tiles=np_tiles)

    xs = (x_prompt.reshape(n_prompt_tok, d), x_sample.reshape(n_sample_tok, d))
    ev_w_in_t = jnp.swapaxes(ev_w_in, 1, 2)

    n_cond = 1 + dec_batch
    cond_rows = -(-n_cond // SUBLANES) * SUBLANES
    cond = jnp.concatenate([c_ctx[None, :], c, jnp.zeros((cond_rows - n_cond, d), F32)], axis=0)
    mods = _adaln_mods(cond, w_mod, b_mod).reshape(depth, cond_rows, 3 * N_SUB, d)

    new_k, new_v, new_sf, new_sb = [], [], [], []
    for layer in range(depth):
        mods_l, gpre_l, gpost_l = mods[layer], norm_pre[layer], norm_post[layer]
        (x,) = ffn(xs, mods_l, gpre_l, gpost_l, ffn_w_gate, ffn_w_up, ffn_w_down,
                   sub=0, layer=layer, half=0, split_out=False)
        if layer % 2 == 0:
            e = layer // 2
            rank = gla_wa_f.shape[1]
            wa_bd = jnp.zeros((LANES, 2 * B_QK), F32)
            wa_bd = wa_bd.at[0:rank, 0:B_QK].set(gla_wa_f[e]).at[rank:2 * rank, B_QK:].set(gla_wa_b[e])
            ba_cat = jnp.concatenate([gla_ba_f[e], gla_ba_b[e]])[None, :]
            qa, ka, va, qb, kb, vb, gb, ldf, ldb = _even_in(
                x, mods_l, gpre_l, ev_w_in_t, e, wa_bd, ba_cat, group=group)
            sink = ev_sink[e]
            attn_p, k_t, v_t = _attn_context(sink, qa, ka, va, n_seq=batch, seq=seq)
            k_ctx = cache_k[:, e].reshape(dec_batch, -1, A_KV)
            v_ctx = cache_v[:, e].reshape(dec_batch, -1, A_KV)
            attn_s = _attn_latent(sink, qa, ka, va, k_ctx, v_ctx, n_seq=dec_batch, seq=dec_seq,
                                  row_block0=n_prompt_tok // dec_seq)
            gla_p, s_f, s_b = _gla(qb, kb, vb, ldf, ldb, None, None, n_seq=batch, seq=seq,
                                   row_block0=0, want_state=True)
            (gla_s,) = _gla(qb, kb, vb, ldf, ldb, state_gla_fwd[:, e], state_gla_bwd[:, e],
                            n_seq=dec_batch, seq=dec_seq, row_block0=n_prompt_tok // dec_seq,
                            want_state=False)
            x = _even_out(x, mods_l, gpost_l, (attn_p, attn_s), (gla_p, gla_s), gb, gla_norm[e][None, :],
                          ev_w_out, e, group=group, np_tiles=np_tiles)
            for cache_t, dst in ((k_t, new_k), (v_t, new_v)):
                dst.append(jnp.transpose(cache_t.reshape(batch, ATTN_KV_HEADS, HEAD_DIM, seq), (0, 3, 1, 2)))
            new_sf.append(s_f)
            new_sb.append(s_b)
        else:
            o = layer // 2
            x = _chunk_mlp(x, mods_l, gpre_l, gpost_l, cm_w_in[o:o + 1], cm_v_gain[o][None, :],
                           cm_v_bias[o][None, :], cm_w_s[o:o + 1],
                           jnp.swapaxes(cm_b_s[o], 0, 1)[None], cm_w_out[o:o + 1], group=group)
        xs = ffn((x,), mods_l, gpre_l, gpost_l, ffn_w_gate, ffn_w_up, ffn_w_down,
                 sub=2, layer=layer, half=1, split_out=(layer == depth - 1))

    y_prompt = xs[0].reshape(batch, seq, d)
    y_sample = xs[1].reshape(dec_batch, dec_seq, d)
    return (y_prompt, y_sample, jnp.stack(new_k, axis=1), jnp.stack(new_v, axis=1),
            jnp.stack(new_sf, axis=1), jnp.stack(new_sb, axis=1))
```

```python
import functools
import math

import jax
import jax.numpy as jnp
from jax import lax
from jax.experimental import pallas as pl
from jax.experimental.pallas import tpu as pltpu

F32 = jnp.float32
BF16 = jnp.bfloat16

EPS = 1e-6
NEG_INF = -1e30
FFN_RESIDUAL = 0.5
N_SUB = 3

ATTN_HEADS = 8
ATTN_KV_HEADS = 2
HEAD_DIM = 64
ATTN_BLOCK = 128
WINDOW = 128
GRID_W = 64
GRID_W_LOG2 = 6
ROPE_BASE = 10000.0
GLA_HEADS = 4
GLA_DK = 64
GLA_DV = 128
GLA_RANK = 16
GLA_TAU = 16.0
CHUNK = 128
CMLP_GROUPS = 4

A_Q = ATTN_HEADS * HEAD_DIM
A_KV = ATTN_KV_HEADS * HEAD_DIM
B_QK = GLA_HEADS * GLA_DK
B_V = GLA_HEADS * GLA_DV
EV_MAIN = A_Q + 2 * A_KV + 2 * B_QK + 2 * B_V
EV_OFFS = (0, A_Q, A_Q + A_KV, A_Q + 2 * A_KV, A_Q + 2 * A_KV + B_QK,
           A_Q + 2 * A_KV + 2 * B_QK, A_Q + 2 * A_KV + 2 * B_QK + B_V, EV_MAIN)

LANES = 128
LANES_LOG2 = 7
SUBLANES = 8
VMEM_LIMIT_BYTES = 56 * 1024 * 1024

TOKEN_TILE = 512
WIDE_TILE = 1024
SUB_ROWS = 512
ADALN_COLS = 2304
FFN_CHUNK = 256
GLA_TILE = 128
GLA_SUB = 8


def _group_index_map(tile_rows, n_prompt_tok, seq_tok):
    def group(i):
        return jnp.maximum(i * tile_rows - (n_prompt_tok - seq_tok), 0) // seq_tok
    return group


def _sub_tiles(n_rows):
    assert n_rows % SUB_ROWS == 0
    return [slice(r, r + SUB_ROWS) for r in range(0, n_rows, SUB_ROWS)]


def _rms(x, g):
    ms = jnp.mean(x * x, axis=-1, keepdims=True)
    return x * lax.rsqrt(ms + EPS) * g


def _sub_in(x, mod_ref, gpre_ref, sub):
    shift = mod_ref[3 * sub + 0:3 * sub + 1, :]
    scale = mod_ref[3 * sub + 1:3 * sub + 2, :]
    return _rms(x, gpre_ref[sub:sub + 1, :]) * (1.0 + scale) + shift


def _sub_out(x, y, mod_ref, gpost_ref, sub, coef):
    gate = mod_ref[3 * sub + 2:3 * sub + 3, :]
    return x + (coef * gate) * _rms(y, gpost_ref[sub:sub + 1, :])


def _silu(x):
    return x * jax.nn.sigmoid(x)


def _mod_kernel(cond_ref, w_ref, b_ref, o_ref):
    s = _silu(cond_ref[...]).astype(BF16)
    o_ref[0] = jnp.dot(s, w_ref[0].astype(BF16), preferred_element_type=F32) + b_ref[0]


def _adaln_mods(cond, w_mod, b_mod):
    depth, d, n = w_mod.shape
    tn = ADALN_COLS
    assert n % tn == 0
    rows = cond.shape[0]
    return pl.pallas_call(
        _mod_kernel,
        grid=(depth, n // tn),
        in_specs=[
            pl.BlockSpec((rows, d), lambda l, j: (0, 0)),
            pl.BlockSpec((1, d, tn), lambda l, j: (l, 0, j)),
            pl.BlockSpec((1, 1, tn), lambda l, j: (l, 0, j)),
        ],
        out_specs=pl.BlockSpec((1, rows, tn), lambda l, j: (l, 0, j)),
        out_shape=jax.ShapeDtypeStruct((depth, rows, n), F32),
        compiler_params=pltpu.CompilerParams(
            dimension_semantics=("parallel", "parallel"), vmem_limit_bytes=VMEM_LIMIT_BYTES),
        name="adaln_mods",
    )(cond, w_mod, b_mod.reshape(depth, 1, n))


def _tile_specs(tm, width, np_tiles):
    return [pl.BlockSpec((tm, width), lambda i: (jnp.minimum(i, np_tiles - 1), 0)),
            pl.BlockSpec((tm, width), lambda i: (jnp.maximum(i - np_tiles, 0), 0))]


def _pick(refs, is_prompt, rows=slice(None)):
    if len(refs) == 1:
        return refs[0][rows, :]
    return jnp.where(is_prompt, refs[0][rows, :], refs[1][rows, :])


def _ffn_kernel(*refs, sub, layer, half, n_chunks, n_x, n_o, np_tiles):
    x_refs = refs[:n_x]
    mod_ref, gpre_ref, gpost_ref, wg_hbm, wu_hbm, wd_hbm = refs[n_x:n_x + 6]
    o_refs = refs[n_x + 6:n_x + 6 + n_o]
    wg_bf, wu_bf, wd_bf, a_ref, st_g, st_u, st_d, sem = refs[n_x + 6 + n_o:]
    fc = FFN_CHUNK
    step = pl.program_id(0)
    is_prompt = step < np_tiles

    def chunk_copies(c, slot):
        cols = pl.ds(c * fc, fc)
        return (
            pltpu.make_async_copy(wg_hbm.at[layer, half, :, cols], st_g.at[slot], sem.at[0, slot]),
            pltpu.make_async_copy(wu_hbm.at[layer, half, :, cols], st_u.at[slot], sem.at[1, slot]),
            pltpu.make_async_copy(wd_hbm.at[layer, half, cols, :], st_d.at[slot], sem.at[2, slot]),
        )

    def run(stage_weights):
        if stage_weights:
            for cp in chunk_copies(0, 0):
                cp.start()
        x = _pick(x_refs, is_prompt)
        h = _sub_in(x, mod_ref, gpre_ref, sub).astype(BF16)
        for c in range(n_chunks):
            sl = slice(c * fc, (c + 1) * fc)
            if stage_weights:
                slot = c % 2
                if c + 1 < n_chunks:
                    for cp in chunk_copies(c + 1, 1 - slot):
                        cp.start()
                for cp in chunk_copies(c, slot):
                    cp.wait()
                wg_bf[:, sl] = st_g[slot].astype(BF16)
                wu_bf[:, sl] = st_u[slot].astype(BF16)
                wd_bf[sl, :] = st_d[slot].astype(BF16)
            g = jnp.dot(h, wg_bf[:, sl], preferred_element_type=F32)
            u = jnp.dot(h, wu_bf[:, sl], preferred_element_type=F32)
            a_ref[:, sl] = (_silu(g) * u).astype(BF16)
        y = jnp.dot(a_ref[...], wd_bf[...], preferred_element_type=F32)
        out = _sub_out(x, y, mod_ref, gpost_ref, sub, FFN_RESIDUAL)
        if n_o == 1:
            o_refs[0][...] = out
        else:
            @pl.when(is_prompt)
            def _store_prompt():
                o_refs[0][...] = out

            @pl.when(jnp.logical_not(is_prompt))
            def _store_latent():
                o_refs[1][...] = out

    @pl.when(step == 0)
    def _first_tile():
        run(True)

    @pl.when(step > 0)
    def _other_tiles():
        run(False)


def _ffn(xs, mods_l, gpre_l, gpost_l, wg, wu, wd, *, sub, layer, half, group, np_tiles, split_out):
    d = xs[0].shape[-1]
    n_tok = sum(x.shape[0] for x in xs)
    d_ff = wg.shape[-1]
    tm = TOKEN_TILE
    fc = FFN_CHUNK
    n_chunks = d_ff // fc
    assert n_chunks * fc == d_ff and n_tok % tm == 0
    n_x, n_o = len(xs), (2 if split_out else 1)
    kern = functools.partial(_ffn_kernel, sub=sub, layer=layer, half=half, n_chunks=n_chunks,
                             n_x=n_x, n_o=n_o, np_tiles=np_tiles)
    whole = pl.BlockSpec((tm, d), lambda i: (i, 0))
    x_specs = _tile_specs(tm, d, np_tiles) if n_x == 2 else [whole]
    n_p = np_tiles * tm
    if split_out:
        out_specs = _tile_specs(tm, d, np_tiles)
        out_shape = [jax.ShapeDtypeStruct((n_p, d), F32), jax.ShapeDtypeStruct((n_tok - n_p, d), F32)]
    else:
        out_specs = [whole]
        out_shape = [jax.ShapeDtypeStruct((n_tok, d), F32)]
    return pl.pallas_call(
        kern,
        grid=(n_tok // tm,),
        in_specs=x_specs + [
            pl.BlockSpec((None, 3 * N_SUB, d), lambda i: (group(i), 0, 0)),
            pl.BlockSpec((N_SUB, d), lambda i: (0, 0)),
            pl.BlockSpec((N_SUB, d), lambda i: (0, 0)),
            pl.BlockSpec(memory_space=pl.ANY),
            pl.BlockSpec(memory_space=pl.ANY),
            pl.BlockSpec(memory_space=pl.ANY),
        ],
        out_specs=out_specs,
        out_shape=out_shape,
        scratch_shapes=[
            pltpu.VMEM((d, d_ff), BF16),
            pltpu.VMEM((d, d_ff), BF16),
            pltpu.VMEM((d_ff, d), BF16),
            pltpu.VMEM((tm, d_ff), BF16),
            pltpu.VMEM((2, d, fc), F32),
            pltpu.VMEM((2, d, fc), F32),
            pltpu.VMEM((2, fc, d), F32),
            pltpu.SemaphoreType.DMA((3, 2)),
        ],
        compiler_params=pltpu.CompilerParams(
            dimension_semantics=("arbitrary",), vmem_limit_bytes=VMEM_LIMIT_BYTES),
        name=f"ffn_l{layer}h{half}",
    )(*xs, mods_l, gpre_l, gpost_l, wg, wu, wd)


def _log_sigmoid(x):
    return jnp.minimum(x, 0.0) - jnp.log1p(jnp.exp(-jnp.abs(x)))


def _evin_kernel(x_ref, mod_ref, gpre_ref, w_ref, wa_ref, ba_ref,
                 qa_ref, ka_ref, va_ref, qb_ref, kb_ref, vb_ref, gb_ref, ldf_ref, ldb_ref,
                 w_bf):
    @pl.when(pl.program_id(0) == 0)
    def _cast_weights():
        w_bf[0:EV_MAIN, :] = w_ref[0, 0:EV_MAIN, :].astype(BF16)
        w_bf[EV_MAIN:EV_MAIN + LANES, :] = jnp.zeros((LANES, w_bf.shape[1]), BF16)
        w_bf[EV_MAIN:EV_MAIN + 2 * GLA_RANK, :] = w_ref[0, EV_MAIN:EV_MAIN + 2 * GLA_RANK, :].astype(BF16)

    o = EV_OFFS
    for rows in _sub_tiles(x_ref.shape[0]):
        h = _sub_in(x_ref[rows, :], mod_ref, gpre_ref, 1).astype(BF16)
        full = _dot_nt(h, w_bf[...])
        qa_ref[rows, :] = (full[:, o[0]:o[1]] * (HEAD_DIM ** -0.5)).astype(qa_ref.dtype)
        ka_ref[rows, :] = full[:, o[1]:o[2]]
        va_ref[rows, :] = full[:, o[2]:o[3]]
        qb_ref[rows, :] = full[:, o[3]:o[4]] * (GLA_DK ** -0.5)
        kb_ref[rows, :] = full[:, o[4]:o[5]]
        vb_ref[rows, :] = full[:, o[5]:o[6]].astype(vb_ref.dtype)
        gb_ref[rows, :] = full[:, o[6]:o[7]].astype(gb_ref.dtype)
        lr = full[:, EV_MAIN:EV_MAIN + LANES].astype(BF16)
        logits = jnp.dot(lr, wa_ref[...].astype(BF16), preferred_element_type=F32) + ba_ref[...]
        ld = _log_sigmoid(logits) * (1.0 / GLA_TAU)
        ldf_ref[rows, :] = ld[:, 0:B_QK]
        ldb_ref[rows, :] = ld[:, B_QK:2 * B_QK]


def _even_in(x, mods_l, gpre_l, w_in_t, e, wa_bd, ba_cat, *, group):
    n_tok, d = x.shape
    tm = WIDE_TILE
    ev_in = w_in_t.shape[1]
    widths = (A_Q, A_KV, A_KV, B_QK, B_QK, B_V, B_V, B_QK, B_QK)
    dtypes = (BF16, F32, F32, F32, F32, BF16, BF16, F32, F32)
    return pl.pallas_call(
        _evin_kernel,
        grid=(n_tok // tm,),
        in_specs=[
            pl.BlockSpec((tm, d), lambda i: (i, 0)),
            pl.BlockSpec((None, 3 * N_SUB, d), lambda i: (group(i), 0, 0)),
            pl.BlockSpec((N_SUB, d), lambda i: (0, 0)),
            pl.BlockSpec((1, ev_in, d), lambda i: (e, 0, 0), pipeline_mode=pl.Buffered(1)),
            pl.BlockSpec(wa_bd.shape, lambda i: (0, 0)),
            pl.BlockSpec(ba_cat.shape, lambda i: (0, 0)),
        ],
        out_specs=[pl.BlockSpec((tm, w), lambda i: (i, 0)) for w in widths],
        out_shape=[jax.ShapeDtypeStruct((n_tok, w), t) for w, t in zip(widths, dtypes)],
        scratch_shapes=[pltpu.VMEM((EV_MAIN + LANES, d), BF16)],
        compiler_params=pltpu.CompilerParams(
            dimension_semantics=("arbitrary",), vmem_limit_bytes=VMEM_LIMIT_BYTES),
        name="even_in",
    )(x, mods_l, gpre_l, w_in_t, wa_bd, ba_cat)


def _lane_lt(shape, n):
    return lax.broadcasted_iota(jnp.int32, shape, len(shape) - 1) < n


def _dup_kv_head(x, g):
    sw = pltpu.roll(x, HEAD_DIM, 1)
    lo = _lane_lt(x.shape, HEAD_DIM)
    return jnp.where(lo, x, sw) if g == 0 else jnp.where(lo, sw, x)


def _dot_nt(a, b):
    return lax.dot_general(a, b, (((1,), (1,)), ((), ())), preferred_element_type=F32)


ATTN_GROUP = ATTN_HEADS // ATTN_KV_HEADS


def _sink_column(sink_ref, g, t):
    rows = lax.broadcasted_iota(jnp.int32, (ATTN_GROUP * t, 1), 0)
    col = jnp.full((ATTN_GROUP * t, 1), sink_ref[ATTN_GROUP * g], F32)
    for j in range(1, ATTN_GROUP):
        col = jnp.where(rows >= j * t, sink_ref[ATTN_GROUP * g + j], col)
    return col


def _attn_group(q_pairs, sink_col, keys):
    t = q_pairs[0].shape[0]
    lo = _lane_lt((t, LANES), HEAD_DIM)
    zero = jnp.zeros((t, LANES), BF16)
    q4 = jnp.concatenate([jnp.where(lo, q_pairs[0], zero), jnp.where(lo, zero, q_pairs[0]),
                          jnp.where(lo, q_pairs[1], zero), jnp.where(lo, zero, q_pairs[1])], axis=0)
    scores = []
    mx = sink_col
    for k2, _, mask in keys:
        s = _dot_nt(q4, k2)
        if mask is not None:
            s = jnp.where(mask, s, NEG_INF)
        scores.append(s)
        mx = jnp.maximum(mx, jnp.max(s, axis=-1, keepdims=True))
    den = jnp.exp(sink_col - mx)
    o = None
    for s, (_, v2, _) in zip(scores, keys):
        p = jnp.exp(s - mx)
        den = den + jnp.sum(p, axis=-1, keepdims=True)
        pv = jnp.dot(p.astype(BF16), v2, preferred_element_type=F32)
        o = pv if o is None else o + pv
    o = o / den
    return (jnp.where(lo, o[0:t], o[t:2 * t]), jnp.where(lo, o[2 * t:3 * t], o[3 * t:4 * t]))


def _attn_ctx_kernel(sink_ref, q_ref, k_ref, v_ref, o_ref, kt_ref, vt_ref):
    k = k_ref[...]
    v = v_ref[...]
    t = k.shape[0]
    kt_ref[0] = k.T
    vt_ref[0] = v.T
    for g in range(ATTN_KV_HEADS):
        keys = [(_dup_kv_head(k, g).astype(BF16), _dup_kv_head(v, g).astype(BF16), None)]
        cols = (2 * g, 2 * g + 1)
        outs = _attn_group([q_ref[:, m * LANES:(m + 1) * LANES] for m in cols],
                           _sink_column(sink_ref, g, t), keys)
        for m, o in zip(cols, outs):
            o_ref[:, m * LANES:(m + 1) * LANES] = o.astype(o_ref.dtype)


def _attn_context(sink, qa, ka, va, *, n_seq, seq):
    cache_spec = pl.BlockSpec((1, A_KV, seq), lambda b: (b, 0, 0))
    return pl.pallas_call(
        _attn_ctx_kernel,
        grid=(n_seq,),
        in_specs=[
            pl.BlockSpec(memory_space=pltpu.SMEM),
            pl.BlockSpec((seq, A_Q), lambda b: (b, 0)),
            pl.BlockSpec((seq, A_KV), lambda b: (b, 0)),
            pl.BlockSpec((seq, A_KV), lambda b: (b, 0)),
        ],
        out_specs=[pl.BlockSpec((seq, A_Q), lambda b: (b, 0)), cache_spec, cache_spec],
        out_shape=[jax.ShapeDtypeStruct((n_seq * seq, A_Q), BF16),
                   jax.ShapeDtypeStruct((n_seq, A_KV, seq), F32),
                   jax.ShapeDtypeStruct((n_seq, A_KV, seq), F32)],
        compiler_params=pltpu.CompilerParams(dimension_semantics=("parallel",)),
        name="attn_context",
    )(sink, qa, ka, va)


def _rope_tables(n_tok):
    n_rows = n_tok // GRID_W
    assert n_rows <= GRID_W
    shape = (GRID_W, LANES)
    pos = lax.broadcasted_iota(jnp.int32, shape, 0).astype(F32)
    lane = lax.broadcasted_iota(jnp.int32, shape, 1)
    half = HEAD_DIM // 2
    nf = half // 2
    within = lane & (HEAD_DIM - 1)
    is_col = within >= half
    second = (within & (half - 1)) >= nf
    f = (within & (nf - 1)).astype(F32)
    inv = jnp.exp(f * (-2.0 / half * math.log(ROPE_BASE)))
    ang = pos * inv
    cos_t = jnp.cos(ang)
    sin_t = jnp.sin(ang)
    sin_t = jnp.where(second, sin_t, -sin_t)

    def per_token(tab):
        return jnp.concatenate(
            [jnp.where(is_col, tab, jnp.broadcast_to(tab[r:r + 1, :], shape)) for r in range(n_rows)], axis=0)

    return per_token(cos_t), per_token(sin_t)


def _rope(x, cos, sin_signed):
    nf = HEAD_DIM // 4
    lane = lax.broadcasted_iota(jnp.int32, x.shape, 1)
    second = (lane & (2 * nf - 1)) >= nf
    up = pltpu.roll(x, nf, 1)
    dn = pltpu.roll(x, LANES - nf, 1)
    return x * cos + jnp.where(second, up, dn) * sin_signed


def _attn_lat_kernel(sink_ref, q_ref, k_ref, v_ref, kc_ref, vc_ref, o_ref, qr_ref, kp_ref, vp_ref,
                     *, seq):
    nb = seq // ATTN_BLOCK
    blk = ATTN_BLOCK
    cos, sin_signed = _rope_tables(seq)
    for m in range(A_Q // LANES):
        qr_ref[:, m * LANES:(m + 1) * LANES] = _rope(
            q_ref[:, m * LANES:(m + 1) * LANES].astype(F32), cos, sin_signed).astype(BF16)
    kr = _rope(k_ref[...], cos, sin_signed)
    v = v_ref[...]
    zpad = jnp.zeros((blk, LANES), BF16)
    ctx, sink_cols = [], []
    for g in range(ATTN_KV_HEADS):
        kp_ref[g, 0:blk, :] = zpad
        kp_ref[g, blk + seq:2 * blk + seq, :] = zpad
        vp_ref[g, 0:blk, :] = zpad
        vp_ref[g, blk + seq:2 * blk + seq, :] = zpad
        kp_ref[g, blk:blk + seq, :] = _dup_kv_head(kr, g).astype(BF16)
        vp_ref[g, blk:blk + seq, :] = _dup_kv_head(v, g).astype(BF16)
        ctx.append((_dup_kv_head(kc_ref[0], g).astype(BF16), _dup_kv_head(vc_ref[0], g).astype(BF16), None))
        sink_cols.append(_sink_column(sink_ref, g, blk))
    r = lax.broadcasted_iota(jnp.int32, (ATTN_GROUP * blk, 3 * blk), 0) & (blk - 1)
    c = lax.broadcasted_iota(jnp.int32, (ATTN_GROUP * blk, 3 * blk), 1)
    band = (c >= r) & (c <= r + 2 * WINDOW)

    def block_body(i, carry):
        row0 = pl.multiple_of(i * blk, blk)
        kj = (i - 1) * blk + c
        mask = band & (kj >= 0) & (kj < seq)
        for g in range(ATTN_KV_HEADS):
            cols = (2 * g, 2 * g + 1)
            window = (kp_ref[g, pl.ds(row0, 3 * blk), :], vp_ref[g, pl.ds(row0, 3 * blk), :], mask)
            outs = _attn_group([qr_ref[pl.ds(row0, blk), m * LANES:(m + 1) * LANES] for m in cols],
                               sink_cols[g], [window, ctx[g]])
            for m, o in zip(cols, outs):
                o_ref[pl.ds(row0, blk), m * LANES:(m + 1) * LANES] = o.astype(o_ref.dtype)
        return carry

    lax.fori_loop(0, nb, block_body, 0)


def _attn_latent(sink, qa, ka, va, k_ctx, v_ctx, *, n_seq, seq, row_block0):
    past = k_ctx.shape[1]
    kern = functools.partial(_attn_lat_kernel, seq=seq)
    return pl.pallas_call(
        kern,
        grid=(n_seq,),
        in_specs=[
            pl.BlockSpec(memory_space=pltpu.SMEM),
            pl.BlockSpec((seq, A_Q), lambda b: (row_block0 + b, 0)),
            pl.BlockSpec((seq, A_KV), lambda b: (row_block0 + b, 0)),
            pl.BlockSpec((seq, A_KV), lambda b: (row_block0 + b, 0)),
            pl.BlockSpec((1, past, A_KV), lambda b: (b, 0, 0)),
            pl.BlockSpec((1, past, A_KV), lambda b: (b, 0, 0)),
        ],
        out_specs=pl.BlockSpec((seq, A_Q), lambda b: (b, 0)),
        out_shape=jax.ShapeDtypeStruct((n_seq * seq, A_Q), BF16),
        scratch_shapes=[
            pltpu.VMEM((seq, A_Q), BF16),
            pltpu.VMEM((ATTN_KV_HEADS, seq + 2 * ATTN_BLOCK, LANES), BF16),
            pltpu.VMEM((ATTN_KV_HEADS, seq + 2 * ATTN_BLOCK, LANES), BF16),
        ],
        compiler_params=pltpu.CompilerParams(dimension_semantics=("parallel",)),
        name="attn_latent",
    )(sink, qa, ka, va, k_ctx, v_ctx)


def _same_block(a, b, size):
    return (a & -size) == (b & -size)


def _split3(x):
    hi = x.astype(BF16)
    r1 = x - hi.astype(F32)
    mid = r1.astype(BF16)
    lo = (r1 - mid.astype(F32)).astype(BF16)
    return hi, mid, lo


def _gla_tile(q, k, v, ld, st_ref, sel_ref, *, rev):
    t = GLA_TILE
    n_pair = GLA_HEADS // 2
    row = lax.broadcasted_iota(jnp.int32, (t, t), 0)
    col = lax.broadcasted_iota(jnp.int32, (t, t), 1)
    tri = (row <= col) if rev else (row >= col)
    tri = jnp.where(tri, 1.0, 0.0).astype(BF16)
    hi, mid, lo = _split3(ld * math.log2(math.e))
    b = (jnp.dot(tri, hi, preferred_element_type=F32) + jnp.dot(tri, mid, preferred_element_type=F32)
         + jnp.dot(tri, lo, preferred_element_type=F32))
    b_tot = b[0:1, :] if rev else b[t - 1:t, :]
    qs = (q * jnp.exp2(b)).astype(BF16)
    kd = (k * jnp.exp2(b_tot - b)).astype(BF16)

    rowv = lax.broadcasted_iota(jnp.int32, (t, B_QK), 0)
    levels = []
    half = GLA_SUB
    while 2 * half <= t:
        blk = 2 * half
        pieces = []
        for bs in range(0, t, blk):
            r = bs + half if rev else bs + half - 1
            pieces.append(jnp.broadcast_to(b[r:r + 1, :], (blk, B_QK)))
        ref = jnp.concatenate(pieces, axis=0) if len(pieces) > 1 else pieces[0]
        second = (rowv & (blk - 1)) >= half
        is_q = ~second if rev else second
        ql = (q * jnp.exp2(jnp.where(is_q, b - ref, -jnp.inf))).astype(BF16)
        kl = (k * jnp.exp2(jnp.where(is_q, -jnp.inf, ref - b))).astype(BF16)
        levels.append((blk, ql, kl))
        half = blk

    rr = lax.broadcasted_iota(jnp.int32, (t, 2 * t), 0)
    ss = lax.broadcasted_iota(jnp.int32, (t, 2 * t), 1) & (t - 1)
    sub_rows = lax.broadcasted_iota(jnp.int32, (GLA_SUB, LANES), 0)
    lane_lo = _lane_lt((t, LANES), GLA_DK)
    lane_lo2 = _lane_lt((2 * t, LANES), GLA_DK)
    par_rows = lax.broadcasted_iota(jnp.int32, (2 * t, LANES), 0) < t
    head_sel = par_rows == lane_lo2

    outs = []
    for p in range(n_pair):
        pl_ = slice(p * LANES, (p + 1) * LANES)
        qp, kp, bp = q[:, pl_], k[:, pl_], b[:, pl_]
        rows = []
        for sb in range(t // GLA_SUB):
            r0 = sb * GLA_SUB
            q_sub = qp[r0:r0 + GLA_SUB, :]
            b_sub = bp[r0:r0 + GLA_SUB, :]
            cols = []
            for j in range(GLA_SUB):
                kj = kp[r0 + j:r0 + j + 1, :]
                bj = bp[r0 + j:r0 + j + 1, :]
                valid = (sub_rows <= j) if rev else (sub_rows >= j)
                e = jnp.exp2(jnp.where(valid, b_sub - bj, -jnp.inf))
                cols.append(q_sub * kj * e)
            rows.append(jnp.concatenate(cols, axis=1))
        e_p = jnp.concatenate(rows, axis=0).astype(BF16)
        scores = jnp.dot(e_p, sel_ref[...], preferred_element_type=F32)
        scores = jnp.where(_same_block(rr, ss, GLA_SUB), scores, 0.0)
        for blk, ql, kl in levels:
            klp = kl[:, pl_]
            kstack = jnp.concatenate([klp, klp], axis=0)
            kstack = jnp.where(head_sel, kstack, jnp.zeros_like(kstack))
            s_l = _dot_nt(ql[:, pl_], kstack)
            scores = scores + jnp.where(_same_block(rr, ss, blk), s_l, 0.0)
        vp = v[:, p * 2 * GLA_DV:(p + 1) * 2 * GLA_DV]
        v_lo = _lane_lt(vp.shape, GLA_DV)
        vbd = jnp.concatenate([jnp.where(v_lo, vp, 0.0), jnp.where(v_lo, 0.0, vp)], axis=0).astype(BF16)
        st = st_ref[p]
        o_p = (jnp.dot(scores.astype(BF16), vbd, preferred_element_type=F32)
               + _dot_nt(qs[:, pl_], st.astype(BF16)))
        outs.append(o_p)
        upd = jnp.dot(vp.T.astype(BF16), kd[:, pl_], preferred_element_type=F32)
        st_rows_lo = lax.broadcasted_iota(jnp.int32, upd.shape, 0) < GLA_DV
        bd = st_rows_lo == _lane_lt(upd.shape, GLA_DK)
        st_ref[p] = st * jnp.exp2(b_tot[:, pl_]) + jnp.where(bd, upd, 0.0)
    return jnp.concatenate(outs, axis=1)


def _gla_kernel(*refs, seq, has_init, want_state):
    q_ref, k_ref, v_ref, ldf_ref, ldb_ref = refs[:5]
    pos = 5
    if has_init:
        s0f_ref, s0b_ref = refs[pos:pos + 2]
        pos += 2
    o_ref = refs[pos]
    pos += 1
    if want_state:
        sf_ref, sb_ref = refs[pos:pos + 2]
        pos += 2
    st_ref, sel_ref = refs[pos:pos + 2]
    t = GLA_TILE
    nt = seq // t
    n_pair = GLA_HEADS // 2

    kk = lax.broadcasted_iota(jnp.int32, (GLA_SUB * LANES, 2 * t), 0)
    nn = lax.broadcasted_iota(jnp.int32, (GLA_SUB * LANES, 2 * t), 1)
    sel = ((nn & (GLA_SUB - 1)) == (kk >> LANES_LOG2)) & (((kk & (LANES - 1)) >= GLA_DK) == (nn >= t))
    sel_ref[...] = jnp.where(sel, 1.0, 0.0).astype(BF16)

    def load_state(st, s0_ref):
        for p in range(n_pair):
            if s0_ref is None:
                st[p] = jnp.zeros(st.shape[1:], F32)
            else:
                x = jnp.concatenate([s0_ref[0, 2 * p], s0_ref[0, 2 * p + 1]], axis=0)
                xt = x.T
                lo = _lane_lt(xt.shape, GLA_DK)
                st[p] = jnp.concatenate([jnp.where(lo, xt, 0.0), jnp.where(lo, 0.0, xt)], axis=0)

    def store_state(st, out_ref):
        for p in range(n_pair):
            s = st[p]
            lo = _lane_lt((GLA_DV, LANES), GLA_DK)
            z = jnp.where(lo, s[0:GLA_DV, :], s[GLA_DV:2 * GLA_DV, :]).T
            out_ref[0, 2 * p] = z[0:GLA_DK, :]
            out_ref[0, 2 * p + 1] = z[GLA_DK:2 * GLA_DK, :]

    dirs = ((False, ldf_ref, st_ref.at[0]), (True, ldb_ref, st_ref.at[1]))
    for rev, _, st in dirs:
        load_state(st, (s0b_ref if rev else s0f_ref) if has_init else None)
    o_ref[...] = jnp.zeros(o_ref.shape, o_ref.dtype)

    def tile_body(ti, carry):
        for rev, ld_ref, st in dirs:
            tile = (nt - 1 - ti) if rev else ti
            r0 = pl.multiple_of(tile * t, t)
            rows = pl.ds(r0, t)
            o = _gla_tile(q_ref[rows, :], k_ref[rows, :], v_ref[rows, :].astype(F32), ld_ref[rows, :],
                          st, sel_ref, rev=rev)
            o_ref[rows, :] = o_ref[rows, :] + o
        return carry

    lax.fori_loop(0, nt, tile_body, 0)
    if want_state:
        for rev, _, st in dirs:
            store_state(st, sb_ref if rev else sf_ref)


def _gla(qb, kb, vb, ldf, ldb, s0f, s0b, *, n_seq, seq, row_block0, want_state):
    has_init = s0f is not None
    kern = functools.partial(_gla_kernel, seq=seq, has_init=has_init, want_state=want_state)
    tok = lambda w: pl.BlockSpec((seq, w), lambda b: (row_block0 + b, 0))
    in_specs = [tok(B_QK), tok(B_QK), tok(B_V), tok(B_QK), tok(B_QK)]
    args = [qb, kb, vb, ldf, ldb]
    state_spec = pl.BlockSpec((1, GLA_HEADS, GLA_DK, GLA_DV), lambda b: (b, 0, 0, 0))
    if has_init:
        in_specs += [state_spec, state_spec]
        args += [s0f, s0b]
    out_specs = [pl.BlockSpec((seq, B_V), lambda b: (b, 0))]
    out_shape = [jax.ShapeDtypeStruct((n_seq * seq, B_V), F32)]
    if want_state:
        out_specs += [state_spec, state_spec]
        out_shape += [jax.ShapeDtypeStruct((n_seq, GLA_HEADS, GLA_DK, GLA_DV), F32)] * 2
    return pl.pallas_call(
        kern,
        grid=(n_seq,),
        in_specs=in_specs,
        out_specs=out_specs,
        out_shape=out_shape,
        scratch_shapes=[
            pltpu.VMEM((2, GLA_HEADS // 2, 2 * GLA_DV, LANES), F32),
            pltpu.VMEM((GLA_SUB * LANES, 2 * GLA_TILE), BF16),
        ],
        compiler_params=pltpu.CompilerParams(
            dimension_semantics=("parallel",), vmem_limit_bytes=VMEM_LIMIT_BYTES),
        name="gla_state" if want_state else "gla_latent",
    )(*args)


def _evout_kernel(x_ref, mod_ref, gpost_ref, attn_p_ref, attn_s_ref, gla_p_ref, gla_s_ref, gb_ref, gn_ref,
                  w_ref, o_ref, w_bf, *, np_tiles):
    @pl.when(pl.program_id(0) == 0)
    def _cast_weights():
        w_bf[...] = w_ref[0].astype(BF16)

    is_prompt = pl.program_id(0) < np_tiles
    for rows in _sub_tiles(x_ref.shape[0]):
        attn = _pick((attn_p_ref, attn_s_ref), is_prompt, rows)
        gla = _pick((gla_p_ref, gla_s_ref), is_prompt, rows)
        parts = [attn]
        for hd in range(GLA_HEADS):
            sl = slice(hd * GLA_DV, (hd + 1) * GLA_DV)
            g = _rms(gla[:, sl], gn_ref[:, sl]) * _silu(gb_ref[rows, sl].astype(F32))
            parts.append(g.astype(BF16))
        y = jnp.dot(jnp.concatenate(parts, axis=1), w_bf[...], preferred_element_type=F32)
        o_ref[rows, :] = _sub_out(x_ref[rows, :], y, mod_ref, gpost_ref, 1, 1.0)


def _even_out(x, mods_l, gpost_l, attn_ps, gla_ps, gb, gla_norm, w_out, e, *, group, np_tiles):
    n_tok, d = x.shape
    tm = WIDE_TILE
    ev_out = w_out.shape[1]
    kern = functools.partial(_evout_kernel, np_tiles=np_tiles)
    return pl.pallas_call(
        kern,
        grid=(n_tok // tm,),
        in_specs=[
            pl.BlockSpec((tm, d), lambda i: (i, 0)),
            pl.BlockSpec((None, 3 * N_SUB, d), lambda i: (group(i), 0, 0)),
            pl.BlockSpec((N_SUB, d), lambda i: (0, 0)),
            *_tile_specs(tm, A_Q, np_tiles),
            *_tile_specs(tm, B_V, np_tiles),
            pl.BlockSpec((tm, B_V), lambda i: (i, 0)),
            pl.BlockSpec((1, B_V), lambda i: (0, 0)),
            pl.BlockSpec((1, ev_out, d), lambda i: (e, 0, 0), pipeline_mode=pl.Buffered(1)),
        ],
        out_specs=pl.BlockSpec((tm, d), lambda i: (i, 0)),
        out_shape=jax.ShapeDtypeStruct((n_tok, d), F32),
        scratch_shapes=[pltpu.VMEM((ev_out, d), BF16)],
        compiler_params=pltpu.CompilerParams(
            dimension_semantics=("arbitrary",), vmem_limit_bytes=VMEM_LIMIT_BYTES),
        name="even_out",
    )(x, mods_l, gpost_l, *attn_ps, *gla_ps, gb, gla_norm, w_out)


def _gelu(x):
    return 0.5 * x * (1.0 + lax.erf(x * (2.0 ** -0.5)))


def _cm_kernel(x_ref, mod_ref, gpre_ref, gpost_ref, win_ref, vg_ref, vb_ref, ws_ref, bs_ref, wout_ref,
               o_ref, win_bf, wout_bf):
    @pl.when(pl.program_id(0) == 0)
    def _cast_weights():
        win_bf[...] = win_ref[0].astype(BF16)
        wout_bf[...] = wout_ref[0].astype(BF16)

    width = wout_bf.shape[0]
    gw = width // CMLP_GROUPS
    ws = [ws_ref[0, g].astype(BF16) for g in range(CMLP_GROUPS)]
    for rows in _sub_tiles(x_ref.shape[0]):
        x = x_ref[rows, :]
        h = _sub_in(x, mod_ref, gpre_ref, 1).astype(BF16)
        u = _gelu(jnp.dot(h, win_bf[:, 0:width], preferred_element_type=F32))
        v = _gelu(jnp.dot(h, win_bf[:, width:2 * width], preferred_element_type=F32))
        mu = jnp.mean(v, axis=-1, keepdims=True)
        vc = v - mu
        var = jnp.mean(vc * vc, axis=-1, keepdims=True)
        vn = (vc * lax.rsqrt(var + EPS) * vg_ref[...] + vb_ref[...]).astype(BF16)
        chunk_rows = []
        for c in range(x.shape[0] // CHUNK):
            cols = []
            for g in range(CMLP_GROUPS):
                blk = vn[c * CHUNK:(c + 1) * CHUNK, g * gw:(g + 1) * gw]
                cols.append(jnp.dot(ws[g], blk, preferred_element_type=F32) + bs_ref[0, :, g:g + 1])
            chunk_rows.append(jnp.concatenate(cols, axis=1))
        mixed = jnp.concatenate(chunk_rows, axis=0)
        y = jnp.dot((u * mixed).astype(BF16), wout_bf[...], preferred_element_type=F32)
        o_ref[rows, :] = _sub_out(x, y, mod_ref, gpost_ref, 1, 1.0)


def _chunk_mlp(x, mods_l, gpre_l, gpost_l, w_in, v_gain, v_bias, w_s, b_s_t, w_out, *, group):
    n_tok, d = x.shape
    tm = WIDE_TILE
    width = w_out.shape[1]
    return pl.pallas_call(
        _cm_kernel,
        grid=(n_tok // tm,),
        in_specs=[
            pl.BlockSpec((tm, d), lambda i: (i, 0)),
            pl.BlockSpec((None, 3 * N_SUB, d), lambda i: (group(i), 0, 0)),
            pl.BlockSpec((N_SUB, d), lambda i: (0, 0)),
            pl.BlockSpec((N_SUB, d), lambda i: (0, 0)),
            pl.BlockSpec((1, d, 2 * width), lambda i: (0, 0, 0), pipeline_mode=pl.Buffered(1)),
            pl.BlockSpec((1, width), lambda i: (0, 0)),
            pl.BlockSpec((1, width), lambda i: (0, 0)),
            pl.BlockSpec((1, CMLP_GROUPS, CHUNK, CHUNK), lambda i: (0, 0, 0, 0)),
            pl.BlockSpec((1, CHUNK, CMLP_GROUPS), lambda i: (0, 0, 0)),
            pl.BlockSpec((1, width, d), lambda i: (0, 0, 0), pipeline_mode=pl.Buffered(1)),
        ],
        out_specs=pl.BlockSpec((tm, d), lambda i: (i, 0)),
        out_shape=jax.ShapeDtypeStruct((n_tok, d), F32),
        scratch_shapes=[pltpu.VMEM((d, 2 * width), BF16), pltpu.VMEM((width, d), BF16)],
        compiler_params=pltpu.CompilerParams(
            dimension_semantics=("arbitrary",), vmem_limit_bytes=VMEM_LIMIT_BYTES),
        name="chunk_mlp",
    )(x, mods_l, gpre_l, gpost_l, w_in, v_gain, v_bias, w_s, b_s_t, w_out)


def kernel(x_prompt, x_sample, cache_k, cache_v, state_gla_fwd, state_gla_bwd, c, c_ctx, w_mod, b_mod, norm_pre, norm_post, ffn_w_gate, ffn_w_up, ffn_w_down, ev_w_in, ev_w_out, ev_sink, gla_wa_f, gla_ba_f, gla_wa_b, gla_ba_b, gla_norm, cm_w_in, cm_v_gain, cm_v_bias, cm_w_s, cm_b_s, cm_w_out):
    batch, seq, d = x_prompt.shape
    dec_batch, dec_seq, _ = x_sample.shape
    depth = w_mod.shape[0]
    n_prompt_tok = batch * seq
    n_sample_tok = dec_batch * dec_seq
    assert n_prompt_tok % dec_seq == 0 and dec_seq % TOKEN_TILE == 0
    group = _group_index_map(TOKEN_TILE, n_prompt_tok, dec_seq)
    np_tiles = n_prompt_tok // TOKEN_TILE
    ffn = functools.partial(_ffn, group=group, np_tiles=np_tiles)
    assert dec_seq % WIDE_TILE == 0
    group_w = _group_index_map(WIDE_TILE, n_prompt_tok, dec_seq)
    np_tiles_w = n_prompt_tok // WIDE_TILE

    xs = (x_prompt.reshape(n_prompt_tok, d), x_sample.reshape(n_sample_tok, d))
    ev_w_in_t = jnp.swapaxes(ev_w_in, 1, 2)

    n_cond = 1 + dec_batch
    cond_rows = -(-n_cond // SUBLANES) * SUBLANES
    cond = jnp.concatenate([c_ctx[None, :], c, jnp.zeros((cond_rows - n_cond, d), F32)], axis=0)
    mods = _adaln_mods(cond, w_mod, b_mod).reshape(depth, cond_rows, 3 * N_SUB, d)

    new_k, new_v, new_sf, new_sb = [], [], [], []
    for layer in range(depth):
        mods_l, gpre_l, gpost_l = mods[layer], norm_pre[layer], norm_post[layer]
        (x,) = ffn(xs, mods_l, gpre_l, gpost_l, ffn_w_gate, ffn_w_up, ffn_w_down,
                   sub=0, layer=layer, half=0, split_out=False)
        if layer % 2 == 0:
            e = layer // 2
            rank = gla_wa_f.shape[1]
            wa_bd = jnp.zeros((LANES, 2 * B_QK), F32)
            wa_bd = wa_bd.at[0:rank, 0:B_QK].set(gla_wa_f[e]).at[rank:2 * rank, B_QK:].set(gla_wa_b[e])
            ba_cat = jnp.concatenate([gla_ba_f[e], gla_ba_b[e]])[None, :]
            qa, ka, va, qb, kb, vb, gb, ldf, ldb = _even_in(
                x, mods_l, gpre_l, ev_w_in_t, e, wa_bd, ba_cat, group=group_w)
            sink = ev_sink[e]
            attn_p, k_t, v_t = _attn_context(sink, qa, ka, va, n_seq=batch, seq=seq)
            k_ctx = cache_k[:, e].reshape(dec_batch, -1, A_KV)
            v_ctx = cache_v[:, e].reshape(dec_batch, -1, A_KV)
            attn_s = _attn_latent(sink, qa, ka, va, k_ctx, v_ctx, n_seq=dec_batch, seq=dec_seq,
                                  row_block0=n_prompt_tok // dec_seq)
            gla_p, s_f, s_b = _gla(qb, kb, vb, ldf, ldb, None, None, n_seq=batch, seq=seq,
                                   row_block0=0, want_state=True)
            (gla_s,) = _gla(qb, kb, vb, ldf, ldb, state_gla_fwd[:, e], state_gla_bwd[:, e],
                            n_seq=dec_batch, seq=dec_seq, row_block0=n_prompt_tok // dec_seq,
                            want_state=False)
            x = _even_out(x, mods_l, gpost_l, (attn_p, attn_s), (gla_p, gla_s), gb, gla_norm[e][None, :],
                          ev_w_out, e, group=group_w, np_tiles=np_tiles_w)
            for cache_t, dst in ((k_t, new_k), (v_t, new_v)):
                dst.append(jnp.transpose(cache_t.reshape(batch, ATTN_KV_HEADS, HEAD_DIM, seq), (0, 3, 1, 2)))
            new_sf.append(s_f)
            new_sb.append(s_b)
        else:
            o = layer // 2
            x = _chunk_mlp(x, mods_l, gpre_l, gpost_l, cm_w_in[o:o + 1], cm_v_gain[o][None, :],
                           cm_v_bias[o][None, :], cm_w_s[o:o + 1],
                           jnp.swapaxes(cm_b_s[o], 0, 1)[None], cm_w_out[o:o + 1], group=group_w)
        xs = ffn((x,), mods_l, gpre_l, gpost_l, ffn_w_gate, ffn_w_up, ffn_w_down,
                 sub=2, layer=layer, half=1, split_out=(layer == depth - 1))

    y_prompt = xs[0].reshape(batch, seq, d)
    y_sample = xs[1].reshape(dec_batch, dec_seq, d)
    return (y_prompt, y_sample, jnp.stack(new_k, axis=1), jnp.stack(new_v, axis=1),
            jnp.stack(new_sf, axis=1), jnp.stack(new_sb, axis=1))
```

```python
import functools
import math

import jax
import jax.numpy as jnp
from jax import lax
from jax.experimental import pallas as pl
from jax.experimental.pallas import tpu as pltpu

F32 = jnp.float32
BF16 = jnp.bfloat16

EPS = 1e-6
NEG_INF = -1e30
FFN_RESIDUAL = 0.5
N_SUB = 3

ATTN_HEADS = 8
ATTN_KV_HEADS = 2
HEAD_DIM = 64
ATTN_BLOCK = 128
WINDOW = 128
GRID_W = 64
GRID_W_LOG2 = 6
ROPE_BASE = 10000.0
GLA_HEADS = 4
GLA_DK = 64
GLA_DV = 128
GLA_RANK = 16
GLA_TAU = 16.0
CHUNK = 128
CMLP_GROUPS = 4

A_Q = ATTN_HEADS * HEAD_DIM
A_KV = ATTN_KV_HEADS * HEAD_DIM
B_QK = GLA_HEADS * GLA_DK
B_V = GLA_HEADS * GLA_DV
EV_MAIN = A_Q + 2 * A_KV + 2 * B_QK + 2 * B_V
EV_OFFS = (0, A_Q, A_Q + A_KV, A_Q + 2 * A_KV, A_Q + 2 * A_KV + B_QK,
           A_Q + 2 * A_KV + 2 * B_QK, A_Q + 2 * A_KV + 2 * B_QK + B_V, EV_MAIN)

LANES = 128
LANES_LOG2 = 7
SUBLANES = 8
VMEM_LIMIT_BYTES = 56 * 1024 * 1024

TOKEN_TILE = 512
WIDE_TILE = 1024
SUB_ROWS = 512
ADALN_COLS = 3072
FFN_CHUNK = 256
GLA_TILE = 128
GLA_SUB = 8


def _group_index_map(tile_rows, n_prompt_tok, seq_tok):
    def group(i):
        return jnp.maximum(i * tile_rows - (n_prompt_tok - seq_tok), 0) // seq_tok
    return group


def _sub_tiles(n_rows):
    assert n_rows % SUB_ROWS == 0
    return [slice(r, r + SUB_ROWS) for r in range(0, n_rows, SUB_ROWS)]


def _rms(x, g):
    ms = jnp.mean(x * x, axis=-1, keepdims=True)
    return x * lax.rsqrt(ms + EPS) * g


def _mod_rows(mod_ref, g):
    return [mod_ref[k, pl.ds(g, 1), :] for k in range(3 * N_SUB)]


def _sub_in(x, mods, g_pre, sub):
    shift, scale = mods[3 * sub + 0], mods[3 * sub + 1]
    return _rms(x, g_pre) * (1.0 + scale) + shift


def _sub_out(x, y, mods, g_post, sub, coef):
    return x + (coef * mods[3 * sub + 2]) * _rms(y, g_post)


def _norm_row(ref, layer, sub):
    r = layer * N_SUB + sub
    return ref[r:r + 1, :]


def _silu(x):
    return x * jax.nn.sigmoid(x)


def _mod_kernel(cctx_ref, c_ref, w_ref, b_ref, o_ref):
    _, n_kinds, rows, d = o_ref.shape
    r = lax.broadcasted_iota(jnp.int32, (rows, d), 0)
    cond = jnp.where(r == 0, jnp.broadcast_to(cctx_ref[...], (rows, d)), 0.0)
    for j in range(c_ref.shape[0]):
        cond = jnp.where(r == 1 + j, jnp.broadcast_to(c_ref[j:j + 1, :], (rows, d)), cond)
    s = _silu(cond).astype(BF16)
    bias = b_ref[pl.ds(pl.program_id(0), 1), :]
    res = jnp.dot(s, w_ref[0].astype(BF16), preferred_element_type=F32) + bias
    for q in range(n_kinds):
        o_ref[0, q] = res[:, q * d:(q + 1) * d]


def _adaln_mods(c_ctx, c, w_mod, b_mod):
    depth, d, n = w_mod.shape
    tn = ADALN_COLS
    assert n % tn == 0 and tn % d == 0
    rows = -(-(1 + c.shape[0]) // SUBLANES) * SUBLANES
    return pl.pallas_call(
        _mod_kernel,
        grid=(depth, n // tn),
        in_specs=[
            pl.BlockSpec((1, d), lambda l, j: (0, 0)),
            pl.BlockSpec(c.shape, lambda l, j: (0, 0)),
            pl.BlockSpec((1, d, tn), lambda l, j: (l, 0, j)),
            pl.BlockSpec((depth, tn), lambda l, j: (0, j)),
        ],
        out_specs=pl.BlockSpec((1, tn // d, rows, d), lambda l, j: (l, j, 0, 0)),
        out_shape=jax.ShapeDtypeStruct((depth, n // d, rows, d), F32),
        compiler_params=pltpu.CompilerParams(
            dimension_semantics=("parallel", "parallel"), vmem_limit_bytes=VMEM_LIMIT_BYTES),
        name="adaln_mods",
    )(c_ctx.reshape(1, d), c, w_mod, b_mod)


def _tile_specs(tm, width, np_tiles):
    return [pl.BlockSpec((tm, width), lambda i: (jnp.minimum(i, np_tiles - 1), 0)),
            pl.BlockSpec((tm, width), lambda i: (jnp.maximum(i - np_tiles, 0), 0))]


def _pick(refs, is_prompt, rows=slice(None)):
    if len(refs) == 1:
        return refs[0][rows, :]
    return jnp.where(is_prompt, refs[0][rows, :], refs[1][rows, :])


def _cond_specs(mods, norm_pre, norm_post, layer):
    return [pl.BlockSpec((None,) + mods.shape[1:], lambda i: (layer, 0, 0, 0)),
            pl.BlockSpec(norm_pre.shape, lambda i: (0, 0)),
            pl.BlockSpec(norm_post.shape, lambda i: (0, 0))]


def _ffn_kernel(*refs, sub, layer, half, n_chunks, n_x, n_o, np_tiles, group):
    x_refs = refs[:n_x]
    mod_ref, gpre_ref, gpost_ref, wg_hbm, wu_hbm, wd_hbm = refs[n_x:n_x + 6]
    o_refs = refs[n_x + 6:n_x + 6 + n_o]
    wg_bf, wu_bf, wd_bf, st_g, st_u, st_d, sem = refs[n_x + 6 + n_o:]
    fc = FFN_CHUNK
    step = pl.program_id(0)
    is_prompt = step < np_tiles
    mods = _mod_rows(mod_ref, group(step))
    g_pre = _norm_row(gpre_ref, layer, sub)
    g_post = _norm_row(gpost_ref, layer, sub)

    def chunk_copies(c, slot):
        cols = pl.ds(c * fc, fc)
        return (
            pltpu.make_async_copy(wg_hbm.at[layer, half, :, cols], st_g.at[slot], sem.at[0, slot]),
            pltpu.make_async_copy(wu_hbm.at[layer, half, :, cols], st_u.at[slot], sem.at[1, slot]),
            pltpu.make_async_copy(wd_hbm.at[layer, half, cols, :], st_d.at[slot], sem.at[2, slot]),
        )

    def run(stage_weights):
        if stage_weights:
            for cp in chunk_copies(0, 0):
                cp.start()
        x = _pick(x_refs, is_prompt)
        h = _sub_in(x, mods, g_pre, sub).astype(BF16)
        y = jnp.zeros(x.shape, F32)
        for c in range(n_chunks):
            sl = slice(c * fc, (c + 1) * fc)
            if stage_weights:
                slot = c % 2
                if c + 1 < n_chunks:
                    for cp in chunk_copies(c + 1, 1 - slot):
                        cp.start()
                for cp in chunk_copies(c, slot):
                    cp.wait()
                wg_bf[:, sl] = st_g[slot].astype(BF16)
                wu_bf[:, sl] = st_u[slot].astype(BF16)
                wd_bf[sl, :] = st_d[slot].astype(BF16)
            g = jnp.dot(h, wg_bf[:, sl], preferred_element_type=F32)
            u = jnp.dot(h, wu_bf[:, sl], preferred_element_type=F32)
            a = (_silu(g) * u).astype(BF16)
            y = y + jnp.dot(a, wd_bf[sl, :], preferred_element_type=F32)
        out = _sub_out(x, y, mods, g_post, sub, FFN_RESIDUAL)
        if n_o == 1:
            o_refs[0][...] = out
        else:
            @pl.when(is_prompt)
            def _store_prompt():
                o_refs[0][...] = out

            @pl.when(jnp.logical_not(is_prompt))
            def _store_latent():
                o_refs[1][...] = out

    @pl.when(step == 0)
    def _first_tile():
        run(True)

    @pl.when(step > 0)
    def _other_tiles():
        run(False)


def _ffn(xs, mods, norm_pre, norm_post, wg, wu, wd, *, sub, layer, half, group, np_tiles, split_out):
    d = xs[0].shape[-1]
    n_tok = sum(x.shape[0] for x in xs)
    d_ff = wg.shape[-1]
    tm = TOKEN_TILE
    fc = FFN_CHUNK
    n_chunks = d_ff // fc
    assert n_chunks * fc == d_ff and n_tok % tm == 0
    n_x, n_o = len(xs), (2 if split_out else 1)
    kern = functools.partial(_ffn_kernel, sub=sub, layer=layer, half=half, n_chunks=n_chunks,
                             n_x=n_x, n_o=n_o, np_tiles=np_tiles, group=group)
    whole = pl.BlockSpec((tm, d), lambda i: (i, 0))
    x_specs = _tile_specs(tm, d, np_tiles) if n_x == 2 else [whole]
    n_p = np_tiles * tm
    if split_out:
        out_specs = _tile_specs(tm, d, np_tiles)
        out_shape = [jax.ShapeDtypeStruct((n_p, d), F32), jax.ShapeDtypeStruct((n_tok - n_p, d), F32)]
    else:
        out_specs = [whole]
        out_shape = [jax.ShapeDtypeStruct((n_tok, d), F32)]
    return pl.pallas_call(
        kern,
        grid=(n_tok // tm,),
        in_specs=x_specs + _cond_specs(mods, norm_pre, norm_post, layer) + [
            pl.BlockSpec(memory_space=pl.ANY),
            pl.BlockSpec(memory_space=pl.ANY),
            pl.BlockSpec(memory_space=pl.ANY),
        ],
        out_specs=out_specs,
        out_shape=out_shape,
        scratch_shapes=[
            pltpu.VMEM((d, d_ff), BF16),
            pltpu.VMEM((d, d_ff), BF16),
            pltpu.VMEM((d_ff, d), BF16),
            pltpu.VMEM((2, d, fc), F32),
            pltpu.VMEM((2, d, fc), F32),
            pltpu.VMEM((2, fc, d), F32),
            pltpu.SemaphoreType.DMA((3, 2)),
        ],
        compiler_params=pltpu.CompilerParams(
            dimension_semantics=("arbitrary",), vmem_limit_bytes=VMEM_LIMIT_BYTES),
        name=f"ffn_l{layer}h{half}",
    )(*xs, mods, norm_pre, norm_post, wg, wu, wd)


def _log_sigmoid(x):
    return jnp.minimum(x, 0.0) - jnp.log1p(jnp.exp(-jnp.abs(x)))


def _evin_kernel(x_ref, mod_ref, gpre_ref, gpost_ref, w_ref, waf_ref, wab_ref, baf_ref, bab_ref,
                 qa_ref, ka_ref, va_ref, qb_ref, kb_ref, vb_ref, gb_ref, ldf_ref, ldb_ref,
                 w_bf, wa_bf, *, layer, group):
    @pl.when(pl.program_id(0) == 0)
    def _cast_weights():
        w_bf[0:EV_MAIN, :] = w_ref[0, 0:EV_MAIN, :].astype(BF16)
        w_bf[EV_MAIN:EV_MAIN + LANES, :] = jnp.zeros((LANES, w_bf.shape[1]), BF16)
        w_bf[EV_MAIN:EV_MAIN + 2 * GLA_RANK, :] = w_ref[0, EV_MAIN:EV_MAIN + 2 * GLA_RANK, :].astype(BF16)
        wa_bf[...] = jnp.zeros(wa_bf.shape, BF16)
        wa_bf[0:GLA_RANK, 0:B_QK] = waf_ref[0].astype(BF16)
        wa_bf[GLA_RANK:2 * GLA_RANK, B_QK:2 * B_QK] = wab_ref[0].astype(BF16)

    o = EV_OFFS
    mods = _mod_rows(mod_ref, group(pl.program_id(0)))
    g_pre = _norm_row(gpre_ref, layer, 1)

    def project(rows):
        h = _sub_in(x_ref[rows, :], mods, g_pre, 1).astype(BF16)
        return _dot_nt(h, w_bf[...])

    def finish(rows, full):
        qa_ref[rows, :] = (full[:, o[0]:o[1]] * (HEAD_DIM ** -0.5)).astype(qa_ref.dtype)
        ka_ref[rows, :] = full[:, o[1]:o[2]]
        va_ref[rows, :] = full[:, o[2]:o[3]]
        qb_ref[rows, :] = full[:, o[3]:o[4]] * (GLA_DK ** -0.5)
        kb_ref[rows, :] = full[:, o[4]:o[5]]
        vb_ref[rows, :] = full[:, o[5]:o[6]].astype(vb_ref.dtype)
        gb_ref[rows, :] = full[:, o[6]:o[7]].astype(gb_ref.dtype)
        lr = full[:, EV_MAIN:EV_MAIN + LANES].astype(BF16)
        logits = jnp.dot(lr, wa_bf[...], preferred_element_type=F32)
        ldf_ref[rows, :] = _log_sigmoid(logits[:, 0:B_QK] + baf_ref[...]) * (1.0 / GLA_TAU)
        ldb_ref[rows, :] = _log_sigmoid(logits[:, B_QK:2 * B_QK] + bab_ref[...]) * (1.0 / GLA_TAU)

    pending = None
    for rows in _sub_tiles(x_ref.shape[0]):
        full = project(rows)
        if pending is not None:
            finish(*pending)
        pending = (rows, full)
    finish(*pending)


def _even_in(x, mods, norm_pre, norm_post, w_in_t, wa_f, wa_b, ba_f, ba_b, *, layer, e, group):
    n_tok, d = x.shape
    tm = WIDE_TILE
    ev_in = w_in_t.shape[1]
    rank = wa_f.shape[1]
    assert rank == GLA_RANK and ev_in == EV_MAIN + 2 * rank
    widths = (A_Q, A_KV, A_KV, B_QK, B_QK, B_V, B_V, B_QK, B_QK)
    dtypes = (BF16, F32, F32, F32, F32, BF16, BF16, F32, F32)
    kern = functools.partial(_evin_kernel, layer=layer, group=group)
    return pl.pallas_call(
        kern,
        grid=(n_tok // tm,),
        in_specs=[pl.BlockSpec((tm, d), lambda i: (i, 0))] + _cond_specs(mods, norm_pre, norm_post, layer) + [
            pl.BlockSpec((1, ev_in, d), lambda i: (e, 0, 0), pipeline_mode=pl.Buffered(1)),
            pl.BlockSpec((1, rank, B_QK), lambda i: (e, 0, 0)),
            pl.BlockSpec((1, rank, B_QK), lambda i: (e, 0, 0)),
            pl.BlockSpec((1, B_QK), lambda i: (e, 0)),
            pl.BlockSpec((1, B_QK), lambda i: (e, 0)),
        ],
        out_specs=[pl.BlockSpec((tm, w), lambda i: (i, 0)) for w in widths],
        out_shape=[jax.ShapeDtypeStruct((n_tok, w), t) for w, t in zip(widths, dtypes)],
        scratch_shapes=[pltpu.VMEM((EV_MAIN + LANES, d), BF16), pltpu.VMEM((LANES, 2 * B_QK), BF16)],
        compiler_params=pltpu.CompilerParams(
            dimension_semantics=("arbitrary",), vmem_limit_bytes=VMEM_LIMIT_BYTES),
        name="even_in",
    )(x, mods, norm_pre, norm_post, w_in_t, wa_f, wa_b, ba_f, ba_b)


def _lane_lt(shape, n):
    return lax.broadcasted_iota(jnp.int32, shape, len(shape) - 1) < n


def _dup_kv_head(x, g):
    sw = pltpu.roll(x, HEAD_DIM, 1)
    lo = _lane_lt(x.shape, HEAD_DIM)
    return jnp.where(lo, x, sw) if g == 0 else jnp.where(lo, sw, x)


def _dot_nt(a, b):
    return lax.dot_general(a, b, (((1,), (1,)), ((), ())), preferred_element_type=F32)


ATTN_GROUP = ATTN_HEADS // ATTN_KV_HEADS


def _sink_column(sink_ref, g, t):
    rows = lax.broadcasted_iota(jnp.int32, (ATTN_GROUP * t, 1), 0)
    col = jnp.full((ATTN_GROUP * t, 1), sink_ref[ATTN_GROUP * g], F32)
    for j in range(1, ATTN_GROUP):
        col = jnp.where(rows >= j * t, sink_ref[ATTN_GROUP * g + j], col)
    return col


def _attn_group(q_pairs, sink_col, keys):
    t = q_pairs[0].shape[0]
    lo = _lane_lt((t, LANES), HEAD_DIM)
    zero = jnp.zeros((t, LANES), BF16)
    q4 = jnp.concatenate([jnp.where(lo, q_pairs[0], zero), jnp.where(lo, zero, q_pairs[0]),
                          jnp.where(lo, q_pairs[1], zero), jnp.where(lo, zero, q_pairs[1])], axis=0)
    scores = []
    mx = sink_col
    for k2, _, mask, feature_major in keys:
        s = jnp.dot(q4, k2, preferred_element_type=F32) if feature_major else _dot_nt(q4, k2)
        if mask is not None:
            s = jnp.where(mask, s, NEG_INF)
        scores.append(s)
        mx = jnp.maximum(mx, jnp.max(s, axis=-1, keepdims=True))
    den = jnp.exp(sink_col - mx)
    o = None
    for s, (_, v2, _, feature_major) in zip(scores, keys):
        p = jnp.exp(s - mx)
        den = den + jnp.sum(p, axis=-1, keepdims=True)
        pb = p.astype(BF16)
        pv = _dot_nt(pb, v2) if feature_major else jnp.dot(pb, v2, preferred_element_type=F32)
        o = pv if o is None else o + pv
    o = o / den
    return (jnp.where(lo, o[0:t], o[t:2 * t]), jnp.where(lo, o[2 * t:3 * t], o[3 * t:4 * t]))


def _attn_ctx_kernel(sink_ref, q_ref, k_ref, v_ref, o_ref, kt_ref, vt_ref):
    k = k_ref[...]
    v = v_ref[...]
    t = k.shape[0]
    kt_ref[0] = k.T
    vt_ref[0] = v.T
    for g in range(ATTN_KV_HEADS):
        keys = [(_dup_kv_head(k, g).astype(BF16), _dup_kv_head(v, g).astype(BF16), None, False)]
        cols = (2 * g, 2 * g + 1)
        outs = _attn_group([q_ref[:, m * LANES:(m + 1) * LANES] for m in cols],
                           _sink_column(sink_ref, g, t), keys)
        for m, o in zip(cols, outs):
            o_ref[:, m * LANES:(m + 1) * LANES] = o.astype(o_ref.dtype)


def _attn_context(sink, qa, ka, va, *, n_seq, seq):
    cache_spec = pl.BlockSpec((1, A_KV, seq), lambda b: (b, 0, 0))
    return pl.pallas_call(
        _attn_ctx_kernel,
        grid=(n_seq,),
        in_specs=[
            pl.BlockSpec(memory_space=pltpu.SMEM),
            pl.BlockSpec((seq, A_Q), lambda b: (b, 0)),
            pl.BlockSpec((seq, A_KV), lambda b: (b, 0)),
            pl.BlockSpec((seq, A_KV), lambda b: (b, 0)),
        ],
        out_specs=[pl.BlockSpec((seq, A_Q), lambda b: (b, 0)), cache_spec, cache_spec],
        out_shape=[jax.ShapeDtypeStruct((n_seq * seq, A_Q), BF16),
                   jax.ShapeDtypeStruct((n_seq, A_KV, seq), F32),
                   jax.ShapeDtypeStruct((n_seq, A_KV, seq), F32)],
        compiler_params=pltpu.CompilerParams(dimension_semantics=("parallel",)),
        name="attn_context",
    )(sink, qa, ka, va)


def _rope_tables(n_tok):
    n_rows = n_tok // GRID_W
    assert n_rows <= GRID_W
    shape = (GRID_W, LANES)
    pos = lax.broadcasted_iota(jnp.int32, shape, 0).astype(F32)
    lane = lax.broadcasted_iota(jnp.int32, shape, 1)
    half = HEAD_DIM // 2
    nf = half // 2
    within = lane & (HEAD_DIM - 1)
    is_col = within >= half
    second = (within & (half - 1)) >= nf
    f = (within & (nf - 1)).astype(F32)
    inv = jnp.exp(f * (-2.0 / half * math.log(ROPE_BASE)))
    ang = pos * inv
    cos_t = jnp.cos(ang)
    sin_t = jnp.sin(ang)
    sin_t = jnp.where(second, sin_t, -sin_t)

    def per_token(tab):
        return jnp.concatenate(
            [jnp.where(is_col, tab, jnp.broadcast_to(tab[r:r + 1, :], shape)) for r in range(n_rows)], axis=0)

    return per_token(cos_t), per_token(sin_t)


def _rope(x, cos, sin_signed):
    nf = HEAD_DIM // 4
    lane = lax.broadcasted_iota(jnp.int32, x.shape, 1)
    second = (lane & (2 * nf - 1)) >= nf
    up = pltpu.roll(x, nf, 1)
    dn = pltpu.roll(x, LANES - nf, 1)
    return x * cos + jnp.where(second, up, dn) * sin_signed


def _attn_lat_kernel(sink_ref, q_ref, k_ref, v_ref, kc_ref, vc_ref, o_ref, qr_ref, kp_ref, vp_ref,
                     *, seq):
    nb = seq // ATTN_BLOCK
    blk = ATTN_BLOCK
    cos, sin_signed = _rope_tables(seq)
    for m in range(A_Q // LANES):
        qr_ref[:, m * LANES:(m + 1) * LANES] = _rope(
            q_ref[:, m * LANES:(m + 1) * LANES].astype(F32), cos, sin_signed).astype(BF16)
    kr = _rope(k_ref[...], cos, sin_signed)
    v = v_ref[...]
    zpad = jnp.zeros((blk, LANES), BF16)
    ctx, sink_cols = [], []
    for g in range(ATTN_KV_HEADS):
        kp_ref[g, 0:blk, :] = zpad
        kp_ref[g, blk + seq:2 * blk + seq, :] = zpad
        vp_ref[g, 0:blk, :] = zpad
        vp_ref[g, blk + seq:2 * blk + seq, :] = zpad
        kp_ref[g, blk:blk + seq, :] = _dup_kv_head(kr, g).astype(BF16)
        vp_ref[g, blk:blk + seq, :] = _dup_kv_head(v, g).astype(BF16)
        hd = slice(g * HEAD_DIM, (g + 1) * HEAD_DIM)
        ctx.append((jnp.concatenate([kc_ref[0, hd, :]] * 2, axis=0).astype(BF16),
                    jnp.concatenate([vc_ref[0, hd, :]] * 2, axis=0).astype(BF16), None, True))
        sink_cols.append(_sink_column(sink_ref, g, blk))
    r = lax.broadcasted_iota(jnp.int32, (ATTN_GROUP * blk, 3 * blk), 0) & (blk - 1)
    c = lax.broadcasted_iota(jnp.int32, (ATTN_GROUP * blk, 3 * blk), 1)
    band = (c >= r) & (c <= r + 2 * WINDOW)

    def block_body(i, carry):
        row0 = pl.multiple_of(i * blk, blk)
        kj = (i - 1) * blk + c
        mask = band & (kj >= 0) & (kj < seq)
        for g in range(ATTN_KV_HEADS):
            cols = (2 * g, 2 * g + 1)
            window = (kp_ref[g, pl.ds(row0, 3 * blk), :], vp_ref[g, pl.ds(row0, 3 * blk), :], mask, False)
            outs = _attn_group([qr_ref[pl.ds(row0, blk), m * LANES:(m + 1) * LANES] for m in cols],
                               sink_cols[g], [window, ctx[g]])
            for m, o in zip(cols, outs):
                o_ref[pl.ds(row0, blk), m * LANES:(m + 1) * LANES] = o.astype(o_ref.dtype)
        return carry

    lax.fori_loop(0, nb, block_body, 0)


def _attn_latent(sink, qa, ka, va, k_ctx, v_ctx, *, n_seq, seq, row_block0):
    past = k_ctx.shape[2]
    kern = functools.partial(_attn_lat_kernel, seq=seq)
    return pl.pallas_call(
        kern,
        grid=(n_seq,),
        in_specs=[
            pl.BlockSpec(memory_space=pltpu.SMEM),
            pl.BlockSpec((seq, A_Q), lambda b: (row_block0 + b, 0)),
            pl.BlockSpec((seq, A_KV), lambda b: (row_block0 + b, 0)),
            pl.BlockSpec((seq, A_KV), lambda b: (row_block0 + b, 0)),
            pl.BlockSpec((1, A_KV, past), lambda b: (b, 0, 0)),
            pl.BlockSpec((1, A_KV, past), lambda b: (b, 0, 0)),
        ],
        out_specs=pl.BlockSpec((seq, A_Q), lambda b: (b, 0)),
        out_shape=jax.ShapeDtypeStruct((n_seq * seq, A_Q), BF16),
        scratch_shapes=[
            pltpu.VMEM((seq, A_Q), BF16),
            pltpu.VMEM((ATTN_KV_HEADS, seq + 2 * ATTN_BLOCK, LANES), BF16),
            pltpu.VMEM((ATTN_KV_HEADS, seq + 2 * ATTN_BLOCK, LANES), BF16),
        ],
        compiler_params=pltpu.CompilerParams(dimension_semantics=("parallel",)),
        name="attn_latent",
    )(sink, qa, ka, va, k_ctx, v_ctx)


def _same_block(a, b, size):
    return (a & -size) == (b & -size)


def _split3(x):
    hi = x.astype(BF16)
    r1 = x - hi.astype(F32)
    mid = r1.astype(BF16)
    lo = (r1 - mid.astype(F32)).astype(BF16)
    return hi, mid, lo


def _gla_tile(q, k, v, ld, st_ref, sel_ref, *, rev):
    t = GLA_TILE
    n_pair = GLA_HEADS // 2
    row = lax.broadcasted_iota(jnp.int32, (t, t), 0)
    col = lax.broadcasted_iota(jnp.int32, (t, t), 1)
    tri = (row <= col) if rev else (row >= col)
    tri = jnp.where(tri, 1.0, 0.0).astype(BF16)
    hi, mid, lo = _split3(ld * math.log2(math.e))
    b = (jnp.dot(tri, hi, preferred_element_type=F32) + jnp.dot(tri, mid, preferred_element_type=F32)
         + jnp.dot(tri, lo, preferred_element_type=F32))
    b_tot = b[0:1, :] if rev else b[t - 1:t, :]
    qs = (q * jnp.exp2(b)).astype(BF16)
    kd = (k * jnp.exp2(b_tot - b)).astype(BF16)

    rowv = lax.broadcasted_iota(jnp.int32, (t, B_QK), 0)
    levels = []
    half = GLA_SUB
    while 2 * half <= t:
        blk = 2 * half
        pieces = []
        for bs in range(0, t, blk):
            r = bs + half if rev else bs + half - 1
            pieces.append(jnp.broadcast_to(b[r:r + 1, :], (blk, B_QK)))
        ref = jnp.concatenate(pieces, axis=0) if len(pieces) > 1 else pieces[0]
        second = (rowv & (blk - 1)) >= half
        is_q = ~second if rev else second
        ql = (q * jnp.exp2(jnp.where(is_q, b - ref, -jnp.inf))).astype(BF16)
        kl = (k * jnp.exp2(jnp.where(is_q, -jnp.inf, ref - b))).astype(BF16)
        levels.append((blk, ql, kl))
        half = blk

    rr = lax.broadcasted_iota(jnp.int32, (t, 2 * t), 0)
    ss = lax.broadcasted_iota(jnp.int32, (t, 2 * t), 1) & (t - 1)
    sub_rows = lax.broadcasted_iota(jnp.int32, (GLA_SUB, LANES), 0)
    lane_lo = _lane_lt((t, LANES), GLA_DK)
    lane_lo2 = _lane_lt((2 * t, LANES), GLA_DK)
    par_rows = lax.broadcasted_iota(jnp.int32, (2 * t, LANES), 0) < t
    head_sel = par_rows == lane_lo2

    outs = []
    for p in range(n_pair):
        pl_ = slice(p * LANES, (p + 1) * LANES)
        qp, kp, bp = q[:, pl_], k[:, pl_], b[:, pl_]
        rows = []
        for sb in range(t // GLA_SUB):
            r0 = sb * GLA_SUB
            q_sub = qp[r0:r0 + GLA_SUB, :]
            b_sub = bp[r0:r0 + GLA_SUB, :]
            cols = []
            for j in range(GLA_SUB):
                kj = kp[r0 + j:r0 + j + 1, :]
                bj = bp[r0 + j:r0 + j + 1, :]
                valid = (sub_rows <= j) if rev else (sub_rows >= j)
                e = jnp.exp2(jnp.where(valid, b_sub - bj, -jnp.inf))
                cols.append(q_sub * kj * e)
            rows.append(jnp.concatenate(cols, axis=1))
        e_p = jnp.concatenate(rows, axis=0).astype(BF16)
        scores = jnp.dot(e_p, sel_ref[...], preferred_element_type=F32)
        scores = jnp.where(_same_block(rr, ss, GLA_SUB), scores, 0.0)
        for blk, ql, kl in levels:
            klp = kl[:, pl_]
            kstack = jnp.concatenate([klp, klp], axis=0)
            kstack = jnp.where(head_sel, kstack, jnp.zeros_like(kstack))
            s_l = _dot_nt(ql[:, pl_], kstack)
            scores = scores + jnp.where(_same_block(rr, ss, blk), s_l, 0.0)
        vp = v[:, p * 2 * GLA_DV:(p + 1) * 2 * GLA_DV]
        v_lo = _lane_lt(vp.shape, GLA_DV)
        vbd = jnp.concatenate([jnp.where(v_lo, vp, 0.0), jnp.where(v_lo, 0.0, vp)], axis=0).astype(BF16)
        st = st_ref[p]
        o_p = (jnp.dot(scores.astype(BF16), vbd, preferred_element_type=F32)
               + _dot_nt(qs[:, pl_], st.astype(BF16)))
        outs.append(o_p)
        upd = jnp.dot(vp.T.astype(BF16), kd[:, pl_], preferred_element_type=F32)
        st_rows_lo = lax.broadcasted_iota(jnp.int32, upd.shape, 0) < GLA_DV
        bd = st_rows_lo == _lane_lt(upd.shape, GLA_DK)
        st_ref[p] = st * jnp.exp2(b_tot[:, pl_]) + jnp.where(bd, upd, 0.0)
    return jnp.concatenate(outs, axis=1)


def _gla_kernel(*refs, seq, has_init, want_state):
    q_ref, k_ref, v_ref, ldf_ref, ldb_ref = refs[:5]
    pos = 5
    if has_init:
        s0f_ref, s0b_ref = refs[pos:pos + 2]
        pos += 2
    o_ref = refs[pos]
    pos += 1
    if want_state:
        sf_ref, sb_ref = refs[pos:pos + 2]
        pos += 2
    st_ref, sel_ref = refs[pos:pos + 2]
    t = GLA_TILE
    nt = seq // t
    n_pair = GLA_HEADS // 2

    kk = lax.broadcasted_iota(jnp.int32, (GLA_SUB * LANES, 2 * t), 0)
    nn = lax.broadcasted_iota(jnp.int32, (GLA_SUB * LANES, 2 * t), 1)
    sel = ((nn & (GLA_SUB - 1)) == (kk >> LANES_LOG2)) & (((kk & (LANES - 1)) >= GLA_DK) == (nn >= t))
    sel_ref[...] = jnp.where(sel, 1.0, 0.0).astype(BF16)

    def load_state(st, s0_ref):
        for p in range(n_pair):
            if s0_ref is None:
                st[p] = jnp.zeros(st.shape[1:], F32)
            else:
                x = jnp.concatenate([s0_ref[0, 2 * p], s0_ref[0, 2 * p + 1]], axis=0)
                xt = x.T
                lo = _lane_lt(xt.shape, GLA_DK)
                st[p] = jnp.concatenate([jnp.where(lo, xt, 0.0), jnp.where(lo, 0.0, xt)], axis=0)

    def store_state(st, out_ref):
        for p in range(n_pair):
            s = st[p]
            lo = _lane_lt((GLA_DV, LANES), GLA_DK)
            z = jnp.where(lo, s[0:GLA_DV, :], s[GLA_DV:2 * GLA_DV, :]).T
            out_ref[0, 2 * p] = z[0:GLA_DK, :]
            out_ref[0, 2 * p + 1] = z[GLA_DK:2 * GLA_DK, :]

    dirs = ((False, ldf_ref, st_ref.at[0]), (True, ldb_ref, st_ref.at[1]))
    for rev, _, st in dirs:
        load_state(st, (s0b_ref if rev else s0f_ref) if has_init else None)
    o_ref[...] = jnp.zeros(o_ref.shape, o_ref.dtype)

    def tile_body(ti, carry):
        for rev, ld_ref, st in dirs:
            tile = (nt - 1 - ti) if rev else ti
            r0 = pl.multiple_of(tile * t, t)
            rows = pl.ds(r0, t)
            o = _gla_tile(q_ref[rows, :], k_ref[rows, :], v_ref[rows, :].astype(F32), ld_ref[rows, :],
                          st, sel_ref, rev=rev)
            o_ref[rows, :] = o_ref[rows, :] + o
        return carry

    lax.fori_loop(0, nt, tile_body, 0)
    if want_state:
        for rev, _, st in dirs:
            store_state(st, sb_ref if rev else sf_ref)


def _gla(qb, kb, vb, ldf, ldb, s0f, s0b, *, n_seq, seq, row_block0, want_state):
    has_init = s0f is not None
    kern = functools.partial(_gla_kernel, seq=seq, has_init=has_init, want_state=want_state)
    tok = lambda w: pl.BlockSpec((seq, w), lambda b: (row_block0 + b, 0))
    in_specs = [tok(B_QK), tok(B_QK), tok(B_V), tok(B_QK), tok(B_QK)]
    args = [qb, kb, vb, ldf, ldb]
    state_spec = pl.BlockSpec((1, GLA_HEADS, GLA_DK, GLA_DV), lambda b: (b, 0, 0, 0))
    if has_init:
        in_specs += [state_spec, state_spec]
        args += [s0f, s0b]
    out_specs = [pl.BlockSpec((seq, B_V), lambda b: (b, 0))]
    out_shape = [jax.ShapeDtypeStruct((n_seq * seq, B_V), F32)]
    if want_state:
        out_specs += [state_spec, state_spec]
        out_shape += [jax.ShapeDtypeStruct((n_seq, GLA_HEADS, GLA_DK, GLA_DV), F32)] * 2
    return pl.pallas_call(
        kern,
        grid=(n_seq,),
        in_specs=in_specs,
        out_specs=out_specs,
        out_shape=out_shape,
        scratch_shapes=[
            pltpu.VMEM((2, GLA_HEADS // 2, 2 * GLA_DV, LANES), F32),
            pltpu.VMEM((GLA_SUB * LANES, 2 * GLA_TILE), BF16),
        ],
        compiler_params=pltpu.CompilerParams(
            dimension_semantics=("parallel",), vmem_limit_bytes=VMEM_LIMIT_BYTES),
        name="gla_state" if want_state else "gla_latent",
    )(*args)


def _evout_kernel(x_ref, mod_ref, gpre_ref, gpost_ref, attn_p_ref, attn_s_ref, gla_p_ref, gla_s_ref, gb_ref,
                  gn_ref, w_ref, o_ref, w_bf, *, np_tiles, layer, group):
    @pl.when(pl.program_id(0) == 0)
    def _cast_weights():
        w_bf[...] = w_ref[0].astype(BF16)

    is_prompt = pl.program_id(0) < np_tiles
    mods = _mod_rows(mod_ref, group(pl.program_id(0)))
    g_post = _norm_row(gpost_ref, layer, 1)
    for rows in _sub_tiles(x_ref.shape[0]):
        attn = _pick((attn_p_ref, attn_s_ref), is_prompt, rows)
        gla = _pick((gla_p_ref, gla_s_ref), is_prompt, rows)
        parts = [attn]
        for hd in range(GLA_HEADS):
            sl = slice(hd * GLA_DV, (hd + 1) * GLA_DV)
            g = _rms(gla[:, sl], gn_ref[:, sl]) * _silu(gb_ref[rows, sl].astype(F32))
            parts.append(g.astype(BF16))
        y = jnp.dot(jnp.concatenate(parts, axis=1), w_bf[...], preferred_element_type=F32)
        o_ref[rows, :] = _sub_out(x_ref[rows, :], y, mods, g_post, 1, 1.0)


def _even_out(x, mods, norm_pre, norm_post, attn_ps, gla_ps, gb, gla_norm, w_out, *, layer, e, group,
              np_tiles):
    n_tok, d = x.shape
    tm = WIDE_TILE
    ev_out = w_out.shape[1]
    kern = functools.partial(_evout_kernel, np_tiles=np_tiles, layer=layer, group=group)
    return pl.pallas_call(
        kern,
        grid=(n_tok // tm,),
        in_specs=[pl.BlockSpec((tm, d), lambda i: (i, 0))] + _cond_specs(mods, norm_pre, norm_post, layer) + [
            *_tile_specs(tm, A_Q, np_tiles),
            *_tile_specs(tm, B_V, np_tiles),
            pl.BlockSpec((tm, B_V), lambda i: (i, 0)),
            pl.BlockSpec((1, B_V), lambda i: (e, 0)),
            pl.BlockSpec((1, ev_out, d), lambda i: (e, 0, 0), pipeline_mode=pl.Buffered(1)),
        ],
        out_specs=pl.BlockSpec((tm, d), lambda i: (i, 0)),
        out_shape=jax.ShapeDtypeStruct((n_tok, d), F32),
        scratch_shapes=[pltpu.VMEM((ev_out, d), BF16)],
        compiler_params=pltpu.CompilerParams(
            dimension_semantics=("arbitrary",), vmem_limit_bytes=VMEM_LIMIT_BYTES),
        name="even_out",
    )(x, mods, norm_pre, norm_post, *attn_ps, *gla_ps, gb, gla_norm, w_out)


def _gelu(x):
    return 0.5 * x * (1.0 + lax.erf(x * (2.0 ** -0.5)))


def _cm_kernel(x_ref, mod_ref, gpre_ref, gpost_ref, win_ref, vg_ref, vb_ref, ws_ref, bs_ref, wout_ref,
               o_ref, win_bf, wout_bf, *, layer, group):
    @pl.when(pl.program_id(0) == 0)
    def _cast_weights():
        win_bf[...] = win_ref[0].astype(BF16)
        wout_bf[...] = wout_ref[0].astype(BF16)

    width = wout_bf.shape[0]
    gw = width // CMLP_GROUPS
    ws = [ws_ref[0, g].astype(BF16) for g in range(CMLP_GROUPS)]
    mods = _mod_rows(mod_ref, group(pl.program_id(0)))
    g_pre = _norm_row(gpre_ref, layer, 1)
    g_post = _norm_row(gpost_ref, layer, 1)
    eye = (lax.broadcasted_iota(jnp.int32, (CHUNK, CHUNK), 0)
           == lax.broadcasted_iota(jnp.int32, (CHUNK, CHUNK), 1))
    bias_cols = [jnp.sum(jnp.where(eye, jnp.broadcast_to(bs_ref[0, g:g + 1, :], (CHUNK, CHUNK)), 0.0),
                         axis=1, keepdims=True) for g in range(CMLP_GROUPS)]

    def phase_in(rows):
        x = x_ref[rows, :]
        h = _sub_in(x, mods, g_pre, 1).astype(BF16)
        v = _gelu(jnp.dot(h, win_bf[:, width:2 * width], preferred_element_type=F32))
        u = _gelu(jnp.dot(h, win_bf[:, 0:width], preferred_element_type=F32))
        return x, u, v

    def phase_mix(x, u, v):
        mu = jnp.mean(v, axis=-1, keepdims=True)
        vc = v - mu
        var = jnp.mean(vc * vc, axis=-1, keepdims=True)
        vn = (vc * lax.rsqrt(var + EPS) * vg_ref[...] + vb_ref[...]).astype(BF16)
        chunk_rows = []
        for c in range(x.shape[0] // CHUNK):
            cols = []
            for g in range(CMLP_GROUPS):
                blk = vn[c * CHUNK:(c + 1) * CHUNK, g * gw:(g + 1) * gw]
                cols.append(jnp.dot(ws[g], blk, preferred_element_type=F32) + bias_cols[g])
            chunk_rows.append(jnp.concatenate(cols, axis=1))
        mixed = jnp.concatenate(chunk_rows, axis=0)
        return x, (u * mixed).astype(BF16)

    def phase_out(rows, x, m):
        y = jnp.dot(m, wout_bf[...], preferred_element_type=F32)
        o_ref[rows, :] = _sub_out(x, y, mods, g_post, 1, 1.0)

    tiles = _sub_tiles(x_ref.shape[0])
    pending = None
    for rows in tiles:
        cur = phase_in(rows)
        if pending is not None:
            phase_out(pending[0], *phase_mix(*pending[1]))
        pending = (rows, cur)
    phase_out(pending[0], *phase_mix(*pending[1]))


def _chunk_mlp(x, mods, norm_pre, norm_post, w_in, v_gain, v_bias, w_s, b_s, w_out, *, layer, o, group):
    n_tok, d = x.shape
    tm = WIDE_TILE
    width = w_out.shape[1]
    kern = functools.partial(_cm_kernel, layer=layer, group=group)
    return pl.pallas_call(
        kern,
        grid=(n_tok // tm,),
        in_specs=[pl.BlockSpec((tm, d), lambda i: (i, 0))] + _cond_specs(mods, norm_pre, norm_post, layer) + [
            pl.BlockSpec((1, d, 2 * width), lambda i: (o, 0, 0), pipeline_mode=pl.Buffered(1)),
            pl.BlockSpec((1, width), lambda i: (o, 0)),
            pl.BlockSpec((1, width), lambda i: (o, 0)),
            pl.BlockSpec((1, CMLP_GROUPS, CHUNK, CHUNK), lambda i: (o, 0, 0, 0)),
            pl.BlockSpec((1, CMLP_GROUPS, CHUNK), lambda i: (o, 0, 0)),
            pl.BlockSpec((1, width, d), lambda i: (o, 0, 0), pipeline_mode=pl.Buffered(1)),
        ],
        out_specs=pl.BlockSpec((tm, d), lambda i: (i, 0)),
        out_shape=jax.ShapeDtypeStruct((n_tok, d), F32),
        scratch_shapes=[pltpu.VMEM((d, 2 * width), BF16), pltpu.VMEM((width, d), BF16)],
        compiler_params=pltpu.CompilerParams(
            dimension_semantics=("arbitrary",), vmem_limit_bytes=VMEM_LIMIT_BYTES),
        name="chunk_mlp",
    )(x, mods, norm_pre, norm_post, w_in, v_gain, v_bias, w_s, b_s, w_out)


def kernel(x_prompt, x_sample, cache_k, cache_v, state_gla_fwd, state_gla_bwd, c, c_ctx, w_mod, b_mod, norm_pre, norm_post, ffn_w_gate, ffn_w_up, ffn_w_down, ev_w_in, ev_w_out, ev_sink, gla_wa_f, gla_ba_f, gla_wa_b, gla_ba_b, gla_norm, cm_w_in, cm_v_gain, cm_v_bias, cm_w_s, cm_b_s, cm_w_out):
    batch, seq, d = x_prompt.shape
    dec_batch, dec_seq, _ = x_sample.shape
    depth = w_mod.shape[0]
    n_prompt_tok = batch * seq
    n_sample_tok = dec_batch * dec_seq
    assert n_prompt_tok % dec_seq == 0 and dec_seq % TOKEN_TILE == 0
    group = _group_index_map(TOKEN_TILE, n_prompt_tok, dec_seq)
    np_tiles = n_prompt_tok // TOKEN_TILE
    ffn = functools.partial(_ffn, group=group, np_tiles=np_tiles)
    assert dec_seq % WIDE_TILE == 0
    group_w = _group_index_map(WIDE_TILE, n_prompt_tok, dec_seq)
    np_tiles_w = n_prompt_tok // WIDE_TILE

    xs = (x_prompt.reshape(n_prompt_tok, d), x_sample.reshape(n_sample_tok, d))
    ev_w_in_t = jnp.swapaxes(ev_w_in, 1, 2)

    mods = _adaln_mods(c_ctx, c, w_mod, b_mod)
    gains_pre = norm_pre.reshape(depth * N_SUB, d)
    gains_post = norm_post.reshape(depth * N_SUB, d)
    cond = (mods, gains_pre, gains_post)

    new_k, new_v, new_sf, new_sb = [], [], [], []
    for layer in range(depth):
        (x,) = ffn(xs, *cond, ffn_w_gate, ffn_w_up, ffn_w_down,
                   sub=0, layer=layer, half=0, split_out=False)
        if layer % 2 == 0:
            e = layer // 2
            qa, ka, va, qb, kb, vb, gb, ldf, ldb = _even_in(
                x, *cond, ev_w_in_t, gla_wa_f, gla_wa_b, gla_ba_f, gla_ba_b,
                layer=layer, e=e, group=group_w)
            sink = ev_sink[e]
            attn_p, k_t, v_t = _attn_context(sink, qa, ka, va, n_seq=batch, seq=seq)
            k_ctx = jnp.transpose(cache_k[:, e], (0, 2, 3, 1)).reshape(dec_batch, A_KV, -1)
            v_ctx = jnp.transpose(cache_v[:, e], (0, 2, 3, 1)).reshape(dec_batch, A_KV, -1)
            attn_s = _attn_latent(sink, qa, ka, va, k_ctx, v_ctx, n_seq=dec_batch, seq=dec_seq,
                                  row_block0=n_prompt_tok // dec_seq)
            gla_p, s_f, s_b = _gla(qb, kb, vb, ldf, ldb, None, None, n_seq=batch, seq=seq,
                                   row_block0=0, want_state=True)
            (gla_s,) = _gla(qb, kb, vb, ldf, ldb, state_gla_fwd[:, e], state_gla_bwd[:, e],
                            n_seq=dec_batch, seq=dec_seq, row_block0=n_prompt_tok // dec_seq,
                            want_state=False)
            x = _even_out(x, *cond, (attn_p, attn_s), (gla_p, gla_s), gb, gla_norm, ev_w_out,
                          layer=layer, e=e, group=group_w, np_tiles=np_tiles_w)
            for cache_t, dst in ((k_t, new_k), (v_t, new_v)):
                dst.append(jnp.transpose(cache_t.reshape(batch, ATTN_KV_HEADS, HEAD_DIM, seq), (0, 3, 1, 2)))
            new_sf.append(s_f)
            new_sb.append(s_b)
        else:
            o = layer // 2
            x = _chunk_mlp(x, *cond, cm_w_in, cm_v_gain, cm_v_bias, cm_w_s, cm_b_s, cm_w_out,
                           layer=layer, o=o, group=group_w)
        xs = ffn((x,), *cond, ffn_w_gate, ffn_w_up, ffn_w_down,
                 sub=2, layer=layer, half=1, split_out=(layer == depth - 1))

    y_prompt = xs[0].reshape(batch, seq, d)
    y_sample = xs[1].reshape(dec_batch, dec_seq, d)
    return (y_prompt, y_sample, jnp.stack(new_k, axis=1), jnp.stack(new_v, axis=1),
            jnp.stack(new_sf, axis=1), jnp.stack(new_sb, axis=1))
```

```python
import functools
import math

import jax
import jax.numpy as jnp
from jax import lax
from jax.experimental import pallas as pl
from jax.experimental.pallas import tpu as pltpu

F32 = jnp.float32
BF16 = jnp.bfloat16

EPS = 1e-6
NEG_INF = -1e30
FFN_RESIDUAL = 0.5
N_SUB = 3

ATTN_HEADS = 8
ATTN_KV_HEADS = 2
HEAD_DIM = 64
ATTN_BLOCK = 128
WINDOW = 128
GRID_W = 64
GRID_W_LOG2 = 6
ROPE_BASE = 10000.0
GLA_HEADS = 4
GLA_DK = 64
GLA_DV = 128
GLA_RANK = 16
GLA_TAU = 16.0
CHUNK = 128
CMLP_GROUPS = 4

A_Q = ATTN_HEADS * HEAD_DIM
A_KV = ATTN_KV_HEADS * HEAD_DIM
B_QK = GLA_HEADS * GLA_DK
B_V = GLA_HEADS * GLA_DV
EV_MAIN = A_Q + 2 * A_KV + 2 * B_QK + 2 * B_V
EV_OFFS = (0, A_Q, A_Q + A_KV, A_Q + 2 * A_KV, A_Q + 2 * A_KV + B_QK,
           A_Q + 2 * A_KV + 2 * B_QK, A_Q + 2 * A_KV + 2 * B_QK + B_V, EV_MAIN)

LANES = 128
LANES_LOG2 = 7
SUBLANES = 8
VMEM_LIMIT_BYTES = 56 * 1024 * 1024

TOKEN_TILE = 512
WIDE_TILE = 1024
SUB_ROWS = 512
ADALN_COLS = 3072
FFN_CHUNK = 256
GLA_TILE = 128
GLA_SUB = 8


def _group_index_map(tile_rows, n_prompt_tok, seq_tok):
    def group(i):
        return jnp.maximum(i * tile_rows - (n_prompt_tok - seq_tok), 0) // seq_tok
    return group


def _sub_tiles(n_rows):
    assert n_rows % SUB_ROWS == 0
    return [slice(r, r + SUB_ROWS) for r in range(0, n_rows, SUB_ROWS)]


def _rms(x, g):
    ms = jnp.mean(x * x, axis=-1, keepdims=True)
    return x * lax.rsqrt(ms + EPS) * g


def _mod_rows(mod_ref, g):
    return [mod_ref[k, pl.ds(g, 1), :] for k in range(3 * N_SUB)]


def _sub_in(x, mods, g_pre, sub):
    shift, scale = mods[3 * sub + 0], mods[3 * sub + 1]
    return _rms(x, g_pre) * (1.0 + scale) + shift


def _sub_out(x, y, mods, g_post, sub, coef):
    return x + (coef * mods[3 * sub + 2]) * _rms(y, g_post)


def _norm_row(ref, layer, sub):
    r = layer * N_SUB + sub
    return ref[r:r + 1, :]


def _silu(x):
    return x * jax.nn.sigmoid(x)


def _mod_kernel(cctx_ref, c_ref, w_ref, b_ref, o_ref):
    _, n_kinds, rows, d = o_ref.shape
    r = lax.broadcasted_iota(jnp.int32, (rows, d), 0)
    cond = jnp.where(r == 0, jnp.broadcast_to(cctx_ref[...], (rows, d)), 0.0)
    for j in range(c_ref.shape[0]):
        cond = jnp.where(r == 1 + j, jnp.broadcast_to(c_ref[j:j + 1, :], (rows, d)), cond)
    s = _silu(cond).astype(BF16)
    bias = b_ref[pl.ds(pl.program_id(0), 1), :]
    res = jnp.dot(s, w_ref[0].astype(BF16), preferred_element_type=F32) + bias
    for q in range(n_kinds):
        o_ref[0, q] = res[:, q * d:(q + 1) * d]


def _adaln_mods(c_ctx, c, w_mod, b_mod):
    depth, d, n = w_mod.shape
    tn = ADALN_COLS
    assert n % tn == 0 and tn % d == 0
    rows = -(-(1 + c.shape[0]) // SUBLANES) * SUBLANES
    return pl.pallas_call(
        _mod_kernel,
        grid=(depth, n // tn),
        in_specs=[
            pl.BlockSpec((1, d), lambda l, j: (0, 0)),
            pl.BlockSpec(c.shape, lambda l, j: (0, 0)),
            pl.BlockSpec((1, d, tn), lambda l, j: (l, 0, j)),
            pl.BlockSpec((depth, tn), lambda l, j: (0, j)),
        ],
        out_specs=pl.BlockSpec((1, tn // d, rows, d), lambda l, j: (l, j, 0, 0)),
        out_shape=jax.ShapeDtypeStruct((depth, n // d, rows, d), F32),
        compiler_params=pltpu.CompilerParams(
            dimension_semantics=("parallel", "parallel"), vmem_limit_bytes=VMEM_LIMIT_BYTES),
        name="adaln_mods",
    )(c_ctx.reshape(1, d), c, w_mod, b_mod)


def _tile_specs(tm, width, np_tiles):
    return [pl.BlockSpec((tm, width), lambda i: (jnp.minimum(i, np_tiles - 1), 0)),
            pl.BlockSpec((tm, width), lambda i: (jnp.maximum(i - np_tiles, 0), 0))]


def _pick(refs, is_prompt, rows=slice(None)):
    if len(refs) == 1:
        return refs[0][rows, :]
    return jnp.where(is_prompt, refs[0][rows, :], refs[1][rows, :])


def _cond_specs(mods, norm_pre, norm_post, layer):
    return [pl.BlockSpec((None,) + mods.shape[1:], lambda i: (layer, 0, 0, 0)),
            pl.BlockSpec(norm_pre.shape, lambda i: (0, 0)),
            pl.BlockSpec(norm_post.shape, lambda i: (0, 0))]


def _ffn_kernel(*refs, sub, layer, half, n_chunks, n_x, n_o, np_tiles, group):
    x_refs = refs[:n_x]
    mod_ref, gpre_ref, gpost_ref, wg_hbm, wu_hbm, wd_hbm = refs[n_x:n_x + 6]
    o_refs = refs[n_x + 6:n_x + 6 + n_o]
    wg_bf, wu_bf, wd_bf, st_g, st_u, st_d, sem = refs[n_x + 6 + n_o:]
    fc = FFN_CHUNK
    step = pl.program_id(0)
    is_prompt = step < np_tiles
    mods = _mod_rows(mod_ref, group(step))
    g_pre = _norm_row(gpre_ref, layer, sub)
    g_post = _norm_row(gpost_ref, layer, sub)

    def chunk_copies(c, slot):
        cols = pl.ds(c * fc, fc)
        return (
            pltpu.make_async_copy(wg_hbm.at[layer, half, :, cols], st_g.at[slot], sem.at[0, slot]),
            pltpu.make_async_copy(wu_hbm.at[layer, half, :, cols], st_u.at[slot], sem.at[1, slot]),
            pltpu.make_async_copy(wd_hbm.at[layer, half, cols, :], st_d.at[slot], sem.at[2, slot]),
        )

    def run(stage_weights):
        if stage_weights:
            for cp in chunk_copies(0, 0):
                cp.start()
        x = _pick(x_refs, is_prompt)
        h = _sub_in(x, mods, g_pre, sub).astype(BF16)
        y = jnp.zeros(x.shape, F32)
        for c in range(n_chunks):
            sl = slice(c * fc, (c + 1) * fc)
            if stage_weights:
                slot = c % 2
                if c + 1 < n_chunks:
                    for cp in chunk_copies(c + 1, 1 - slot):
                        cp.start()
                for cp in chunk_copies(c, slot):
                    cp.wait()
                wg_bf[:, sl] = st_g[slot].astype(BF16)
                wu_bf[:, sl] = st_u[slot].astype(BF16)
                wd_bf[sl, :] = st_d[slot].astype(BF16)
            g = jnp.dot(h, wg_bf[:, sl], preferred_element_type=F32)
            u = jnp.dot(h, wu_bf[:, sl], preferred_element_type=F32)
            a = (_silu(g) * u).astype(BF16)
            y = y + jnp.dot(a, wd_bf[sl, :], preferred_element_type=F32)
        out = _sub_out(x, y, mods, g_post, sub, FFN_RESIDUAL)
        if n_o == 1:
            o_refs[0][...] = out
        else:
            @pl.when(is_prompt)
            def _store_prompt():
                o_refs[0][...] = out

            @pl.when(jnp.logical_not(is_prompt))
            def _store_latent():
                o_refs[1][...] = out

    @pl.when(step == 0)
    def _first_tile():
        run(True)

    @pl.when(step > 0)
    def _other_tiles():
        run(False)


def _ffn(xs, mods, norm_pre, norm_post, wg, wu, wd, *, sub, layer, half, group, np_tiles, split_out):
    d = xs[0].shape[-1]
    n_tok = sum(x.shape[0] for x in xs)
    d_ff = wg.shape[-1]
    tm = TOKEN_TILE
    fc = FFN_CHUNK
    n_chunks = d_ff // fc
    assert n_chunks * fc == d_ff and n_tok % tm == 0
    n_x, n_o = len(xs), (2 if split_out else 1)
    kern = functools.partial(_ffn_kernel, sub=sub, layer=layer, half=half, n_chunks=n_chunks,
                             n_x=n_x, n_o=n_o, np_tiles=np_tiles, group=group)
    whole = pl.BlockSpec((tm, d), lambda i: (i, 0))
    x_specs = _tile_specs(tm, d, np_tiles) if n_x == 2 else [whole]
    n_p = np_tiles * tm
    if split_out:
        out_specs = _tile_specs(tm, d, np_tiles)
        out_shape = [jax.ShapeDtypeStruct((n_p, d), F32), jax.ShapeDtypeStruct((n_tok - n_p, d), F32)]
    else:
        out_specs = [whole]
        out_shape = [jax.ShapeDtypeStruct((n_tok, d), F32)]
    return pl.pallas_call(
        kern,
        grid=(n_tok // tm,),
        in_specs=x_specs + _cond_specs(mods, norm_pre, norm_post, layer) + [
            pl.BlockSpec(memory_space=pl.ANY),
            pl.BlockSpec(memory_space=pl.ANY),
            pl.BlockSpec(memory_space=pl.ANY),
        ],
        out_specs=out_specs,
        out_shape=out_shape,
        scratch_shapes=[
            pltpu.VMEM((d, d_ff), BF16),
            pltpu.VMEM((d, d_ff), BF16),
            pltpu.VMEM((d_ff, d), BF16),
            pltpu.VMEM((2, d, fc), F32),
            pltpu.VMEM((2, d, fc), F32),
            pltpu.VMEM((2, fc, d), F32),
            pltpu.SemaphoreType.DMA((3, 2)),
        ],
        compiler_params=pltpu.CompilerParams(
            dimension_semantics=("arbitrary",), vmem_limit_bytes=VMEM_LIMIT_BYTES),
        name=f"ffn_l{layer}h{half}",
    )(*xs, mods, norm_pre, norm_post, wg, wu, wd)


def _log_sigmoid(x):
    return jnp.minimum(x, 0.0) - jnp.log1p(jnp.exp(-jnp.abs(x)))


def _evin_kernel(x_ref, mod_ref, gpre_ref, gpost_ref, w_ref, waf_ref, wab_ref, baf_ref, bab_ref,
                 qa_ref, ka_ref, va_ref, qb_ref, kb_ref, vb_ref, gb_ref, ldf_ref, ldb_ref,
                 w_bf, wa_bf, *, layer, group):
    @pl.when(pl.program_id(0) == 0)
    def _cast_weights():
        w_bf[0:EV_MAIN, :] = w_ref[0, 0:EV_MAIN, :].astype(BF16)
        w_bf[EV_MAIN:EV_MAIN + LANES, :] = jnp.zeros((LANES, w_bf.shape[1]), BF16)
        w_bf[EV_MAIN:EV_MAIN + 2 * GLA_RANK, :] = w_ref[0, EV_MAIN:EV_MAIN + 2 * GLA_RANK, :].astype(BF16)
        wa_bf[...] = jnp.zeros(wa_bf.shape, BF16)
        wa_bf[0:GLA_RANK, 0:B_QK] = waf_ref[0].astype(BF16)
        wa_bf[GLA_RANK:2 * GLA_RANK, B_QK:2 * B_QK] = wab_ref[0].astype(BF16)

    o = EV_OFFS
    mods = _mod_rows(mod_ref, group(pl.program_id(0)))
    g_pre = _norm_row(gpre_ref, layer, 1)

    def project(rows):
        h = _sub_in(x_ref[rows, :], mods, g_pre, 1).astype(BF16)
        return _dot_nt(h, w_bf[...])

    def finish(rows, full):
        qa_ref[rows, :] = (full[:, o[0]:o[1]] * (HEAD_DIM ** -0.5)).astype(qa_ref.dtype)
        ka_ref[rows, :] = full[:, o[1]:o[2]]
        va_ref[rows, :] = full[:, o[2]:o[3]]
        qb_ref[rows, :] = full[:, o[3]:o[4]] * (GLA_DK ** -0.5)
        kb_ref[rows, :] = full[:, o[4]:o[5]]
        vb_ref[rows, :] = full[:, o[5]:o[6]].astype(vb_ref.dtype)
        gb_ref[rows, :] = full[:, o[6]:o[7]].astype(gb_ref.dtype)
        lr = full[:, EV_MAIN:EV_MAIN + LANES].astype(BF16)
        logits = jnp.dot(lr, wa_bf[...], preferred_element_type=F32)
        ldf_ref[rows, :] = _log_sigmoid(logits[:, 0:B_QK] + baf_ref[...]) * (1.0 / GLA_TAU)
        ldb_ref[rows, :] = _log_sigmoid(logits[:, B_QK:2 * B_QK] + bab_ref[...]) * (1.0 / GLA_TAU)

    pending = None
    for rows in _sub_tiles(x_ref.shape[0]):
        full = project(rows)
        if pending is not None:
            finish(*pending)
        pending = (rows, full)
    finish(*pending)


def _even_in(x, mods, norm_pre, norm_post, w_in_t, wa_f, wa_b, ba_f, ba_b, *, layer, e, group):
    n_tok, d = x.shape
    tm = WIDE_TILE
    ev_in = w_in_t.shape[1]
    rank = wa_f.shape[1]
    assert rank == GLA_RANK and ev_in == EV_MAIN + 2 * rank
    widths = (A_Q, A_KV, A_KV, B_QK, B_QK, B_V, B_V, B_QK, B_QK)
    dtypes = (BF16, F32, F32, F32, F32, BF16, BF16, F32, F32)
    kern = functools.partial(_evin_kernel, layer=layer, group=group)
    return pl.pallas_call(
        kern,
        grid=(n_tok // tm,),
        in_specs=[pl.BlockSpec((tm, d), lambda i: (i, 0))] + _cond_specs(mods, norm_pre, norm_post, layer) + [
            pl.BlockSpec((1, ev_in, d), lambda i: (e, 0, 0), pipeline_mode=pl.Buffered(1)),
            pl.BlockSpec((1, rank, B_QK), lambda i: (e, 0, 0)),
            pl.BlockSpec((1, rank, B_QK), lambda i: (e, 0, 0)),
            pl.BlockSpec((1, B_QK), lambda i: (e, 0)),
            pl.BlockSpec((1, B_QK), lambda i: (e, 0)),
        ],
        out_specs=[pl.BlockSpec((tm, w), lambda i: (i, 0)) for w in widths],
        out_shape=[jax.ShapeDtypeStruct((n_tok, w), t) for w, t in zip(widths, dtypes)],
        scratch_shapes=[pltpu.VMEM((EV_MAIN + LANES, d), BF16), pltpu.VMEM((LANES, 2 * B_QK), BF16)],
        compiler_params=pltpu.CompilerParams(
            dimension_semantics=("arbitrary",), vmem_limit_bytes=VMEM_LIMIT_BYTES),
        name="even_in",
    )(x, mods, norm_pre, norm_post, w_in_t, wa_f, wa_b, ba_f, ba_b)


def _lane_lt(shape, n):
    return lax.broadcasted_iota(jnp.int32, shape, len(shape) - 1) < n


def _dup_kv_head(x, g):
    sw = pltpu.roll(x, HEAD_DIM, 1)
    lo = _lane_lt(x.shape, HEAD_DIM)
    return jnp.where(lo, x, sw) if g == 0 else jnp.where(lo, sw, x)


def _dot_nt(a, b):
    return lax.dot_general(a, b, (((1,), (1,)), ((), ())), preferred_element_type=F32)


ATTN_GROUP = ATTN_HEADS // ATTN_KV_HEADS


def _sink_column(sink_ref, g, t):
    rows = lax.broadcasted_iota(jnp.int32, (ATTN_GROUP * t, 1), 0)
    col = jnp.full((ATTN_GROUP * t, 1), sink_ref[ATTN_GROUP * g], F32)
    for j in range(1, ATTN_GROUP):
        col = jnp.where(rows >= j * t, sink_ref[ATTN_GROUP * g + j], col)
    return col


def _attn_scores(q_pairs, keys):
    t = q_pairs[0].shape[0]
    lo = _lane_lt((t, LANES), HEAD_DIM)
    zero = jnp.zeros((t, LANES), BF16)
    q4 = jnp.concatenate([jnp.where(lo, q_pairs[0], zero), jnp.where(lo, zero, q_pairs[0]),
                          jnp.where(lo, q_pairs[1], zero), jnp.where(lo, zero, q_pairs[1])], axis=0)
    scores = []
    for k2, _, mask, feature_major in keys:
        s = jnp.dot(q4, k2, preferred_element_type=F32) if feature_major else _dot_nt(q4, k2)
        if mask is not None:
            s = jnp.where(mask, s, NEG_INF)
        scores.append(s)
    return scores


def _attn_finish(scores, sink_col, keys):
    t = scores[0].shape[0] // ATTN_GROUP
    lo = _lane_lt((t, LANES), HEAD_DIM)
    mx = sink_col
    for s in scores:
        mx = jnp.maximum(mx, jnp.max(s, axis=-1, keepdims=True))
    den = jnp.exp(sink_col - mx)
    o = None
    for s, (_, v2, _, feature_major) in zip(scores, keys):
        p = jnp.exp(s - mx)
        den = den + jnp.sum(p, axis=-1, keepdims=True)
        pb = p.astype(BF16)
        pv = _dot_nt(pb, v2) if feature_major else jnp.dot(pb, v2, preferred_element_type=F32)
        o = pv if o is None else o + pv
    o = o / den
    return (jnp.where(lo, o[0:t], o[t:2 * t]), jnp.where(lo, o[2 * t:3 * t], o[3 * t:4 * t]))


def _attn_ctx_kernel(sink_ref, q_ref, k_ref, v_ref, o_ref, kt_ref, vt_ref):
    k = k_ref[...]
    v = v_ref[...]
    t = k.shape[0]
    kt_ref[0] = k.T
    vt_ref[0] = v.T
    work = []
    for g in range(ATTN_KV_HEADS):
        keys = [(_dup_kv_head(k, g).astype(BF16), _dup_kv_head(v, g).astype(BF16), None, False)]
        cols = (2 * g, 2 * g + 1)
        work.append((cols, _attn_scores([q_ref[:, m * LANES:(m + 1) * LANES] for m in cols], keys), keys, g))
    for cols, scores, keys, g in work:
        outs = _attn_finish(scores, _sink_column(sink_ref, g, t), keys)
        for m, o in zip(cols, outs):
            o_ref[:, m * LANES:(m + 1) * LANES] = o.astype(o_ref.dtype)


def _attn_context(sink, qa, ka, va, *, n_seq, seq):
    cache_spec = pl.BlockSpec((1, A_KV, seq), lambda b: (b, 0, 0))
    return pl.pallas_call(
        _attn_ctx_kernel,
        grid=(n_seq,),
        in_specs=[
            pl.BlockSpec(memory_space=pltpu.SMEM),
            pl.BlockSpec((seq, A_Q), lambda b: (b, 0)),
            pl.BlockSpec((seq, A_KV), lambda b: (b, 0)),
            pl.BlockSpec((seq, A_KV), lambda b: (b, 0)),
        ],
        out_specs=[pl.BlockSpec((seq, A_Q), lambda b: (b, 0)), cache_spec, cache_spec],
        out_shape=[jax.ShapeDtypeStruct((n_seq * seq, A_Q), BF16),
                   jax.ShapeDtypeStruct((n_seq, A_KV, seq), F32),
                   jax.ShapeDtypeStruct((n_seq, A_KV, seq), F32)],
        compiler_params=pltpu.CompilerParams(dimension_semantics=("parallel",)),
        name="attn_context",
    )(sink, qa, ka, va)


def _rope_tables(n_tok):
    n_rows = n_tok // GRID_W
    assert n_rows <= GRID_W
    shape = (GRID_W, LANES)
    pos = lax.broadcasted_iota(jnp.int32, shape, 0).astype(F32)
    lane = lax.broadcasted_iota(jnp.int32, shape, 1)
    half = HEAD_DIM // 2
    nf = half // 2
    within = lane & (HEAD_DIM - 1)
    is_col = within >= half
    second = (within & (half - 1)) >= nf
    f = (within & (nf - 1)).astype(F32)
    inv = jnp.exp(f * (-2.0 / half * math.log(ROPE_BASE)))
    ang = pos * inv
    cos_t = jnp.cos(ang)
    sin_t = jnp.sin(ang)
    sin_t = jnp.where(second, sin_t, -sin_t)

    def per_token(tab):
        return jnp.concatenate(
            [jnp.where(is_col, tab, jnp.broadcast_to(tab[r:r + 1, :], shape)) for r in range(n_rows)], axis=0)

    return per_token(cos_t), per_token(sin_t)


def _rope(x, cos, sin_signed):
    nf = HEAD_DIM // 4
    lane = lax.broadcasted_iota(jnp.int32, x.shape, 1)
    second = (lane & (2 * nf - 1)) >= nf
    up = pltpu.roll(x, nf, 1)
    dn = pltpu.roll(x, LANES - nf, 1)
    return x * cos + jnp.where(second, up, dn) * sin_signed


def _attn_lat_kernel(sink_ref, q_ref, k_ref, v_ref, kc_ref, vc_ref, o_ref, qr_ref, kp_ref, vp_ref,
                     *, seq):
    nb = seq // ATTN_BLOCK
    blk = ATTN_BLOCK
    cos, sin_signed = _rope_tables(seq)
    for m in range(A_Q // LANES):
        qr_ref[:, m * LANES:(m + 1) * LANES] = _rope(
            q_ref[:, m * LANES:(m + 1) * LANES].astype(F32), cos, sin_signed).astype(BF16)
    kr = _rope(k_ref[...], cos, sin_signed)
    v = v_ref[...]
    zpad = jnp.zeros((blk, LANES), BF16)
    ctx, sink_cols = [], []
    for g in range(ATTN_KV_HEADS):
        kp_ref[g, 0:blk, :] = zpad
        kp_ref[g, blk + seq:2 * blk + seq, :] = zpad
        vp_ref[g, 0:blk, :] = zpad
        vp_ref[g, blk + seq:2 * blk + seq, :] = zpad
        kp_ref[g, blk:blk + seq, :] = _dup_kv_head(kr, g).astype(BF16)
        vp_ref[g, blk:blk + seq, :] = _dup_kv_head(v, g).astype(BF16)
        hd = slice(g * HEAD_DIM, (g + 1) * HEAD_DIM)
        ctx.append((jnp.concatenate([kc_ref[0, hd, :]] * 2, axis=0).astype(BF16),
                    jnp.concatenate([vc_ref[0, hd, :]] * 2, axis=0).astype(BF16), None, True))
        sink_cols.append(_sink_column(sink_ref, g, blk))
    r = lax.broadcasted_iota(jnp.int32, (ATTN_GROUP * blk, 3 * blk), 0) & (blk - 1)
    c = lax.broadcasted_iota(jnp.int32, (ATTN_GROUP * blk, 3 * blk), 1)
    band = (c >= r) & (c <= r + 2 * WINDOW)

    def block_body(i, carry):
        row0 = pl.multiple_of(i * blk, blk)
        kj = (i - 1) * blk + c
        mask = band & (kj >= 0) & (kj < seq)
        work = []
        for g in range(ATTN_KV_HEADS):
            cols = (2 * g, 2 * g + 1)
            window = (kp_ref[g, pl.ds(row0, 3 * blk), :], vp_ref[g, pl.ds(row0, 3 * blk), :], mask, False)
            keys = [window, ctx[g]]
            scores = _attn_scores([qr_ref[pl.ds(row0, blk), m * LANES:(m + 1) * LANES] for m in cols], keys)
            work.append((cols, scores, keys, g))
        for cols, scores, keys, g in work:
            outs = _attn_finish(scores, sink_cols[g], keys)
            for m, o in zip(cols, outs):
                o_ref[pl.ds(row0, blk), m * LANES:(m + 1) * LANES] = o.astype(o_ref.dtype)
        return carry

    lax.fori_loop(0, nb, block_body, 0)


def _attn_latent(sink, qa, ka, va, k_ctx, v_ctx, *, n_seq, seq, row_block0):
    past = k_ctx.shape[2]
    kern = functools.partial(_attn_lat_kernel, seq=seq)
    return pl.pallas_call(
        kern,
        grid=(n_seq,),
        in_specs=[
            pl.BlockSpec(memory_space=pltpu.SMEM),
            pl.BlockSpec((seq, A_Q), lambda b: (row_block0 + b, 0)),
            pl.BlockSpec((seq, A_KV), lambda b: (row_block0 + b, 0)),
            pl.BlockSpec((seq, A_KV), lambda b: (row_block0 + b, 0)),
            pl.BlockSpec((1, A_KV, past), lambda b: (b, 0, 0)),
            pl.BlockSpec((1, A_KV, past), lambda b: (b, 0, 0)),
        ],
        out_specs=pl.BlockSpec((seq, A_Q), lambda b: (b, 0)),
        out_shape=jax.ShapeDtypeStruct((n_seq * seq, A_Q), BF16),
        scratch_shapes=[
            pltpu.VMEM((seq, A_Q), BF16),
            pltpu.VMEM((ATTN_KV_HEADS, seq + 2 * ATTN_BLOCK, LANES), BF16),
            pltpu.VMEM((ATTN_KV_HEADS, seq + 2 * ATTN_BLOCK, LANES), BF16),
        ],
        compiler_params=pltpu.CompilerParams(dimension_semantics=("parallel",)),
        name="attn_latent",
    )(sink, qa, ka, va, k_ctx, v_ctx)


def _same_block(a, b, size):
    return (a & -size) == (b & -size)


def _split3(x):
    hi = x.astype(BF16)
    r1 = x - hi.astype(F32)
    mid = r1.astype(BF16)
    lo = (r1 - mid.astype(F32)).astype(BF16)
    return hi, mid, lo


def _interleave(generators):
    live = list(generators)
    while live:
        for gen in list(live):
            try:
                next(gen)
            except StopIteration:
                live.remove(gen)


def _gla_tile(q, k, v, ld, st_ref, sel_ref, emit, *, rev):
    t = GLA_TILE
    n_pair = GLA_HEADS // 2
    row = lax.broadcasted_iota(jnp.int32, (t, t), 0)
    col = lax.broadcasted_iota(jnp.int32, (t, t), 1)
    tri = (row <= col) if rev else (row >= col)
    tri = jnp.where(tri, 1.0, 0.0).astype(BF16)
    hi, mid, lo = _split3(ld * math.log2(math.e))
    b = (jnp.dot(tri, hi, preferred_element_type=F32) + jnp.dot(tri, mid, preferred_element_type=F32)
         + jnp.dot(tri, lo, preferred_element_type=F32))
    yield
    b_tot = b[0:1, :] if rev else b[t - 1:t, :]
    qs = (q * jnp.exp2(b)).astype(BF16)
    kd = (k * jnp.exp2(b_tot - b)).astype(BF16)

    rowv = lax.broadcasted_iota(jnp.int32, (t, B_QK), 0)
    levels = []
    half = GLA_SUB
    while 2 * half <= t:
        blk = 2 * half
        pieces = []
        for bs in range(0, t, blk):
            r = bs + half if rev else bs + half - 1
            pieces.append(jnp.broadcast_to(b[r:r + 1, :], (blk, B_QK)))
        ref = jnp.concatenate(pieces, axis=0) if len(pieces) > 1 else pieces[0]
        second = (rowv & (blk - 1)) >= half
        is_q = ~second if rev else second
        ql = (q * jnp.exp2(jnp.where(is_q, b - ref, -jnp.inf))).astype(BF16)
        kl = (k * jnp.exp2(jnp.where(is_q, -jnp.inf, ref - b))).astype(BF16)
        levels.append((blk, ql, kl))
        half = blk
    yield

    rr = lax.broadcasted_iota(jnp.int32, (t, 2 * t), 0)
    ss = lax.broadcasted_iota(jnp.int32, (t, 2 * t), 1) & (t - 1)
    sub_rows = lax.broadcasted_iota(jnp.int32, (GLA_SUB, LANES), 0)
    lane_lo = _lane_lt((t, LANES), GLA_DK)
    lane_lo2 = _lane_lt((2 * t, LANES), GLA_DK)
    par_rows = lax.broadcasted_iota(jnp.int32, (2 * t, LANES), 0) < t
    head_sel = par_rows == lane_lo2

    outs = []
    for p in range(n_pair):
        pl_ = slice(p * LANES, (p + 1) * LANES)
        qp, kp, bp = q[:, pl_], k[:, pl_], b[:, pl_]
        rows = []
        for sb in range(t // GLA_SUB):
            r0 = sb * GLA_SUB
            q_sub = qp[r0:r0 + GLA_SUB, :]
            b_sub = bp[r0:r0 + GLA_SUB, :]
            cols = []
            for j in range(GLA_SUB):
                kj = kp[r0 + j:r0 + j + 1, :]
                bj = bp[r0 + j:r0 + j + 1, :]
                valid = (sub_rows <= j) if rev else (sub_rows >= j)
                e = jnp.exp2(jnp.where(valid, b_sub - bj, -jnp.inf))
                cols.append(q_sub * kj * e)
            rows.append(jnp.concatenate(cols, axis=1))
        e_p = jnp.concatenate(rows, axis=0).astype(BF16)
        yield
        scores = jnp.dot(e_p, sel_ref[...], preferred_element_type=F32)
        scores = jnp.where(_same_block(rr, ss, GLA_SUB), scores, 0.0)
        for blk, ql, kl in levels:
            klp = kl[:, pl_]
            kstack = jnp.concatenate([klp, klp], axis=0)
            kstack = jnp.where(head_sel, kstack, jnp.zeros_like(kstack))
            s_l = _dot_nt(ql[:, pl_], kstack)
            scores = scores + jnp.where(_same_block(rr, ss, blk), s_l, 0.0)
        yield
        vp = v[:, p * 2 * GLA_DV:(p + 1) * 2 * GLA_DV]
        v_lo = _lane_lt(vp.shape, GLA_DV)
        vbd = jnp.concatenate([jnp.where(v_lo, vp, 0.0), jnp.where(v_lo, 0.0, vp)], axis=0).astype(BF16)
        st = st_ref[p]
        o_p = (jnp.dot(scores.astype(BF16), vbd, preferred_element_type=F32)
               + _dot_nt(qs[:, pl_], st.astype(BF16)))
        outs.append(o_p)
        upd = jnp.dot(vp.T.astype(BF16), kd[:, pl_], preferred_element_type=F32)
        st_rows_lo = lax.broadcasted_iota(jnp.int32, upd.shape, 0) < GLA_DV
        bd = st_rows_lo == _lane_lt(upd.shape, GLA_DK)
        st_ref[p] = st * jnp.exp2(b_tot[:, pl_]) + jnp.where(bd, upd, 0.0)
        yield
    emit(jnp.concatenate(outs, axis=1))


def _gla_kernel(*refs, seq, has_init, want_state):
    q_ref, k_ref, v_ref, ldf_ref, ldb_ref = refs[:5]
    pos = 5
    if has_init:
        s0f_ref, s0b_ref = refs[pos:pos + 2]
        pos += 2
    o_ref = refs[pos]
    pos += 1
    if want_state:
        sf_ref, sb_ref = refs[pos:pos + 2]
        pos += 2
    st_ref, sel_ref = refs[pos:pos + 2]
    t = GLA_TILE
    nt = seq // t
    n_pair = GLA_HEADS // 2

    kk = lax.broadcasted_iota(jnp.int32, (GLA_SUB * LANES, 2 * t), 0)
    nn = lax.broadcasted_iota(jnp.int32, (GLA_SUB * LANES, 2 * t), 1)
    sel = ((nn & (GLA_SUB - 1)) == (kk >> LANES_LOG2)) & (((kk & (LANES - 1)) >= GLA_DK) == (nn >= t))
    sel_ref[...] = jnp.where(sel, 1.0, 0.0).astype(BF16)

    def load_state(st, s0_ref):
        for p in range(n_pair):
            if s0_ref is None:
                st[p] = jnp.zeros(st.shape[1:], F32)
            else:
                x = jnp.concatenate([s0_ref[0, 2 * p], s0_ref[0, 2 * p + 1]], axis=0)
                xt = x.T
                lo = _lane_lt(xt.shape, GLA_DK)
                st[p] = jnp.concatenate([jnp.where(lo, xt, 0.0), jnp.where(lo, 0.0, xt)], axis=0)

    def store_state(st, out_ref):
        for p in range(n_pair):
            s = st[p]
            lo = _lane_lt((GLA_DV, LANES), GLA_DK)
            z = jnp.where(lo, s[0:GLA_DV, :], s[GLA_DV:2 * GLA_DV, :]).T
            out_ref[0, 2 * p] = z[0:GLA_DK, :]
            out_ref[0, 2 * p + 1] = z[GLA_DK:2 * GLA_DK, :]

    dirs = ((False, ldf_ref, st_ref.at[0]), (True, ldb_ref, st_ref.at[1]))
    for rev, _, st in dirs:
        load_state(st, (s0b_ref if rev else s0f_ref) if has_init else None)
    o_ref[...] = jnp.zeros(o_ref.shape, o_ref.dtype)

    def tile_body(ti, carry):
        def walk(rev, ld_ref, st):
            tile = (nt - 1 - ti) if rev else ti
            r0 = pl.multiple_of(tile * t, t)
            rows = pl.ds(r0, t)

            def emit(o):
                o_ref[rows, :] = o_ref[rows, :] + o

            return _gla_tile(q_ref[rows, :], k_ref[rows, :], v_ref[rows, :].astype(F32), ld_ref[rows, :],
                             st, sel_ref, emit, rev=rev)

        _interleave([walk(*d) for d in dirs])
        return carry

    lax.fori_loop(0, nt, tile_body, 0)
    if want_state:
        for rev, _, st in dirs:
            store_state(st, sb_ref if rev else sf_ref)


def _gla(qb, kb, vb, ldf, ldb, s0f, s0b, *, n_seq, seq, row_block0, want_state):
    has_init = s0f is not None
    kern = functools.partial(_gla_kernel, seq=seq, has_init=has_init, want_state=want_state)
    tok = lambda w: pl.BlockSpec((seq, w), lambda b: (row_block0 + b, 0))
    in_specs = [tok(B_QK), tok(B_QK), tok(B_V), tok(B_QK), tok(B_QK)]
    args = [qb, kb, vb, ldf, ldb]
    state_spec = pl.BlockSpec((1, GLA_HEADS, GLA_DK, GLA_DV), lambda b: (b, 0, 0, 0))
    if has_init:
        in_specs += [state_spec, state_spec]
        args += [s0f, s0b]
    out_specs = [pl.BlockSpec((seq, B_V), lambda b: (b, 0))]
    out_shape = [jax.ShapeDtypeStruct((n_seq * seq, B_V), F32)]
    if want_state:
        out_specs += [state_spec, state_spec]
        out_shape += [jax.ShapeDtypeStruct((n_seq, GLA_HEADS, GLA_DK, GLA_DV), F32)] * 2
    return pl.pallas_call(
        kern,
        grid=(n_seq,),
        in_specs=in_specs,
        out_specs=out_specs,
        out_shape=out_shape,
        scratch_shapes=[
            pltpu.VMEM((2, GLA_HEADS // 2, 2 * GLA_DV, LANES), F32),
            pltpu.VMEM((GLA_SUB * LANES, 2 * GLA_TILE), BF16),
        ],
        compiler_params=pltpu.CompilerParams(
            dimension_semantics=("parallel",), vmem_limit_bytes=VMEM_LIMIT_BYTES),
        name="gla_state" if want_state else "gla_latent",
    )(*args)


def _evout_kernel(x_ref, mod_ref, gpre_ref, gpost_ref, attn_p_ref, attn_s_ref, gla_p_ref, gla_s_ref, gb_ref,
                  gn_ref, w_ref, o_ref, w_bf, *, np_tiles, layer, group):
    @pl.when(pl.program_id(0) == 0)
    def _cast_weights():
        w_bf[...] = w_ref[0].astype(BF16)

    is_prompt = pl.program_id(0) < np_tiles
    mods = _mod_rows(mod_ref, group(pl.program_id(0)))
    g_post = _norm_row(gpost_ref, layer, 1)
    for rows in _sub_tiles(x_ref.shape[0]):
        attn = _pick((attn_p_ref, attn_s_ref), is_prompt, rows)
        gla = _pick((gla_p_ref, gla_s_ref), is_prompt, rows)
        parts = [attn]
        for hd in range(GLA_HEADS):
            sl = slice(hd * GLA_DV, (hd + 1) * GLA_DV)
            g = _rms(gla[:, sl], gn_ref[:, sl]) * _silu(gb_ref[rows, sl].astype(F32))
            parts.append(g.astype(BF16))
        y = jnp.dot(jnp.concatenate(parts, axis=1), w_bf[...], preferred_element_type=F32)
        o_ref[rows, :] = _sub_out(x_ref[rows, :], y, mods, g_post, 1, 1.0)


def _even_out(x, mods, norm_pre, norm_post, attn_ps, gla_ps, gb, gla_norm, w_out, *, layer, e, group,
              np_tiles):
    n_tok, d = x.shape
    tm = WIDE_TILE
    ev_out = w_out.shape[1]
    kern = functools.partial(_evout_kernel, np_tiles=np_tiles, layer=layer, group=group)
    return pl.pallas_call(
        kern,
        grid=(n_tok // tm,),
        in_specs=[pl.BlockSpec((tm, d), lambda i: (i, 0))] + _cond_specs(mods, norm_pre, norm_post, layer) + [
            *_tile_specs(tm, A_Q, np_tiles),
            *_tile_specs(tm, B_V, np_tiles),
            pl.BlockSpec((tm, B_V), lambda i: (i, 0)),
            pl.BlockSpec((1, B_V), lambda i: (e, 0)),
            pl.BlockSpec((1, ev_out, d), lambda i: (e, 0, 0), pipeline_mode=pl.Buffered(1)),
        ],
        out_specs=pl.BlockSpec((tm, d), lambda i: (i, 0)),
        out_shape=jax.ShapeDtypeStruct((n_tok, d), F32),
        scratch_shapes=[pltpu.VMEM((ev_out, d), BF16)],
        compiler_params=pltpu.CompilerParams(
            dimension_semantics=("arbitrary",), vmem_limit_bytes=VMEM_LIMIT_BYTES),
        name="even_out",
    )(x, mods, norm_pre, norm_post, *attn_ps, *gla_ps, gb, gla_norm, w_out)


def _gelu(x):
    return 0.5 * x * (1.0 + lax.erf(x * (2.0 ** -0.5)))


def _cm_kernel(x_ref, mod_ref, gpre_ref, gpost_ref, win_ref, vg_ref, vb_ref, ws_ref, bs_ref, wout_ref,
               o_ref, win_bf, wout_bf, *, layer, group):
    @pl.when(pl.program_id(0) == 0)
    def _cast_weights():
        win_bf[...] = win_ref[0].astype(BF16)
        wout_bf[...] = wout_ref[0].astype(BF16)

    width = wout_bf.shape[0]
    gw = width // CMLP_GROUPS
    ws = [ws_ref[0, g].astype(BF16) for g in range(CMLP_GROUPS)]
    mods = _mod_rows(mod_ref, group(pl.program_id(0)))
    g_pre = _norm_row(gpre_ref, layer, 1)
    g_post = _norm_row(gpost_ref, layer, 1)
    eye = (lax.broadcasted_iota(jnp.int32, (CHUNK, CHUNK), 0)
           == lax.broadcasted_iota(jnp.int32, (CHUNK, CHUNK), 1))
    bias_cols = [jnp.sum(jnp.where(eye, jnp.broadcast_to(bs_ref[0, g:g + 1, :], (CHUNK, CHUNK)), 0.0),
                         axis=1, keepdims=True) for g in range(CMLP_GROUPS)]

    def phase_in(rows):
        x = x_ref[rows, :]
        h = _sub_in(x, mods, g_pre, 1).astype(BF16)
        v = _gelu(jnp.dot(h, win_bf[:, width:2 * width], preferred_element_type=F32))
        u = _gelu(jnp.dot(h, win_bf[:, 0:width], preferred_element_type=F32))
        return x, u, v

    def phase_mix(x, u, v):
        mu = jnp.mean(v, axis=-1, keepdims=True)
        vc = v - mu
        var = jnp.mean(vc * vc, axis=-1, keepdims=True)
        vn = (vc * lax.rsqrt(var + EPS) * vg_ref[...] + vb_ref[...]).astype(BF16)
        chunk_rows = []
        for c in range(x.shape[0] // CHUNK):
            cols = []
            for g in range(CMLP_GROUPS):
                blk = vn[c * CHUNK:(c + 1) * CHUNK, g * gw:(g + 1) * gw]
                cols.append(jnp.dot(ws[g], blk, preferred_element_type=F32) + bias_cols[g])
            chunk_rows.append(jnp.concatenate(cols, axis=1))
        mixed = jnp.concatenate(chunk_rows, axis=0)
        return x, (u * mixed).astype(BF16)

    def phase_out(rows, x, m):
        y = jnp.dot(m, wout_bf[...], preferred_element_type=F32)
        o_ref[rows, :] = _sub_out(x, y, mods, g_post, 1, 1.0)

    tiles = _sub_tiles(x_ref.shape[0])
    pending = None
    for rows in tiles:
        cur = phase_in(rows)
        if pending is not None:
            phase_out(pending[0], *phase_mix(*pending[1]))
        pending = (rows, cur)
    phase_out(pending[0], *phase_mix(*pending[1]))


def _chunk_mlp(x, mods, norm_pre, norm_post, w_in, v_gain, v_bias, w_s, b_s, w_out, *, layer, o, group):
    n_tok, d = x.shape
    tm = WIDE_TILE
    width = w_out.shape[1]
    kern = functools.partial(_cm_kernel, layer=layer, group=group)
    return pl.pallas_call(
        kern,
        grid=(n_tok // tm,),
        in_specs=[pl.BlockSpec((tm, d), lambda i: (i, 0))] + _cond_specs(mods, norm_pre, norm_post, layer) + [
            pl.BlockSpec((1, d, 2 * width), lambda i: (o, 0, 0), pipeline_mode=pl.Buffered(1)),
            pl.BlockSpec((1, width), lambda i: (o, 0)),
            pl.BlockSpec((1, width), lambda i: (o, 0)),
            pl.BlockSpec((1, CMLP_GROUPS, CHUNK, CHUNK), lambda i: (o, 0, 0, 0)),
            pl.BlockSpec((1, CMLP_GROUPS, CHUNK), lambda i: (o, 0, 0)),
            pl.BlockSpec((1, width, d), lambda i: (o, 0, 0), pipeline_mode=pl.Buffered(1)),
        ],
        out_specs=pl.BlockSpec((tm, d), lambda i: (i, 0)),
        out_shape=jax.ShapeDtypeStruct((n_tok, d), F32),
        scratch_shapes=[pltpu.VMEM((d, 2 * width), BF16), pltpu.VMEM((width, d), BF16)],
        compiler_params=pltpu.CompilerParams(
            dimension_semantics=("arbitrary",), vmem_limit_bytes=VMEM_LIMIT_BYTES),
        name="chunk_mlp",
    )(x, mods, norm_pre, norm_post, w_in, v_gain, v_bias, w_s, b_s, w_out)


def kernel(x_prompt, x_sample, cache_k, cache_v, state_gla_fwd, state_gla_bwd, c, c_ctx, w_mod, b_mod, norm_pre, norm_post, ffn_w_gate, ffn_w_up, ffn_w_down, ev_w_in, ev_w_out, ev_sink, gla_wa_f, gla_ba_f, gla_wa_b, gla_ba_b, gla_norm, cm_w_in, cm_v_gain, cm_v_bias, cm_w_s, cm_b_s, cm_w_out):
    batch, seq, d = x_prompt.shape
    dec_batch, dec_seq, _ = x_sample.shape
    depth = w_mod.shape[0]
    n_prompt_tok = batch * seq
    n_sample_tok = dec_batch * dec_seq
    assert n_prompt_tok % dec_seq == 0 and dec_seq % TOKEN_TILE == 0
    group = _group_index_map(TOKEN_TILE, n_prompt_tok, dec_seq)
    np_tiles = n_prompt_tok // TOKEN_TILE
    ffn = functools.partial(_ffn, group=group, np_tiles=np_tiles)
    assert dec_seq % WIDE_TILE == 0
    group_w = _group_index_map(WIDE_TILE, n_prompt_tok, dec_seq)
    np_tiles_w = n_prompt_tok // WIDE_TILE

    xs = (x_prompt.reshape(n_prompt_tok, d), x_sample.reshape(n_sample_tok, d))
    ev_w_in_t = jnp.swapaxes(ev_w_in, 1, 2)

    mods = _adaln_mods(c_ctx, c, w_mod, b_mod)
    gains_pre = norm_pre.reshape(depth * N_SUB, d)
    gains_post = norm_post.reshape(depth * N_SUB, d)
    cond = (mods, gains_pre, gains_post)

    new_k, new_v, new_sf, new_sb = [], [], [], []
    for layer in range(depth):
        (x,) = ffn(xs, *cond, ffn_w_gate, ffn_w_up, ffn_w_down,
                   sub=0, layer=layer, half=0, split_out=False)
        if layer % 2 == 0:
            e = layer // 2
            qa, ka, va, qb, kb, vb, gb, ldf, ldb = _even_in(
                x, *cond, ev_w_in_t, gla_wa_f, gla_wa_b, gla_ba_f, gla_ba_b,
                layer=layer, e=e, group=group_w)
            sink = ev_sink[e]
            attn_p, k_t, v_t = _attn_context(sink, qa, ka, va, n_seq=batch, seq=seq)
            k_ctx = jnp.transpose(cache_k[:, e], (0, 2, 3, 1)).reshape(dec_batch, A_KV, -1)
            v_ctx = jnp.transpose(cache_v[:, e], (0, 2, 3, 1)).reshape(dec_batch, A_KV, -1)
            attn_s = _attn_latent(sink, qa, ka, va, k_ctx, v_ctx, n_seq=dec_batch, seq=dec_seq,
                                  row_block0=n_prompt_tok // dec_seq)
            gla_p, s_f, s_b = _gla(qb, kb, vb, ldf, ldb, None, None, n_seq=batch, seq=seq,
                                   row_block0=0, want_state=True)
            (gla_s,) = _gla(qb, kb, vb, ldf, ldb, state_gla_fwd[:, e], state_gla_bwd[:, e],
                            n_seq=dec_batch, seq=dec_seq, row_block0=n_prompt_tok // dec_seq,
                            want_state=False)
            x = _even_out(x, *cond, (attn_p, attn_s), (gla_p, gla_s), gb, gla_norm, ev_w_out,
                          layer=layer, e=e, group=group_w, np_tiles=np_tiles_w)
            for cache_t, dst in ((k_t, new_k), (v_t, new_v)):
                dst.append(jnp.transpose(cache_t.reshape(batch, ATTN_KV_HEADS, HEAD_DIM, seq), (0, 3, 1, 2)))
            new_sf.append(s_f)
            new_sb.append(s_b)
        else:
            o = layer // 2
            x = _chunk_mlp(x, *cond, cm_w_in, cm_v_gain, cm_v_bias, cm_w_s, cm_b_s, cm_w_out,
                           layer=layer, o=o, group=group_w)
        xs = ffn((x,), *cond, ffn_w_gate, ffn_w_up, ffn_w_down,
                 sub=2, layer=layer, half=1, split_out=(layer == depth - 1))

    y_prompt = xs[0].reshape(batch, seq, d)
    y_sample = xs[1].reshape(dec_batch, dec_seq, d)
    return (y_prompt, y_sample, jnp.stack(new_k, axis=1), jnp.stack(new_v, axis=1),
            jnp.stack(new_sf, axis=1), jnp.stack(new_sb, axis=1))
```

```python
import functools
import math

import jax
import jax.numpy as jnp
from jax import lax
from jax.experimental import pallas as pl
from jax.experimental.pallas import tpu as pltpu

F32 = jnp.float32
BF16 = jnp.bfloat16

EPS = 1e-6
NEG_INF = -1e30
FFN_RESIDUAL = 0.5
N_SUB = 3

ATTN_HEADS = 8
ATTN_KV_HEADS = 2
HEAD_DIM = 64
ATTN_BLOCK = 128
WINDOW = 128
GRID_W = 64
GRID_W_LOG2 = 6
ROPE_BASE = 10000.0
GLA_HEADS = 4
GLA_DK = 64
GLA_DV = 128
GLA_RANK = 16
GLA_TAU = 16.0
CHUNK = 128
CMLP_GROUPS = 4

A_Q = ATTN_HEADS * HEAD_DIM
A_KV = ATTN_KV_HEADS * HEAD_DIM
B_QK = GLA_HEADS * GLA_DK
B_V = GLA_HEADS * GLA_DV
EV_MAIN = A_Q + 2 * A_KV + 2 * B_QK + 2 * B_V
EV_OFFS = (0, A_Q, A_Q + A_KV, A_Q + 2 * A_KV, A_Q + 2 * A_KV + B_QK,
           A_Q + 2 * A_KV + 2 * B_QK, A_Q + 2 * A_KV + 2 * B_QK + B_V, EV_MAIN)

LANES = 128
LANES_LOG2 = 7
SUBLANES = 8
VMEM_LIMIT_BYTES = 56 * 1024 * 1024

TOKEN_TILE = 512
WIDE_TILE = 1024
SUB_ROWS = 512
ADALN_COLS = 3072
FFN_CHUNK = 256
GLA_TILE = 128
GLA_SUB = 8


def _group_index_map(tile_rows, n_prompt_tok, seq_tok):
    def group(i):
        return jnp.maximum(i * tile_rows - (n_prompt_tok - seq_tok), 0) // seq_tok
    return group


def _sub_tiles(n_rows):
    assert n_rows % SUB_ROWS == 0
    return [slice(r, r + SUB_ROWS) for r in range(0, n_rows, SUB_ROWS)]


def _rms(x, g):
    ms = jnp.mean(x * x, axis=-1, keepdims=True)
    return x * lax.rsqrt(ms + EPS) * g


def _mod_rows(mod_ref, g):
    return [mod_ref[k, pl.ds(g, 1), :] for k in range(3 * N_SUB)]


def _sub_in(x, mods, g_pre, sub):
    shift, scale = mods[3 * sub + 0], mods[3 * sub + 1]
    return _rms(x, g_pre) * (1.0 + scale) + shift


def _sub_out(x, y, mods, g_post, sub, coef):
    return x + (coef * mods[3 * sub + 2]) * _rms(y, g_post)


def _norm_row(ref, layer, sub):
    r = layer * N_SUB + sub
    return ref[r:r + 1, :]


def _silu(x):
    return x * jax.nn.sigmoid(x)


def _mod_kernel(cctx_ref, c_ref, w_ref, b_ref, o_ref):
    _, n_kinds, rows, d = o_ref.shape
    r = lax.broadcasted_iota(jnp.int32, (rows, d), 0)
    cond = jnp.where(r == 0, jnp.broadcast_to(cctx_ref[...], (rows, d)), 0.0)
    for j in range(c_ref.shape[0]):
        cond = jnp.where(r == 1 + j, jnp.broadcast_to(c_ref[j:j + 1, :], (rows, d)), cond)
    s = _silu(cond).astype(BF16)
    bias = b_ref[pl.ds(pl.program_id(0), 1), :]
    res = jnp.dot(s, w_ref[0].astype(BF16), preferred_element_type=F32) + bias
    for q in range(n_kinds):
        o_ref[0, q] = res[:, q * d:(q + 1) * d]


def _adaln_mods(c_ctx, c, w_mod, b_mod):
    depth, d, n = w_mod.shape
    tn = ADALN_COLS
    assert n % tn == 0 and tn % d == 0
    rows = -(-(1 + c.shape[0]) // SUBLANES) * SUBLANES
    return pl.pallas_call(
        _mod_kernel,
        grid=(depth, n // tn),
        in_specs=[
            pl.BlockSpec((1, d), lambda l, j: (0, 0)),
            pl.BlockSpec(c.shape, lambda l, j: (0, 0)),
            pl.BlockSpec((1, d, tn), lambda l, j: (l, 0, j)),
            pl.BlockSpec((depth, tn), lambda l, j: (0, j)),
        ],
        out_specs=pl.BlockSpec((1, tn // d, rows, d), lambda l, j: (l, j, 0, 0)),
        out_shape=jax.ShapeDtypeStruct((depth, n // d, rows, d), F32),
        compiler_params=pltpu.CompilerParams(
            dimension_semantics=("parallel", "parallel"), vmem_limit_bytes=VMEM_LIMIT_BYTES),
        name="adaln_mods",
    )(c_ctx.reshape(1, d), c, w_mod, b_mod)


def _tile_specs(tm, width, np_tiles):
    return [pl.BlockSpec((tm, width), lambda i: (jnp.minimum(i, np_tiles - 1), 0)),
            pl.BlockSpec((tm, width), lambda i: (jnp.maximum(i - np_tiles, 0), 0))]


def _pick(refs, is_prompt, rows=slice(None)):
    if len(refs) == 1:
        return refs[0][rows, :]
    return jnp.where(is_prompt, refs[0][rows, :], refs[1][rows, :])


def _cond_specs(mods, norm_pre, norm_post, layer):
    return [pl.BlockSpec((None,) + mods.shape[1:], lambda i: (layer, 0, 0, 0)),
            pl.BlockSpec(norm_pre.shape, lambda i: (0, 0)),
            pl.BlockSpec(norm_post.shape, lambda i: (0, 0))]


def _ffn_kernel(*refs, sub, layer, half, n_chunks, n_x, n_o, np_tiles, group, mix_layer):
    refs = list(refs)
    x_refs = [refs.pop(0) for _ in range(n_x)]
    if mix_layer is not None:
        attn_refs = (refs.pop(0), refs.pop(0))
        gla_refs = (refs.pop(0), refs.pop(0))
        gb_ref, gn_ref, wout_hbm = refs.pop(0), refs.pop(0), refs.pop(0)
    mod_ref, gpre_ref, gpost_ref, wg_hbm, wu_hbm, wd_hbm = [refs.pop(0) for _ in range(6)]
    o_refs = [refs.pop(0) for _ in range(n_o)]
    wg_bf, wu_bf, wd_bf = [refs.pop(0) for _ in range(3)]
    wout_bf = refs.pop(0) if mix_layer is not None else None
    st_g, st_u, st_d, sem = refs
    fc = FFN_CHUNK
    step = pl.program_id(0)
    is_prompt = step < np_tiles
    mods = _mod_rows(mod_ref, group(step))
    g_pre = _norm_row(gpre_ref, layer, sub)
    g_post = _norm_row(gpost_ref, layer, sub)

    def stage_mix_weight():
        rows_per = st_d.shape[1]
        n = wout_bf.shape[0] // rows_per

        def copy(c, slot):
            return pltpu.make_async_copy(wout_hbm.at[mix_layer, pl.ds(c * rows_per, rows_per), :],
                                         st_d.at[slot], sem.at[2, slot])

        copy(0, 0).start()
        for c in range(n):
            slot = c % 2
            if c + 1 < n:
                copy(c + 1, 1 - slot).start()
            copy(c, slot).wait()
            wout_bf[c * rows_per:(c + 1) * rows_per, :] = st_d[slot].astype(BF16)

    def chunk_copies(c, slot):
        cols = pl.ds(c * fc, fc)
        return (
            pltpu.make_async_copy(wg_hbm.at[layer, half, :, cols], st_g.at[slot], sem.at[0, slot]),
            pltpu.make_async_copy(wu_hbm.at[layer, half, :, cols], st_u.at[slot], sem.at[1, slot]),
            pltpu.make_async_copy(wd_hbm.at[layer, half, cols, :], st_d.at[slot], sem.at[2, slot]),
        )

    def run(stage_weights):
        if stage_weights:
            if mix_layer is not None:
                stage_mix_weight()
            for cp in chunk_copies(0, 0):
                cp.start()
        x = _pick(x_refs, is_prompt)
        if mix_layer is not None:
            mixed = _gated_mix(_pick(attn_refs, is_prompt), _pick(gla_refs, is_prompt), gb_ref[...], gn_ref)
            y_mix = jnp.dot(mixed, wout_bf[...], preferred_element_type=F32)
            x = _sub_out(x, y_mix, mods, _norm_row(gpost_ref, layer, 1), 1, 1.0)
        h = _sub_in(x, mods, g_pre, sub).astype(BF16)
        y = jnp.zeros(x.shape, F32)
        for c in range(n_chunks):
            sl = slice(c * fc, (c + 1) * fc)
            if stage_weights:
                slot = c % 2
                if c + 1 < n_chunks:
                    for cp in chunk_copies(c + 1, 1 - slot):
                        cp.start()
                for cp in chunk_copies(c, slot):
                    cp.wait()
                wg_bf[:, sl] = st_g[slot].astype(BF16)
                wu_bf[:, sl] = st_u[slot].astype(BF16)
                wd_bf[sl, :] = st_d[slot].astype(BF16)
            g = jnp.dot(h, wg_bf[:, sl], preferred_element_type=F32)
            u = jnp.dot(h, wu_bf[:, sl], preferred_element_type=F32)
            a = (_silu(g) * u).astype(BF16)
            y = y + jnp.dot(a, wd_bf[sl, :], preferred_element_type=F32)
        out = _sub_out(x, y, mods, g_post, sub, FFN_RESIDUAL)
        if n_o == 1:
            o_refs[0][...] = out
        else:
            @pl.when(is_prompt)
            def _store_prompt():
                o_refs[0][...] = out

            @pl.when(jnp.logical_not(is_prompt))
            def _store_latent():
                o_refs[1][...] = out

    @pl.when(step == 0)
    def _first_tile():
        run(True)

    @pl.when(step > 0)
    def _other_tiles():
        run(False)


def _ffn(xs, mods, norm_pre, norm_post, wg, wu, wd, *, sub, layer, half, group, np_tiles, split_out,
         mix=None):
    d = xs[0].shape[-1]
    n_tok = sum(x.shape[0] for x in xs)
    d_ff = wg.shape[-1]
    tm = TOKEN_TILE
    fc = FFN_CHUNK
    n_chunks = d_ff // fc
    assert n_chunks * fc == d_ff and n_tok % tm == 0
    n_x, n_o = len(xs), (2 if split_out else 1)
    kern = functools.partial(_ffn_kernel, sub=sub, layer=layer, half=half, n_chunks=n_chunks,
                             n_x=n_x, n_o=n_o, np_tiles=np_tiles, group=group,
                             mix_layer=None if mix is None else mix[0])
    whole = pl.BlockSpec((tm, d), lambda i: (i, 0))
    x_specs = _tile_specs(tm, d, np_tiles) if n_x == 2 else [whole]
    mix_specs, mix_args, mix_scratch = [], [], []
    if mix is not None:
        e, attn_ps, gla_ps, gb, gla_norm, w_out = mix
        assert w_out.shape[1:] == (d, d) and d % fc == 0
        mix_specs = [*_tile_specs(tm, A_Q, np_tiles), *_tile_specs(tm, B_V, np_tiles),
                     pl.BlockSpec((tm, B_V), lambda i: (i, 0)),
                     pl.BlockSpec((1, B_V), lambda i: (e, 0)),
                     pl.BlockSpec(memory_space=pl.ANY)]
        mix_args = [*attn_ps, *gla_ps, gb, gla_norm, w_out]
        mix_scratch = [pltpu.VMEM((d, d), BF16)]
    n_p = np_tiles * tm
    if split_out:
        out_specs = _tile_specs(tm, d, np_tiles)
        out_shape = [jax.ShapeDtypeStruct((n_p, d), F32), jax.ShapeDtypeStruct((n_tok - n_p, d), F32)]
    else:
        out_specs = [whole]
        out_shape = [jax.ShapeDtypeStruct((n_tok, d), F32)]
    return pl.pallas_call(
        kern,
        grid=(n_tok // tm,),
        in_specs=x_specs + mix_specs + _cond_specs(mods, norm_pre, norm_post, layer) + [
            pl.BlockSpec(memory_space=pl.ANY),
            pl.BlockSpec(memory_space=pl.ANY),
            pl.BlockSpec(memory_space=pl.ANY),
        ],
        out_specs=out_specs,
        out_shape=out_shape,
        scratch_shapes=[
            pltpu.VMEM((d, d_ff), BF16),
            pltpu.VMEM((d, d_ff), BF16),
            pltpu.VMEM((d_ff, d), BF16),
        ] + mix_scratch + [
            pltpu.VMEM((2, d, fc), F32),
            pltpu.VMEM((2, d, fc), F32),
            pltpu.VMEM((2, fc, d), F32),
            pltpu.SemaphoreType.DMA((3, 2)),
        ],
        compiler_params=pltpu.CompilerParams(
            dimension_semantics=("arbitrary",), vmem_limit_bytes=VMEM_LIMIT_BYTES),
        name=f"ffn_l{layer}h{half}",
    )(*xs, *mix_args, mods, norm_pre, norm_post, wg, wu, wd)


def _log_sigmoid(x):
    return jnp.minimum(x, 0.0) - jnp.log1p(jnp.exp(-jnp.abs(x)))


def _evin_kernel(x_ref, mod_ref, gpre_ref, gpost_ref, w_ref, waf_ref, wab_ref, baf_ref, bab_ref,
                 qa_ref, ka_ref, va_ref, qb_ref, kb_ref, vb_ref, gb_ref, ldf_ref, ldb_ref,
                 w_bf, wa_bf, *, layer, group):
    @pl.when(pl.program_id(0) == 0)
    def _cast_weights():
        w_bf[0:EV_MAIN, :] = w_ref[0, 0:EV_MAIN, :].astype(BF16)
        w_bf[EV_MAIN:EV_MAIN + LANES, :] = jnp.zeros((LANES, w_bf.shape[1]), BF16)
        w_bf[EV_MAIN:EV_MAIN + 2 * GLA_RANK, :] = w_ref[0, EV_MAIN:EV_MAIN + 2 * GLA_RANK, :].astype(BF16)
        wa_bf[...] = jnp.zeros(wa_bf.shape, BF16)
        wa_bf[0:GLA_RANK, 0:B_QK] = waf_ref[0].astype(BF16)
        wa_bf[GLA_RANK:2 * GLA_RANK, B_QK:2 * B_QK] = wab_ref[0].astype(BF16)

    o = EV_OFFS
    mods = _mod_rows(mod_ref, group(pl.program_id(0)))
    g_pre = _norm_row(gpre_ref, layer, 1)

    def project(rows):
        h = _sub_in(x_ref[rows, :], mods, g_pre, 1).astype(BF16)
        return _dot_nt(h, w_bf[...])

    def finish(rows, full):
        qa_ref[rows, :] = (full[:, o[0]:o[1]] * (HEAD_DIM ** -0.5)).astype(qa_ref.dtype)
        ka_ref[rows, :] = full[:, o[1]:o[2]]
        va_ref[rows, :] = full[:, o[2]:o[3]]
        qb_ref[rows, :] = full[:, o[3]:o[4]] * (GLA_DK ** -0.5)
        kb_ref[rows, :] = full[:, o[4]:o[5]]
        vb_ref[rows, :] = full[:, o[5]:o[6]].astype(vb_ref.dtype)
        gb_ref[rows, :] = full[:, o[6]:o[7]].astype(gb_ref.dtype)
        lr = full[:, EV_MAIN:EV_MAIN + LANES].astype(BF16)
        logits = jnp.dot(lr, wa_bf[...], preferred_element_type=F32)
        ldf_ref[rows, :] = _log_sigmoid(logits[:, 0:B_QK] + baf_ref[...]) * (1.0 / GLA_TAU)
        ldb_ref[rows, :] = _log_sigmoid(logits[:, B_QK:2 * B_QK] + bab_ref[...]) * (1.0 / GLA_TAU)

    pending = None
    for rows in _sub_tiles(x_ref.shape[0]):
        full = project(rows)
        if pending is not None:
            finish(*pending)
        pending = (rows, full)
    finish(*pending)


def _even_in(x, mods, norm_pre, norm_post, w_in_t, wa_f, wa_b, ba_f, ba_b, *, layer, e, group):
    n_tok, d = x.shape
    tm = WIDE_TILE
    ev_in = w_in_t.shape[1]
    rank = wa_f.shape[1]
    assert rank == GLA_RANK and ev_in == EV_MAIN + 2 * rank
    widths = (A_Q, A_KV, A_KV, B_QK, B_QK, B_V, B_V, B_QK, B_QK)
    dtypes = (BF16, F32, F32, F32, F32, BF16, BF16, F32, F32)
    kern = functools.partial(_evin_kernel, layer=layer, group=group)
    return pl.pallas_call(
        kern,
        grid=(n_tok // tm,),
        in_specs=[pl.BlockSpec((tm, d), lambda i: (i, 0))] + _cond_specs(mods, norm_pre, norm_post, layer) + [
            pl.BlockSpec((1, ev_in, d), lambda i: (e, 0, 0), pipeline_mode=pl.Buffered(1)),
            pl.BlockSpec((1, rank, B_QK), lambda i: (e, 0, 0)),
            pl.BlockSpec((1, rank, B_QK), lambda i: (e, 0, 0)),
            pl.BlockSpec((1, B_QK), lambda i: (e, 0)),
            pl.BlockSpec((1, B_QK), lambda i: (e, 0)),
        ],
        out_specs=[pl.BlockSpec((tm, w), lambda i: (i, 0)) for w in widths],
        out_shape=[jax.ShapeDtypeStruct((n_tok, w), t) for w, t in zip(widths, dtypes)],
        scratch_shapes=[pltpu.VMEM((EV_MAIN + LANES, d), BF16), pltpu.VMEM((LANES, 2 * B_QK), BF16)],
        compiler_params=pltpu.CompilerParams(
            dimension_semantics=("arbitrary",), vmem_limit_bytes=VMEM_LIMIT_BYTES),
        name="even_in",
    )(x, mods, norm_pre, norm_post, w_in_t, wa_f, wa_b, ba_f, ba_b)


def _lane_lt(shape, n):
    return lax.broadcasted_iota(jnp.int32, shape, len(shape) - 1) < n


def _dup_kv_head(x, g):
    sw = pltpu.roll(x, HEAD_DIM, 1)
    lo = _lane_lt(x.shape, HEAD_DIM)
    return jnp.where(lo, x, sw) if g == 0 else jnp.where(lo, sw, x)


def _dot_nt(a, b):
    return lax.dot_general(a, b, (((1,), (1,)), ((), ())), preferred_element_type=F32)


ATTN_GROUP = ATTN_HEADS // ATTN_KV_HEADS


def _sink_column(sink_ref, g, t):
    rows = lax.broadcasted_iota(jnp.int32, (ATTN_GROUP * t, 1), 0)
    col = jnp.full((ATTN_GROUP * t, 1), sink_ref[ATTN_GROUP * g], F32)
    for j in range(1, ATTN_GROUP):
        col = jnp.where(rows >= j * t, sink_ref[ATTN_GROUP * g + j], col)
    return col


def _attn_scores(q_pairs, keys):
    t = q_pairs[0].shape[0]
    lo = _lane_lt((t, LANES), HEAD_DIM)
    zero = jnp.zeros((t, LANES), BF16)
    q4 = jnp.concatenate([jnp.where(lo, q_pairs[0], zero), jnp.where(lo, zero, q_pairs[0]),
                          jnp.where(lo, q_pairs[1], zero), jnp.where(lo, zero, q_pairs[1])], axis=0)
    scores = []
    for k2, _, mask, feature_major in keys:
        s = jnp.dot(q4, k2, preferred_element_type=F32) if feature_major else _dot_nt(q4, k2)
        if mask is not None:
            s = jnp.where(mask, s, NEG_INF)
        scores.append(s)
    return scores


def _attn_finish(scores, sink_col, keys):
    t = scores[0].shape[0] // ATTN_GROUP
    lo = _lane_lt((t, LANES), HEAD_DIM)
    mx = sink_col
    for s in scores:
        mx = jnp.maximum(mx, jnp.max(s, axis=-1, keepdims=True))
    den = jnp.exp(sink_col - mx)
    o = None
    for s, (_, v2, _, feature_major) in zip(scores, keys):
        p = jnp.exp(s - mx)
        den = den + jnp.sum(p, axis=-1, keepdims=True)
        pb = p.astype(BF16)
        pv = _dot_nt(pb, v2) if feature_major else jnp.dot(pb, v2, preferred_element_type=F32)
        o = pv if o is None else o + pv
    o = o / den
    return (jnp.where(lo, o[0:t], o[t:2 * t]), jnp.where(lo, o[2 * t:3 * t], o[3 * t:4 * t]))


def _attn_ctx_kernel(sink_ref, q_ref, k_ref, v_ref, o_ref, kt_ref, vt_ref):
    k = k_ref[...]
    v = v_ref[...]
    t = k.shape[0]
    kt_ref[0] = k.T
    vt_ref[0] = v.T
    work = []
    for g in range(ATTN_KV_HEADS):
        keys = [(_dup_kv_head(k, g).astype(BF16), _dup_kv_head(v, g).astype(BF16), None, False)]
        cols = (2 * g, 2 * g + 1)
        work.append((cols, _attn_scores([q_ref[:, m * LANES:(m + 1) * LANES] for m in cols], keys), keys, g))
    for cols, scores, keys, g in work:
        outs = _attn_finish(scores, _sink_column(sink_ref, g, t), keys)
        for m, o in zip(cols, outs):
            o_ref[:, m * LANES:(m + 1) * LANES] = o.astype(o_ref.dtype)


def _attn_context(sink, qa, ka, va, *, n_seq, seq):
    cache_spec = pl.BlockSpec((1, A_KV, seq), lambda b: (b, 0, 0))
    return pl.pallas_call(
        _attn_ctx_kernel,
        grid=(n_seq,),
        in_specs=[
            pl.BlockSpec(memory_space=pltpu.SMEM),
            pl.BlockSpec((seq, A_Q), lambda b: (b, 0)),
            pl.BlockSpec((seq, A_KV), lambda b: (b, 0)),
            pl.BlockSpec((seq, A_KV), lambda b: (b, 0)),
        ],
        out_specs=[pl.BlockSpec((seq, A_Q), lambda b: (b, 0)), cache_spec, cache_spec],
        out_shape=[jax.ShapeDtypeStruct((n_seq * seq, A_Q), BF16),
                   jax.ShapeDtypeStruct((n_seq, A_KV, seq), F32),
                   jax.ShapeDtypeStruct((n_seq, A_KV, seq), F32)],
        compiler_params=pltpu.CompilerParams(dimension_semantics=("parallel",)),
        name="attn_context",
    )(sink, qa, ka, va)


def _rope_tables(n_tok):
    n_rows = n_tok // GRID_W
    assert n_rows <= GRID_W
    shape = (GRID_W, LANES)
    pos = lax.broadcasted_iota(jnp.int32, shape, 0).astype(F32)
    lane = lax.broadcasted_iota(jnp.int32, shape, 1)
    half = HEAD_DIM // 2
    nf = half // 2
    within = lane & (HEAD_DIM - 1)
    is_col = within >= half
    second = (within & (half - 1)) >= nf
    f = (within & (nf - 1)).astype(F32)
    inv = jnp.exp(f * (-2.0 / half * math.log(ROPE_BASE)))
    ang = pos * inv
    cos_t = jnp.cos(ang)
    sin_t = jnp.sin(ang)
    sin_t = jnp.where(second, sin_t, -sin_t)

    def per_token(tab):
        return jnp.concatenate(
            [jnp.where(is_col, tab, jnp.broadcast_to(tab[r:r + 1, :], shape)) for r in range(n_rows)], axis=0)

    return per_token(cos_t), per_token(sin_t)


def _rope(x, cos, sin_signed):
    nf = HEAD_DIM // 4
    lane = lax.broadcasted_iota(jnp.int32, x.shape, 1)
    second = (lane & (2 * nf - 1)) >= nf
    up = pltpu.roll(x, nf, 1)
    dn = pltpu.roll(x, LANES - nf, 1)
    return x * cos + jnp.where(second, up, dn) * sin_signed


def _attn_lat_kernel(sink_ref, q_ref, k_ref, v_ref, kc_ref, vc_ref, o_ref, qr_ref, kp_ref, vp_ref,
                     *, seq):
    nb = seq // ATTN_BLOCK
    blk = ATTN_BLOCK
    cos, sin_signed = _rope_tables(seq)
    for m in range(A_Q // LANES):
        qr_ref[:, m * LANES:(m + 1) * LANES] = _rope(
            q_ref[:, m * LANES:(m + 1) * LANES].astype(F32), cos, sin_signed).astype(BF16)
    kr = _rope(k_ref[...], cos, sin_signed)
    v = v_ref[...]
    zpad = jnp.zeros((blk, LANES), BF16)
    ctx, sink_cols = [], []
    for g in range(ATTN_KV_HEADS):
        kp_ref[g, 0:blk, :] = zpad
        kp_ref[g, blk + seq:2 * blk + seq, :] = zpad
        vp_ref[g, 0:blk, :] = zpad
        vp_ref[g, blk + seq:2 * blk + seq, :] = zpad
        kp_ref[g, blk:blk + seq, :] = _dup_kv_head(kr, g).astype(BF16)
        vp_ref[g, blk:blk + seq, :] = _dup_kv_head(v, g).astype(BF16)
        hd = slice(g * HEAD_DIM, (g + 1) * HEAD_DIM)
        ctx.append((jnp.concatenate([kc_ref[0, hd, :]] * 2, axis=0).astype(BF16),
                    jnp.concatenate([vc_ref[0, hd, :]] * 2, axis=0).astype(BF16), None, True))
        sink_cols.append(_sink_column(sink_ref, g, blk))
    r = lax.broadcasted_iota(jnp.int32, (ATTN_GROUP * blk, 3 * blk), 0) & (blk - 1)
    c = lax.broadcasted_iota(jnp.int32, (ATTN_GROUP * blk, 3 * blk), 1)
    band = (c >= r) & (c <= r + 2 * WINDOW)

    def block_body(i, carry):
        row0 = pl.multiple_of(i * blk, blk)
        kj = (i - 1) * blk + c
        mask = band & (kj >= 0) & (kj < seq)
        work = []
        for g in range(ATTN_KV_HEADS):
            cols = (2 * g, 2 * g + 1)
            window = (kp_ref[g, pl.ds(row0, 3 * blk), :], vp_ref[g, pl.ds(row0, 3 * blk), :], mask, False)
            keys = [window, ctx[g]]
            scores = _attn_scores([qr_ref[pl.ds(row0, blk), m * LANES:(m + 1) * LANES] for m in cols], keys)
            work.append((cols, scores, keys, g))
        for cols, scores, keys, g in work:
            outs = _attn_finish(scores, sink_cols[g], keys)
            for m, o in zip(cols, outs):
                o_ref[pl.ds(row0, blk), m * LANES:(m + 1) * LANES] = o.astype(o_ref.dtype)
        return carry

    lax.fori_loop(0, nb, block_body, 0)


def _attn_latent(sink, qa, ka, va, k_ctx, v_ctx, *, n_seq, seq, row_block0):
    past = k_ctx.shape[2]
    kern = functools.partial(_attn_lat_kernel, seq=seq)
    return pl.pallas_call(
        kern,
        grid=(n_seq,),
        in_specs=[
            pl.BlockSpec(memory_space=pltpu.SMEM),
            pl.BlockSpec((seq, A_Q), lambda b: (row_block0 + b, 0)),
            pl.BlockSpec((seq, A_KV), lambda b: (row_block0 + b, 0)),
            pl.BlockSpec((seq, A_KV), lambda b: (row_block0 + b, 0)),
            pl.BlockSpec((1, A_KV, past), lambda b: (b, 0, 0)),
            pl.BlockSpec((1, A_KV, past), lambda b: (b, 0, 0)),
        ],
        out_specs=pl.BlockSpec((seq, A_Q), lambda b: (b, 0)),
        out_shape=jax.ShapeDtypeStruct((n_seq * seq, A_Q), BF16),
        scratch_shapes=[
            pltpu.VMEM((seq, A_Q), BF16),
            pltpu.VMEM((ATTN_KV_HEADS, seq + 2 * ATTN_BLOCK, LANES), BF16),
            pltpu.VMEM((ATTN_KV_HEADS, seq + 2 * ATTN_BLOCK, LANES), BF16),
        ],
        compiler_params=pltpu.CompilerParams(dimension_semantics=("parallel",)),
        name="attn_latent",
    )(sink, qa, ka, va, k_ctx, v_ctx)


def _same_block(a, b, size):
    return (a & -size) == (b & -size)


def _split3(x):
    hi = x.astype(BF16)
    r1 = x - hi.astype(F32)
    mid = r1.astype(BF16)
    lo = (r1 - mid.astype(F32)).astype(BF16)
    return hi, mid, lo


def _interleave(generators):
    live = list(generators)
    while live:
        for gen in list(live):
            try:
                next(gen)
            except StopIteration:
                live.remove(gen)


def _gla_tile(q, k, v, ld, st_ref, sel_ref, emit, *, rev):
    t = GLA_TILE
    n_pair = GLA_HEADS // 2
    row = lax.broadcasted_iota(jnp.int32, (t, t), 0)
    col = lax.broadcasted_iota(jnp.int32, (t, t), 1)
    tri = (row <= col) if rev else (row >= col)
    tri = jnp.where(tri, 1.0, 0.0).astype(BF16)
    hi, mid, lo = _split3(ld * math.log2(math.e))
    b = (jnp.dot(tri, hi, preferred_element_type=F32) + jnp.dot(tri, mid, preferred_element_type=F32)
         + jnp.dot(tri, lo, preferred_element_type=F32))
    yield
    b_tot = b[0:1, :] if rev else b[t - 1:t, :]
    qs = (q * jnp.exp2(b)).astype(BF16)
    kd = (k * jnp.exp2(b_tot - b)).astype(BF16)

    rowv = lax.broadcasted_iota(jnp.int32, (t, B_QK), 0)
    levels = []
    half = GLA_SUB
    while 2 * half <= t:
        blk = 2 * half
        pieces = []
        for bs in range(0, t, blk):
            r = bs + half if rev else bs + half - 1
            pieces.append(jnp.broadcast_to(b[r:r + 1, :], (blk, B_QK)))
        ref = jnp.concatenate(pieces, axis=0) if len(pieces) > 1 else pieces[0]
        second = (rowv & (blk - 1)) >= half
        is_q = ~second if rev else second
        ql = (q * jnp.exp2(jnp.where(is_q, b - ref, -jnp.inf))).astype(BF16)
        kl = (k * jnp.exp2(jnp.where(is_q, -jnp.inf, ref - b))).astype(BF16)
        levels.append((blk, ql, kl))
        half = blk
    yield

    rr = lax.broadcasted_iota(jnp.int32, (t, 2 * t), 0)
    ss = lax.broadcasted_iota(jnp.int32, (t, 2 * t), 1) & (t - 1)
    sub_rows = lax.broadcasted_iota(jnp.int32, (GLA_SUB, LANES), 0)
    lane_lo = _lane_lt((t, LANES), GLA_DK)
    lane_lo2 = _lane_lt((2 * t, LANES), GLA_DK)
    par_rows = lax.broadcasted_iota(jnp.int32, (2 * t, LANES), 0) < t
    head_sel = par_rows == lane_lo2

    outs = []
    for p in range(n_pair):
        pl_ = slice(p * LANES, (p + 1) * LANES)
        qp, kp, bp = q[:, pl_], k[:, pl_], b[:, pl_]
        rows = []
        for sb in range(t // GLA_SUB):
            r0 = sb * GLA_SUB
            q_sub = qp[r0:r0 + GLA_SUB, :]
            b_sub = bp[r0:r0 + GLA_SUB, :]
            cols = []
            for j in range(GLA_SUB):
                kj = kp[r0 + j:r0 + j + 1, :]
                bj = bp[r0 + j:r0 + j + 1, :]
                valid = (sub_rows <= j) if rev else (sub_rows >= j)
                e = jnp.exp2(jnp.where(valid, b_sub - bj, -jnp.inf))
                cols.append(q_sub * kj * e)
            rows.append(jnp.concatenate(cols, axis=1))
        e_p = jnp.concatenate(rows, axis=0).astype(BF16)
        yield
        scores = jnp.dot(e_p, sel_ref[...], preferred_element_type=F32)
        scores = jnp.where(_same_block(rr, ss, GLA_SUB), scores, 0.0)
        for blk, ql, kl in levels:
            klp = kl[:, pl_]
            kstack = jnp.concatenate([klp, klp], axis=0)
            kstack = jnp.where(head_sel, kstack, jnp.zeros_like(kstack))
            s_l = _dot_nt(ql[:, pl_], kstack)
            scores = scores + jnp.where(_same_block(rr, ss, blk), s_l, 0.0)
        yield
        vp = v[:, p * 2 * GLA_DV:(p + 1) * 2 * GLA_DV]
        v_lo = _lane_lt(vp.shape, GLA_DV)
        vbd = jnp.concatenate([jnp.where(v_lo, vp, 0.0), jnp.where(v_lo, 0.0, vp)], axis=0).astype(BF16)
        st = st_ref[p]
        o_p = (jnp.dot(scores.astype(BF16), vbd, preferred_element_type=F32)
               + _dot_nt(qs[:, pl_], st.astype(BF16)))
        outs.append(o_p)
        upd = jnp.dot(vp.T.astype(BF16), kd[:, pl_], preferred_element_type=F32)
        st_rows_lo = lax.broadcasted_iota(jnp.int32, upd.shape, 0) < GLA_DV
        bd = st_rows_lo == _lane_lt(upd.shape, GLA_DK)
        st_ref[p] = st * jnp.exp2(b_tot[:, pl_]) + jnp.where(bd, upd, 0.0)
        yield
    emit(jnp.concatenate(outs, axis=1))


def _gla_kernel(*refs, seq, has_init, want_state):
    q_ref, k_ref, v_ref, ldf_ref, ldb_ref = refs[:5]
    pos = 5
    if has_init:
        s0f_ref, s0b_ref = refs[pos:pos + 2]
        pos += 2
    o_ref = refs[pos]
    pos += 1
    if want_state:
        sf_ref, sb_ref = refs[pos:pos + 2]
        pos += 2
    st_ref, sel_ref, acc_ref = refs[pos:pos + 3]
    t = GLA_TILE
    nt = seq // t
    n_pair = GLA_HEADS // 2

    kk = lax.broadcasted_iota(jnp.int32, (GLA_SUB * LANES, 2 * t), 0)
    nn = lax.broadcasted_iota(jnp.int32, (GLA_SUB * LANES, 2 * t), 1)
    sel = ((nn & (GLA_SUB - 1)) == (kk >> LANES_LOG2)) & (((kk & (LANES - 1)) >= GLA_DK) == (nn >= t))
    sel_ref[...] = jnp.where(sel, 1.0, 0.0).astype(BF16)

    def load_state(st, s0_ref):
        for p in range(n_pair):
            if s0_ref is None:
                st[p] = jnp.zeros(st.shape[1:], F32)
            else:
                x = jnp.concatenate([s0_ref[0, 2 * p], s0_ref[0, 2 * p + 1]], axis=0)
                xt = x.T
                lo = _lane_lt(xt.shape, GLA_DK)
                st[p] = jnp.concatenate([jnp.where(lo, xt, 0.0), jnp.where(lo, 0.0, xt)], axis=0)

    def store_state(st, out_ref):
        for p in range(n_pair):
            s = st[p]
            lo = _lane_lt((GLA_DV, LANES), GLA_DK)
            z = jnp.where(lo, s[0:GLA_DV, :], s[GLA_DV:2 * GLA_DV, :]).T
            out_ref[0, 2 * p] = z[0:GLA_DK, :]
            out_ref[0, 2 * p + 1] = z[GLA_DK:2 * GLA_DK, :]

    dirs = ((False, ldf_ref, st_ref.at[0]), (True, ldb_ref, st_ref.at[1]))
    for rev, _, st in dirs:
        load_state(st, (s0b_ref if rev else s0f_ref) if has_init else None)
    acc_ref[...] = jnp.zeros(acc_ref.shape, F32)

    def tile_body(ti, carry):
        def walk(rev, ld_ref, st):
            tile = (nt - 1 - ti) if rev else ti
            r0 = pl.multiple_of(tile * t, t)
            rows = pl.ds(r0, t)

            def emit(o):
                acc_ref[rows, :] = acc_ref[rows, :] + o

            return _gla_tile(q_ref[rows, :], k_ref[rows, :], v_ref[rows, :].astype(F32), ld_ref[rows, :],
                             st, sel_ref, emit, rev=rev)

        _interleave([walk(*d) for d in dirs])
        return carry

    lax.fori_loop(0, nt, tile_body, 0)
    o_ref[...] = acc_ref[...].astype(o_ref.dtype)
    if want_state:
        for rev, _, st in dirs:
            store_state(st, sb_ref if rev else sf_ref)


def _gla(qb, kb, vb, ldf, ldb, s0f, s0b, *, n_seq, seq, row_block0, want_state):
    has_init = s0f is not None
    kern = functools.partial(_gla_kernel, seq=seq, has_init=has_init, want_state=want_state)
    tok = lambda w: pl.BlockSpec((seq, w), lambda b: (row_block0 + b, 0))
    in_specs = [tok(B_QK), tok(B_QK), tok(B_V), tok(B_QK), tok(B_QK)]
    args = [qb, kb, vb, ldf, ldb]
    state_spec = pl.BlockSpec((1, GLA_HEADS, GLA_DK, GLA_DV), lambda b: (b, 0, 0, 0))
    if has_init:
        in_specs += [state_spec, state_spec]
        args += [s0f, s0b]
    out_specs = [pl.BlockSpec((seq, B_V), lambda b: (b, 0))]
    out_shape = [jax.ShapeDtypeStruct((n_seq * seq, B_V), BF16)]
    if want_state:
        out_specs += [state_spec, state_spec]
        out_shape += [jax.ShapeDtypeStruct((n_seq, GLA_HEADS, GLA_DK, GLA_DV), F32)] * 2
    return pl.pallas_call(
        kern,
        grid=(n_seq,),
        in_specs=in_specs,
        out_specs=out_specs,
        out_shape=out_shape,
        scratch_shapes=[
            pltpu.VMEM((2, GLA_HEADS // 2, 2 * GLA_DV, LANES), F32),
            pltpu.VMEM((GLA_SUB * LANES, 2 * GLA_TILE), BF16),
            pltpu.VMEM((seq, B_V), F32),
        ],
        compiler_params=pltpu.CompilerParams(
            dimension_semantics=("parallel",), vmem_limit_bytes=VMEM_LIMIT_BYTES),
        name="gla_state" if want_state else "gla_latent",
    )(*args)


def _gated_mix(attn, gla, gb, gn_ref):
    parts = [attn]
    for hd in range(GLA_HEADS):
        sl = slice(hd * GLA_DV, (hd + 1) * GLA_DV)
        g = _rms(gla[:, sl].astype(F32), gn_ref[:, sl]) * _silu(gb[:, sl].astype(F32))
        parts.append(g.astype(BF16))
    return jnp.concatenate(parts, axis=1)


def _gelu(x):
    return 0.5 * x * (1.0 + lax.erf(x * (2.0 ** -0.5)))


def _cm_kernel(x_ref, mod_ref, gpre_ref, gpost_ref, win_ref, vg_ref, vb_ref, ws_ref, bs_ref, wout_ref,
               o_ref, win_bf, wout_bf, *, layer, group):
    @pl.when(pl.program_id(0) == 0)
    def _cast_weights():
        win_bf[...] = win_ref[0].astype(BF16)
        wout_bf[...] = wout_ref[0].astype(BF16)

    width = wout_bf.shape[0]
    gw = width // CMLP_GROUPS
    ws = [ws_ref[0, g].astype(BF16) for g in range(CMLP_GROUPS)]
    mods = _mod_rows(mod_ref, group(pl.program_id(0)))
    g_pre = _norm_row(gpre_ref, layer, 1)
    g_post = _norm_row(gpost_ref, layer, 1)
    eye = (lax.broadcasted_iota(jnp.int32, (CHUNK, CHUNK), 0)
           == lax.broadcasted_iota(jnp.int32, (CHUNK, CHUNK), 1))
    bias_cols = [jnp.sum(jnp.where(eye, jnp.broadcast_to(bs_ref[0, g:g + 1, :], (CHUNK, CHUNK)), 0.0),
                         axis=1, keepdims=True) for g in range(CMLP_GROUPS)]

    def phase_in(rows):
        x = x_ref[rows, :]
        h = _sub_in(x, mods, g_pre, 1).astype(BF16)
        v = _gelu(jnp.dot(h, win_bf[:, width:2 * width], preferred_element_type=F32))
        u = _gelu(jnp.dot(h, win_bf[:, 0:width], preferred_element_type=F32))
        return x, u, v

    def phase_mix(x, u, v):
        mu = jnp.mean(v, axis=-1, keepdims=True)
        vc = v - mu
        var = jnp.mean(vc * vc, axis=-1, keepdims=True)
        vn = (vc * lax.rsqrt(var + EPS) * vg_ref[...] + vb_ref[...]).astype(BF16)
        chunk_rows = []
        for c in range(x.shape[0] // CHUNK):
            cols = []
            for g in range(CMLP_GROUPS):
                blk = vn[c * CHUNK:(c + 1) * CHUNK, g * gw:(g + 1) * gw]
                cols.append(jnp.dot(ws[g], blk, preferred_element_type=F32) + bias_cols[g])
            chunk_rows.append(jnp.concatenate(cols, axis=1))
        mixed = jnp.concatenate(chunk_rows, axis=0)
        return x, (u * mixed).astype(BF16)

    def phase_out(rows, x, m):
        y = jnp.dot(m, wout_bf[...], preferred_element_type=F32)
        o_ref[rows, :] = _sub_out(x, y, mods, g_post, 1, 1.0)

    tiles = _sub_tiles(x_ref.shape[0])
    pending = None
    for rows in tiles:
        cur = phase_in(rows)
        if pending is not None:
            phase_out(pending[0], *phase_mix(*pending[1]))
        pending = (rows, cur)
    phase_out(pending[0], *phase_mix(*pending[1]))


def _chunk_mlp(x, mods, norm_pre, norm_post, w_in, v_gain, v_bias, w_s, b_s, w_out, *, layer, o, group):
    n_tok, d = x.shape
    tm = WIDE_TILE
    width = w_out.shape[1]
    kern = functools.partial(_cm_kernel, layer=layer, group=group)
    return pl.pallas_call(
        kern,
        grid=(n_tok // tm,),
        in_specs=[pl.BlockSpec((tm, d), lambda i: (i, 0))] + _cond_specs(mods, norm_pre, norm_post, layer) + [
            pl.BlockSpec((1, d, 2 * width), lambda i: (o, 0, 0), pipeline_mode=pl.Buffered(1)),
            pl.BlockSpec((1, width), lambda i: (o, 0)),
            pl.BlockSpec((1, width), lambda i: (o, 0)),
            pl.BlockSpec((1, CMLP_GROUPS, CHUNK, CHUNK), lambda i: (o, 0, 0, 0)),
            pl.BlockSpec((1, CMLP_GROUPS, CHUNK), lambda i: (o, 0, 0)),
            pl.BlockSpec((1, width, d), lambda i: (o, 0, 0), pipeline_mode=pl.Buffered(1)),
        ],
        out_specs=pl.BlockSpec((tm, d), lambda i: (i, 0)),
        out_shape=jax.ShapeDtypeStruct((n_tok, d), F32),
        scratch_shapes=[pltpu.VMEM((d, 2 * width), BF16), pltpu.VMEM((width, d), BF16)],
        compiler_params=pltpu.CompilerParams(
            dimension_semantics=("arbitrary",), vmem_limit_bytes=VMEM_LIMIT_BYTES),
        name="chunk_mlp",
    )(x, mods, norm_pre, norm_post, w_in, v_gain, v_bias, w_s, b_s, w_out)


def kernel(x_prompt, x_sample, cache_k, cache_v, state_gla_fwd, state_gla_bwd, c, c_ctx, w_mod, b_mod, norm_pre, norm_post, ffn_w_gate, ffn_w_up, ffn_w_down, ev_w_in, ev_w_out, ev_sink, gla_wa_f, gla_ba_f, gla_wa_b, gla_ba_b, gla_norm, cm_w_in, cm_v_gain, cm_v_bias, cm_w_s, cm_b_s, cm_w_out):
    batch, seq, d = x_prompt.shape
    dec_batch, dec_seq, _ = x_sample.shape
    depth = w_mod.shape[0]
    n_prompt_tok = batch * seq
    n_sample_tok = dec_batch * dec_seq
    assert n_prompt_tok % dec_seq == 0 and dec_seq % TOKEN_TILE == 0
    group = _group_index_map(TOKEN_TILE, n_prompt_tok, dec_seq)
    np_tiles = n_prompt_tok // TOKEN_TILE
    ffn = functools.partial(_ffn, group=group, np_tiles=np_tiles)
    assert dec_seq % WIDE_TILE == 0
    group_w = _group_index_map(WIDE_TILE, n_prompt_tok, dec_seq)

    xs = (x_prompt.reshape(n_prompt_tok, d), x_sample.reshape(n_sample_tok, d))
    ev_w_in_t = jnp.swapaxes(ev_w_in, 1, 2)

    mods = _adaln_mods(c_ctx, c, w_mod, b_mod)
    gains_pre = norm_pre.reshape(depth * N_SUB, d)
    gains_post = norm_post.reshape(depth * N_SUB, d)
    cond = (mods, gains_pre, gains_post)

    new_k, new_v, new_sf, new_sb = [], [], [], []
    for layer in range(depth):
        (x,) = ffn(xs, *cond, ffn_w_gate, ffn_w_up, ffn_w_down,
                   sub=0, layer=layer, half=0, split_out=False)
        if layer % 2 == 0:
            e = layer // 2
            qa, ka, va, qb, kb, vb, gb, ldf, ldb = _even_in(
                x, *cond, ev_w_in_t, gla_wa_f, gla_wa_b, gla_ba_f, gla_ba_b,
                layer=layer, e=e, group=group_w)
            sink = ev_sink[e]
            attn_p, k_t, v_t = _attn_context(sink, qa, ka, va, n_seq=batch, seq=seq)
            k_ctx = jnp.transpose(cache_k[:, e], (0, 2, 3, 1)).reshape(dec_batch, A_KV, -1)
            v_ctx = jnp.transpose(cache_v[:, e], (0, 2, 3, 1)).reshape(dec_batch, A_KV, -1)
            attn_s = _attn_latent(sink, qa, ka, va, k_ctx, v_ctx, n_seq=dec_batch, seq=dec_seq,
                                  row_block0=n_prompt_tok // dec_seq)
            gla_p, s_f, s_b = _gla(qb, kb, vb, ldf, ldb, None, None, n_seq=batch, seq=seq,
                                   row_block0=0, want_state=True)
            (gla_s,) = _gla(qb, kb, vb, ldf, ldb, state_gla_fwd[:, e], state_gla_bwd[:, e],
                            n_seq=dec_batch, seq=dec_seq, row_block0=n_prompt_tok // dec_seq,
                            want_state=False)
            mix = (e, (attn_p, attn_s), (gla_p, gla_s), gb, gla_norm, ev_w_out)
            for cache_t, dst in ((k_t, new_k), (v_t, new_v)):
                dst.append(jnp.transpose(cache_t.reshape(batch, ATTN_KV_HEADS, HEAD_DIM, seq), (0, 3, 1, 2)))
            new_sf.append(s_f)
            new_sb.append(s_b)
        else:
            o = layer // 2
            mix = None
            x = _chunk_mlp(x, *cond, cm_w_in, cm_v_gain, cm_v_bias, cm_w_s, cm_b_s, cm_w_out,
                           layer=layer, o=o, group=group_w)
        xs = ffn((x,), *cond, ffn_w_gate, ffn_w_up, ffn_w_down,
                 sub=2, layer=layer, half=1, split_out=(layer == depth - 1), mix=mix)

    y_prompt = xs[0].reshape(batch, seq, d)
    y_sample = xs[1].reshape(dec_batch, dec_seq, d)
    return (y_prompt, y_sample, jnp.stack(new_k, axis=1), jnp.stack(new_v, axis=1),
            jnp.stack(new_sf, axis=1), jnp.stack(new_sb, axis=1))
```

```python
import functools
import math

import jax
import jax.numpy as jnp
from jax import lax
from jax.experimental import pallas as pl
from jax.experimental.pallas import tpu as pltpu

F32 = jnp.float32
BF16 = jnp.bfloat16

EPS = 1e-6
NEG_INF = -1e30
FFN_RESIDUAL = 0.5
N_SUB = 3

ATTN_HEADS = 8
ATTN_KV_HEADS = 2
HEAD_DIM = 64
ATTN_BLOCK = 128
WINDOW = 128
GRID_W = 64
GRID_W_LOG2 = 6
ROPE_BASE = 10000.0
GLA_HEADS = 4
GLA_DK = 64
GLA_DV = 128
GLA_RANK = 16
GLA_TAU = 16.0
CHUNK = 128
CMLP_GROUPS = 4

A_Q = ATTN_HEADS * HEAD_DIM
A_KV = ATTN_KV_HEADS * HEAD_DIM
B_QK = GLA_HEADS * GLA_DK
B_V = GLA_HEADS * GLA_DV
EV_MAIN = A_Q + 2 * A_KV + 2 * B_QK + 2 * B_V
EV_OFFS = (0, A_Q, A_Q + A_KV, A_Q + 2 * A_KV, A_Q + 2 * A_KV + B_QK,
           A_Q + 2 * A_KV + 2 * B_QK, A_Q + 2 * A_KV + 2 * B_QK + B_V, EV_MAIN)

LANES = 128
LANES_LOG2 = 7
SUBLANES = 8
VMEM_LIMIT_BYTES = 56 * 1024 * 1024

TOKEN_TILE = 512
WIDE_TILE = 1024
SUB_ROWS = 512
ADALN_COLS = 3072
FFN_CHUNK = 256
GLA_TILE = 128
GLA_SUB = 8
GLA_PROMPT_SEQS = 4
ATTN_PROMPT_SEQS = 2


def _group_index_map(tile_rows, n_prompt_tok, seq_tok):
    def group(i):
        return jnp.maximum(i * tile_rows - (n_prompt_tok - seq_tok), 0) // seq_tok
    return group


def _sub_tiles(n_rows):
    assert n_rows % SUB_ROWS == 0
    return [slice(r, r + SUB_ROWS) for r in range(0, n_rows, SUB_ROWS)]


def _rms(x, g):
    ms = jnp.mean(x * x, axis=-1, keepdims=True)
    return x * lax.rsqrt(ms + EPS) * g


def _mod_rows(mod_ref, g):
    return [mod_ref[k, pl.ds(g, 1), :] for k in range(3 * N_SUB)]


def _sub_in(x, mods, g_pre, sub):
    shift, scale = mods[3 * sub + 0], mods[3 * sub + 1]
    return _rms(x, g_pre) * (1.0 + scale) + shift


def _sub_out(x, y, mods, g_post, sub, coef):
    return x + (coef * mods[3 * sub + 2]) * _rms(y, g_post)


def _norm_row(ref, layer, sub):
    r = layer * N_SUB + sub
    return ref[r:r + 1, :]


def _silu(x):
    return x * jax.nn.sigmoid(x)


def _mod_kernel(cctx_ref, c_ref, w_ref, b_ref, o_ref):
    _, n_kinds, rows, d = o_ref.shape
    r = lax.broadcasted_iota(jnp.int32, (rows, d), 0)
    cond = jnp.where(r == 0, jnp.broadcast_to(cctx_ref[...], (rows, d)), 0.0)
    for j in range(c_ref.shape[0]):
        cond = jnp.where(r == 1 + j, jnp.broadcast_to(c_ref[j:j + 1, :], (rows, d)), cond)
    s = _silu(cond).astype(BF16)
    bias = b_ref[pl.ds(pl.program_id(0), 1), :]
    res = jnp.dot(s, w_ref[0].astype(BF16), preferred_element_type=F32) + bias
    for q in range(n_kinds):
        o_ref[0, q] = res[:, q * d:(q + 1) * d]


def _adaln_mods(c_ctx, c, w_mod, b_mod):
    depth, d, n = w_mod.shape
    tn = ADALN_COLS
    assert n % tn == 0 and tn % d == 0
    rows = -(-(1 + c.shape[0]) // SUBLANES) * SUBLANES
    return pl.pallas_call(
        _mod_kernel,
        grid=(depth, n // tn),
        in_specs=[
            pl.BlockSpec((1, d), lambda l, j: (0, 0)),
            pl.BlockSpec(c.shape, lambda l, j: (0, 0)),
            pl.BlockSpec((1, d, tn), lambda l, j: (l, 0, j)),
            pl.BlockSpec((depth, tn), lambda l, j: (0, j)),
        ],
        out_specs=pl.BlockSpec((1, tn // d, rows, d), lambda l, j: (l, j, 0, 0)),
        out_shape=jax.ShapeDtypeStruct((depth, n // d, rows, d), F32),
        compiler_params=pltpu.CompilerParams(
            dimension_semantics=("parallel", "parallel"), vmem_limit_bytes=VMEM_LIMIT_BYTES),
        name="adaln_mods",
    )(c_ctx.reshape(1, d), c, w_mod, b_mod)


def _tile_specs(tm, width, np_tiles):
    return [pl.BlockSpec((tm, width), lambda i: (jnp.minimum(i, np_tiles - 1), 0)),
            pl.BlockSpec((tm, width), lambda i: (jnp.maximum(i - np_tiles, 0), 0))]


def _pick(refs, is_prompt, rows=slice(None)):
    if len(refs) == 1:
        return refs[0][rows, :]
    return jnp.where(is_prompt, refs[0][rows, :], refs[1][rows, :])


def _cond_specs(mods, norm_pre, norm_post, layer):
    return [pl.BlockSpec((None,) + mods.shape[1:], lambda i: (layer, 0, 0, 0)),
            pl.BlockSpec(norm_pre.shape, lambda i: (0, 0)),
            pl.BlockSpec(norm_post.shape, lambda i: (0, 0))]


def _ffn_kernel(*refs, sub, layer, half, n_chunks, n_x, n_o, np_tiles, group):
    x_refs = refs[:n_x]
    mod_ref, gpre_ref, gpost_ref, wg_hbm, wu_hbm, wd_hbm = refs[n_x:n_x + 6]
    o_refs = refs[n_x + 6:n_x + 6 + n_o]
    wg_bf, wu_bf, wd_bf, st_g, st_u, st_d, sem = refs[n_x + 6 + n_o:]
    fc = FFN_CHUNK
    step = pl.program_id(0)
    is_prompt = step < np_tiles
    mods = _mod_rows(mod_ref, group(step))
    g_pre = _norm_row(gpre_ref, layer, sub)
    g_post = _norm_row(gpost_ref, layer, sub)

    def chunk_copies(c, slot):
        cols = pl.ds(c * fc, fc)
        return (
            pltpu.make_async_copy(wg_hbm.at[layer, half, :, cols], st_g.at[slot], sem.at[0, slot]),
            pltpu.make_async_copy(wu_hbm.at[layer, half, :, cols], st_u.at[slot], sem.at[1, slot]),
            pltpu.make_async_copy(wd_hbm.at[layer, half, cols, :], st_d.at[slot], sem.at[2, slot]),
        )

    def run(stage_weights):
        if stage_weights:
            for cp in chunk_copies(0, 0):
                cp.start()
        x = _pick(x_refs, is_prompt)
        h = _sub_in(x, mods, g_pre, sub).astype(BF16)
        y = jnp.zeros(x.shape, F32)
        for c in range(n_chunks):
            sl = slice(c * fc, (c + 1) * fc)
            if stage_weights:
                slot = c % 2
                if c + 1 < n_chunks:
                    for cp in chunk_copies(c + 1, 1 - slot):
                        cp.start()
                for cp in chunk_copies(c, slot):
                    cp.wait()
                wg_bf[:, sl] = st_g[slot].astype(BF16)
                wu_bf[:, sl] = st_u[slot].astype(BF16)
                wd_bf[sl, :] = st_d[slot].astype(BF16)
            g = jnp.dot(h, wg_bf[:, sl], preferred_element_type=F32)
            u = jnp.dot(h, wu_bf[:, sl], preferred_element_type=F32)
            a = (_silu(g) * u).astype(BF16)
            y = y + jnp.dot(a, wd_bf[sl, :], preferred_element_type=F32)
        out = _sub_out(x, y, mods, g_post, sub, FFN_RESIDUAL)
        if n_o == 1:
            o_refs[0][...] = out
        else:
            @pl.when(is_prompt)
            def _store_prompt():
                o_refs[0][...] = out

            @pl.when(jnp.logical_not(is_prompt))
            def _store_latent():
                o_refs[1][...] = out

    @pl.when(step == 0)
    def _first_tile():
        run(True)

    @pl.when(step > 0)
    def _other_tiles():
        run(False)


def _ffn(xs, mods, norm_pre, norm_post, wg, wu, wd, *, sub, layer, half, group, np_tiles, split_out):
    d = xs[0].shape[-1]
    n_tok = sum(x.shape[0] for x in xs)
    d_ff = wg.shape[-1]
    tm = TOKEN_TILE
    fc = FFN_CHUNK
    n_chunks = d_ff // fc
    assert n_chunks * fc == d_ff and n_tok % tm == 0
    n_x, n_o = len(xs), (2 if split_out else 1)
    kern = functools.partial(_ffn_kernel, sub=sub, layer=layer, half=half, n_chunks=n_chunks,
                             n_x=n_x, n_o=n_o, np_tiles=np_tiles, group=group)
    whole = pl.BlockSpec((tm, d), lambda i: (i, 0))
    x_specs = _tile_specs(tm, d, np_tiles) if n_x == 2 else [whole]
    n_p = np_tiles * tm
    if split_out:
        out_specs = _tile_specs(tm, d, np_tiles)
        out_shape = [jax.ShapeDtypeStruct((n_p, d), F32), jax.ShapeDtypeStruct((n_tok - n_p, d), F32)]
    else:
        out_specs = [whole]
        out_shape = [jax.ShapeDtypeStruct((n_tok, d), F32)]
    return pl.pallas_call(
        kern,
        grid=(n_tok // tm,),
        in_specs=x_specs + _cond_specs(mods, norm_pre, norm_post, layer) + [
            pl.BlockSpec(memory_space=pl.ANY),
            pl.BlockSpec(memory_space=pl.ANY),
            pl.BlockSpec(memory_space=pl.ANY),
        ],
        out_specs=out_specs,
        out_shape=out_shape,
        scratch_shapes=[
            pltpu.VMEM((d, d_ff), BF16),
            pltpu.VMEM((d, d_ff), BF16),
            pltpu.VMEM((d_ff, d), BF16),
            pltpu.VMEM((2, d, fc), F32),
            pltpu.VMEM((2, d, fc), F32),
            pltpu.VMEM((2, fc, d), F32),
            pltpu.SemaphoreType.DMA((3, 2)),
        ],
        compiler_params=pltpu.CompilerParams(
            dimension_semantics=("arbitrary",), vmem_limit_bytes=VMEM_LIMIT_BYTES),
        name=f"ffn_l{layer}h{half}",
    )(*xs, mods, norm_pre, norm_post, wg, wu, wd)


def _log_sigmoid(x):
    return jnp.minimum(x, 0.0) - jnp.log1p(jnp.exp(-jnp.abs(x)))


def _evin_kernel(x_ref, mod_ref, gpre_ref, gpost_ref, w_ref, waf_ref, wab_ref, baf_ref, bab_ref,
                 qa_ref, ka_ref, va_ref, qb_ref, kb_ref, vb_ref, gb_ref, ldf_ref, ldb_ref,
                 w_bf, wa_bf, *, layer, group):
    @pl.when(pl.program_id(0) == 0)
    def _cast_weights():
        w_bf[0:EV_MAIN, :] = w_ref[0, 0:EV_MAIN, :].astype(BF16)
        w_bf[EV_MAIN:EV_MAIN + LANES, :] = jnp.zeros((LANES, w_bf.shape[1]), BF16)
        w_bf[EV_MAIN:EV_MAIN + 2 * GLA_RANK, :] = w_ref[0, EV_MAIN:EV_MAIN + 2 * GLA_RANK, :].astype(BF16)
        wa_bf[...] = jnp.zeros(wa_bf.shape, BF16)
        wa_bf[0:GLA_RANK, 0:B_QK] = waf_ref[0].astype(BF16)
        wa_bf[GLA_RANK:2 * GLA_RANK, B_QK:2 * B_QK] = wab_ref[0].astype(BF16)

    o = EV_OFFS
    mods = _mod_rows(mod_ref, group(pl.program_id(0)))
    g_pre = _norm_row(gpre_ref, layer, 1)

    def project(rows):
        h = _sub_in(x_ref[rows, :], mods, g_pre, 1).astype(BF16)
        return _dot_nt(h, w_bf[...])

    def finish(rows, full):
        qa_ref[rows, :] = (full[:, o[0]:o[1]] * (HEAD_DIM ** -0.5)).astype(qa_ref.dtype)
        ka_ref[rows, :] = full[:, o[1]:o[2]]
        va_ref[rows, :] = full[:, o[2]:o[3]]
        qb_ref[rows, :] = full[:, o[3]:o[4]] * (GLA_DK ** -0.5)
        kb_ref[rows, :] = full[:, o[4]:o[5]]
        vb_ref[rows, :] = full[:, o[5]:o[6]].astype(vb_ref.dtype)
        gb_ref[rows, :] = full[:, o[6]:o[7]].astype(gb_ref.dtype)
        lr = full[:, EV_MAIN:EV_MAIN + LANES].astype(BF16)
        logits = jnp.dot(lr, wa_bf[...], preferred_element_type=F32)
        ldf_ref[rows, :] = _log_sigmoid(logits[:, 0:B_QK] + baf_ref[...]) * (1.0 / GLA_TAU)
        ldb_ref[rows, :] = _log_sigmoid(logits[:, B_QK:2 * B_QK] + bab_ref[...]) * (1.0 / GLA_TAU)

    pending = None
    for rows in _sub_tiles(x_ref.shape[0]):
        full = project(rows)
        if pending is not None:
            finish(*pending)
        pending = (rows, full)
    finish(*pending)


def _even_in(x, mods, norm_pre, norm_post, w_in_t, wa_f, wa_b, ba_f, ba_b, *, layer, e, group):
    n_tok, d = x.shape
    tm = WIDE_TILE
    ev_in = w_in_t.shape[1]
    rank = wa_f.shape[1]
    assert rank == GLA_RANK and ev_in == EV_MAIN + 2 * rank
    widths = (A_Q, A_KV, A_KV, B_QK, B_QK, B_V, B_V, B_QK, B_QK)
    dtypes = (BF16, F32, F32, F32, F32, BF16, BF16, F32, F32)
    kern = functools.partial(_evin_kernel, layer=layer, group=group)
    return pl.pallas_call(
        kern,
        grid=(n_tok // tm,),
        in_specs=[pl.BlockSpec((tm, d), lambda i: (i, 0))] + _cond_specs(mods, norm_pre, norm_post, layer) + [
            pl.BlockSpec((1, ev_in, d), lambda i: (e, 0, 0), pipeline_mode=pl.Buffered(1)),
            pl.BlockSpec((1, rank, B_QK), lambda i: (e, 0, 0)),
            pl.BlockSpec((1, rank, B_QK), lambda i: (e, 0, 0)),
            pl.BlockSpec((1, B_QK), lambda i: (e, 0)),
            pl.BlockSpec((1, B_QK), lambda i: (e, 0)),
        ],
        out_specs=[pl.BlockSpec((tm, w), lambda i: (i, 0)) for w in widths],
        out_shape=[jax.ShapeDtypeStruct((n_tok, w), t) for w, t in zip(widths, dtypes)],
        scratch_shapes=[pltpu.VMEM((EV_MAIN + LANES, d), BF16), pltpu.VMEM((LANES, 2 * B_QK), BF16)],
        compiler_params=pltpu.CompilerParams(
            dimension_semantics=("arbitrary",), vmem_limit_bytes=VMEM_LIMIT_BYTES),
        name="even_in",
    )(x, mods, norm_pre, norm_post, w_in_t, wa_f, wa_b, ba_f, ba_b)


def _lane_lt(shape, n):
    return lax.broadcasted_iota(jnp.int32, shape, len(shape) - 1) < n


def _dup_kv_head(x, g):
    sw = pltpu.roll(x, HEAD_DIM, 1)
    lo = _lane_lt(x.shape, HEAD_DIM)
    return jnp.where(lo, x, sw) if g == 0 else jnp.where(lo, sw, x)


def _dot_nt(a, b):
    return lax.dot_general(a, b, (((1,), (1,)), ((), ())), preferred_element_type=F32)


ATTN_GROUP = ATTN_HEADS // ATTN_KV_HEADS


def _sink_column(sink_ref, g, t):
    rows = lax.broadcasted_iota(jnp.int32, (ATTN_GROUP * t, 1), 0)
    col = jnp.full((ATTN_GROUP * t, 1), sink_ref[ATTN_GROUP * g], F32)
    for j in range(1, ATTN_GROUP):
        col = jnp.where(rows >= j * t, sink_ref[ATTN_GROUP * g + j], col)
    return col


def _attn_scores(q_pairs, keys):
    t = q_pairs[0].shape[0]
    lo = _lane_lt((t, LANES), HEAD_DIM)
    zero = jnp.zeros((t, LANES), BF16)
    q4 = jnp.concatenate([jnp.where(lo, q_pairs[0], zero), jnp.where(lo, zero, q_pairs[0]),
                          jnp.where(lo, q_pairs[1], zero), jnp.where(lo, zero, q_pairs[1])], axis=0)
    scores = []
    for k2, _, mask, feature_major in keys:
        s = jnp.dot(q4, k2, preferred_element_type=F32) if feature_major else _dot_nt(q4, k2)
        if mask is not None:
            s = jnp.where(mask, s, NEG_INF)
        scores.append(s)
    return scores


def _attn_finish(scores, sink_col, keys):
    t = scores[0].shape[0] // ATTN_GROUP
    lo = _lane_lt((t, LANES), HEAD_DIM)
    mx = sink_col
    for s in scores:
        mx = jnp.maximum(mx, jnp.max(s, axis=-1, keepdims=True))
    den = jnp.exp(sink_col - mx)
    o = None
    for s, (_, v2, _, feature_major) in zip(scores, keys):
        p = jnp.exp(s - mx)
        den = den + jnp.sum(p, axis=-1, keepdims=True)
        pb = p.astype(BF16)
        pv = _dot_nt(pb, v2) if feature_major else jnp.dot(pb, v2, preferred_element_type=F32)
        o = pv if o is None else o + pv
    o = o / den
    return (jnp.where(lo, o[0:t], o[t:2 * t]), jnp.where(lo, o[2 * t:3 * t], o[3 * t:4 * t]))


def _attn_ctx_kernel(sink_ref, q_ref, k_ref, v_ref, o_ref, kt_ref, vt_ref):
    per_step, _, t = kt_ref.shape
    work = []
    for s in range(per_step):
        rows = slice(s * t, (s + 1) * t)
        k = k_ref[rows, :]
        v = v_ref[rows, :]
        kt_ref[s] = k.T
        vt_ref[s] = v.T
        for g in range(ATTN_KV_HEADS):
            keys = [(_dup_kv_head(k, g).astype(BF16), _dup_kv_head(v, g).astype(BF16), None, False)]
            cols = (2 * g, 2 * g + 1)
            scores = _attn_scores([q_ref[rows, m * LANES:(m + 1) * LANES] for m in cols], keys)
            work.append((rows, cols, scores, keys, g))
    for rows, cols, scores, keys, g in work:
        outs = _attn_finish(scores, _sink_column(sink_ref, g, t), keys)
        for m, o in zip(cols, outs):
            o_ref[rows, m * LANES:(m + 1) * LANES] = o.astype(o_ref.dtype)


def _attn_context(sink, qa, ka, va, *, n_seq, seq):
    per_step = ATTN_PROMPT_SEQS
    assert n_seq % per_step == 0
    n_seq, rows = n_seq // per_step, per_step * seq
    cache_spec = pl.BlockSpec((per_step, A_KV, seq), lambda b: (b, 0, 0))
    return pl.pallas_call(
        _attn_ctx_kernel,
        grid=(n_seq,),
        in_specs=[
            pl.BlockSpec(memory_space=pltpu.SMEM),
            pl.BlockSpec((rows, A_Q), lambda b: (b, 0)),
            pl.BlockSpec((rows, A_KV), lambda b: (b, 0)),
            pl.BlockSpec((rows, A_KV), lambda b: (b, 0)),
        ],
        out_specs=[pl.BlockSpec((rows, A_Q), lambda b: (b, 0)), cache_spec, cache_spec],
        out_shape=[jax.ShapeDtypeStruct((n_seq * rows, A_Q), BF16),
                   jax.ShapeDtypeStruct((n_seq * per_step, A_KV, seq), F32),
                   jax.ShapeDtypeStruct((n_seq * per_step, A_KV, seq), F32)],
        compiler_params=pltpu.CompilerParams(dimension_semantics=("parallel",)),
        name="attn_context",
    )(sink, qa, ka, va)


def _rope_tables(n_tok):
    n_rows = n_tok // GRID_W
    assert n_rows <= GRID_W
    shape = (GRID_W, LANES)
    pos = lax.broadcasted_iota(jnp.int32, shape, 0).astype(F32)
    lane = lax.broadcasted_iota(jnp.int32, shape, 1)
    half = HEAD_DIM // 2
    nf = half // 2
    within = lane & (HEAD_DIM - 1)
    is_col = within >= half
    second = (within & (half - 1)) >= nf
    f = (within & (nf - 1)).astype(F32)
    inv = jnp.exp(f * (-2.0 / half * math.log(ROPE_BASE)))
    ang = pos * inv
    cos_t = jnp.cos(ang)
    sin_t = jnp.sin(ang)
    sin_t = jnp.where(second, sin_t, -sin_t)

    def per_token(tab):
        return jnp.concatenate(
            [jnp.where(is_col, tab, jnp.broadcast_to(tab[r:r + 1, :], shape)) for r in range(n_rows)], axis=0)

    return per_token(cos_t), per_token(sin_t)


def _rope(x, cos, sin_signed):
    nf = HEAD_DIM // 4
    lane = lax.broadcasted_iota(jnp.int32, x.shape, 1)
    second = (lane & (2 * nf - 1)) >= nf
    up = pltpu.roll(x, nf, 1)
    dn = pltpu.roll(x, LANES - nf, 1)
    return x * cos + jnp.where(second, up, dn) * sin_signed


def _attn_lat_kernel(sink_ref, q_ref, k_ref, v_ref, kc_ref, vc_ref, o_ref, qr_ref, kp_ref, vp_ref,
                     *, seq):
    nb = seq // ATTN_BLOCK
    blk = ATTN_BLOCK
    cos, sin_signed = _rope_tables(seq)
    for m in range(A_Q // LANES):
        qr_ref[:, m * LANES:(m + 1) * LANES] = _rope(
            q_ref[:, m * LANES:(m + 1) * LANES].astype(F32), cos, sin_signed).astype(BF16)
    kr = _rope(k_ref[...], cos, sin_signed)
    v = v_ref[...]
    zpad = jnp.zeros((blk, LANES), BF16)
    ctx, sink_cols = [], []
    for g in range(ATTN_KV_HEADS):
        kp_ref[g, 0:blk, :] = zpad
        kp_ref[g, blk + seq:2 * blk + seq, :] = zpad
        vp_ref[g, 0:blk, :] = zpad
        vp_ref[g, blk + seq:2 * blk + seq, :] = zpad
        kp_ref[g, blk:blk + seq, :] = _dup_kv_head(kr, g).astype(BF16)
        vp_ref[g, blk:blk + seq, :] = _dup_kv_head(v, g).astype(BF16)
        hd = slice(g * HEAD_DIM, (g + 1) * HEAD_DIM)
        ctx.append((jnp.concatenate([kc_ref[0, hd, :]] * 2, axis=0).astype(BF16),
                    jnp.concatenate([vc_ref[0, hd, :]] * 2, axis=0).astype(BF16), None, True))
        sink_cols.append(_sink_column(sink_ref, g, blk))
    r = lax.broadcasted_iota(jnp.int32, (ATTN_GROUP * blk, 3 * blk), 0) & (blk - 1)
    c = lax.broadcasted_iota(jnp.int32, (ATTN_GROUP * blk, 3 * blk), 1)
    band = (c >= r) & (c <= r + 2 * WINDOW)

    def block_body(i, carry):
        row0 = pl.multiple_of(i * blk, blk)
        kj = (i - 1) * blk + c
        mask = band & (kj >= 0) & (kj < seq)
        work = []
        for g in range(ATTN_KV_HEADS):
            cols = (2 * g, 2 * g + 1)
            window = (kp_ref[g, pl.ds(row0, 3 * blk), :], vp_ref[g, pl.ds(row0, 3 * blk), :], mask, False)
            keys = [window, ctx[g]]
            scores = _attn_scores([qr_ref[pl.ds(row0, blk), m * LANES:(m + 1) * LANES] for m in cols], keys)
            work.append((cols, scores, keys, g))
        for cols, scores, keys, g in work:
            outs = _attn_finish(scores, sink_cols[g], keys)
            for m, o in zip(cols, outs):
                o_ref[pl.ds(row0, blk), m * LANES:(m + 1) * LANES] = o.astype(o_ref.dtype)
        return carry

    lax.fori_loop(0, nb, block_body, 0)


def _attn_latent(sink, qa, ka, va, k_ctx, v_ctx, *, n_seq, seq, row_block0):
    past = k_ctx.shape[2]
    kern = functools.partial(_attn_lat_kernel, seq=seq)
    return pl.pallas_call(
        kern,
        grid=(n_seq,),
        in_specs=[
            pl.BlockSpec(memory_space=pltpu.SMEM),
            pl.BlockSpec((seq, A_Q), lambda b: (row_block0 + b, 0)),
            pl.BlockSpec((seq, A_KV), lambda b: (row_block0 + b, 0)),
            pl.BlockSpec((seq, A_KV), lambda b: (row_block0 + b, 0)),
            pl.BlockSpec((1, A_KV, past), lambda b: (b, 0, 0)),
            pl.BlockSpec((1, A_KV, past), lambda b: (b, 0, 0)),
        ],
        out_specs=pl.BlockSpec((seq, A_Q), lambda b: (b, 0)),
        out_shape=jax.ShapeDtypeStruct((n_seq * seq, A_Q), BF16),
        scratch_shapes=[
            pltpu.VMEM((seq, A_Q), BF16),
            pltpu.VMEM((ATTN_KV_HEADS, seq + 2 * ATTN_BLOCK, LANES), BF16),
            pltpu.VMEM((ATTN_KV_HEADS, seq + 2 * ATTN_BLOCK, LANES), BF16),
        ],
        compiler_params=pltpu.CompilerParams(dimension_semantics=("parallel",)),
        name="attn_latent",
    )(sink, qa, ka, va, k_ctx, v_ctx)


def _same_block(a, b, size):
    return (a & -size) == (b & -size)


def _split3(x):
    hi = x.astype(BF16)
    r1 = x - hi.astype(F32)
    mid = r1.astype(BF16)
    lo = (r1 - mid.astype(F32)).astype(BF16)
    return hi, mid, lo


def _interleave(generators):
    live = list(generators)
    while live:
        for gen in list(live):
            try:
                next(gen)
            except StopIteration:
                live.remove(gen)


def _gla_tile(q, k, v, ld, st_ref, sel_ref, emit, *, rev):
    t = GLA_TILE
    n_pair = GLA_HEADS // 2
    row = lax.broadcasted_iota(jnp.int32, (t, t), 0)
    col = lax.broadcasted_iota(jnp.int32, (t, t), 1)
    tri = (row <= col) if rev else (row >= col)
    tri = jnp.where(tri, 1.0, 0.0).astype(BF16)
    hi, mid, lo = _split3(ld * math.log2(math.e))
    b = (jnp.dot(tri, hi, preferred_element_type=F32) + jnp.dot(tri, mid, preferred_element_type=F32)
         + jnp.dot(tri, lo, preferred_element_type=F32))
    yield
    b_tot = b[0:1, :] if rev else b[t - 1:t, :]
    qs = (q * jnp.exp2(b)).astype(BF16)
    kd = (k * jnp.exp2(b_tot - b)).astype(BF16)

    rowv = lax.broadcasted_iota(jnp.int32, (t, B_QK), 0)
    levels = []
    half = GLA_SUB
    while 2 * half <= t:
        blk = 2 * half
        pieces = []
        for bs in range(0, t, blk):
            r = bs + half if rev else bs + half - 1
            pieces.append(jnp.broadcast_to(b[r:r + 1, :], (blk, B_QK)))
        ref = jnp.concatenate(pieces, axis=0) if len(pieces) > 1 else pieces[0]
        second = (rowv & (blk - 1)) >= half
        is_q = ~second if rev else second
        ql = (q * jnp.exp2(jnp.where(is_q, b - ref, -jnp.inf))).astype(BF16)
        kl = (k * jnp.exp2(jnp.where(is_q, -jnp.inf, ref - b))).astype(BF16)
        levels.append((blk, ql, kl))
        half = blk
    yield

    rr = lax.broadcasted_iota(jnp.int32, (t, 2 * t), 0)
    ss = lax.broadcasted_iota(jnp.int32, (t, 2 * t), 1) & (t - 1)
    sub_rows = lax.broadcasted_iota(jnp.int32, (GLA_SUB, LANES), 0)
    lane_lo = _lane_lt((t, LANES), GLA_DK)
    lane_lo2 = _lane_lt((2 * t, LANES), GLA_DK)
    par_rows = lax.broadcasted_iota(jnp.int32, (2 * t, LANES), 0) < t
    head_sel = par_rows == lane_lo2

    outs = []
    for p in range(n_pair):
        pl_ = slice(p * LANES, (p + 1) * LANES)
        qp, kp, bp = q[:, pl_], k[:, pl_], b[:, pl_]
        rows = []
        for sb in range(t // GLA_SUB):
            r0 = sb * GLA_SUB
            q_sub = qp[r0:r0 + GLA_SUB, :]
            b_sub = bp[r0:r0 + GLA_SUB, :]
            cols = []
            for j in range(GLA_SUB):
                kj = kp[r0 + j:r0 + j + 1, :]
                bj = bp[r0 + j:r0 + j + 1, :]
                valid = (sub_rows <= j) if rev else (sub_rows >= j)
                e = jnp.exp2(jnp.where(valid, b_sub - bj, -jnp.inf))
                cols.append(q_sub * kj * e)
            rows.append(jnp.concatenate(cols, axis=1))
        e_p = jnp.concatenate(rows, axis=0).astype(BF16)
        yield
        scores = jnp.dot(e_p, sel_ref[...], preferred_element_type=F32)
        scores = jnp.where(_same_block(rr, ss, GLA_SUB), scores, 0.0)
        for blk, ql, kl in levels:
            klp = kl[:, pl_]
            kstack = jnp.concatenate([klp, klp], axis=0)
            kstack = jnp.where(head_sel, kstack, jnp.zeros_like(kstack))
            s_l = _dot_nt(ql[:, pl_], kstack)
            scores = scores + jnp.where(_same_block(rr, ss, blk), s_l, 0.0)
        yield
        vp = v[:, p * 2 * GLA_DV:(p + 1) * 2 * GLA_DV]
        v_lo = _lane_lt(vp.shape, GLA_DV)
        vbd = jnp.concatenate([jnp.where(v_lo, vp, 0.0), jnp.where(v_lo, 0.0, vp)], axis=0).astype(BF16)
        st = st_ref[p]
        o_p = (jnp.dot(scores.astype(BF16), vbd, preferred_element_type=F32)
               + _dot_nt(qs[:, pl_], st.astype(BF16)))
        outs.append(o_p)
        upd = jnp.dot(vp.T.astype(BF16), kd[:, pl_], preferred_element_type=F32)
        st_rows_lo = lax.broadcasted_iota(jnp.int32, upd.shape, 0) < GLA_DV
        bd = st_rows_lo == _lane_lt(upd.shape, GLA_DK)
        st_ref[p] = st * jnp.exp2(b_tot[:, pl_]) + jnp.where(bd, upd, 0.0)
        yield
    emit(jnp.concatenate(outs, axis=1))


def _gla_kernel(*refs, seq, has_init, want_state, per_step):
    q_ref, k_ref, v_ref, ldf_ref, ldb_ref = refs[:5]
    pos = 5
    if has_init:
        s0f_ref, s0b_ref = refs[pos:pos + 2]
        pos += 2
    o_ref = refs[pos]
    pos += 1
    if want_state:
        sf_ref, sb_ref = refs[pos:pos + 2]
        pos += 2
    st_ref, sel_ref, acc_ref = refs[pos:pos + 3]
    t = GLA_TILE
    nt = seq // t
    n_pair = GLA_HEADS // 2

    kk = lax.broadcasted_iota(jnp.int32, (GLA_SUB * LANES, 2 * t), 0)
    nn = lax.broadcasted_iota(jnp.int32, (GLA_SUB * LANES, 2 * t), 1)
    sel = ((nn & (GLA_SUB - 1)) == (kk >> LANES_LOG2)) & (((kk & (LANES - 1)) >= GLA_DK) == (nn >= t))
    sel_ref[...] = jnp.where(sel, 1.0, 0.0).astype(BF16)

    def load_state(st, s0_ref, s):
        for p in range(n_pair):
            if s0_ref is None:
                st[p] = jnp.zeros(st.shape[1:], F32)
            else:
                x = jnp.concatenate([s0_ref[s, 2 * p], s0_ref[s, 2 * p + 1]], axis=0)
                xt = x.T
                lo = _lane_lt(xt.shape, GLA_DK)
                st[p] = jnp.concatenate([jnp.where(lo, xt, 0.0), jnp.where(lo, 0.0, xt)], axis=0)

    def store_state(st, out_ref, s):
        for p in range(n_pair):
            z = st[p]
            lo = _lane_lt((GLA_DV, LANES), GLA_DK)
            z = jnp.where(lo, z[0:GLA_DV, :], z[GLA_DV:2 * GLA_DV, :]).T
            out_ref[s, 2 * p] = z[0:GLA_DK, :]
            out_ref[s, 2 * p + 1] = z[GLA_DK:2 * GLA_DK, :]

    walks = [(s, rev, ldb_ref if rev else ldf_ref, st_ref.at[2 * s + int(rev)])
             for s in range(per_step) for rev in (False, True)]
    for s, rev, _, st in walks:
        load_state(st, (s0b_ref if rev else s0f_ref) if has_init else None, s)
    acc_ref[...] = jnp.zeros(acc_ref.shape, F32)

    def tile_body(ti, carry):
        def walk(s, rev, ld_ref, st):
            tile = (nt - 1 - ti) if rev else ti
            r0 = pl.multiple_of(s * seq + tile * t, t)
            rows = pl.ds(r0, t)

            def emit(o):
                acc_ref[rows, :] = acc_ref[rows, :] + o

            return _gla_tile(q_ref[rows, :], k_ref[rows, :], v_ref[rows, :].astype(F32), ld_ref[rows, :],
                             st, sel_ref, emit, rev=rev)

        _interleave([walk(*w) for w in walks])
        return carry

    lax.fori_loop(0, nt, tile_body, 0)
    o_ref[...] = acc_ref[...].astype(o_ref.dtype)
    if want_state:
        for s, rev, _, st in walks:
            store_state(st, sb_ref if rev else sf_ref, s)


def _gla(qb, kb, vb, ldf, ldb, s0f, s0b, *, n_seq, seq, row_block0, want_state, per_step):
    has_init = s0f is not None
    assert n_seq % per_step == 0 and row_block0 % per_step == 0
    kern = functools.partial(_gla_kernel, seq=seq, has_init=has_init, want_state=want_state,
                             per_step=per_step)
    n_seq, seq_rows, row_block0 = n_seq // per_step, per_step * seq, row_block0 // per_step
    tok = lambda w: pl.BlockSpec((seq_rows, w), lambda b: (row_block0 + b, 0))
    in_specs = [tok(B_QK), tok(B_QK), tok(B_V), tok(B_QK), tok(B_QK)]
    args = [qb, kb, vb, ldf, ldb]
    state_spec = pl.BlockSpec((per_step, GLA_HEADS, GLA_DK, GLA_DV), lambda b: (b, 0, 0, 0))
    if has_init:
        in_specs += [state_spec, state_spec]
        args += [s0f, s0b]
    out_specs = [pl.BlockSpec((seq_rows, B_V), lambda b: (b, 0))]
    out_shape = [jax.ShapeDtypeStruct((n_seq * seq_rows, B_V), BF16)]
    if want_state:
        out_specs += [state_spec, state_spec]
        out_shape += [jax.ShapeDtypeStruct((n_seq * per_step, GLA_HEADS, GLA_DK, GLA_DV), F32)] * 2
    return pl.pallas_call(
        kern,
        grid=(n_seq,),
        in_specs=in_specs,
        out_specs=out_specs,
        out_shape=out_shape,
        scratch_shapes=[
            pltpu.VMEM((2 * per_step, GLA_HEADS // 2, 2 * GLA_DV, LANES), F32),
            pltpu.VMEM((GLA_SUB * LANES, 2 * GLA_TILE), BF16),
            pltpu.VMEM((seq_rows, B_V), F32),
        ],
        compiler_params=pltpu.CompilerParams(
            dimension_semantics=("parallel",), vmem_limit_bytes=VMEM_LIMIT_BYTES),
        name="gla_state" if want_state else "gla_latent",
    )(*args)


def _evout_kernel(x_ref, mod_ref, gpre_ref, gpost_ref, attn_p_ref, attn_s_ref, gla_p_ref, gla_s_ref, gb_ref,
                  gn_ref, w_ref, o_ref, w_bf, *, np_tiles, layer, group):
    @pl.when(pl.program_id(0) == 0)
    def _cast_weights():
        w_bf[...] = w_ref[0].astype(BF16)

    is_prompt = pl.program_id(0) < np_tiles
    mods = _mod_rows(mod_ref, group(pl.program_id(0)))
    g_post = _norm_row(gpost_ref, layer, 1)
    for rows in _sub_tiles(x_ref.shape[0]):
        attn = _pick((attn_p_ref, attn_s_ref), is_prompt, rows)
        gla = _pick((gla_p_ref, gla_s_ref), is_prompt, rows)
        parts = [attn]
        for hd in range(GLA_HEADS):
            sl = slice(hd * GLA_DV, (hd + 1) * GLA_DV)
            g = _rms(gla[:, sl].astype(F32), gn_ref[:, sl]) * _silu(gb_ref[rows, sl].astype(F32))
            parts.append(g.astype(BF16))
        y = jnp.dot(jnp.concatenate(parts, axis=1), w_bf[...], preferred_element_type=F32)
        o_ref[rows, :] = _sub_out(x_ref[rows, :], y, mods, g_post, 1, 1.0)


def _even_out(x, mods, norm_pre, norm_post, attn_ps, gla_ps, gb, gla_norm, w_out, *, layer, e, group,
              np_tiles):
    n_tok, d = x.shape
    tm = WIDE_TILE
    ev_out = w_out.shape[1]
    kern = functools.partial(_evout_kernel, np_tiles=np_tiles, layer=layer, group=group)
    return pl.pallas_call(
        kern,
        grid=(n_tok // tm,),
        in_specs=[pl.BlockSpec((tm, d), lambda i: (i, 0))] + _cond_specs(mods, norm_pre, norm_post, layer) + [
            *_tile_specs(tm, A_Q, np_tiles),
            *_tile_specs(tm, B_V, np_tiles),
            pl.BlockSpec((tm, B_V), lambda i: (i, 0)),
            pl.BlockSpec((1, B_V), lambda i: (e, 0)),
            pl.BlockSpec((1, ev_out, d), lambda i: (e, 0, 0), pipeline_mode=pl.Buffered(1)),
        ],
        out_specs=pl.BlockSpec((tm, d), lambda i: (i, 0)),
        out_shape=jax.ShapeDtypeStruct((n_tok, d), F32),
        scratch_shapes=[pltpu.VMEM((ev_out, d), BF16)],
        compiler_params=pltpu.CompilerParams(
            dimension_semantics=("arbitrary",), vmem_limit_bytes=VMEM_LIMIT_BYTES),
        name="even_out",
    )(x, mods, norm_pre, norm_post, *attn_ps, *gla_ps, gb, gla_norm, w_out)


def _gelu(x):
    return 0.5 * x * (1.0 + lax.erf(x * (2.0 ** -0.5)))


def _cm_kernel(x_ref, mod_ref, gpre_ref, gpost_ref, win_ref, vg_ref, vb_ref, ws_ref, bs_ref, wout_ref,
               o_ref, win_bf, wout_bf, *, layer, group):
    @pl.when(pl.program_id(0) == 0)
    def _cast_weights():
        win_bf[...] = win_ref[0].astype(BF16)
        wout_bf[...] = wout_ref[0].astype(BF16)

    width = wout_bf.shape[0]
    gw = width // CMLP_GROUPS
    ws = [ws_ref[0, g].astype(BF16) for g in range(CMLP_GROUPS)]
    mods = _mod_rows(mod_ref, group(pl.program_id(0)))
    g_pre = _norm_row(gpre_ref, layer, 1)
    g_post = _norm_row(gpost_ref, layer, 1)
    eye = (lax.broadcasted_iota(jnp.int32, (CHUNK, CHUNK), 0)
           == lax.broadcasted_iota(jnp.int32, (CHUNK, CHUNK), 1))
    bias_cols = [jnp.sum(jnp.where(eye, jnp.broadcast_to(bs_ref[0, g:g + 1, :], (CHUNK, CHUNK)), 0.0),
                         axis=1, keepdims=True) for g in range(CMLP_GROUPS)]

    def phase_in(rows):
        x = x_ref[rows, :]
        h = _sub_in(x, mods, g_pre, 1).astype(BF16)
        v = _gelu(jnp.dot(h, win_bf[:, width:2 * width], preferred_element_type=F32))
        u = _gelu(jnp.dot(h, win_bf[:, 0:width], preferred_element_type=F32))
        return x, u, v

    def phase_mix(x, u, v):
        mu = jnp.mean(v, axis=-1, keepdims=True)
        vc = v - mu
        var = jnp.mean(vc * vc, axis=-1, keepdims=True)
        vn = (vc * lax.rsqrt(var + EPS) * vg_ref[...] + vb_ref[...]).astype(BF16)
        chunk_rows = []
        for c in range(x.shape[0] // CHUNK):
            cols = []
            for g in range(CMLP_GROUPS):
                blk = vn[c * CHUNK:(c + 1) * CHUNK, g * gw:(g + 1) * gw]
                cols.append(jnp.dot(ws[g], blk, preferred_element_type=F32) + bias_cols[g])
            chunk_rows.append(jnp.concatenate(cols, axis=1))
        mixed = jnp.concatenate(chunk_rows, axis=0)
        return x, (u * mixed).astype(BF16)

    def phase_out(rows, x, m):
        y = jnp.dot(m, wout_bf[...], preferred_element_type=F32)
        o_ref[rows, :] = _sub_out(x, y, mods, g_post, 1, 1.0)

    tiles = _sub_tiles(x_ref.shape[0])
    pending = None
    for rows in tiles:
        cur = phase_in(rows)
        if pending is not None:
            phase_out(pending[0], *phase_mix(*pending[1]))
        pending = (rows, cur)
    phase_out(pending[0], *phase_mix(*pending[1]))


def _chunk_mlp(x, mods, norm_pre, norm_post, w_in, v_gain, v_bias, w_s, b_s, w_out, *, layer, o, group):
    n_tok, d = x.shape
    tm = WIDE_TILE
    width = w_out.shape[1]
    kern = functools.partial(_cm_kernel, layer=layer, group=group)
    return pl.pallas_call(
        kern,
        grid=(n_tok // tm,),
        in_specs=[pl.BlockSpec((tm, d), lambda i: (i, 0))] + _cond_specs(mods, norm_pre, norm_post, layer) + [
            pl.BlockSpec((1, d, 2 * width), lambda i: (o, 0, 0), pipeline_mode=pl.Buffered(1)),
            pl.BlockSpec((1, width), lambda i: (o, 0)),
            pl.BlockSpec((1, width), lambda i: (o, 0)),
            pl.BlockSpec((1, CMLP_GROUPS, CHUNK, CHUNK), lambda i: (o, 0, 0, 0)),
            pl.BlockSpec((1, CMLP_GROUPS, CHUNK), lambda i: (o, 0, 0)),
            pl.BlockSpec((1, width, d), lambda i: (o, 0, 0), pipeline_mode=pl.Buffered(1)),
        ],
        out_specs=pl.BlockSpec((tm, d), lambda i: (i, 0)),
        out_shape=jax.ShapeDtypeStruct((n_tok, d), F32),
        scratch_shapes=[pltpu.VMEM((d, 2 * width), BF16), pltpu.VMEM((width, d), BF16)],
        compiler_params=pltpu.CompilerParams(
            dimension_semantics=("arbitrary",), vmem_limit_bytes=VMEM_LIMIT_BYTES),
        name="chunk_mlp",
    )(x, mods, norm_pre, norm_post, w_in, v_gain, v_bias, w_s, b_s, w_out)


def kernel(x_prompt, x_sample, cache_k, cache_v, state_gla_fwd, state_gla_bwd, c, c_ctx, w_mod, b_mod, norm_pre, norm_post, ffn_w_gate, ffn_w_up, ffn_w_down, ev_w_in, ev_w_out, ev_sink, gla_wa_f, gla_ba_f, gla_wa_b, gla_ba_b, gla_norm, cm_w_in, cm_v_gain, cm_v_bias, cm_w_s, cm_b_s, cm_w_out):
    batch, seq, d = x_prompt.shape
    dec_batch, dec_seq, _ = x_sample.shape
    depth = w_mod.shape[0]
    n_prompt_tok = batch * seq
    n_sample_tok = dec_batch * dec_seq
    assert n_prompt_tok % dec_seq == 0 and dec_seq % TOKEN_TILE == 0
    group = _group_index_map(TOKEN_TILE, n_prompt_tok, dec_seq)
    np_tiles = n_prompt_tok // TOKEN_TILE
    ffn = functools.partial(_ffn, group=group, np_tiles=np_tiles)
    assert dec_seq % WIDE_TILE == 0
    group_w = _group_index_map(WIDE_TILE, n_prompt_tok, dec_seq)
    np_tiles_w = n_prompt_tok // WIDE_TILE

    xs = (x_prompt.reshape(n_prompt_tok, d), x_sample.reshape(n_sample_tok, d))
    ev_w_in_t = jnp.swapaxes(ev_w_in, 1, 2)

    mods = _adaln_mods(c_ctx, c, w_mod, b_mod)
    gains_pre = norm_pre.reshape(depth * N_SUB, d)
    gains_post = norm_post.reshape(depth * N_SUB, d)
    cond = (mods, gains_pre, gains_post)

    new_k, new_v, new_sf, new_sb = [], [], [], []
    for layer in range(depth):
        (x,) = ffn(xs, *cond, ffn_w_gate, ffn_w_up, ffn_w_down,
                   sub=0, layer=layer, half=0, split_out=False)
        if layer % 2 == 0:
            e = layer // 2
            qa, ka, va, qb, kb, vb, gb, ldf, ldb = _even_in(
                x, *cond, ev_w_in_t, gla_wa_f, gla_wa_b, gla_ba_f, gla_ba_b,
                layer=layer, e=e, group=group_w)
            sink = ev_sink[e]
            attn_p, k_t, v_t = _attn_context(sink, qa, ka, va, n_seq=batch, seq=seq)
            k_ctx = jnp.transpose(cache_k[:, e], (0, 2, 3, 1)).reshape(dec_batch, A_KV, -1)
            v_ctx = jnp.transpose(cache_v[:, e], (0, 2, 3, 1)).reshape(dec_batch, A_KV, -1)
            attn_s = _attn_latent(sink, qa, ka, va, k_ctx, v_ctx, n_seq=dec_batch, seq=dec_seq,
                                  row_block0=n_prompt_tok // dec_seq)
            gla_p, s_f, s_b = _gla(qb, kb, vb, ldf, ldb, None, None, n_seq=batch, seq=seq,
                                   row_block0=0, want_state=True, per_step=GLA_PROMPT_SEQS)
            (gla_s,) = _gla(qb, kb, vb, ldf, ldb, state_gla_fwd[:, e], state_gla_bwd[:, e],
                            n_seq=dec_batch, seq=dec_seq, row_block0=n_prompt_tok // dec_seq,
                            want_state=False, per_step=1)
            x = _even_out(x, *cond, (attn_p, attn_s), (gla_p, gla_s), gb, gla_norm, ev_w_out,
                          layer=layer, e=e, group=group_w, np_tiles=np_tiles_w)
            for cache_t, dst in ((k_t, new_k), (v_t, new_v)):
                dst.append(jnp.transpose(cache_t.reshape(batch, ATTN_KV_HEADS, HEAD_DIM, seq), (0, 3, 1, 2)))
            new_sf.append(s_f)
            new_sb.append(s_b)
        else:
            o = layer // 2
            x = _chunk_mlp(x, *cond, cm_w_in, cm_v_gain, cm_v_bias, cm_w_s, cm_b_s, cm_w_out,
                           layer=layer, o=o, group=group_w)
        xs = ffn((x,), *cond, ffn_w_gate, ffn_w_up, ffn_w_down,
                 sub=2, layer=layer, half=1, split_out=(layer == depth - 1))

    y_prompt = xs[0].reshape(batch, seq, d)
    y_sample = xs[1].reshape(dec_batch, dec_seq, d)
    return (y_prompt, y_sample, jnp.stack(new_k, axis=1), jnp.stack(new_v, axis=1),
            jnp.stack(new_sf, axis=1), jnp.stack(new_sb, axis=1))
```

```python
import functools
import math

import jax
import jax.numpy as jnp
from jax import lax
from jax.experimental import pallas as pl
from jax.experimental.pallas import tpu as pltpu

F32 = jnp.float32
BF16 = jnp.bfloat16

EPS = 1e-6
NEG_INF = -1e30
FFN_RESIDUAL = 0.5
N_SUB = 3

ATTN_HEADS = 8
ATTN_KV_HEADS = 2
HEAD_DIM = 64
ATTN_BLOCK = 128
WINDOW = 128
GRID_W = 64
GRID_W_LOG2 = 6
ROPE_BASE = 10000.0
GLA_HEADS = 4
GLA_DK = 64
GLA_DV = 128
GLA_RANK = 16
GLA_TAU = 16.0
CHUNK = 128
CMLP_GROUPS = 4

A_Q = ATTN_HEADS * HEAD_DIM
A_KV = ATTN_KV_HEADS * HEAD_DIM
B_QK = GLA_HEADS * GLA_DK
B_V = GLA_HEADS * GLA_DV
EV_MAIN = A_Q + 2 * A_KV + 2 * B_QK + 2 * B_V
EV_OFFS = (0, A_Q, A_Q + A_KV, A_Q + 2 * A_KV, A_Q + 2 * A_KV + B_QK,
           A_Q + 2 * A_KV + 2 * B_QK, A_Q + 2 * A_KV + 2 * B_QK + B_V, EV_MAIN)

LANES = 128
LANES_LOG2 = 7
SUBLANES = 8
VMEM_LIMIT_BYTES = 56 * 1024 * 1024

TOKEN_TILE = 512
WIDE_TILE = 1024
SUB_ROWS = 512
ADALN_BUFFERS = 4
FFN_CHUNK = 256
GLA_TILE = 128
GLA_SUB = 8
GLA_PROMPT_SEQS = 4
ATTN_PROMPT_SEQS = 2
ATTN_LATENT_SEQS = 2


def _group_index_map(tile_rows, n_prompt_tok, seq_tok):
    def group(i):
        return jnp.maximum(i * tile_rows - (n_prompt_tok - seq_tok), 0) // seq_tok
    return group


def _sub_tiles(n_rows):
    assert n_rows % SUB_ROWS == 0
    return [slice(r, r + SUB_ROWS) for r in range(0, n_rows, SUB_ROWS)]


def _rms(x, g):
    ms = jnp.mean(x * x, axis=-1, keepdims=True)
    return x * lax.rsqrt(ms + EPS) * g


def _mod_rows(mod_ref, g):
    return [mod_ref[k, pl.ds(g, 1), :] for k in range(3 * N_SUB)]


def _sub_in(x, mods, g_pre, sub):
    shift, scale = mods[3 * sub + 0], mods[3 * sub + 1]
    return _rms(x, g_pre) * (1.0 + scale) + shift


def _sub_out(x, y, mods, g_post, sub, coef):
    return x + (coef * mods[3 * sub + 2]) * _rms(y, g_post)


def _norm_row(ref, layer, sub):
    r = layer * N_SUB + sub
    return ref[r:r + 1, :]


def _silu(x):
    return x * jax.nn.sigmoid(x)


def _mod_kernel(cctx_ref, c_ref, w_hbm, b_ref, o_ref, buf, sem):
    depth, n_kinds, rows, d = o_ref.shape
    r = lax.broadcasted_iota(jnp.int32, (rows, d), 0)
    cond = jnp.where(r == 0, jnp.broadcast_to(cctx_ref[...], (rows, d)), 0.0)
    for j in range(c_ref.shape[0]):
        cond = jnp.where(r == 1 + j, jnp.broadcast_to(c_ref[j:j + 1, :], (rows, d)), cond)
    s = _silu(cond).astype(BF16)
    chunks = [(l, q) for l in range(depth) for q in range(n_kinds)]
    n_buf = buf.shape[0]

    def copy(i):
        l, q = chunks[i]
        return pltpu.make_async_copy(w_hbm.at[l, :, pl.ds(q * d, d)], buf.at[i % n_buf], sem.at[i % n_buf])

    for i in range(min(n_buf, len(chunks))):
        copy(i).start()
    for i, (l, q) in enumerate(chunks):
        copy(i).wait()
        res = jnp.dot(s, buf[i % n_buf].astype(BF16), preferred_element_type=F32)
        o_ref[l, q] = res + b_ref[l:l + 1, q * d:(q + 1) * d]
        if i + n_buf < len(chunks):
            copy(i + n_buf).start()


def _adaln_mods(c_ctx, c, w_mod, b_mod):
    depth, d, n = w_mod.shape
    assert n % d == 0
    rows = -(-(1 + c.shape[0]) // SUBLANES) * SUBLANES
    return pl.pallas_call(
        _mod_kernel,
        grid=(1,),
        in_specs=[
            pl.BlockSpec((1, d), lambda i: (0, 0)),
            pl.BlockSpec(c.shape, lambda i: (0, 0)),
            pl.BlockSpec(memory_space=pl.ANY),
            pl.BlockSpec((depth, n), lambda i: (0, 0)),
        ],
        out_specs=pl.BlockSpec((depth, n // d, rows, d), lambda i: (0, 0, 0, 0)),
        out_shape=jax.ShapeDtypeStruct((depth, n // d, rows, d), F32),
        scratch_shapes=[pltpu.VMEM((ADALN_BUFFERS, d, d), F32), pltpu.SemaphoreType.DMA((ADALN_BUFFERS,))],
        compiler_params=pltpu.CompilerParams(
            dimension_semantics=("arbitrary",), vmem_limit_bytes=VMEM_LIMIT_BYTES),
        name="adaln_mods",
    )(c_ctx.reshape(1, d), c, w_mod, b_mod)


def _tile_specs(tm, width, np_tiles):
    return [pl.BlockSpec((tm, width), lambda i: (jnp.minimum(i, np_tiles - 1), 0)),
            pl.BlockSpec((tm, width), lambda i: (jnp.maximum(i - np_tiles, 0), 0))]


def _pick(refs, is_prompt, rows=slice(None)):
    if len(refs) == 1:
        return refs[0][rows, :]
    return jnp.where(is_prompt, refs[0][rows, :], refs[1][rows, :])


def _cond_specs(mods, norm_pre, norm_post, layer):
    return [pl.BlockSpec((None,) + mods.shape[1:], lambda i: (layer, 0, 0, 0)),
            pl.BlockSpec(norm_pre.shape, lambda i: (0, 0)),
            pl.BlockSpec(norm_post.shape, lambda i: (0, 0))]


def _ffn_kernel(*refs, sub, layer, half, n_chunks, n_x, n_o, np_tiles, group):
    x_refs = refs[:n_x]
    mod_ref, gpre_ref, gpost_ref, wg_hbm, wu_hbm, wd_hbm = refs[n_x:n_x + 6]
    o_refs = refs[n_x + 6:n_x + 6 + n_o]
    wg_bf, wu_bf, wd_bf, st_g, st_u, st_d, sem = refs[n_x + 6 + n_o:]
    fc = FFN_CHUNK
    step = pl.program_id(0)
    mods = _mod_rows(mod_ref, group(step))
    g_pre = _norm_row(gpre_ref, layer, sub)
    g_post = _norm_row(gpost_ref, layer, sub)

    def chunk_copies(c, slot):
        cols = pl.ds(c * fc, fc)
        return (
            pltpu.make_async_copy(wg_hbm.at[layer, half, :, cols], st_g.at[slot], sem.at[0, slot]),
            pltpu.make_async_copy(wu_hbm.at[layer, half, :, cols], st_u.at[slot], sem.at[1, slot]),
            pltpu.make_async_copy(wd_hbm.at[layer, half, cols, :], st_d.at[slot], sem.at[2, slot]),
        )

    def run(stage_weights, prompt):
        if stage_weights:
            for cp in chunk_copies(0, 0):
                cp.start()
        x = _pick(x_refs, step < np_tiles)
        h = _sub_in(x, mods, g_pre, sub).astype(BF16)
        y = jnp.zeros(x.shape, F32)
        for c in range(n_chunks):
            sl = slice(c * fc, (c + 1) * fc)
            if stage_weights:
                slot = c % 2
                if c + 1 < n_chunks:
                    for cp in chunk_copies(c + 1, 1 - slot):
                        cp.start()
                for cp in chunk_copies(c, slot):
                    cp.wait()
                wg_bf[:, sl] = st_g[slot].astype(BF16)
                wu_bf[:, sl] = st_u[slot].astype(BF16)
                wd_bf[sl, :] = st_d[slot].astype(BF16)
            g = jnp.dot(h, wg_bf[:, sl], preferred_element_type=F32)
            u = jnp.dot(h, wu_bf[:, sl], preferred_element_type=F32)
            a = (_silu(g) * u).astype(BF16)
            y = y + jnp.dot(a, wd_bf[sl, :], preferred_element_type=F32)
        o_refs[0 if (prompt or n_o == 1) else 1][...] = _sub_out(x, y, mods, g_post, sub, FFN_RESIDUAL)

    @pl.when(step == 0)
    def _first_tile():
        run(True, True)

    if n_o == 1:
        @pl.when(step > 0)
        def _other_tiles():
            run(False, True)
    else:
        @pl.when(jnp.logical_and(step > 0, step < np_tiles))
        def _prompt_tiles():
            run(False, True)

        @pl.when(step >= np_tiles)
        def _latent_tiles():
            run(False, False)


def _ffn(xs, mods, norm_pre, norm_post, wg, wu, wd, *, sub, layer, half, group, np_tiles, split_out):
    d = xs[0].shape[-1]
    n_tok = sum(x.shape[0] for x in xs)
    d_ff = wg.shape[-1]
    tm = TOKEN_TILE
    fc = FFN_CHUNK
    n_chunks = d_ff // fc
    assert n_chunks * fc == d_ff and n_tok % tm == 0
    n_x, n_o = len(xs), (2 if split_out else 1)
    kern = functools.partial(_ffn_kernel, sub=sub, layer=layer, half=half, n_chunks=n_chunks,
                             n_x=n_x, n_o=n_o, np_tiles=np_tiles, group=group)
    whole = pl.BlockSpec((tm, d), lambda i: (i, 0))
    x_specs = _tile_specs(tm, d, np_tiles) if n_x == 2 else [whole]
    n_p = np_tiles * tm
    if split_out:
        out_specs = _tile_specs(tm, d, np_tiles)
        out_shape = [jax.ShapeDtypeStruct((n_p, d), F32), jax.ShapeDtypeStruct((n_tok - n_p, d), F32)]
    else:
        out_specs = [whole]
        out_shape = [jax.ShapeDtypeStruct((n_tok, d), F32)]
    return pl.pallas_call(
        kern,
        grid=(n_tok // tm,),
        in_specs=x_specs + _cond_specs(mods, norm_pre, norm_post, layer) + [
            pl.BlockSpec(memory_space=pl.ANY),
            pl.BlockSpec(memory_space=pl.ANY),
            pl.BlockSpec(memory_space=pl.ANY),
        ],
        out_specs=out_specs,
        out_shape=out_shape,
        scratch_shapes=[
            pltpu.VMEM((d, d_ff), BF16),
            pltpu.VMEM((d, d_ff), BF16),
            pltpu.VMEM((d_ff, d), BF16),
            pltpu.VMEM((2, d, fc), F32),
            pltpu.VMEM((2, d, fc), F32),
            pltpu.VMEM((2, fc, d), F32),
            pltpu.SemaphoreType.DMA((3, 2)),
        ],
        compiler_params=pltpu.CompilerParams(
            dimension_semantics=("arbitrary",), vmem_limit_bytes=VMEM_LIMIT_BYTES),
        name=f"ffn_l{layer}h{half}",
    )(*xs, mods, norm_pre, norm_post, wg, wu, wd)


def _log_sigmoid(x):
    return jnp.minimum(x, 0.0) - jnp.log1p(jnp.exp(-jnp.abs(x)))


def _evin_kernel(x_ref, mod_ref, gpre_ref, gpost_ref, w_ref, waf_ref, wab_ref, baf_ref, bab_ref,
                 qa_ref, ka_ref, va_ref, qb_ref, kb_ref, vb_ref, gb_ref, ldf_ref, ldb_ref,
                 w_bf, wa_bf, *, layer, group):
    @pl.when(pl.program_id(0) == 0)
    def _cast_weights():
        w_bf[0:EV_MAIN, :] = w_ref[0, 0:EV_MAIN, :].astype(BF16)
        w_bf[EV_MAIN:EV_MAIN + LANES, :] = jnp.zeros((LANES, w_bf.shape[1]), BF16)
        w_bf[EV_MAIN:EV_MAIN + 2 * GLA_RANK, :] = w_ref[0, EV_MAIN:EV_MAIN + 2 * GLA_RANK, :].astype(BF16)
        wa_bf[...] = jnp.zeros(wa_bf.shape, BF16)
        wa_bf[0:GLA_RANK, 0:B_QK] = waf_ref[0].astype(BF16)
        wa_bf[GLA_RANK:2 * GLA_RANK, B_QK:2 * B_QK] = wab_ref[0].astype(BF16)

    o = EV_OFFS
    mods = _mod_rows(mod_ref, group(pl.program_id(0)))
    g_pre = _norm_row(gpre_ref, layer, 1)

    def project(rows):
        h = _sub_in(x_ref[rows, :], mods, g_pre, 1).astype(BF16)
        return _dot_nt(h, w_bf[...])

    def finish(rows, full):
        qa_ref[rows, :] = (full[:, o[0]:o[1]] * (HEAD_DIM ** -0.5)).astype(qa_ref.dtype)
        ka_ref[rows, :] = full[:, o[1]:o[2]]
        va_ref[rows, :] = full[:, o[2]:o[3]]
        qb_ref[rows, :] = full[:, o[3]:o[4]] * (GLA_DK ** -0.5)
        kb_ref[rows, :] = full[:, o[4]:o[5]]
        vb_ref[rows, :] = full[:, o[5]:o[6]].astype(vb_ref.dtype)
        gb_ref[rows, :] = full[:, o[6]:o[7]].astype(gb_ref.dtype)
        lr = full[:, EV_MAIN:EV_MAIN + LANES].astype(BF16)
        logits = jnp.dot(lr, wa_bf[...], preferred_element_type=F32)
        ldf_ref[rows, :] = _log_sigmoid(logits[:, 0:B_QK] + baf_ref[...]) * (1.0 / GLA_TAU)
        ldb_ref[rows, :] = _log_sigmoid(logits[:, B_QK:2 * B_QK] + bab_ref[...]) * (1.0 / GLA_TAU)

    pending = None
    for rows in _sub_tiles(x_ref.shape[0]):
        full = project(rows)
        if pending is not None:
            finish(*pending)
        pending = (rows, full)
    finish(*pending)


def _even_in(x, mods, norm_pre, norm_post, w_in_t, wa_f, wa_b, ba_f, ba_b, *, layer, e, group):
    n_tok, d = x.shape
    tm = WIDE_TILE
    ev_in = w_in_t.shape[1]
    rank = wa_f.shape[1]
    assert rank == GLA_RANK and ev_in == EV_MAIN + 2 * rank
    widths = (A_Q, A_KV, A_KV, B_QK, B_QK, B_V, B_V, B_QK, B_QK)
    dtypes = (BF16, F32, F32, F32, F32, BF16, BF16, F32, F32)
    kern = functools.partial(_evin_kernel, layer=layer, group=group)
    return pl.pallas_call(
        kern,
        grid=(n_tok // tm,),
        in_specs=[pl.BlockSpec((tm, d), lambda i: (i, 0))] + _cond_specs(mods, norm_pre, norm_post, layer) + [
            pl.BlockSpec((1, ev_in, d), lambda i: (e, 0, 0), pipeline_mode=pl.Buffered(1)),
            pl.BlockSpec((1, rank, B_QK), lambda i: (e, 0, 0)),
            pl.BlockSpec((1, rank, B_QK), lambda i: (e, 0, 0)),
            pl.BlockSpec((1, B_QK), lambda i: (e, 0)),
            pl.BlockSpec((1, B_QK), lambda i: (e, 0)),
        ],
        out_specs=[pl.BlockSpec((tm, w), lambda i: (i, 0)) for w in widths],
        out_shape=[jax.ShapeDtypeStruct((n_tok, w), t) for w, t in zip(widths, dtypes)],
        scratch_shapes=[pltpu.VMEM((EV_MAIN + LANES, d), BF16), pltpu.VMEM((LANES, 2 * B_QK), BF16)],
        compiler_params=pltpu.CompilerParams(
            dimension_semantics=("arbitrary",), vmem_limit_bytes=VMEM_LIMIT_BYTES),
        name="even_in",
    )(x, mods, norm_pre, norm_post, w_in_t, wa_f, wa_b, ba_f, ba_b)


def _lane_lt(shape, n):
    return lax.broadcasted_iota(jnp.int32, shape, len(shape) - 1) < n


def _dup_kv_head(x, g):
    sw = pltpu.roll(x, HEAD_DIM, 1)
    lo = _lane_lt(x.shape, HEAD_DIM)
    return jnp.where(lo, x, sw) if g == 0 else jnp.where(lo, sw, x)


def _dot_nt(a, b):
    return lax.dot_general(a, b, (((1,), (1,)), ((), ())), preferred_element_type=F32)


ATTN_GROUP = ATTN_HEADS // ATTN_KV_HEADS


def _sink_column(sink_ref, g, t):
    rows = lax.broadcasted_iota(jnp.int32, (ATTN_GROUP * t, 1), 0)
    col = jnp.full((ATTN_GROUP * t, 1), sink_ref[ATTN_GROUP * g], F32)
    for j in range(1, ATTN_GROUP):
        col = jnp.where(rows >= j * t, sink_ref[ATTN_GROUP * g + j], col)
    return col


def _attn_scores(q_pairs, keys):
    t = q_pairs[0].shape[0]
    lo = _lane_lt((t, LANES), HEAD_DIM)
    zero = jnp.zeros((t, LANES), BF16)
    q4 = jnp.concatenate([jnp.where(lo, q_pairs[0], zero), jnp.where(lo, zero, q_pairs[0]),
                          jnp.where(lo, q_pairs[1], zero), jnp.where(lo, zero, q_pairs[1])], axis=0)
    scores = []
    for k2, _, bias, feature_major in keys:
        s = jnp.dot(q4, k2, preferred_element_type=F32) if feature_major else _dot_nt(q4, k2)
        if bias is not None:
            s = s + bias[...]
        scores.append(s)
    return scores


def _attn_finish(scores, sink_col, keys):
    t = scores[0].shape[0] // ATTN_GROUP
    lo = _lane_lt((t, LANES), HEAD_DIM)
    mx = sink_col
    for s in scores:
        mx = jnp.maximum(mx, jnp.max(s, axis=-1, keepdims=True))
    den = jnp.exp(sink_col - mx)
    o = None
    for s, (_, v2, _, feature_major) in zip(scores, keys):
        p = jnp.exp(s - mx)
        den = den + jnp.sum(p, axis=-1, keepdims=True)
        pb = p.astype(BF16)
        pv = _dot_nt(pb, v2) if feature_major else jnp.dot(pb, v2, preferred_element_type=F32)
        o = pv if o is None else o + pv
    o = o / den
    return (jnp.where(lo, o[0:t], o[t:2 * t]), jnp.where(lo, o[2 * t:3 * t], o[3 * t:4 * t]))


def _attn_ctx_kernel(sink_ref, q_ref, k_ref, v_ref, o_ref, kt_ref, vt_ref):
    per_step, _, t = kt_ref.shape
    work = []
    for s in range(per_step):
        rows = slice(s * t, (s + 1) * t)
        k = k_ref[rows, :]
        v = v_ref[rows, :]
        kt_ref[s] = k.T
        vt_ref[s] = v.T
        for g in range(ATTN_KV_HEADS):
            keys = [(_dup_kv_head(k, g).astype(BF16), _dup_kv_head(v, g).astype(BF16), None, False)]
            cols = (2 * g, 2 * g + 1)
            scores = _attn_scores([q_ref[rows, m * LANES:(m + 1) * LANES] for m in cols], keys)
            work.append((rows, cols, scores, keys, g))
    for rows, cols, scores, keys, g in work:
        outs = _attn_finish(scores, _sink_column(sink_ref, g, t), keys)
        for m, o in zip(cols, outs):
            o_ref[rows, m * LANES:(m + 1) * LANES] = o.astype(o_ref.dtype)


def _attn_context(sink, qa, ka, va, *, n_seq, seq):
    per_step = ATTN_PROMPT_SEQS
    assert n_seq % per_step == 0
    n_seq, rows = n_seq // per_step, per_step * seq
    cache_spec = pl.BlockSpec((per_step, A_KV, seq), lambda b: (b, 0, 0))
    return pl.pallas_call(
        _attn_ctx_kernel,
        grid=(n_seq,),
        in_specs=[
            pl.BlockSpec(memory_space=pltpu.SMEM),
            pl.BlockSpec((rows, A_Q), lambda b: (b, 0)),
            pl.BlockSpec((rows, A_KV), lambda b: (b, 0)),
            pl.BlockSpec((rows, A_KV), lambda b: (b, 0)),
        ],
        out_specs=[pl.BlockSpec((rows, A_Q), lambda b: (b, 0)), cache_spec, cache_spec],
        out_shape=[jax.ShapeDtypeStruct((n_seq * rows, A_Q), BF16),
                   jax.ShapeDtypeStruct((n_seq * per_step, A_KV, seq), F32),
                   jax.ShapeDtypeStruct((n_seq * per_step, A_KV, seq), F32)],
        compiler_params=pltpu.CompilerParams(dimension_semantics=("parallel",)),
        name="attn_context",
    )(sink, qa, ka, va)


def _rope_tables(n_tok):
    n_rows = n_tok // GRID_W
    assert n_rows <= GRID_W
    shape = (GRID_W, LANES)
    pos = lax.broadcasted_iota(jnp.int32, shape, 0).astype(F32)
    lane = lax.broadcasted_iota(jnp.int32, shape, 1)
    half = HEAD_DIM // 2
    nf = half // 2
    within = lane & (HEAD_DIM - 1)
    is_col = within >= half
    second = (within & (half - 1)) >= nf
    f = (within & (nf - 1)).astype(F32)
    inv = jnp.exp(f * (-2.0 / half * math.log(ROPE_BASE)))
    ang = pos * inv
    cos_t = jnp.cos(ang)
    sin_t = jnp.sin(ang)
    sin_t = jnp.where(second, sin_t, -sin_t)

    def per_token(tab):
        return jnp.concatenate(
            [jnp.where(is_col, tab, jnp.broadcast_to(tab[r:r + 1, :], shape)) for r in range(n_rows)], axis=0)

    return per_token(cos_t), per_token(sin_t)


def _rope(x, cos, sin_signed):
    nf = HEAD_DIM // 4
    lane = lax.broadcasted_iota(jnp.int32, x.shape, 1)
    second = (lane & (2 * nf - 1)) >= nf
    up = pltpu.roll(x, nf, 1)
    dn = pltpu.roll(x, LANES - nf, 1)
    return x * cos + jnp.where(second, up, dn) * sin_signed


def _attn_lat_kernel(sink_ref, q_ref, k_ref, v_ref, kc_ref, vc_ref, o_ref, qr_ref, kp_ref, vp_ref, bias_ref,
                     *, seq):
    per_step = kc_ref.shape[0]
    nb = seq // ATTN_BLOCK
    blk = ATTN_BLOCK
    cos, sin_signed = _rope_tables(seq)
    zpad = jnp.zeros((blk, LANES), BF16)
    ctx = []
    for s in range(per_step):
        rows = slice(s * seq, (s + 1) * seq)
        for m in range(A_Q // LANES):
            qr_ref[rows, m * LANES:(m + 1) * LANES] = _rope(
                q_ref[rows, m * LANES:(m + 1) * LANES].astype(F32), cos, sin_signed).astype(BF16)
        kr = _rope(k_ref[rows, :], cos, sin_signed)
        v = v_ref[rows, :]
        for g in range(ATTN_KV_HEADS):
            n = s * ATTN_KV_HEADS + g
            kp_ref[n, 0:blk, :] = zpad
            kp_ref[n, blk + seq:2 * blk + seq, :] = zpad
            vp_ref[n, 0:blk, :] = zpad
            vp_ref[n, blk + seq:2 * blk + seq, :] = zpad
            kp_ref[n, blk:blk + seq, :] = _dup_kv_head(kr, g).astype(BF16)
            vp_ref[n, blk:blk + seq, :] = _dup_kv_head(v, g).astype(BF16)
            hd = slice(g * HEAD_DIM, (g + 1) * HEAD_DIM)
            ctx.append((jnp.concatenate([kc_ref[s, hd, :]] * 2, axis=0).astype(BF16),
                        jnp.concatenate([vc_ref[s, hd, :]] * 2, axis=0).astype(BF16), None, True))
    sink_cols = [_sink_column(sink_ref, g, blk) for g in range(ATTN_KV_HEADS)]
    r = lax.broadcasted_iota(jnp.int32, (ATTN_GROUP * blk, 3 * blk), 0) & (blk - 1)
    c = lax.broadcasted_iota(jnp.int32, (ATTN_GROUP * blk, 3 * blk), 1)
    band = (c >= r) & (c <= r + 2 * WINDOW)
    bias_ref[0] = jnp.where(band & (c >= blk), 0.0, NEG_INF)
    bias_ref[1] = jnp.where(band, 0.0, NEG_INF)
    bias_ref[2] = jnp.where(band & (c < 2 * blk), 0.0, NEG_INF)

    def block_body(i, carry):
        row0 = pl.multiple_of(i * blk, blk)
        edge = jnp.where(i == 0, 0, jnp.where(i == nb - 1, 2, 1))
        work = []
        for s in range(per_step):
            q0 = pl.multiple_of(s * seq + i * blk, blk)
            for g in range(ATTN_KV_HEADS):
                n = s * ATTN_KV_HEADS + g
                cols = (2 * g, 2 * g + 1)
                window = (kp_ref[n, pl.ds(row0, 3 * blk), :], vp_ref[n, pl.ds(row0, 3 * blk), :],
                          bias_ref.at[edge], False)
                keys = [window, ctx[n]]
                scores = _attn_scores([qr_ref[pl.ds(q0, blk), m * LANES:(m + 1) * LANES] for m in cols], keys)
                work.append((q0, cols, scores, keys, g))
        for q0, cols, scores, keys, g in work:
            outs = _attn_finish(scores, sink_cols[g], keys)
            for m, o in zip(cols, outs):
                o_ref[pl.ds(q0, blk), m * LANES:(m + 1) * LANES] = o.astype(o_ref.dtype)
        return carry

    lax.fori_loop(0, nb, block_body, 0)


def _attn_latent(sink, qa, ka, va, k_ctx, v_ctx, *, n_seq, seq, row_block0):
    past = k_ctx.shape[2]
    per_step = ATTN_LATENT_SEQS
    assert n_seq % per_step == 0 and row_block0 % per_step == 0 and seq // ATTN_BLOCK >= 2
    rows = per_step * seq
    first = row_block0 // per_step
    kern = functools.partial(_attn_lat_kernel, seq=seq)
    return pl.pallas_call(
        kern,
        grid=(n_seq // per_step,),
        in_specs=[
            pl.BlockSpec(memory_space=pltpu.SMEM),
            pl.BlockSpec((rows, A_Q), lambda b: (first + b, 0)),
            pl.BlockSpec((rows, A_KV), lambda b: (first + b, 0)),
            pl.BlockSpec((rows, A_KV), lambda b: (first + b, 0)),
            pl.BlockSpec((per_step, A_KV, past), lambda b: (b, 0, 0)),
            pl.BlockSpec((per_step, A_KV, past), lambda b: (b, 0, 0)),
        ],
        out_specs=pl.BlockSpec((rows, A_Q), lambda b: (b, 0)),
        out_shape=jax.ShapeDtypeStruct((n_seq * seq, A_Q), BF16),
        scratch_shapes=[
            pltpu.VMEM((rows, A_Q), BF16),
            pltpu.VMEM((per_step * ATTN_KV_HEADS, seq + 2 * ATTN_BLOCK, LANES), BF16),
            pltpu.VMEM((per_step * ATTN_KV_HEADS, seq + 2 * ATTN_BLOCK, LANES), BF16),
            pltpu.VMEM((3, ATTN_GROUP * ATTN_BLOCK, 3 * ATTN_BLOCK), F32),
        ],
        compiler_params=pltpu.CompilerParams(dimension_semantics=("parallel",)),
        name="attn_latent",
    )(sink, qa, ka, va, k_ctx, v_ctx)


def _same_block(a, b, size):
    return (a & -size) == (b & -size)


def _split3(x):
    hi = x.astype(BF16)
    r1 = x - hi.astype(F32)
    mid = r1.astype(BF16)
    lo = (r1 - mid.astype(F32)).astype(BF16)
    return hi, mid, lo


def _interleave(generators):
    live = list(generators)
    while live:
        for gen in list(live):
            try:
                next(gen)
            except StopIteration:
                live.remove(gen)


def _gla_tile(q, k, v, ld, st_ref, sel_ref, emit, *, rev):
    t = GLA_TILE
    n_pair = GLA_HEADS // 2
    row = lax.broadcasted_iota(jnp.int32, (t, t), 0)
    col = lax.broadcasted_iota(jnp.int32, (t, t), 1)
    tri = (row <= col) if rev else (row >= col)
    tri = jnp.where(tri, 1.0, 0.0).astype(BF16)
    hi, mid, lo = _split3(ld * math.log2(math.e))
    b = (jnp.dot(tri, hi, preferred_element_type=F32) + jnp.dot(tri, mid, preferred_element_type=F32)
         + jnp.dot(tri, lo, preferred_element_type=F32))
    yield
    b_tot = b[0:1, :] if rev else b[t - 1:t, :]
    qs = (q * jnp.exp2(b)).astype(BF16)
    kd = (k * jnp.exp2(b_tot - b)).astype(BF16)

    rowv = lax.broadcasted_iota(jnp.int32, (t, B_QK), 0)
    levels = []
    half = GLA_SUB
    while 2 * half <= t:
        blk = 2 * half
        pieces = []
        for bs in range(0, t, blk):
            r = bs + half if rev else bs + half - 1
            pieces.append(jnp.broadcast_to(b[r:r + 1, :], (blk, B_QK)))
        ref = jnp.concatenate(pieces, axis=0) if len(pieces) > 1 else pieces[0]
        second = (rowv & (blk - 1)) >= half
        is_q = ~second if rev else second
        ql = (q * jnp.exp2(jnp.where(is_q, b - ref, -jnp.inf))).astype(BF16)
        kl = (k * jnp.exp2(jnp.where(is_q, -jnp.inf, ref - b))).astype(BF16)
        levels.append((blk, ql, kl))
        half = blk
    yield

    rr = lax.broadcasted_iota(jnp.int32, (t, 2 * t), 0)
    ss = lax.broadcasted_iota(jnp.int32, (t, 2 * t), 1) & (t - 1)
    sub_rows = lax.broadcasted_iota(jnp.int32, (GLA_SUB, LANES), 0)
    lane_lo = _lane_lt((t, LANES), GLA_DK)
    lane_lo2 = _lane_lt((2 * t, LANES), GLA_DK)
    par_rows = lax.broadcasted_iota(jnp.int32, (2 * t, LANES), 0) < t
    head_sel = par_rows == lane_lo2

    outs = []
    for p in range(n_pair):
        pl_ = slice(p * LANES, (p + 1) * LANES)
        qp, kp, bp = q[:, pl_], k[:, pl_], b[:, pl_]
        rows = []
        for sb in range(t // GLA_SUB):
            r0 = sb * GLA_SUB
            q_sub = qp[r0:r0 + GLA_SUB, :]
            b_sub = bp[r0:r0 + GLA_SUB, :]
            cols = []
            for j in range(GLA_SUB):
                kj = kp[r0 + j:r0 + j + 1, :]
                bj = bp[r0 + j:r0 + j + 1, :]
                valid = (sub_rows <= j) if rev else (sub_rows >= j)
                e = jnp.exp2(jnp.where(valid, b_sub - bj, -jnp.inf))
                cols.append(q_sub * kj * e)
            rows.append(jnp.concatenate(cols, axis=1))
        e_p = jnp.concatenate(rows, axis=0).astype(BF16)
        yield
        scores = jnp.dot(e_p, sel_ref[...], preferred_element_type=F32)
        scores = jnp.where(_same_block(rr, ss, GLA_SUB), scores, 0.0)
        for blk, ql, kl in levels:
            klp = kl[:, pl_]
            kstack = jnp.concatenate([klp, klp], axis=0)
            kstack = jnp.where(head_sel, kstack, jnp.zeros_like(kstack))
            s_l = _dot_nt(ql[:, pl_], kstack)
            scores = scores + jnp.where(_same_block(rr, ss, blk), s_l, 0.0)
        yield
        vp = v[:, p * 2 * GLA_DV:(p + 1) * 2 * GLA_DV]
        v_lo = _lane_lt(vp.shape, GLA_DV)
        vbd = jnp.concatenate([jnp.where(v_lo, vp, 0.0), jnp.where(v_lo, 0.0, vp)], axis=0).astype(BF16)
        st = st_ref[p]
        o_p = (jnp.dot(scores.astype(BF16), vbd, preferred_element_type=F32)
               + _dot_nt(qs[:, pl_], st.astype(BF16)))
        outs.append(o_p)
        upd = jnp.dot(vp.T.astype(BF16), kd[:, pl_], preferred_element_type=F32)
        st_rows_lo = lax.broadcasted_iota(jnp.int32, upd.shape, 0) < GLA_DV
        bd = st_rows_lo == _lane_lt(upd.shape, GLA_DK)
        st_ref[p] = st * jnp.exp2(b_tot[:, pl_]) + jnp.where(bd, upd, 0.0)
        yield
    emit(jnp.concatenate(outs, axis=1))


def _gla_kernel(*refs, seq, has_init, want_state, per_step):
    q_ref, k_ref, v_ref, ldf_ref, ldb_ref = refs[:5]
    pos = 5
    if has_init:
        s0f_ref, s0b_ref = refs[pos:pos + 2]
        pos += 2
    o_ref = refs[pos]
    pos += 1
    if want_state:
        sf_ref, sb_ref = refs[pos:pos + 2]
        pos += 2
    st_ref, sel_ref, acc_ref = refs[pos:pos + 3]
    t = GLA_TILE
    nt = seq // t
    n_pair = GLA_HEADS // 2

    kk = lax.broadcasted_iota(jnp.int32, (GLA_SUB * LANES, 2 * t), 0)
    nn = lax.broadcasted_iota(jnp.int32, (GLA_SUB * LANES, 2 * t), 1)
    sel = ((nn & (GLA_SUB - 1)) == (kk >> LANES_LOG2)) & (((kk & (LANES - 1)) >= GLA_DK) == (nn >= t))
    sel_ref[...] = jnp.where(sel, 1.0, 0.0).astype(BF16)

    def load_state(st, s0_ref, s):
        for p in range(n_pair):
            if s0_ref is None:
                st[p] = jnp.zeros(st.shape[1:], F32)
            else:
                x = jnp.concatenate([s0_ref[s, 2 * p], s0_ref[s, 2 * p + 1]], axis=0)
                xt = x.T
                lo = _lane_lt(xt.shape, GLA_DK)
                st[p] = jnp.concatenate([jnp.where(lo, xt, 0.0), jnp.where(lo, 0.0, xt)], axis=0)

    def store_state(st, out_ref, s):
        for p in range(n_pair):
            z = st[p]
            lo = _lane_lt((GLA_DV, LANES), GLA_DK)
            z = jnp.where(lo, z[0:GLA_DV, :], z[GLA_DV:2 * GLA_DV, :]).T
            out_ref[s, 2 * p] = z[0:GLA_DK, :]
            out_ref[s, 2 * p + 1] = z[GLA_DK:2 * GLA_DK, :]

    walks = [(s, rev, ldb_ref if rev else ldf_ref, st_ref.at[2 * s + int(rev)])
             for s in range(per_step) for rev in (False, True)]
    for s, rev, _, st in walks:
        load_state(st, (s0b_ref if rev else s0f_ref) if has_init else None, s)
    acc_ref[...] = jnp.zeros(acc_ref.shape, F32)

    def tile_body(ti, carry):
        def walk(s, rev, ld_ref, st):
            tile = (nt - 1 - ti) if rev else ti
            r0 = pl.multiple_of(s * seq + tile * t, t)
            rows = pl.ds(r0, t)

            def emit(o):
                acc_ref[rows, :] = acc_ref[rows, :] + o

            return _gla_tile(q_ref[rows, :], k_ref[rows, :], v_ref[rows, :].astype(F32), ld_ref[rows, :],
                             st, sel_ref, emit, rev=rev)

        _interleave([walk(*w) for w in walks])
        return carry

    lax.fori_loop(0, nt, tile_body, 0)
    o_ref[...] = acc_ref[...].astype(o_ref.dtype)
    if want_state:
        for s, rev, _, st in walks:
            store_state(st, sb_ref if rev else sf_ref, s)


def _gla(qb, kb, vb, ldf, ldb, s0f, s0b, *, n_seq, seq, row_block0, want_state, per_step):
    has_init = s0f is not None
    assert n_seq % per_step == 0 and row_block0 % per_step == 0
    kern = functools.partial(_gla_kernel, seq=seq, has_init=has_init, want_state=want_state,
                             per_step=per_step)
    n_seq, seq_rows, row_block0 = n_seq // per_step, per_step * seq, row_block0 // per_step
    tok = lambda w: pl.BlockSpec((seq_rows, w), lambda b: (row_block0 + b, 0))
    in_specs = [tok(B_QK), tok(B_QK), tok(B_V), tok(B_QK), tok(B_QK)]
    args = [qb, kb, vb, ldf, ldb]
    state_spec = pl.BlockSpec((per_step, GLA_HEADS, GLA_DK, GLA_DV), lambda b: (b, 0, 0, 0))
    if has_init:
        in_specs += [state_spec, state_spec]
        args += [s0f, s0b]
    out_specs = [pl.BlockSpec((seq_rows, B_V), lambda b: (b, 0))]
    out_shape = [jax.ShapeDtypeStruct((n_seq * seq_rows, B_V), BF16)]
    if want_state:
        out_specs += [state_spec, state_spec]
        out_shape += [jax.ShapeDtypeStruct((n_seq * per_step, GLA_HEADS, GLA_DK, GLA_DV), F32)] * 2
    return pl.pallas_call(
        kern,
        grid=(n_seq,),
        in_specs=in_specs,
        out_specs=out_specs,
        out_shape=out_shape,
        scratch_shapes=[
            pltpu.VMEM((2 * per_step, GLA_HEADS // 2, 2 * GLA_DV, LANES), F32),
            pltpu.VMEM((GLA_SUB * LANES, 2 * GLA_TILE), BF16),
            pltpu.VMEM((seq_rows, B_V), F32),
        ],
        compiler_params=pltpu.CompilerParams(
            dimension_semantics=("parallel",), vmem_limit_bytes=VMEM_LIMIT_BYTES),
        name="gla_state" if want_state else "gla_latent",
    )(*args)


def _evout_kernel(x_ref, mod_ref, gpre_ref, gpost_ref, attn_p_ref, attn_s_ref, gla_p_ref, gla_s_ref, gb_ref,
                  gn_ref, w_ref, o_ref, w_bf, *, np_tiles, layer, group):
    @pl.when(pl.program_id(0) == 0)
    def _cast_weights():
        w_bf[...] = w_ref[0].astype(BF16)

    is_prompt = pl.program_id(0) < np_tiles
    mods = _mod_rows(mod_ref, group(pl.program_id(0)))
    g_post = _norm_row(gpost_ref, layer, 1)
    for rows in _sub_tiles(x_ref.shape[0]):
        attn = _pick((attn_p_ref, attn_s_ref), is_prompt, rows)
        gla = _pick((gla_p_ref, gla_s_ref), is_prompt, rows)
        parts = [attn]
        for hd in range(GLA_HEADS):
            sl = slice(hd * GLA_DV, (hd + 1) * GLA_DV)
            g = _rms(gla[:, sl].astype(F32), gn_ref[:, sl]) * _silu(gb_ref[rows, sl].astype(F32))
            parts.append(g.astype(BF16))
        y = jnp.dot(jnp.concatenate(parts, axis=1), w_bf[...], preferred_element_type=F32)
        o_ref[rows, :] = _sub_out(x_ref[rows, :], y, mods, g_post, 1, 1.0)


def _even_out(x, mods, norm_pre, norm_post, attn_ps, gla_ps, gb, gla_norm, w_out, *, layer, e, group,
              np_tiles):
    n_tok, d = x.shape
    tm = WIDE_TILE
    ev_out = w_out.shape[1]
    kern = functools.partial(_evout_kernel, np_tiles=np_tiles, layer=layer, group=group)
    return pl.pallas_call(
        kern,
        grid=(n_tok // tm,),
        in_specs=[pl.BlockSpec((tm, d), lambda i: (i, 0))] + _cond_specs(mods, norm_pre, norm_post, layer) + [
            *_tile_specs(tm, A_Q, np_tiles),
            *_tile_specs(tm, B_V, np_tiles),
            pl.BlockSpec((tm, B_V), lambda i: (i, 0)),
            pl.BlockSpec((1, B_V), lambda i: (e, 0)),
            pl.BlockSpec((1, ev_out, d), lambda i: (e, 0, 0), pipeline_mode=pl.Buffered(1)),
        ],
        out_specs=pl.BlockSpec((tm, d), lambda i: (i, 0)),
        out_shape=jax.ShapeDtypeStruct((n_tok, d), F32),
        scratch_shapes=[pltpu.VMEM((ev_out, d), BF16)],
        compiler_params=pltpu.CompilerParams(
            dimension_semantics=("arbitrary",), vmem_limit_bytes=VMEM_LIMIT_BYTES),
        name="even_out",
    )(x, mods, norm_pre, norm_post, *attn_ps, *gla_ps, gb, gla_norm, w_out)


def _gelu(x):
    return 0.5 * x * (1.0 + lax.erf(x * (2.0 ** -0.5)))


def _cm_kernel(x_ref, mod_ref, gpre_ref, gpost_ref, win_ref, vg_ref, vb_ref, ws_ref, bs_ref, wout_ref,
               o_ref, win_bf, wout_bf, *, layer, group):
    @pl.when(pl.program_id(0) == 0)
    def _cast_weights():
        win_bf[...] = win_ref[0].astype(BF16)
        wout_bf[...] = wout_ref[0].astype(BF16)

    width = wout_bf.shape[0]
    gw = width // CMLP_GROUPS
    ws = [ws_ref[0, g].astype(BF16) for g in range(CMLP_GROUPS)]
    mods = _mod_rows(mod_ref, group(pl.program_id(0)))
    g_pre = _norm_row(gpre_ref, layer, 1)
    g_post = _norm_row(gpost_ref, layer, 1)
    eye = (lax.broadcasted_iota(jnp.int32, (CHUNK, CHUNK), 0)
           == lax.broadcasted_iota(jnp.int32, (CHUNK, CHUNK), 1))
    bias_cols = [jnp.sum(jnp.where(eye, jnp.broadcast_to(bs_ref[0, g:g + 1, :], (CHUNK, CHUNK)), 0.0),
                         axis=1, keepdims=True) for g in range(CMLP_GROUPS)]

    def phase_in(rows):
        x = x_ref[rows, :]
        h = _sub_in(x, mods, g_pre, 1).astype(BF16)
        v = _gelu(jnp.dot(h, win_bf[:, width:2 * width], preferred_element_type=F32))
        u = _gelu(jnp.dot(h, win_bf[:, 0:width], preferred_element_type=F32))
        return x, u, v

    def phase_mix(x, u, v):
        mu = jnp.mean(v, axis=-1, keepdims=True)
        vc = v - mu
        var = jnp.mean(vc * vc, axis=-1, keepdims=True)
        vn = (vc * lax.rsqrt(var + EPS) * vg_ref[...] + vb_ref[...]).astype(BF16)
        chunk_rows = []
        for c in range(x.shape[0] // CHUNK):
            cols = []
            for g in range(CMLP_GROUPS):
                blk = vn[c * CHUNK:(c + 1) * CHUNK, g * gw:(g + 1) * gw]
                cols.append(jnp.dot(ws[g], blk, preferred_element_type=F32) + bias_cols[g])
            chunk_rows.append(jnp.concatenate(cols, axis=1))
        mixed = jnp.concatenate(chunk_rows, axis=0)
        return x, (u * mixed).astype(BF16)

    def phase_out(rows, x, m):
        y = jnp.dot(m, wout_bf[...], preferred_element_type=F32)
        o_ref[rows, :] = _sub_out(x, y, mods, g_post, 1, 1.0)

    tiles = _sub_tiles(x_ref.shape[0])
    pending = None
    for rows in tiles:
        cur = phase_in(rows)
        if pending is not None:
            phase_out(pending[0], *phase_mix(*pending[1]))
        pending = (rows, cur)
    phase_out(pending[0], *phase_mix(*pending[1]))


def _chunk_mlp(x, mods, norm_pre, norm_post, w_in, v_gain, v_bias, w_s, b_s, w_out, *, layer, o, group):
    n_tok, d = x.shape
    tm = WIDE_TILE
    width = w_out.shape[1]
    kern = functools.partial(_cm_kernel, layer=layer, group=group)
    return pl.pallas_call(
        kern,
        grid=(n_tok // tm,),
        in_specs=[pl.BlockSpec((tm, d), lambda i: (i, 0))] + _cond_specs(mods, norm_pre, norm_post, layer) + [
            pl.BlockSpec((1, d, 2 * width), lambda i: (o, 0, 0), pipeline_mode=pl.Buffered(1)),
            pl.BlockSpec((1, width), lambda i: (o, 0)),
            pl.BlockSpec((1, width), lambda i: (o, 0)),
            pl.BlockSpec((1, CMLP_GROUPS, CHUNK, CHUNK), lambda i: (o, 0, 0, 0)),
            pl.BlockSpec((1, CMLP_GROUPS, CHUNK), lambda i: (o, 0, 0)),
            pl.BlockSpec((1, width, d), lambda i: (o, 0, 0), pipeline_mode=pl.Buffered(1)),
        ],
        out_specs=pl.BlockSpec((tm, d), lambda i: (i, 0)),
        out_shape=jax.ShapeDtypeStruct((n_tok, d), F32),
        scratch_shapes=[pltpu.VMEM((d, 2 * width), BF16), pltpu.VMEM((width, d), BF16)],
        compiler_params=pltpu.CompilerParams(
            dimension_semantics=("arbitrary",), vmem_limit_bytes=VMEM_LIMIT_BYTES),
        name="chunk_mlp",
    )(x, mods, norm_pre, norm_post, w_in, v_gain, v_bias, w_s, b_s, w_out)


def kernel(x_prompt, x_sample, cache_k, cache_v, state_gla_fwd, state_gla_bwd, c, c_ctx, w_mod, b_mod, norm_pre, norm_post, ffn_w_gate, ffn_w_up, ffn_w_down, ev_w_in, ev_w_out, ev_sink, gla_wa_f, gla_ba_f, gla_wa_b, gla_ba_b, gla_norm, cm_w_in, cm_v_gain, cm_v_bias, cm_w_s, cm_b_s, cm_w_out):
    batch, seq, d = x_prompt.shape
    dec_batch, dec_seq, _ = x_sample.shape
    depth = w_mod.shape[0]
    n_prompt_tok = batch * seq
    n_sample_tok = dec_batch * dec_seq
    assert n_prompt_tok % dec_seq == 0 and dec_seq % TOKEN_TILE == 0
    group = _group_index_map(TOKEN_TILE, n_prompt_tok, dec_seq)
    np_tiles = n_prompt_tok // TOKEN_TILE
    ffn = functools.partial(_ffn, group=group, np_tiles=np_tiles)
    assert dec_seq % WIDE_TILE == 0
    group_w = _group_index_map(WIDE_TILE, n_prompt_tok, dec_seq)
    np_tiles_w = n_prompt_tok // WIDE_TILE

    xs = (x_prompt.reshape(n_prompt_tok, d), x_sample.reshape(n_sample_tok, d))
    ev_w_in_t = jnp.swapaxes(ev_w_in, 1, 2)

    mods = _adaln_mods(c_ctx, c, w_mod, b_mod)
    gains_pre = norm_pre.reshape(depth * N_SUB, d)
    gains_post = norm_post.reshape(depth * N_SUB, d)
    cond = (mods, gains_pre, gains_post)

    new_k, new_v, new_sf, new_sb = [], [], [], []
    for layer in range(depth):
        (x,) = ffn(xs, *cond, ffn_w_gate, ffn_w_up, ffn_w_down,
                   sub=0, layer=layer, half=0, split_out=False)
        if layer % 2 == 0:
            e = layer // 2
            qa, ka, va, qb, kb, vb, gb, ldf, ldb = _even_in(
                x, *cond, ev_w_in_t, gla_wa_f, gla_wa_b, gla_ba_f, gla_ba_b,
                layer=layer, e=e, group=group_w)
            sink = ev_sink[e]
            attn_p, k_t, v_t = _attn_context(sink, qa, ka, va, n_seq=batch, seq=seq)
            k_ctx = jnp.transpose(cache_k[:, e], (0, 2, 3, 1)).reshape(dec_batch, A_KV, -1)
            v_ctx = jnp.transpose(cache_v[:, e], (0, 2, 3, 1)).reshape(dec_batch, A_KV, -1)
            attn_s = _attn_latent(sink, qa, ka, va, k_ctx, v_ctx, n_seq=dec_batch, seq=dec_seq,
                                  row_block0=n_prompt_tok // dec_seq)
            gla_p, s_f, s_b = _gla(qb, kb, vb, ldf, ldb, None, None, n_seq=batch, seq=seq,
                                   row_block0=0, want_state=True, per_step=GLA_PROMPT_SEQS)
            (gla_s,) = _gla(qb, kb, vb, ldf, ldb, state_gla_fwd[:, e], state_gla_bwd[:, e],
                            n_seq=dec_batch, seq=dec_seq, row_block0=n_prompt_tok // dec_seq,
                            want_state=False, per_step=1)
            x = _even_out(x, *cond, (attn_p, attn_s), (gla_p, gla_s), gb, gla_norm, ev_w_out,
                          layer=layer, e=e, group=group_w, np_tiles=np_tiles_w)
            for cache_t, dst in ((k_t, new_k), (v_t, new_v)):
                dst.append(jnp.transpose(cache_t.reshape(batch, ATTN_KV_HEADS, HEAD_DIM, seq), (0, 3, 1, 2)))
            new_sf.append(s_f)
            new_sb.append(s_b)
        else:
            o = layer // 2
            x = _chunk_mlp(x, *cond, cm_w_in, cm_v_gain, cm_v_bias, cm_w_s, cm_b_s, cm_w_out,
                           layer=layer, o=o, group=group_w)
        xs = ffn((x,), *cond, ffn_w_gate, ffn_w_up, ffn_w_down,
                 sub=2, layer=layer, half=1, split_out=(layer == depth - 1))

    y_prompt = xs[0].reshape(batch, seq, d)
    y_sample = xs[1].reshape(dec_batch, dec_seq, d)
    return (y_prompt, y_sample, jnp.stack(new_k, axis=1), jnp.stack(new_v, axis=1),
            jnp.stack(new_sf, axis=1), jnp.stack(new_sb, axis=1))
```

```python
import functools
import math

import jax
import jax.numpy as jnp
from jax import lax
from jax.experimental import pallas as pl
from jax.experimental.pallas import tpu as pltpu

F32 = jnp.float32
BF16 = jnp.bfloat16

EPS = 1e-6
NEG_INF = -1e30
FFN_RESIDUAL = 0.5
N_SUB = 3

ATTN_HEADS = 8
ATTN_KV_HEADS = 2
HEAD_DIM = 64
ATTN_BLOCK = 128
WINDOW = 128
GRID_W = 64
GRID_W_LOG2 = 6
ROPE_BASE = 10000.0
GLA_HEADS = 4
GLA_DK = 64
GLA_DV = 128
GLA_RANK = 16
GLA_TAU = 16.0
CHUNK = 128
CMLP_GROUPS = 4

A_Q = ATTN_HEADS * HEAD_DIM
A_KV = ATTN_KV_HEADS * HEAD_DIM
B_QK = GLA_HEADS * GLA_DK
B_V = GLA_HEADS * GLA_DV
EV_MAIN = A_Q + 2 * A_KV + 2 * B_QK + 2 * B_V
EV_OFFS = (0, A_Q, A_Q + A_KV, A_Q + 2 * A_KV, A_Q + 2 * A_KV + B_QK,
           A_Q + 2 * A_KV + 2 * B_QK, A_Q + 2 * A_KV + 2 * B_QK + B_V, EV_MAIN)

LANES = 128
LANES_LOG2 = 7
SUBLANES = 8
VMEM_LIMIT_BYTES = 56 * 1024 * 1024

TOKEN_TILE = 512
WIDE_TILE = 1024
SUB_ROWS = 512
ADALN_BUFFERS = 4
FFN_CHUNK = 256
GLA_TILE = 128
GLA_SUB = 8
GLA_PROMPT_SEQS = 4
ATTN_PROMPT_SEQS = 2
ATTN_LATENT_SEQS = 2


def _group_index_map(tile_rows, n_prompt_tok, seq_tok):
    def group(i):
        return jnp.maximum(i * tile_rows - (n_prompt_tok - seq_tok), 0) // seq_tok
    return group


def _sub_tiles(n_rows):
    assert n_rows % SUB_ROWS == 0
    return [slice(r, r + SUB_ROWS) for r in range(0, n_rows, SUB_ROWS)]


def _rms(x, g):
    ms = jnp.mean(x * x, axis=-1, keepdims=True)
    return x * lax.rsqrt(ms + EPS) * g


def _mod_rows(mod_ref, g):
    return [mod_ref[k, pl.ds(g, 1), :] for k in range(3 * N_SUB)]


def _sub_in(x, mods, g_pre, sub):
    shift, scale = mods[3 * sub + 0], mods[3 * sub + 1]
    return _rms(x, g_pre) * (1.0 + scale) + shift


def _sub_out(x, y, mods, g_post, sub, coef):
    return x + (coef * mods[3 * sub + 2]) * _rms(y, g_post)


def _norm_row(ref, layer, sub):
    r = layer * N_SUB + sub
    return ref[r:r + 1, :]


def _silu(x):
    return x * jax.nn.sigmoid(x)


def _mod_kernel(cctx_ref, c_ref, w_hbm, b_ref, o_ref, buf, sem):
    depth, n_kinds, rows, d = o_ref.shape
    r = lax.broadcasted_iota(jnp.int32, (rows, d), 0)
    cond = jnp.where(r == 0, jnp.broadcast_to(cctx_ref[...], (rows, d)), 0.0)
    for j in range(c_ref.shape[0]):
        cond = jnp.where(r == 1 + j, jnp.broadcast_to(c_ref[j:j + 1, :], (rows, d)), cond)
    s = _silu(cond).astype(BF16)
    chunks = [(l, q) for l in range(depth) for q in range(n_kinds)]
    n_buf = buf.shape[0]

    def copy(i):
        l, q = chunks[i]
        return pltpu.make_async_copy(w_hbm.at[l, :, pl.ds(q * d, d)], buf.at[i % n_buf], sem.at[i % n_buf])

    for i in range(min(n_buf, len(chunks))):
        copy(i).start()
    for i, (l, q) in enumerate(chunks):
        copy(i).wait()
        res = jnp.dot(s, buf[i % n_buf].astype(BF16), preferred_element_type=F32)
        o_ref[l, q] = res + b_ref[l:l + 1, q * d:(q + 1) * d]
        if i + n_buf < len(chunks):
            copy(i + n_buf).start()


def _adaln_mods(c_ctx, c, w_mod, b_mod):
    depth, d, n = w_mod.shape
    assert n % d == 0
    rows = -(-(1 + c.shape[0]) // SUBLANES) * SUBLANES
    return pl.pallas_call(
        _mod_kernel,
        grid=(1,),
        in_specs=[
            pl.BlockSpec((1, d), lambda i: (0, 0)),
            pl.BlockSpec(c.shape, lambda i: (0, 0)),
            pl.BlockSpec(memory_space=pl.ANY),
            pl.BlockSpec((depth, n), lambda i: (0, 0)),
        ],
        out_specs=pl.BlockSpec((depth, n // d, rows, d), lambda i: (0, 0, 0, 0)),
        out_shape=jax.ShapeDtypeStruct((depth, n // d, rows, d), F32),
        scratch_shapes=[pltpu.VMEM((ADALN_BUFFERS, d, d), F32), pltpu.SemaphoreType.DMA((ADALN_BUFFERS,))],
        compiler_params=pltpu.CompilerParams(
            dimension_semantics=("arbitrary",), vmem_limit_bytes=VMEM_LIMIT_BYTES),
        name="adaln_mods",
    )(c_ctx.reshape(1, d), c, w_mod, b_mod)


def _tile_specs(tm, width, np_tiles):
    return [pl.BlockSpec((tm, width), lambda i: (jnp.minimum(i, np_tiles - 1), 0)),
            pl.BlockSpec((tm, width), lambda i: (jnp.maximum(i - np_tiles, 0), 0))]


def _pick(refs, is_prompt, rows=slice(None)):
    if len(refs) == 1:
        return refs[0][rows, :]
    return jnp.where(is_prompt, refs[0][rows, :], refs[1][rows, :])


def _cond_specs(mods, norm_pre, norm_post, layer):
    return [pl.BlockSpec((None,) + mods.shape[1:], lambda i: (layer, 0, 0, 0)),
            pl.BlockSpec(norm_pre.shape, lambda i: (0, 0)),
            pl.BlockSpec(norm_post.shape, lambda i: (0, 0))]


def _ffn_kernel(*refs, sub, layer, half, n_chunks, n_x, n_o, np_tiles, group):
    x_refs = refs[:n_x]
    mod_ref, gpre_ref, gpost_ref, wg_hbm, wu_hbm, wd_hbm = refs[n_x:n_x + 6]
    o_refs = refs[n_x + 6:n_x + 6 + n_o]
    wg_bf, wu_bf, wd_bf, st_g, st_u, st_d, sem = refs[n_x + 6 + n_o:]
    fc = FFN_CHUNK
    step = pl.program_id(0)
    mods = _mod_rows(mod_ref, group(step))
    g_pre = _norm_row(gpre_ref, layer, sub)
    g_post = _norm_row(gpost_ref, layer, sub)

    def chunk_copies(c, slot):
        cols = pl.ds(c * fc, fc)
        return (
            pltpu.make_async_copy(wg_hbm.at[layer, half, :, cols], st_g.at[slot], sem.at[0, slot]),
            pltpu.make_async_copy(wu_hbm.at[layer, half, :, cols], st_u.at[slot], sem.at[1, slot]),
            pltpu.make_async_copy(wd_hbm.at[layer, half, cols, :], st_d.at[slot], sem.at[2, slot]),
        )

    def run(stage_weights, prompt):
        if stage_weights:
            for cp in chunk_copies(0, 0):
                cp.start()
        x = _pick(x_refs, step < np_tiles)
        h = _sub_in(x, mods, g_pre, sub).astype(BF16)
        y = jnp.zeros(x.shape, F32)
        for c in range(n_chunks):
            sl = slice(c * fc, (c + 1) * fc)
            if stage_weights:
                slot = c % 2
                if c + 1 < n_chunks:
                    for cp in chunk_copies(c + 1, 1 - slot):
                        cp.start()
                for cp in chunk_copies(c, slot):
                    cp.wait()
                wg_bf[:, sl] = st_g[slot].astype(BF16)
                wu_bf[:, sl] = st_u[slot].astype(BF16)
                wd_bf[sl, :] = st_d[slot].astype(BF16)
            g = jnp.dot(h, wg_bf[:, sl], preferred_element_type=F32)
            u = jnp.dot(h, wu_bf[:, sl], preferred_element_type=F32)
            a = (_silu(g) * u).astype(BF16)
            y = y + jnp.dot(a, wd_bf[sl, :], preferred_element_type=F32)
        o_refs[0 if (prompt or n_o == 1) else 1][...] = _sub_out(x, y, mods, g_post, sub, FFN_RESIDUAL)

    @pl.when(step == 0)
    def _first_tile():
        run(True, True)

    if n_o == 1:
        @pl.when(step > 0)
        def _other_tiles():
            run(False, True)
    else:
        @pl.when(jnp.logical_and(step > 0, step < np_tiles))
        def _prompt_tiles():
            run(False, True)

        @pl.when(step >= np_tiles)
        def _latent_tiles():
            run(False, False)


def _ffn(xs, mods, norm_pre, norm_post, wg, wu, wd, *, sub, layer, half, group, np_tiles, split_out):
    d = xs[0].shape[-1]
    n_tok = sum(x.shape[0] for x in xs)
    d_ff = wg.shape[-1]
    tm = TOKEN_TILE
    fc = FFN_CHUNK
    n_chunks = d_ff // fc
    assert n_chunks * fc == d_ff and n_tok % tm == 0
    n_x, n_o = len(xs), (2 if split_out else 1)
    kern = functools.partial(_ffn_kernel, sub=sub, layer=layer, half=half, n_chunks=n_chunks,
                             n_x=n_x, n_o=n_o, np_tiles=np_tiles, group=group)
    whole = pl.BlockSpec((tm, d), lambda i: (i, 0))
    x_specs = _tile_specs(tm, d, np_tiles) if n_x == 2 else [whole]
    n_p = np_tiles * tm
    if split_out:
        out_specs = _tile_specs(tm, d, np_tiles)
        out_shape = [jax.ShapeDtypeStruct((n_p, d), F32), jax.ShapeDtypeStruct((n_tok - n_p, d), F32)]
    else:
        out_specs = [whole]
        out_shape = [jax.ShapeDtypeStruct((n_tok, d), F32)]
    return pl.pallas_call(
        kern,
        grid=(n_tok // tm,),
        in_specs=x_specs + _cond_specs(mods, norm_pre, norm_post, layer) + [
            pl.BlockSpec(memory_space=pl.ANY),
            pl.BlockSpec(memory_space=pl.ANY),
            pl.BlockSpec(memory_space=pl.ANY),
        ],
        out_specs=out_specs,
        out_shape=out_shape,
        scratch_shapes=[
            pltpu.VMEM((d, d_ff), BF16),
            pltpu.VMEM((d, d_ff), BF16),
            pltpu.VMEM((d_ff, d), BF16),
            pltpu.VMEM((2, d, fc), F32),
            pltpu.VMEM((2, d, fc), F32),
            pltpu.VMEM((2, fc, d), F32),
            pltpu.SemaphoreType.DMA((3, 2)),
        ],
        compiler_params=pltpu.CompilerParams(
            dimension_semantics=("arbitrary",), vmem_limit_bytes=VMEM_LIMIT_BYTES),
        name=f"ffn_l{layer}h{half}",
    )(*xs, mods, norm_pre, norm_post, wg, wu, wd)


def _log_sigmoid(x):
    return jnp.minimum(x, 0.0) - jnp.log(1.0 + jnp.exp(-jnp.abs(x)))


def _evin_kernel(x_ref, mod_ref, gpre_ref, gpost_ref, w_ref, waf_ref, wab_ref, baf_ref, bab_ref,
                 qa_ref, ka_ref, va_ref, qb_ref, kb_ref, vb_ref, gb_ref, ldf_ref, ldb_ref,
                 w_bf, wa_bf, *, layer, group):
    @pl.when(pl.program_id(0) == 0)
    def _cast_weights():
        w_bf[0:EV_MAIN, :] = w_ref[0, 0:EV_MAIN, :].astype(BF16)
        w_bf[EV_MAIN:EV_MAIN + LANES, :] = jnp.zeros((LANES, w_bf.shape[1]), BF16)
        w_bf[EV_MAIN:EV_MAIN + 2 * GLA_RANK, :] = w_ref[0, EV_MAIN:EV_MAIN + 2 * GLA_RANK, :].astype(BF16)
        wa_bf[...] = jnp.zeros(wa_bf.shape, BF16)
        wa_bf[0:GLA_RANK, 0:B_QK] = waf_ref[0].astype(BF16)
        wa_bf[GLA_RANK:2 * GLA_RANK, B_QK:2 * B_QK] = wab_ref[0].astype(BF16)

    o = EV_OFFS
    mods = _mod_rows(mod_ref, group(pl.program_id(0)))
    g_pre = _norm_row(gpre_ref, layer, 1)

    def project(rows):
        h = _sub_in(x_ref[rows, :], mods, g_pre, 1).astype(BF16)
        return _dot_nt(h, w_bf[...])

    def finish(rows, full):
        qa_ref[rows, :] = (full[:, o[0]:o[1]] * (HEAD_DIM ** -0.5)).astype(qa_ref.dtype)
        ka_ref[rows, :] = full[:, o[1]:o[2]]
        va_ref[rows, :] = full[:, o[2]:o[3]]
        qb_ref[rows, :] = full[:, o[3]:o[4]] * (GLA_DK ** -0.5)
        kb_ref[rows, :] = full[:, o[4]:o[5]]
        vb_ref[rows, :] = full[:, o[5]:o[6]].astype(vb_ref.dtype)
        gb_ref[rows, :] = full[:, o[6]:o[7]].astype(gb_ref.dtype)
        lr = full[:, EV_MAIN:EV_MAIN + LANES].astype(BF16)
        logits = jnp.dot(lr, wa_bf[...], preferred_element_type=F32)
        ldf_ref[rows, :] = _log_sigmoid(logits[:, 0:B_QK] + baf_ref[...]) * (1.0 / GLA_TAU)
        ldb_ref[rows, :] = _log_sigmoid(logits[:, B_QK:2 * B_QK] + bab_ref[...]) * (1.0 / GLA_TAU)

    pending = None
    for rows in _sub_tiles(x_ref.shape[0]):
        full = project(rows)
        if pending is not None:
            finish(*pending)
        pending = (rows, full)
    finish(*pending)


def _even_in(x, mods, norm_pre, norm_post, w_in_t, wa_f, wa_b, ba_f, ba_b, *, layer, e, group):
    n_tok, d = x.shape
    tm = WIDE_TILE
    ev_in = w_in_t.shape[1]
    rank = wa_f.shape[1]
    assert rank == GLA_RANK and ev_in == EV_MAIN + 2 * rank
    widths = (A_Q, A_KV, A_KV, B_QK, B_QK, B_V, B_V, B_QK, B_QK)
    dtypes = (BF16, F32, F32, F32, F32, BF16, BF16, F32, F32)
    kern = functools.partial(_evin_kernel, layer=layer, group=group)
    return pl.pallas_call(
        kern,
        grid=(n_tok // tm,),
        in_specs=[pl.BlockSpec((tm, d), lambda i: (i, 0))] + _cond_specs(mods, norm_pre, norm_post, layer) + [
            pl.BlockSpec((1, ev_in, d), lambda i: (e, 0, 0), pipeline_mode=pl.Buffered(1)),
            pl.BlockSpec((1, rank, B_QK), lambda i: (e, 0, 0)),
            pl.BlockSpec((1, rank, B_QK), lambda i: (e, 0, 0)),
            pl.BlockSpec((1, B_QK), lambda i: (e, 0)),
            pl.BlockSpec((1, B_QK), lambda i: (e, 0)),
        ],
        out_specs=[pl.BlockSpec((tm, w), lambda i: (i, 0)) for w in widths],
        out_shape=[jax.ShapeDtypeStruct((n_tok, w), t) for w, t in zip(widths, dtypes)],
        scratch_shapes=[pltpu.VMEM((EV_MAIN + LANES, d), BF16), pltpu.VMEM((LANES, 2 * B_QK), BF16)],
        compiler_params=pltpu.CompilerParams(
            dimension_semantics=("arbitrary",), vmem_limit_bytes=VMEM_LIMIT_BYTES),
        name="even_in",
    )(x, mods, norm_pre, norm_post, w_in_t, wa_f, wa_b, ba_f, ba_b)


def _lane_lt(shape, n):
    return lax.broadcasted_iota(jnp.int32, shape, len(shape) - 1) < n


def _dup_kv_head(x, g):
    sw = pltpu.roll(x, HEAD_DIM, 1)
    lo = _lane_lt(x.shape, HEAD_DIM)
    return jnp.where(lo, x, sw) if g == 0 else jnp.where(lo, sw, x)


def _dot_nt(a, b):
    return lax.dot_general(a, b, (((1,), (1,)), ((), ())), preferred_element_type=F32)


ATTN_GROUP = ATTN_HEADS // ATTN_KV_HEADS


def _sink_column(sink_ref, g, t):
    rows = lax.broadcasted_iota(jnp.int32, (ATTN_GROUP * t, 1), 0)
    col = jnp.full((ATTN_GROUP * t, 1), sink_ref[ATTN_GROUP * g], F32)
    for j in range(1, ATTN_GROUP):
        col = jnp.where(rows >= j * t, sink_ref[ATTN_GROUP * g + j], col)
    return col


def _attn_scores(q_pairs, keys):
    t = q_pairs[0].shape[0]
    lo = _lane_lt((t, LANES), HEAD_DIM)
    zero = jnp.zeros((t, LANES), BF16)
    q4 = jnp.concatenate([jnp.where(lo, q_pairs[0], zero), jnp.where(lo, zero, q_pairs[0]),
                          jnp.where(lo, q_pairs[1], zero), jnp.where(lo, zero, q_pairs[1])], axis=0)
    scores = []
    for k2, _, bias, feature_major in keys:
        s = jnp.dot(q4, k2, preferred_element_type=F32) if feature_major else _dot_nt(q4, k2)
        if bias is not None:
            s = s + bias[...]
        scores.append(s)
    return scores


def _attn_finish(scores, sink_col, keys):
    t = scores[0].shape[0] // ATTN_GROUP
    lo = _lane_lt((t, LANES), HEAD_DIM)
    mx = sink_col
    for s in scores:
        mx = jnp.maximum(mx, jnp.max(s, axis=-1, keepdims=True))
    den = jnp.exp(sink_col - mx)
    o = None
    for s, (_, v2, _, feature_major) in zip(scores, keys):
        p = jnp.exp(s - mx)
        den = den + jnp.sum(p, axis=-1, keepdims=True)
        pb = p.astype(BF16)
        pv = _dot_nt(pb, v2) if feature_major else jnp.dot(pb, v2, preferred_element_type=F32)
        o = pv if o is None else o + pv
    o = o / den
    return (jnp.where(lo, o[0:t], o[t:2 * t]), jnp.where(lo, o[2 * t:3 * t], o[3 * t:4 * t]))


def _attn_ctx_kernel(sink_ref, q_ref, k_ref, v_ref, o_ref, kt_ref, vt_ref):
    per_step, _, t = kt_ref.shape
    work = []
    for s in range(per_step):
        rows = slice(s * t, (s + 1) * t)
        k = k_ref[rows, :]
        v = v_ref[rows, :]
        kt_ref[s] = k.T
        vt_ref[s] = v.T
        for g in range(ATTN_KV_HEADS):
            keys = [(_dup_kv_head(k, g).astype(BF16), _dup_kv_head(v, g).astype(BF16), None, False)]
            cols = (2 * g, 2 * g + 1)
            scores = _attn_scores([q_ref[rows, m * LANES:(m + 1) * LANES] for m in cols], keys)
            work.append((rows, cols, scores, keys, g))
    for rows, cols, scores, keys, g in work:
        outs = _attn_finish(scores, _sink_column(sink_ref, g, t), keys)
        for m, o in zip(cols, outs):
            o_ref[rows, m * LANES:(m + 1) * LANES] = o.astype(o_ref.dtype)


def _attn_context(sink, qa, ka, va, *, n_seq, seq):
    per_step = ATTN_PROMPT_SEQS
    assert n_seq % per_step == 0
    n_seq, rows = n_seq // per_step, per_step * seq
    cache_spec = pl.BlockSpec((per_step, A_KV, seq), lambda b: (b, 0, 0))
    return pl.pallas_call(
        _attn_ctx_kernel,
        grid=(n_seq,),
        in_specs=[
            pl.BlockSpec(memory_space=pltpu.SMEM),
            pl.BlockSpec((rows, A_Q), lambda b: (b, 0)),
            pl.BlockSpec((rows, A_KV), lambda b: (b, 0)),
            pl.BlockSpec((rows, A_KV), lambda b: (b, 0)),
        ],
        out_specs=[pl.BlockSpec((rows, A_Q), lambda b: (b, 0)), cache_spec, cache_spec],
        out_shape=[jax.ShapeDtypeStruct((n_seq * rows, A_Q), BF16),
                   jax.ShapeDtypeStruct((n_seq * per_step, A_KV, seq), F32),
                   jax.ShapeDtypeStruct((n_seq * per_step, A_KV, seq), F32)],
        compiler_params=pltpu.CompilerParams(dimension_semantics=("parallel",)),
        name="attn_context",
    )(sink, qa, ka, va)


def _rope_tables(n_tok):
    n_rows = n_tok // GRID_W
    assert n_rows <= GRID_W
    shape = (GRID_W, LANES)
    pos = lax.broadcasted_iota(jnp.int32, shape, 0).astype(F32)
    lane = lax.broadcasted_iota(jnp.int32, shape, 1)
    half = HEAD_DIM // 2
    nf = half // 2
    within = lane & (HEAD_DIM - 1)
    is_col = within >= half
    second = (within & (half - 1)) >= nf
    f = (within & (nf - 1)).astype(F32)
    inv = jnp.exp(f * (-2.0 / half * math.log(ROPE_BASE)))
    ang = pos * inv
    cos_t = jnp.cos(ang)
    sin_t = jnp.sin(ang)
    sin_t = jnp.where(second, sin_t, -sin_t)

    def per_token(tab):
        return jnp.concatenate(
            [jnp.where(is_col, tab, jnp.broadcast_to(tab[r:r + 1, :], shape)) for r in range(n_rows)], axis=0)

    return per_token(cos_t), per_token(sin_t)


def _rope(x, cos, sin_signed):
    nf = HEAD_DIM // 4
    lane = lax.broadcasted_iota(jnp.int32, x.shape, 1)
    second = (lane & (2 * nf - 1)) >= nf
    up = pltpu.roll(x, nf, 1)
    dn = pltpu.roll(x, LANES - nf, 1)
    return x * cos + jnp.where(second, up, dn) * sin_signed


def _rope_bf16(x, cos, sin_signed):
    nf = HEAD_DIM // 4
    src = lax.broadcasted_iota(jnp.int32, (LANES, LANES), 0)
    lane = lax.broadcasted_iota(jnp.int32, (LANES, LANES), 1)
    second = (lane & (2 * nf - 1)) >= nf
    perm = jnp.where(src == jnp.where(second, lane - nf, lane + nf), 1.0, 0.0).astype(BF16)
    return x.astype(F32) * cos + jnp.dot(x, perm, preferred_element_type=F32) * sin_signed


def _attn_lat_kernel(sink_ref, q_ref, k_ref, v_ref, kc_ref, vc_ref, o_ref, qr_ref, kp_ref, vp_ref, bias_ref,
                     *, seq):
    per_step = kc_ref.shape[0]
    nb = seq // ATTN_BLOCK
    blk = ATTN_BLOCK
    cos, sin_signed = _rope_tables(seq)
    zpad = jnp.zeros((blk, LANES), BF16)
    ctx = []
    for s in range(per_step):
        rows = slice(s * seq, (s + 1) * seq)
        for m in range(A_Q // LANES):
            qr_ref[rows, m * LANES:(m + 1) * LANES] = _rope_bf16(
                q_ref[rows, m * LANES:(m + 1) * LANES], cos, sin_signed).astype(BF16)
        kr = _rope(k_ref[rows, :], cos, sin_signed)
        v = v_ref[rows, :]
        for g in range(ATTN_KV_HEADS):
            n = s * ATTN_KV_HEADS + g
            kp_ref[n, 0:blk, :] = zpad
            kp_ref[n, blk + seq:2 * blk + seq, :] = zpad
            vp_ref[n, 0:blk, :] = zpad
            vp_ref[n, blk + seq:2 * blk + seq, :] = zpad
            kp_ref[n, blk:blk + seq, :] = _dup_kv_head(kr, g).astype(BF16)
            vp_ref[n, blk:blk + seq, :] = _dup_kv_head(v, g).astype(BF16)
            hd = slice(g * HEAD_DIM, (g + 1) * HEAD_DIM)
            ctx.append((jnp.concatenate([kc_ref[s, hd, :]] * 2, axis=0).astype(BF16),
                        jnp.concatenate([vc_ref[s, hd, :]] * 2, axis=0).astype(BF16), None, True))
    sink_cols = [_sink_column(sink_ref, g, blk) for g in range(ATTN_KV_HEADS)]
    r = lax.broadcasted_iota(jnp.int32, (ATTN_GROUP * blk, 3 * blk), 0) & (blk - 1)
    c = lax.broadcasted_iota(jnp.int32, (ATTN_GROUP * blk, 3 * blk), 1)
    band = (c >= r) & (c <= r + 2 * WINDOW)
    bias_ref[0] = jnp.where(band & (c >= blk), 0.0, NEG_INF)
    bias_ref[1] = jnp.where(band, 0.0, NEG_INF)
    bias_ref[2] = jnp.where(band & (c < 2 * blk), 0.0, NEG_INF)

    def block_body(i, carry):
        row0 = pl.multiple_of(i * blk, blk)
        edge = jnp.where(i == 0, 0, jnp.where(i == nb - 1, 2, 1))
        work = []
        for s in range(per_step):
            q0 = pl.multiple_of(s * seq + i * blk, blk)
            for g in range(ATTN_KV_HEADS):
                n = s * ATTN_KV_HEADS + g
                cols = (2 * g, 2 * g + 1)
                window = (kp_ref[n, pl.ds(row0, 3 * blk), :], vp_ref[n, pl.ds(row0, 3 * blk), :],
                          bias_ref.at[edge], False)
                keys = [window, ctx[n]]
                scores = _attn_scores([qr_ref[pl.ds(q0, blk), m * LANES:(m + 1) * LANES] for m in cols], keys)
                work.append((q0, cols, scores, keys, g))
        for q0, cols, scores, keys, g in work:
            outs = _attn_finish(scores, sink_cols[g], keys)
            for m, o in zip(cols, outs):
                o_ref[pl.ds(q0, blk), m * LANES:(m + 1) * LANES] = o.astype(o_ref.dtype)
        return carry

    lax.fori_loop(0, nb, block_body, 0)


def _attn_latent(sink, qa, ka, va, k_ctx, v_ctx, *, n_seq, seq, row_block0):
    past = k_ctx.shape[2]
    per_step = ATTN_LATENT_SEQS
    assert n_seq % per_step == 0 and row_block0 % per_step == 0 and seq // ATTN_BLOCK >= 2
    rows = per_step * seq
    first = row_block0 // per_step
    kern = functools.partial(_attn_lat_kernel, seq=seq)
    return pl.pallas_call(
        kern,
        grid=(n_seq // per_step,),
        in_specs=[
            pl.BlockSpec(memory_space=pltpu.SMEM),
            pl.BlockSpec((rows, A_Q), lambda b: (first + b, 0)),
            pl.BlockSpec((rows, A_KV), lambda b: (first + b, 0)),
            pl.BlockSpec((rows, A_KV), lambda b: (first + b, 0)),
            pl.BlockSpec((per_step, A_KV, past), lambda b: (b, 0, 0)),
            pl.BlockSpec((per_step, A_KV, past), lambda b: (b, 0, 0)),
        ],
        out_specs=pl.BlockSpec((rows, A_Q), lambda b: (b, 0)),
        out_shape=jax.ShapeDtypeStruct((n_seq * seq, A_Q), BF16),
        scratch_shapes=[
            pltpu.VMEM((rows, A_Q), BF16),
            pltpu.VMEM((per_step * ATTN_KV_HEADS, seq + 2 * ATTN_BLOCK, LANES), BF16),
            pltpu.VMEM((per_step * ATTN_KV_HEADS, seq + 2 * ATTN_BLOCK, LANES), BF16),
            pltpu.VMEM((3, ATTN_GROUP * ATTN_BLOCK, 3 * ATTN_BLOCK), F32),
        ],
        compiler_params=pltpu.CompilerParams(dimension_semantics=("parallel",)),
        name="attn_latent",
    )(sink, qa, ka, va, k_ctx, v_ctx)


def _same_block(a, b, size):
    return (a & -size) == (b & -size)


def _split3(x):
    hi = x.astype(BF16)
    r1 = x - hi.astype(F32)
    mid = r1.astype(BF16)
    lo = (r1 - mid.astype(F32)).astype(BF16)
    return hi, mid, lo


def _interleave(generators):
    live = list(generators)
    while live:
        for gen in list(live):
            try:
                next(gen)
            except StopIteration:
                live.remove(gen)


def _gla_tile(q, k, v, ld, st_ref, sel_ref, emit, *, rev):
    t = GLA_TILE
    n_pair = GLA_HEADS // 2
    row = lax.broadcasted_iota(jnp.int32, (t, t), 0)
    col = lax.broadcasted_iota(jnp.int32, (t, t), 1)
    tri = (row <= col) if rev else (row >= col)
    tri = jnp.where(tri, 1.0, 0.0).astype(BF16)
    hi, mid, lo = _split3(ld * math.log2(math.e))
    b = (jnp.dot(tri, hi, preferred_element_type=F32) + jnp.dot(tri, mid, preferred_element_type=F32)
         + jnp.dot(tri, lo, preferred_element_type=F32))
    yield
    b_tot = b[0:1, :] if rev else b[t - 1:t, :]
    qs = (q * jnp.exp2(b)).astype(BF16)
    kd = (k * jnp.exp2(b_tot - b)).astype(BF16)

    rowv = lax.broadcasted_iota(jnp.int32, (t, B_QK), 0)
    levels = []
    half = GLA_SUB
    while 2 * half <= t:
        blk = 2 * half
        pieces = []
        for bs in range(0, t, blk):
            r = bs + half if rev else bs + half - 1
            pieces.append(jnp.broadcast_to(b[r:r + 1, :], (blk, B_QK)))
        ref = jnp.concatenate(pieces, axis=0) if len(pieces) > 1 else pieces[0]
        second = (rowv & (blk - 1)) >= half
        is_q = ~second if rev else second
        ql = (q * jnp.exp2(jnp.where(is_q, b - ref, -jnp.inf))).astype(BF16)
        kl = (k * jnp.exp2(jnp.where(is_q, -jnp.inf, ref - b))).astype(BF16)
        levels.append((blk, ql, kl))
        half = blk
    yield

    rr = lax.broadcasted_iota(jnp.int32, (t, 2 * t), 0)
    ss = lax.broadcasted_iota(jnp.int32, (t, 2 * t), 1) & (t - 1)
    sub_rows = lax.broadcasted_iota(jnp.int32, (GLA_SUB, LANES), 0)
    lane_lo = _lane_lt((t, LANES), GLA_DK)
    lane_lo2 = _lane_lt((2 * t, LANES), GLA_DK)
    par_rows = lax.broadcasted_iota(jnp.int32, (2 * t, LANES), 0) < t
    head_sel = par_rows == lane_lo2

    outs = []
    for p in range(n_pair):
        pl_ = slice(p * LANES, (p + 1) * LANES)
        qp, kp, bp = q[:, pl_], k[:, pl_], b[:, pl_]
        rows = []
        for sb in range(t // GLA_SUB):
            r0 = sb * GLA_SUB
            q_sub = qp[r0:r0 + GLA_SUB, :]
            b_sub = bp[r0:r0 + GLA_SUB, :]
            cols = []
            for j in range(GLA_SUB):
                kj = kp[r0 + j:r0 + j + 1, :]
                bj = bp[r0 + j:r0 + j + 1, :]
                valid = (sub_rows <= j) if rev else (sub_rows >= j)
                e = jnp.exp2(jnp.where(valid, b_sub - bj, -jnp.inf))
                cols.append(q_sub * kj * e)
            rows.append(jnp.concatenate(cols, axis=1))
        e_p = jnp.concatenate(rows, axis=0).astype(BF16)
        yield
        sub_scores = jnp.dot(e_p, sel_ref[...], preferred_element_type=F32)
        scores = jnp.zeros_like(sub_scores)
        for blk, ql, kl in reversed(levels):
            klp = kl[:, pl_]
            kstack = jnp.concatenate([klp, klp], axis=0)
            kstack = jnp.where(head_sel, kstack, jnp.zeros_like(kstack))
            s_l = _dot_nt(ql[:, pl_], kstack)
            scores = jnp.where(_same_block(rr, ss, blk), s_l, scores)
        scores = jnp.where(_same_block(rr, ss, GLA_SUB), sub_scores, scores)
        yield
        vp = v[:, p * 2 * GLA_DV:(p + 1) * 2 * GLA_DV]
        v_lo = _lane_lt(vp.shape, GLA_DV)
        vbd = jnp.concatenate([jnp.where(v_lo, vp, 0.0), jnp.where(v_lo, 0.0, vp)], axis=0).astype(BF16)
        st = st_ref[p]
        o_p = (jnp.dot(scores.astype(BF16), vbd, preferred_element_type=F32)
               + _dot_nt(qs[:, pl_], st.astype(BF16)))
        outs.append(o_p)
        upd = jnp.dot(vp.T.astype(BF16), kd[:, pl_], preferred_element_type=F32)
        st_rows_lo = lax.broadcasted_iota(jnp.int32, upd.shape, 0) < GLA_DV
        bd = st_rows_lo == _lane_lt(upd.shape, GLA_DK)
        st_ref[p] = st * jnp.exp2(b_tot[:, pl_]) + jnp.where(bd, upd, 0.0)
        yield
    emit(jnp.concatenate(outs, axis=1))


def _gla_kernel(*refs, seq, has_init, want_state, per_step):
    q_ref, k_ref, v_ref, ldf_ref, ldb_ref = refs[:5]
    pos = 5
    if has_init:
        s0f_ref, s0b_ref = refs[pos:pos + 2]
        pos += 2
    o_ref = refs[pos]
    pos += 1
    if want_state:
        sf_ref, sb_ref = refs[pos:pos + 2]
        pos += 2
    st_ref, sel_ref, acc_ref = refs[pos:pos + 3]
    t = GLA_TILE
    nt = seq // t
    n_pair = GLA_HEADS // 2

    kk = lax.broadcasted_iota(jnp.int32, (GLA_SUB * LANES, 2 * t), 0)
    nn = lax.broadcasted_iota(jnp.int32, (GLA_SUB * LANES, 2 * t), 1)
    sel = ((nn & (GLA_SUB - 1)) == (kk >> LANES_LOG2)) & (((kk & (LANES - 1)) >= GLA_DK) == (nn >= t))
    sel_ref[...] = jnp.where(sel, 1.0, 0.0).astype(BF16)

    def load_state(st, s0_ref, s):
        for p in range(n_pair):
            if s0_ref is None:
                st[p] = jnp.zeros(st.shape[1:], F32)
            else:
                x = jnp.concatenate([s0_ref[s, 2 * p], s0_ref[s, 2 * p + 1]], axis=0)
                xt = x.T
                lo = _lane_lt(xt.shape, GLA_DK)
                st[p] = jnp.concatenate([jnp.where(lo, xt, 0.0), jnp.where(lo, 0.0, xt)], axis=0)

    def store_state(st, out_ref, s):
        for p in range(n_pair):
            z = st[p]
            lo = _lane_lt((GLA_DV, LANES), GLA_DK)
            z = jnp.where(lo, z[0:GLA_DV, :], z[GLA_DV:2 * GLA_DV, :]).T
            out_ref[s, 2 * p] = z[0:GLA_DK, :]
            out_ref[s, 2 * p + 1] = z[GLA_DK:2 * GLA_DK, :]

    walks = [(s, rev, ldb_ref if rev else ldf_ref, st_ref.at[2 * s + int(rev)])
             for s in range(per_step) for rev in (False, True)]
    for s, rev, _, st in walks:
        load_state(st, (s0b_ref if rev else s0f_ref) if has_init else None, s)
    acc_ref[...] = jnp.zeros(acc_ref.shape, F32)

    def tile_body(ti, carry):
        def walk(s, rev, ld_ref, st):
            tile = (nt - 1 - ti) if rev else ti
            r0 = pl.multiple_of(s * seq + tile * t, t)
            rows = pl.ds(r0, t)

            def emit(o):
                acc_ref[rows, :] = acc_ref[rows, :] + o

            return _gla_tile(q_ref[rows, :], k_ref[rows, :], v_ref[rows, :].astype(F32), ld_ref[rows, :],
                             st, sel_ref, emit, rev=rev)

        _interleave([walk(*w) for w in walks])
        return carry

    lax.fori_loop(0, nt, tile_body, 0)
    o_ref[...] = acc_ref[...].astype(o_ref.dtype)
    if want_state:
        for s, rev, _, st in walks:
            store_state(st, sb_ref if rev else sf_ref, s)


def _gla(qb, kb, vb, ldf, ldb, s0f, s0b, *, n_seq, seq, row_block0, want_state, per_step):
    has_init = s0f is not None
    assert n_seq % per_step == 0 and row_block0 % per_step == 0
    kern = functools.partial(_gla_kernel, seq=seq, has_init=has_init, want_state=want_state,
                             per_step=per_step)
    n_seq, seq_rows, row_block0 = n_seq // per_step, per_step * seq, row_block0 // per_step
    tok = lambda w: pl.BlockSpec((seq_rows, w), lambda b: (row_block0 + b, 0))
    in_specs = [tok(B_QK), tok(B_QK), tok(B_V), tok(B_QK), tok(B_QK)]
    args = [qb, kb, vb, ldf, ldb]
    state_spec = pl.BlockSpec((per_step, GLA_HEADS, GLA_DK, GLA_DV), lambda b: (b, 0, 0, 0))
    if has_init:
        in_specs += [state_spec, state_spec]
        args += [s0f, s0b]
    out_specs = [pl.BlockSpec((seq_rows, B_V), lambda b: (b, 0))]
    out_shape = [jax.ShapeDtypeStruct((n_seq * seq_rows, B_V), BF16)]
    if want_state:
        out_specs += [state_spec, state_spec]
        out_shape += [jax.ShapeDtypeStruct((n_seq * per_step, GLA_HEADS, GLA_DK, GLA_DV), F32)] * 2
    return pl.pallas_call(
        kern,
        grid=(n_seq,),
        in_specs=in_specs,
        out_specs=out_specs,
        out_shape=out_shape,
        scratch_shapes=[
            pltpu.VMEM((2 * per_step, GLA_HEADS // 2, 2 * GLA_DV, LANES), F32),
            pltpu.VMEM((GLA_SUB * LANES, 2 * GLA_TILE), BF16),
            pltpu.VMEM((seq_rows, B_V), F32),
        ],
        compiler_params=pltpu.CompilerParams(
            dimension_semantics=("parallel",), vmem_limit_bytes=VMEM_LIMIT_BYTES),
        name="gla_state" if want_state else "gla_latent",
    )(*args)


def _evout_kernel(x_ref, mod_ref, gpre_ref, gpost_ref, attn_p_ref, attn_s_ref, gla_p_ref, gla_s_ref, gb_ref,
                  gn_ref, w_ref, o_ref, w_bf, *, np_tiles, layer, group):
    @pl.when(pl.program_id(0) == 0)
    def _cast_weights():
        w_bf[...] = w_ref[0].astype(BF16)

    is_prompt = pl.program_id(0) < np_tiles
    mods = _mod_rows(mod_ref, group(pl.program_id(0)))
    g_post = _norm_row(gpost_ref, layer, 1)
    for rows in _sub_tiles(x_ref.shape[0]):
        attn = _pick((attn_p_ref, attn_s_ref), is_prompt, rows)
        gla = _pick((gla_p_ref, gla_s_ref), is_prompt, rows)
        parts = [attn]
        for hd in range(GLA_HEADS):
            sl = slice(hd * GLA_DV, (hd + 1) * GLA_DV)
            g = _rms(gla[:, sl].astype(F32), gn_ref[:, sl]) * _silu(gb_ref[rows, sl].astype(F32))
            parts.append(g.astype(BF16))
        y = jnp.dot(jnp.concatenate(parts, axis=1), w_bf[...], preferred_element_type=F32)
        o_ref[rows, :] = _sub_out(x_ref[rows, :], y, mods, g_post, 1, 1.0)


def _even_out(x, mods, norm_pre, norm_post, attn_ps, gla_ps, gb, gla_norm, w_out, *, layer, e, group,
              np_tiles):
    n_tok, d = x.shape
    tm = WIDE_TILE
    ev_out = w_out.shape[1]
    kern = functools.partial(_evout_kernel, np_tiles=np_tiles, layer=layer, group=group)
    return pl.pallas_call(
        kern,
        grid=(n_tok // tm,),
        in_specs=[pl.BlockSpec((tm, d), lambda i: (i, 0))] + _cond_specs(mods, norm_pre, norm_post, layer) + [
            *_tile_specs(tm, A_Q, np_tiles),
            *_tile_specs(tm, B_V, np_tiles),
            pl.BlockSpec((tm, B_V), lambda i: (i, 0)),
            pl.BlockSpec((1, B_V), lambda i: (e, 0)),
            pl.BlockSpec((1, ev_out, d), lambda i: (e, 0, 0), pipeline_mode=pl.Buffered(1)),
        ],
        out_specs=pl.BlockSpec((tm, d), lambda i: (i, 0)),
        out_shape=jax.ShapeDtypeStruct((n_tok, d), F32),
        scratch_shapes=[pltpu.VMEM((ev_out, d), BF16)],
        compiler_params=pltpu.CompilerParams(
            dimension_semantics=("arbitrary",), vmem_limit_bytes=VMEM_LIMIT_BYTES),
        name="even_out",
    )(x, mods, norm_pre, norm_post, *attn_ps, *gla_ps, gb, gla_norm, w_out)


def _gelu(x):
    return 0.5 * x * (1.0 + lax.erf(x * (2.0 ** -0.5)))


def _cm_kernel(x_ref, mod_ref, gpre_ref, gpost_ref, win_ref, vg_ref, vb_ref, ws_ref, bs_ref, wout_ref,
               o_ref, win_bf, wout_bf, *, layer, group):
    @pl.when(pl.program_id(0) == 0)
    def _cast_weights():
        win_bf[...] = win_ref[0].astype(BF16)
        wout_bf[...] = wout_ref[0].astype(BF16)

    width = wout_bf.shape[0]
    gw = width // CMLP_GROUPS
    ws = [ws_ref[0, g].astype(BF16) for g in range(CMLP_GROUPS)]
    mods = _mod_rows(mod_ref, group(pl.program_id(0)))
    g_pre = _norm_row(gpre_ref, layer, 1)
    g_post = _norm_row(gpost_ref, layer, 1)
    eye = (lax.broadcasted_iota(jnp.int32, (CHUNK, CHUNK), 0)
           == lax.broadcasted_iota(jnp.int32, (CHUNK, CHUNK), 1))
    bias_cols = [jnp.sum(jnp.where(eye, jnp.broadcast_to(bs_ref[0, g:g + 1, :], (CHUNK, CHUNK)), 0.0),
                         axis=1, keepdims=True) for g in range(CMLP_GROUPS)]

    def phase_in(rows):
        x = x_ref[rows, :]
        h = _sub_in(x, mods, g_pre, 1).astype(BF16)
        v = _gelu(jnp.dot(h, win_bf[:, width:2 * width], preferred_element_type=F32))
        u = _gelu(jnp.dot(h, win_bf[:, 0:width], preferred_element_type=F32))
        return x, u, v

    def phase_mix(x, u, v):
        mu = jnp.mean(v, axis=-1, keepdims=True)
        vc = v - mu
        var = jnp.mean(vc * vc, axis=-1, keepdims=True)
        vn = (vc * lax.rsqrt(var + EPS) * vg_ref[...] + vb_ref[...]).astype(BF16)
        chunk_rows = []
        for c in range(x.shape[0] // CHUNK):
            cols = []
            for g in range(CMLP_GROUPS):
                blk = vn[c * CHUNK:(c + 1) * CHUNK, g * gw:(g + 1) * gw]
                cols.append(jnp.dot(ws[g], blk, preferred_element_type=F32) + bias_cols[g])
            chunk_rows.append(jnp.concatenate(cols, axis=1))
        mixed = jnp.concatenate(chunk_rows, axis=0)
        return x, (u * mixed).astype(BF16)

    def phase_out(rows, x, m):
        y = jnp.dot(m, wout_bf[...], preferred_element_type=F32)
        o_ref[rows, :] = _sub_out(x, y, mods, g_post, 1, 1.0)

    tiles = _sub_tiles(x_ref.shape[0])
    pending = None
    for rows in tiles:
        cur = phase_in(rows)
        if pending is not None:
            phase_out(pending[0], *phase_mix(*pending[1]))
        pending = (rows, cur)
    phase_out(pending[0], *phase_mix(*pending[1]))


def _chunk_mlp(x, mods, norm_pre, norm_post, w_in, v_gain, v_bias, w_s, b_s, w_out, *, layer, o, group):
    n_tok, d = x.shape
    tm = WIDE_TILE
    width = w_out.shape[1]
    kern = functools.partial(_cm_kernel, layer=layer, group=group)
    return pl.pallas_call(
        kern,
        grid=(n_tok // tm,),
        in_specs=[pl.BlockSpec((tm, d), lambda i: (i, 0))] + _cond_specs(mods, norm_pre, norm_post, layer) + [
            pl.BlockSpec((1, d, 2 * width), lambda i: (o, 0, 0), pipeline_mode=pl.Buffered(1)),
            pl.BlockSpec((1, width), lambda i: (o, 0)),
            pl.BlockSpec((1, width), lambda i: (o, 0)),
            pl.BlockSpec((1, CMLP_GROUPS, CHUNK, CHUNK), lambda i: (o, 0, 0, 0)),
            pl.BlockSpec((1, CMLP_GROUPS, CHUNK), lambda i: (o, 0, 0)),
            pl.BlockSpec((1, width, d), lambda i: (o, 0, 0), pipeline_mode=pl.Buffered(1)),
        ],
        out_specs=pl.BlockSpec((tm, d), lambda i: (i, 0)),
        out_shape=jax.ShapeDtypeStruct((n_tok, d), F32),
        scratch_shapes=[pltpu.VMEM((d, 2 * width), BF16), pltpu.VMEM((width, d), BF16)],
        compiler_params=pltpu.CompilerParams(
            dimension_semantics=("arbitrary",), vmem_limit_bytes=VMEM_LIMIT_BYTES),
        name="chunk_mlp",
    )(x, mods, norm_pre, norm_post, w_in, v_gain, v_bias, w_s, b_s, w_out)


def kernel(x_prompt, x_sample, cache_k, cache_v, state_gla_fwd, state_gla_bwd, c, c_ctx, w_mod, b_mod, norm_pre, norm_post, ffn_w_gate, ffn_w_up, ffn_w_down, ev_w_in, ev_w_out, ev_sink, gla_wa_f, gla_ba_f, gla_wa_b, gla_ba_b, gla_norm, cm_w_in, cm_v_gain, cm_v_bias, cm_w_s, cm_b_s, cm_w_out):
    batch, seq, d = x_prompt.shape
    dec_batch, dec_seq, _ = x_sample.shape
    depth = w_mod.shape[0]
    n_prompt_tok = batch * seq
    n_sample_tok = dec_batch * dec_seq
    assert n_prompt_tok % dec_seq == 0 and dec_seq % TOKEN_TILE == 0
    group = _group_index_map(TOKEN_TILE, n_prompt_tok, dec_seq)
    np_tiles = n_prompt_tok // TOKEN_TILE
    ffn = functools.partial(_ffn, group=group, np_tiles=np_tiles)
    assert dec_seq % WIDE_TILE == 0
    group_w = _group_index_map(WIDE_TILE, n_prompt_tok, dec_seq)
    np_tiles_w = n_prompt_tok // WIDE_TILE

    xs = (x_prompt.reshape(n_prompt_tok, d), x_sample.reshape(n_sample_tok, d))
    ev_w_in_t = jnp.swapaxes(ev_w_in, 1, 2)

    mods = _adaln_mods(c_ctx, c, w_mod, b_mod)
    gains_pre = norm_pre.reshape(depth * N_SUB, d)
    gains_post = norm_post.reshape(depth * N_SUB, d)
    cond = (mods, gains_pre, gains_post)

    new_k, new_v, new_sf, new_sb = [], [], [], []
    for layer in range(depth):
        (x,) = ffn(xs, *cond, ffn_w_gate, ffn_w_up, ffn_w_down,
                   sub=0, layer=layer, half=0, split_out=False)
        if layer % 2 == 0:
            e = layer // 2
            qa, ka, va, qb, kb, vb, gb, ldf, ldb = _even_in(
                x, *cond, ev_w_in_t, gla_wa_f, gla_wa_b, gla_ba_f, gla_ba_b,
                layer=layer, e=e, group=group_w)
            sink = ev_sink[e]
            attn_p, k_t, v_t = _attn_context(sink, qa, ka, va, n_seq=batch, seq=seq)
            k_ctx = jnp.transpose(cache_k[:, e], (0, 2, 3, 1)).reshape(dec_batch, A_KV, -1)
            v_ctx = jnp.transpose(cache_v[:, e], (0, 2, 3, 1)).reshape(dec_batch, A_KV, -1)
            attn_s = _attn_latent(sink, qa, ka, va, k_ctx, v_ctx, n_seq=dec_batch, seq=dec_seq,
                                  row_block0=n_prompt_tok // dec_seq)
            gla_p, s_f, s_b = _gla(qb, kb, vb, ldf, ldb, None, None, n_seq=batch, seq=seq,
                                   row_block0=0, want_state=True, per_step=GLA_PROMPT_SEQS)
            (gla_s,) = _gla(qb, kb, vb, ldf, ldb, state_gla_fwd[:, e], state_gla_bwd[:, e],
                            n_seq=dec_batch, seq=dec_seq, row_block0=n_prompt_tok // dec_seq,
                            want_state=False, per_step=1)
            x = _even_out(x, *cond, (attn_p, attn_s), (gla_p, gla_s), gb, gla_norm, ev_w_out,
                          layer=layer, e=e, group=group_w, np_tiles=np_tiles_w)
            for cache_t, dst in ((k_t, new_k), (v_t, new_v)):
                dst.append(jnp.transpose(cache_t.reshape(batch, ATTN_KV_HEADS, HEAD_DIM, seq), (0, 3, 1, 2)))
            new_sf.append(s_f)
            new_sb.append(s_b)
        else:
            o = layer // 2
            x = _chunk_mlp(x, *cond, cm_w_in, cm_v_gain, cm_v_bias, cm_w_s, cm_b_s, cm_w_out,
                           layer=layer, o=o, group=group_w)
        xs = ffn((x,), *cond, ffn_w_gate, ffn_w_up, ffn_w_down,
                 sub=2, layer=layer, half=1, split_out=(layer == depth - 1))

    y_prompt = xs[0].reshape(batch, seq, d)
    y_sample = xs[1].reshape(dec_batch, dec_seq, d)
    return (y_prompt, y_sample, jnp.stack(new_k, axis=1), jnp.stack(new_v, axis=1),
            jnp.stack(new_sf, axis=1), jnp.stack(new_sb, axis=1))
```

```python
import functools
import math

import jax
import jax.numpy as jnp
from jax import lax
from jax.experimental import pallas as pl
from jax.experimental.pallas import tpu as pltpu

F32 = jnp.float32
BF16 = jnp.bfloat16

EPS = 1e-6
NEG_INF = -1e30
FFN_RESIDUAL = 0.5
N_SUB = 3

ATTN_HEADS = 8
ATTN_KV_HEADS = 2
HEAD_DIM = 64
ATTN_BLOCK = 128
WINDOW = 128
GRID_W = 64
GRID_W_LOG2 = 6
ROPE_BASE = 10000.0
GLA_HEADS = 4
GLA_DK = 64
GLA_DV = 128
GLA_RANK = 16
GLA_TAU = 16.0
CHUNK = 128
CMLP_GROUPS = 4

A_Q = ATTN_HEADS * HEAD_DIM
A_KV = ATTN_KV_HEADS * HEAD_DIM
B_QK = GLA_HEADS * GLA_DK
B_V = GLA_HEADS * GLA_DV
EV_MAIN = A_Q + 2 * A_KV + 2 * B_QK + 2 * B_V
EV_OFFS = (0, A_Q, A_Q + A_KV, A_Q + 2 * A_KV, A_Q + 2 * A_KV + B_QK,
           A_Q + 2 * A_KV + 2 * B_QK, A_Q + 2 * A_KV + 2 * B_QK + B_V, EV_MAIN)

LANES = 128
LANES_LOG2 = 7
SUBLANES = 8
VMEM_LIMIT_BYTES = 56 * 1024 * 1024

TOKEN_TILE = 512
WIDE_TILE = 1024
SUB_ROWS = 512
ADALN_BUFFERS = 4
FFN_CHUNK = 256
FFN_STAGE_SLOTS = 3
GLA_TILE = 128
GLA_SUB = 8
GLA_PROMPT_SEQS = 4
GLA_LATENT_SEQS = 2
ATTN_PROMPT_SEQS = 2
ATTN_LATENT_SEQS = 2


def _group_index_map(tile_rows, n_prompt_tok, seq_tok):
    def group(i):
        return jnp.maximum(i * tile_rows - (n_prompt_tok - seq_tok), 0) // seq_tok
    return group


def _sub_tiles(n_rows):
    assert n_rows % SUB_ROWS == 0
    return [slice(r, r + SUB_ROWS) for r in range(0, n_rows, SUB_ROWS)]


def _rms(x, g):
    ms = jnp.mean(x * x, axis=-1, keepdims=True)
    return x * lax.rsqrt(ms + EPS) * g


def _mod_rows(mod_ref, g):
    return [mod_ref[k, pl.ds(g, 1), :] for k in range(3 * N_SUB)]


def _sub_in(x, mods, g_pre, sub):
    shift, scale = mods[3 * sub + 0], mods[3 * sub + 1]
    return _rms(x, g_pre) * (1.0 + scale) + shift


def _sub_out(x, y, mods, g_post, sub, coef):
    return x + (coef * mods[3 * sub + 2]) * _rms(y, g_post)


def _norm_row(ref, layer, sub):
    r = layer * N_SUB + sub
    return ref[r:r + 1, :]


def _silu(x):
    return x * jax.nn.sigmoid(x)


def _mod_kernel(cctx_ref, c_ref, w_hbm, b_ref, o_ref, buf, sem):
    depth, n_kinds, rows, d = o_ref.shape
    r = lax.broadcasted_iota(jnp.int32, (rows, d), 0)
    cond = jnp.where(r == 0, jnp.broadcast_to(cctx_ref[...], (rows, d)), 0.0)
    for j in range(c_ref.shape[0]):
        cond = jnp.where(r == 1 + j, jnp.broadcast_to(c_ref[j:j + 1, :], (rows, d)), cond)
    s = _silu(cond).astype(BF16)
    chunks = [(l, q) for l in range(depth) for q in range(n_kinds)]
    n_buf = buf.shape[0]

    def copy(i):
        l, q = chunks[i]
        return pltpu.make_async_copy(w_hbm.at[l, :, pl.ds(q * d, d)], buf.at[i % n_buf], sem.at[i % n_buf])

    for i in range(min(n_buf, len(chunks))):
        copy(i).start()
    for i, (l, q) in enumerate(chunks):
        copy(i).wait()
        res = jnp.dot(s, buf[i % n_buf].astype(BF16), preferred_element_type=F32)
        o_ref[l, q] = res + b_ref[l:l + 1, q * d:(q + 1) * d]
        if i + n_buf < len(chunks):
            copy(i + n_buf).start()


def _adaln_mods(c_ctx, c, w_mod, b_mod):
    depth, d, n = w_mod.shape
    assert n % d == 0
    rows = -(-(1 + c.shape[0]) // SUBLANES) * SUBLANES
    return pl.pallas_call(
        _mod_kernel,
        grid=(1,),
        in_specs=[
            pl.BlockSpec((1, d), lambda i: (0, 0)),
            pl.BlockSpec(c.shape, lambda i: (0, 0)),
            pl.BlockSpec(memory_space=pl.ANY),
            pl.BlockSpec((depth, n), lambda i: (0, 0)),
        ],
        out_specs=pl.BlockSpec((depth, n // d, rows, d), lambda i: (0, 0, 0, 0)),
        out_shape=jax.ShapeDtypeStruct((depth, n // d, rows, d), F32),
        scratch_shapes=[pltpu.VMEM((ADALN_BUFFERS, d, d), F32), pltpu.SemaphoreType.DMA((ADALN_BUFFERS,))],
        compiler_params=pltpu.CompilerParams(
            dimension_semantics=("arbitrary",), vmem_limit_bytes=VMEM_LIMIT_BYTES),
        name="adaln_mods",
    )(c_ctx.reshape(1, d), c, w_mod, b_mod)


def _tile_specs(tm, width, np_tiles):
    return [pl.BlockSpec((tm, width), lambda i: (jnp.minimum(i, np_tiles - 1), 0)),
            pl.BlockSpec((tm, width), lambda i: (jnp.maximum(i - np_tiles, 0), 0))]


def _pick(refs, is_prompt, rows=slice(None)):
    if len(refs) == 1:
        return refs[0][rows, :]
    return jnp.where(is_prompt, refs[0][rows, :], refs[1][rows, :])


def _cond_specs(mods, norm_pre, norm_post, layer):
    return [pl.BlockSpec((None,) + mods.shape[1:], lambda i: (layer, 0, 0, 0)),
            pl.BlockSpec(norm_pre.shape, lambda i: (0, 0)),
            pl.BlockSpec(norm_post.shape, lambda i: (0, 0))]


def _ffn_kernel(*refs, sub, layer, half, n_chunks, n_x, n_o, np_tiles, group):
    x_refs = refs[:n_x]
    mod_ref, gpre_ref, gpost_ref, wg_hbm, wu_hbm, wd_hbm = refs[n_x:n_x + 6]
    o_refs = refs[n_x + 6:n_x + 6 + n_o]
    wg_bf, wu_bf, wd_bf, st_g, st_u, st_d, sem = refs[n_x + 6 + n_o:]
    fc = FFN_CHUNK
    step = pl.program_id(0)
    mods = _mod_rows(mod_ref, group(step))
    g_pre = _norm_row(gpre_ref, layer, sub)
    g_post = _norm_row(gpost_ref, layer, sub)

    def chunk_copies(c, slot):
        cols = pl.ds(c * fc, fc)
        return (
            pltpu.make_async_copy(wg_hbm.at[layer, half, :, cols], st_g.at[slot], sem.at[0, slot]),
            pltpu.make_async_copy(wu_hbm.at[layer, half, :, cols], st_u.at[slot], sem.at[1, slot]),
            pltpu.make_async_copy(wd_hbm.at[layer, half, cols, :], st_d.at[slot], sem.at[2, slot]),
        )

    def run(stage_weights, prompt):
        n_slots = st_g.shape[0]
        if stage_weights:
            for c in range(min(n_slots - 1, n_chunks)):
                for cp in chunk_copies(c, c):
                    cp.start()
        x = _pick(x_refs, step < np_tiles)
        h = _sub_in(x, mods, g_pre, sub).astype(BF16)
        y = jnp.zeros(x.shape, F32)
        for c in range(n_chunks):
            sl = slice(c * fc, (c + 1) * fc)
            if stage_weights:
                slot = c % n_slots
                ahead = c + n_slots - 1
                if ahead < n_chunks:
                    for cp in chunk_copies(ahead, ahead % n_slots):
                        cp.start()
                for cp in chunk_copies(c, slot):
                    cp.wait()
                wg_bf[:, sl] = st_g[slot].astype(BF16)
                wu_bf[:, sl] = st_u[slot].astype(BF16)
                wd_bf[sl, :] = st_d[slot].astype(BF16)
            g = jnp.dot(h, wg_bf[:, sl], preferred_element_type=F32)
            u = jnp.dot(h, wu_bf[:, sl], preferred_element_type=F32)
            a = (_silu(g) * u).astype(BF16)
            y = y + jnp.dot(a, wd_bf[sl, :], preferred_element_type=F32)
        o_refs[0 if (prompt or n_o == 1) else 1][...] = _sub_out(x, y, mods, g_post, sub, FFN_RESIDUAL)

    @pl.when(step == 0)
    def _first_tile():
        run(True, True)

    if n_o == 1:
        @pl.when(step > 0)
        def _other_tiles():
            run(False, True)
    else:
        @pl.when(jnp.logical_and(step > 0, step < np_tiles))
        def _prompt_tiles():
            run(False, True)

        @pl.when(step >= np_tiles)
        def _latent_tiles():
            run(False, False)


def _ffn(xs, mods, norm_pre, norm_post, wg, wu, wd, *, sub, layer, half, group, np_tiles, split_out):
    d = xs[0].shape[-1]
    n_tok = sum(x.shape[0] for x in xs)
    d_ff = wg.shape[-1]
    tm = TOKEN_TILE
    fc = FFN_CHUNK
    n_chunks = d_ff // fc
    assert n_chunks * fc == d_ff and n_tok % tm == 0
    n_x, n_o = len(xs), (2 if split_out else 1)
    kern = functools.partial(_ffn_kernel, sub=sub, layer=layer, half=half, n_chunks=n_chunks,
                             n_x=n_x, n_o=n_o, np_tiles=np_tiles, group=group)
    whole = pl.BlockSpec((tm, d), lambda i: (i, 0))
    x_specs = _tile_specs(tm, d, np_tiles) if n_x == 2 else [whole]
    n_p = np_tiles * tm
    if split_out:
        out_specs = _tile_specs(tm, d, np_tiles)
        out_shape = [jax.ShapeDtypeStruct((n_p, d), F32), jax.ShapeDtypeStruct((n_tok - n_p, d), F32)]
    else:
        out_specs = [whole]
        out_shape = [jax.ShapeDtypeStruct((n_tok, d), F32)]
    return pl.pallas_call(
        kern,
        grid=(n_tok // tm,),
        in_specs=x_specs + _cond_specs(mods, norm_pre, norm_post, layer) + [
            pl.BlockSpec(memory_space=pl.ANY),
            pl.BlockSpec(memory_space=pl.ANY),
            pl.BlockSpec(memory_space=pl.ANY),
        ],
        out_specs=out_specs,
        out_shape=out_shape,
        scratch_shapes=[
            pltpu.VMEM((d, d_ff), BF16),
            pltpu.VMEM((d, d_ff), BF16),
            pltpu.VMEM((d_ff, d), BF16),
            pltpu.VMEM((FFN_STAGE_SLOTS, d, fc), F32),
            pltpu.VMEM((FFN_STAGE_SLOTS, d, fc), F32),
            pltpu.VMEM((FFN_STAGE_SLOTS, fc, d), F32),
            pltpu.SemaphoreType.DMA((3, FFN_STAGE_SLOTS)),
        ],
        compiler_params=pltpu.CompilerParams(
            dimension_semantics=("arbitrary",), vmem_limit_bytes=VMEM_LIMIT_BYTES),
        name=f"ffn_l{layer}h{half}",
    )(*xs, mods, norm_pre, norm_post, wg, wu, wd)


def _log_sigmoid(x):
    return jnp.minimum(x, 0.0) - jnp.log(1.0 + jnp.exp(-jnp.abs(x)))


def _evin_kernel(x_ref, mod_ref, gpre_ref, gpost_ref, w_ref, waf_ref, wab_ref, baf_ref, bab_ref,
                 qa_ref, ka_ref, va_ref, qb_ref, kb_ref, vb_ref, gb_ref, ldf_ref, ldb_ref,
                 w_bf, wa_bf, *, layer, group):
    @pl.when(pl.program_id(0) == 0)
    def _cast_weights():
        w_bf[0:EV_MAIN, :] = w_ref[0, 0:EV_MAIN, :].astype(BF16)
        w_bf[EV_MAIN:EV_MAIN + LANES, :] = jnp.zeros((LANES, w_bf.shape[1]), BF16)
        w_bf[EV_MAIN:EV_MAIN + 2 * GLA_RANK, :] = w_ref[0, EV_MAIN:EV_MAIN + 2 * GLA_RANK, :].astype(BF16)
        wa_bf[...] = jnp.zeros(wa_bf.shape, BF16)
        wa_bf[0:GLA_RANK, 0:B_QK] = waf_ref[0].astype(BF16)
        wa_bf[GLA_RANK:2 * GLA_RANK, B_QK:2 * B_QK] = wab_ref[0].astype(BF16)

    o = EV_OFFS
    mods = _mod_rows(mod_ref, group(pl.program_id(0)))
    g_pre = _norm_row(gpre_ref, layer, 1)

    def project(rows):
        h = _sub_in(x_ref[rows, :], mods, g_pre, 1).astype(BF16)
        return _dot_nt(h, w_bf[...])

    def finish(rows, full):
        qa_ref[rows, :] = (full[:, o[0]:o[1]] * (HEAD_DIM ** -0.5)).astype(qa_ref.dtype)
        ka_ref[rows, :] = full[:, o[1]:o[2]]
        va_ref[rows, :] = full[:, o[2]:o[3]]
        qb_ref[rows, :] = full[:, o[3]:o[4]] * (GLA_DK ** -0.5)
        kb_ref[rows, :] = full[:, o[4]:o[5]]
        vb_ref[rows, :] = full[:, o[5]:o[6]].astype(vb_ref.dtype)
        gb_ref[rows, :] = full[:, o[6]:o[7]].astype(gb_ref.dtype)
        lr = full[:, EV_MAIN:EV_MAIN + LANES].astype(BF16)
        logits = jnp.dot(lr, wa_bf[...], preferred_element_type=F32)
        ldf_ref[rows, :] = _log_sigmoid(logits[:, 0:B_QK] + baf_ref[...]) * (1.0 / GLA_TAU)
        ldb_ref[rows, :] = _log_sigmoid(logits[:, B_QK:2 * B_QK] + bab_ref[...]) * (1.0 / GLA_TAU)

    pending = None
    for rows in _sub_tiles(x_ref.shape[0]):
        full = project(rows)
        if pending is not None:
            finish(*pending)
        pending = (rows, full)
    finish(*pending)


def _even_in(x, mods, norm_pre, norm_post, w_in_t, wa_f, wa_b, ba_f, ba_b, *, layer, e, group):
    n_tok, d = x.shape
    tm = WIDE_TILE
    ev_in = w_in_t.shape[1]
    rank = wa_f.shape[1]
    assert rank == GLA_RANK and ev_in == EV_MAIN + 2 * rank
    widths = (A_Q, A_KV, A_KV, B_QK, B_QK, B_V, B_V, B_QK, B_QK)
    dtypes = (BF16, F32, F32, F32, F32, BF16, BF16, F32, F32)
    kern = functools.partial(_evin_kernel, layer=layer, group=group)
    return pl.pallas_call(
        kern,
        grid=(n_tok // tm,),
        in_specs=[pl.BlockSpec((tm, d), lambda i: (i, 0))] + _cond_specs(mods, norm_pre, norm_post, layer) + [
            pl.BlockSpec((1, ev_in, d), lambda i: (e, 0, 0), pipeline_mode=pl.Buffered(1)),
            pl.BlockSpec((1, rank, B_QK), lambda i: (e, 0, 0)),
            pl.BlockSpec((1, rank, B_QK), lambda i: (e, 0, 0)),
            pl.BlockSpec((1, B_QK), lambda i: (e, 0)),
            pl.BlockSpec((1, B_QK), lambda i: (e, 0)),
        ],
        out_specs=[pl.BlockSpec((tm, w), lambda i: (i, 0)) for w in widths],
        out_shape=[jax.ShapeDtypeStruct((n_tok, w), t) for w, t in zip(widths, dtypes)],
        scratch_shapes=[pltpu.VMEM((EV_MAIN + LANES, d), BF16), pltpu.VMEM((LANES, 2 * B_QK), BF16)],
        compiler_params=pltpu.CompilerParams(
            dimension_semantics=("arbitrary",), vmem_limit_bytes=VMEM_LIMIT_BYTES),
        name="even_in",
    )(x, mods, norm_pre, norm_post, w_in_t, wa_f, wa_b, ba_f, ba_b)


def _lane_lt(shape, n):
    return lax.broadcasted_iota(jnp.int32, shape, len(shape) - 1) < n


def _dup_kv_head(x, g):
    sw = pltpu.roll(x, HEAD_DIM, 1)
    lo = _lane_lt(x.shape, HEAD_DIM)
    return jnp.where(lo, x, sw) if g == 0 else jnp.where(lo, sw, x)


def _dot_nt(a, b):
    return lax.dot_general(a, b, (((1,), (1,)), ((), ())), preferred_element_type=F32)


ATTN_GROUP = ATTN_HEADS // ATTN_KV_HEADS


def _sink_column(sink_ref, g, t):
    rows = lax.broadcasted_iota(jnp.int32, (ATTN_GROUP * t, 1), 0)
    col = jnp.full((ATTN_GROUP * t, 1), sink_ref[ATTN_GROUP * g], F32)
    for j in range(1, ATTN_GROUP):
        col = jnp.where(rows >= j * t, sink_ref[ATTN_GROUP * g + j], col)
    return col


def _attn_scores(q_pairs, keys):
    t = q_pairs[0].shape[0]
    lo = _lane_lt((t, LANES), HEAD_DIM)
    zero = jnp.zeros((t, LANES), BF16)
    q4 = jnp.concatenate([jnp.where(lo, q_pairs[0], zero), jnp.where(lo, zero, q_pairs[0]),
                          jnp.where(lo, q_pairs[1], zero), jnp.where(lo, zero, q_pairs[1])], axis=0)
    scores = []
    for k2, _, bias, feature_major in keys:
        s = jnp.dot(q4, k2, preferred_element_type=F32) if feature_major else _dot_nt(q4, k2)
        if bias is not None:
            s = s + bias[...]
        scores.append(s)
    return scores


def _attn_finish(scores, sink_col, keys):
    t = scores[0].shape[0] // ATTN_GROUP
    lo = _lane_lt((t, LANES), HEAD_DIM)
    mx = sink_col
    for s in scores:
        mx = jnp.maximum(mx, jnp.max(s, axis=-1, keepdims=True))
    den = jnp.exp(sink_col - mx)
    o = None
    for s, (_, v2, _, feature_major) in zip(scores, keys):
        p = jnp.exp(s - mx)
        den = den + jnp.sum(p, axis=-1, keepdims=True)
        pb = p.astype(BF16)
        pv = _dot_nt(pb, v2) if feature_major else jnp.dot(pb, v2, preferred_element_type=F32)
        o = pv if o is None else o + pv
    o = o / den
    return (jnp.where(lo, o[0:t], o[t:2 * t]), jnp.where(lo, o[2 * t:3 * t], o[3 * t:4 * t]))


def _attn_ctx_kernel(sink_ref, q_ref, k_ref, v_ref, o_ref, kt_ref, vt_ref):
    per_step, _, t = kt_ref.shape
    work = []
    for s in range(per_step):
        rows = slice(s * t, (s + 1) * t)
        k = k_ref[rows, :]
        v = v_ref[rows, :]
        kt_ref[s] = k.T
        vt_ref[s] = v.T
        for g in range(ATTN_KV_HEADS):
            keys = [(_dup_kv_head(k, g).astype(BF16), _dup_kv_head(v, g).astype(BF16), None, False)]
            cols = (2 * g, 2 * g + 1)
            scores = _attn_scores([q_ref[rows, m * LANES:(m + 1) * LANES] for m in cols], keys)
            work.append((rows, cols, scores, keys, g))
    for rows, cols, scores, keys, g in work:
        outs = _attn_finish(scores, _sink_column(sink_ref, g, t), keys)
        for m, o in zip(cols, outs):
            o_ref[rows, m * LANES:(m + 1) * LANES] = o.astype(o_ref.dtype)


def _attn_context(sink, qa, ka, va, *, n_seq, seq):
    per_step = ATTN_PROMPT_SEQS
    assert n_seq % per_step == 0
    n_seq, rows = n_seq // per_step, per_step * seq
    cache_spec = pl.BlockSpec((per_step, A_KV, seq), lambda b: (b, 0, 0))
    return pl.pallas_call(
        _attn_ctx_kernel,
        grid=(n_seq,),
        in_specs=[
            pl.BlockSpec(memory_space=pltpu.SMEM),
            pl.BlockSpec((rows, A_Q), lambda b: (b, 0)),
            pl.BlockSpec((rows, A_KV), lambda b: (b, 0)),
            pl.BlockSpec((rows, A_KV), lambda b: (b, 0)),
        ],
        out_specs=[pl.BlockSpec((rows, A_Q), lambda b: (b, 0)), cache_spec, cache_spec],
        out_shape=[jax.ShapeDtypeStruct((n_seq * rows, A_Q), BF16),
                   jax.ShapeDtypeStruct((n_seq * per_step, A_KV, seq), F32),
                   jax.ShapeDtypeStruct((n_seq * per_step, A_KV, seq), F32)],
        compiler_params=pltpu.CompilerParams(dimension_semantics=("parallel",)),
        name="attn_context",
    )(sink, qa, ka, va)


def _rope_tables(n_tok):
    n_rows = n_tok // GRID_W
    assert n_rows <= GRID_W
    shape = (GRID_W, LANES)
    pos = lax.broadcasted_iota(jnp.int32, shape, 0).astype(F32)
    lane = lax.broadcasted_iota(jnp.int32, shape, 1)
    half = HEAD_DIM // 2
    nf = half // 2
    within = lane & (HEAD_DIM - 1)
    is_col = within >= half
    second = (within & (half - 1)) >= nf
    f = (within & (nf - 1)).astype(F32)
    inv = jnp.exp(f * (-2.0 / half * math.log(ROPE_BASE)))
    ang = pos * inv
    cos_t = jnp.cos(ang)
    sin_t = jnp.sin(ang)
    sin_t = jnp.where(second, sin_t, -sin_t)

    def per_token(tab):
        return jnp.concatenate(
            [jnp.where(is_col, tab, jnp.broadcast_to(tab[r:r + 1, :], shape)) for r in range(n_rows)], axis=0)

    return per_token(cos_t), per_token(sin_t)


def _rope(x, cos, sin_signed):
    nf = HEAD_DIM // 4
    lane = lax.broadcasted_iota(jnp.int32, x.shape, 1)
    second = (lane & (2 * nf - 1)) >= nf
    up = pltpu.roll(x, nf, 1)
    dn = pltpu.roll(x, LANES - nf, 1)
    return x * cos + jnp.where(second, up, dn) * sin_signed


def _rope_bf16(x, cos, sin_signed):
    nf = HEAD_DIM // 4
    src = lax.broadcasted_iota(jnp.int32, (LANES, LANES), 0)
    lane = lax.broadcasted_iota(jnp.int32, (LANES, LANES), 1)
    second = (lane & (2 * nf - 1)) >= nf
    perm = jnp.where(src == jnp.where(second, lane - nf, lane + nf), 1.0, 0.0).astype(BF16)
    return x.astype(F32) * cos + jnp.dot(x, perm, preferred_element_type=F32) * sin_signed


def _attn_lat_kernel(sink_ref, q_ref, k_ref, v_ref, kc_ref, vc_ref, o_ref, qr_ref, kp_ref, vp_ref, bias_ref,
                     *, seq):
    per_step = kc_ref.shape[0]
    nb = seq // ATTN_BLOCK
    blk = ATTN_BLOCK
    cos, sin_signed = _rope_tables(seq)
    zpad = jnp.zeros((blk, LANES), BF16)
    ctx = []
    for s in range(per_step):
        rows = slice(s * seq, (s + 1) * seq)
        for m in range(A_Q // LANES):
            qr_ref[rows, m * LANES:(m + 1) * LANES] = _rope_bf16(
                q_ref[rows, m * LANES:(m + 1) * LANES], cos, sin_signed).astype(BF16)
        kr = _rope(k_ref[rows, :], cos, sin_signed)
        v = v_ref[rows, :]
        for g in range(ATTN_KV_HEADS):
            n = s * ATTN_KV_HEADS + g
            kp_ref[n, 0:blk, :] = zpad
            kp_ref[n, blk + seq:2 * blk + seq, :] = zpad
            vp_ref[n, 0:blk, :] = zpad
            vp_ref[n, blk + seq:2 * blk + seq, :] = zpad
            kp_ref[n, blk:blk + seq, :] = _dup_kv_head(kr, g).astype(BF16)
            vp_ref[n, blk:blk + seq, :] = _dup_kv_head(v, g).astype(BF16)
            hd = slice(g * HEAD_DIM, (g + 1) * HEAD_DIM)
            ctx.append((jnp.concatenate([kc_ref[s, hd, :]] * 2, axis=0).astype(BF16),
                        jnp.concatenate([vc_ref[s, hd, :]] * 2, axis=0).astype(BF16), None, True))
    sink_cols = [_sink_column(sink_ref, g, blk) for g in range(ATTN_KV_HEADS)]
    r = lax.broadcasted_iota(jnp.int32, (ATTN_GROUP * blk, 3 * blk), 0) & (blk - 1)
    c = lax.broadcasted_iota(jnp.int32, (ATTN_GROUP * blk, 3 * blk), 1)
    band = (c >= r) & (c <= r + 2 * WINDOW)
    bias_ref[0] = jnp.where(band & (c >= blk), 0.0, NEG_INF)
    bias_ref[1] = jnp.where(band, 0.0, NEG_INF)
    bias_ref[2] = jnp.where(band & (c < 2 * blk), 0.0, NEG_INF)

    def block_body(i, carry):
        row0 = pl.multiple_of(i * blk, blk)
        edge = jnp.where(i == 0, 0, jnp.where(i == nb - 1, 2, 1))
        work = []
        for s in range(per_step):
            q0 = pl.multiple_of(s * seq + i * blk, blk)
            for g in range(ATTN_KV_HEADS):
                n = s * ATTN_KV_HEADS + g
                cols = (2 * g, 2 * g + 1)
                window = (kp_ref[n, pl.ds(row0, 3 * blk), :], vp_ref[n, pl.ds(row0, 3 * blk), :],
                          bias_ref.at[edge], False)
                keys = [window, ctx[n]]
                scores = _attn_scores([qr_ref[pl.ds(q0, blk), m * LANES:(m + 1) * LANES] for m in cols], keys)
                work.append((q0, cols, scores, keys, g))
        for q0, cols, scores, keys, g in work:
            outs = _attn_finish(scores, sink_cols[g], keys)
            for m, o in zip(cols, outs):
                o_ref[pl.ds(q0, blk), m * LANES:(m + 1) * LANES] = o.astype(o_ref.dtype)
        return carry

    lax.fori_loop(0, nb, block_body, 0)


def _attn_latent(sink, qa, ka, va, k_ctx, v_ctx, *, n_seq, seq, row_block0):
    past = k_ctx.shape[2]
    per_step = ATTN_LATENT_SEQS
    assert n_seq % per_step == 0 and row_block0 % per_step == 0 and seq // ATTN_BLOCK >= 2
    rows = per_step * seq
    first = row_block0 // per_step
    kern = functools.partial(_attn_lat_kernel, seq=seq)
    return pl.pallas_call(
        kern,
        grid=(n_seq // per_step,),
        in_specs=[
            pl.BlockSpec(memory_space=pltpu.SMEM),
            pl.BlockSpec((rows, A_Q), lambda b: (first + b, 0)),
            pl.BlockSpec((rows, A_KV), lambda b: (first + b, 0)),
            pl.BlockSpec((rows, A_KV), lambda b: (first + b, 0)),
            pl.BlockSpec((per_step, A_KV, past), lambda b: (b, 0, 0)),
            pl.BlockSpec((per_step, A_KV, past), lambda b: (b, 0, 0)),
        ],
        out_specs=pl.BlockSpec((rows, A_Q), lambda b: (b, 0)),
        out_shape=jax.ShapeDtypeStruct((n_seq * seq, A_Q), BF16),
        scratch_shapes=[
            pltpu.VMEM((rows, A_Q), BF16),
            pltpu.VMEM((per_step * ATTN_KV_HEADS, seq + 2 * ATTN_BLOCK, LANES), BF16),
            pltpu.VMEM((per_step * ATTN_KV_HEADS, seq + 2 * ATTN_BLOCK, LANES), BF16),
            pltpu.VMEM((3, ATTN_GROUP * ATTN_BLOCK, 3 * ATTN_BLOCK), F32),
        ],
        compiler_params=pltpu.CompilerParams(dimension_semantics=("parallel",)),
        name="attn_latent",
    )(sink, qa, ka, va, k_ctx, v_ctx)


def _same_block(a, b, size):
    return (a & -size) == (b & -size)


def _split3(x):
    hi = x.astype(BF16)
    r1 = x - hi.astype(F32)
    mid = r1.astype(BF16)
    lo = (r1 - mid.astype(F32)).astype(BF16)
    return hi, mid, lo


def _interleave(generators):
    live = list(generators)
    while live:
        for gen in list(live):
            try:
                next(gen)
            except StopIteration:
                live.remove(gen)


def _gla_tile(q, k, v, ld, st_ref, sel_ref, emit, *, rev):
    t = GLA_TILE
    n_pair = GLA_HEADS // 2
    row = lax.broadcasted_iota(jnp.int32, (t, t), 0)
    col = lax.broadcasted_iota(jnp.int32, (t, t), 1)
    tri = (row <= col) if rev else (row >= col)
    tri = jnp.where(tri, 1.0, 0.0).astype(BF16)
    hi, mid, lo = _split3(ld * math.log2(math.e))
    b = (jnp.dot(tri, hi, preferred_element_type=F32) + jnp.dot(tri, mid, preferred_element_type=F32)
         + jnp.dot(tri, lo, preferred_element_type=F32))
    yield
    b_tot = b[0:1, :] if rev else b[t - 1:t, :]
    qs = (q * jnp.exp2(b)).astype(BF16)
    kd = (k * jnp.exp2(b_tot - b)).astype(BF16)

    rowv = lax.broadcasted_iota(jnp.int32, (t, B_QK), 0)
    levels = []
    half = GLA_SUB
    while 2 * half <= t:
        blk = 2 * half
        pieces = []
        for bs in range(0, t, blk):
            r = bs + half if rev else bs + half - 1
            pieces.append(jnp.broadcast_to(b[r:r + 1, :], (blk, B_QK)))
        ref = jnp.concatenate(pieces, axis=0) if len(pieces) > 1 else pieces[0]
        second = (rowv & (blk - 1)) >= half
        is_q = ~second if rev else second
        ql = (q * jnp.exp2(jnp.where(is_q, b - ref, -jnp.inf))).astype(BF16)
        kl = (k * jnp.exp2(jnp.where(is_q, -jnp.inf, ref - b))).astype(BF16)
        levels.append((blk, ql, kl))
        half = blk
    yield

    rr = lax.broadcasted_iota(jnp.int32, (t, 2 * t), 0)
    ss = lax.broadcasted_iota(jnp.int32, (t, 2 * t), 1) & (t - 1)
    sub_rows = lax.broadcasted_iota(jnp.int32, (GLA_SUB, LANES), 0)
    lane_lo = _lane_lt((t, LANES), GLA_DK)
    lane_lo2 = _lane_lt((2 * t, LANES), GLA_DK)
    par_rows = lax.broadcasted_iota(jnp.int32, (2 * t, LANES), 0) < t
    head_sel = par_rows == lane_lo2

    outs = []
    for p in range(n_pair):
        pl_ = slice(p * LANES, (p + 1) * LANES)
        qp, kp, bp = q[:, pl_], k[:, pl_], b[:, pl_]
        rows = []
        for sb in range(t // GLA_SUB):
            r0 = sb * GLA_SUB
            q_sub = qp[r0:r0 + GLA_SUB, :]
            b_sub = bp[r0:r0 + GLA_SUB, :]
            cols = []
            for j in range(GLA_SUB):
                kj = kp[r0 + j:r0 + j + 1, :]
                bj = bp[r0 + j:r0 + j + 1, :]
                valid = (sub_rows <= j) if rev else (sub_rows >= j)
                e = jnp.exp2(jnp.where(valid, b_sub - bj, -jnp.inf))
                cols.append(q_sub * kj * e)
            rows.append(jnp.concatenate(cols, axis=1))
        e_p = jnp.concatenate(rows, axis=0).astype(BF16)
        yield
        sub_scores = jnp.dot(e_p, sel_ref[...], preferred_element_type=F32)
        scores = jnp.zeros_like(sub_scores)
        for blk, ql, kl in reversed(levels):
            klp = kl[:, pl_]
            kstack = jnp.concatenate([klp, klp], axis=0)
            kstack = jnp.where(head_sel, kstack, jnp.zeros_like(kstack))
            s_l = _dot_nt(ql[:, pl_], kstack)
            scores = jnp.where(_same_block(rr, ss, blk), s_l, scores)
        scores = jnp.where(_same_block(rr, ss, GLA_SUB), sub_scores, scores)
        yield
        vp = v[:, p * 2 * GLA_DV:(p + 1) * 2 * GLA_DV]
        v_lo = _lane_lt(vp.shape, GLA_DV)
        vbd = jnp.concatenate([jnp.where(v_lo, vp, 0.0), jnp.where(v_lo, 0.0, vp)], axis=0).astype(BF16)
        st = st_ref[p]
        o_p = (jnp.dot(scores.astype(BF16), vbd, preferred_element_type=F32)
               + _dot_nt(qs[:, pl_], st.astype(BF16)))
        outs.append(o_p)
        upd = jnp.dot(vp.T.astype(BF16), kd[:, pl_], preferred_element_type=F32)
        st_rows_lo = lax.broadcasted_iota(jnp.int32, upd.shape, 0) < GLA_DV
        bd = st_rows_lo == _lane_lt(upd.shape, GLA_DK)
        st_ref[p] = st * jnp.exp2(b_tot[:, pl_]) + jnp.where(bd, upd, 0.0)
        yield
    emit(jnp.concatenate(outs, axis=1))


def _gla_kernel(*refs, seq, has_init, want_state, per_step):
    q_ref, k_ref, v_ref, ldf_ref, ldb_ref = refs[:5]
    pos = 5
    if has_init:
        s0f_ref, s0b_ref = refs[pos:pos + 2]
        pos += 2
    o_ref = refs[pos]
    pos += 1
    if want_state:
        sf_ref, sb_ref = refs[pos:pos + 2]
        pos += 2
    st_ref, sel_ref, acc_ref = refs[pos:pos + 3]
    t = GLA_TILE
    nt = seq // t
    n_pair = GLA_HEADS // 2

    kk = lax.broadcasted_iota(jnp.int32, (GLA_SUB * LANES, 2 * t), 0)
    nn = lax.broadcasted_iota(jnp.int32, (GLA_SUB * LANES, 2 * t), 1)
    sel = ((nn & (GLA_SUB - 1)) == (kk >> LANES_LOG2)) & (((kk & (LANES - 1)) >= GLA_DK) == (nn >= t))
    sel_ref[...] = jnp.where(sel, 1.0, 0.0).astype(BF16)

    def load_state(st, s0_ref, s):
        for p in range(n_pair):
            if s0_ref is None:
                st[p] = jnp.zeros(st.shape[1:], F32)
            else:
                x = jnp.concatenate([s0_ref[s, 2 * p], s0_ref[s, 2 * p + 1]], axis=0)
                xt = x.T
                lo = _lane_lt(xt.shape, GLA_DK)
                st[p] = jnp.concatenate([jnp.where(lo, xt, 0.0), jnp.where(lo, 0.0, xt)], axis=0)

    def store_state(st, out_ref, s):
        for p in range(n_pair):
            z = st[p]
            lo = _lane_lt((GLA_DV, LANES), GLA_DK)
            z = jnp.where(lo, z[0:GLA_DV, :], z[GLA_DV:2 * GLA_DV, :]).T
            out_ref[s, 2 * p] = z[0:GLA_DK, :]
            out_ref[s, 2 * p + 1] = z[GLA_DK:2 * GLA_DK, :]

    walks = [(s, rev, ldb_ref if rev else ldf_ref, st_ref.at[2 * s + int(rev)])
             for s in range(per_step) for rev in (False, True)]
    for s, rev, _, st in walks:
        load_state(st, (s0b_ref if rev else s0f_ref) if has_init else None, s)
    acc_ref[...] = jnp.zeros(acc_ref.shape, F32)

    def tile_body(ti, carry):
        def walk(s, rev, ld_ref, st):
            tile = (nt - 1 - ti) if rev else ti
            r0 = pl.multiple_of(s * seq + tile * t, t)
            rows = pl.ds(r0, t)

            def emit(o):
                acc_ref[rows, :] = acc_ref[rows, :] + o

            return _gla_tile(q_ref[rows, :], k_ref[rows, :], v_ref[rows, :].astype(F32), ld_ref[rows, :],
                             st, sel_ref, emit, rev=rev)

        _interleave([walk(*w) for w in walks])
        return carry

    lax.fori_loop(0, nt, tile_body, 0)
    o_ref[...] = acc_ref[...].astype(o_ref.dtype)
    if want_state:
        for s, rev, _, st in walks:
            store_state(st, sb_ref if rev else sf_ref, s)


def _gla(qb, kb, vb, ldf, ldb, s0f, s0b, *, n_seq, seq, row_block0, want_state, per_step):
    has_init = s0f is not None
    assert n_seq % per_step == 0 and row_block0 % per_step == 0
    kern = functools.partial(_gla_kernel, seq=seq, has_init=has_init, want_state=want_state,
                             per_step=per_step)
    n_seq, seq_rows, row_block0 = n_seq // per_step, per_step * seq, row_block0 // per_step
    tok = lambda w: pl.BlockSpec((seq_rows, w), lambda b: (row_block0 + b, 0))
    in_specs = [tok(B_QK), tok(B_QK), tok(B_V), tok(B_QK), tok(B_QK)]
    args = [qb, kb, vb, ldf, ldb]
    state_spec = pl.BlockSpec((per_step, GLA_HEADS, GLA_DK, GLA_DV), lambda b: (b, 0, 0, 0))
    if has_init:
        in_specs += [state_spec, state_spec]
        args += [s0f, s0b]
    out_specs = [pl.BlockSpec((seq_rows, B_V), lambda b: (b, 0))]
    out_shape = [jax.ShapeDtypeStruct((n_seq * seq_rows, B_V), BF16)]
    if want_state:
        out_specs += [state_spec, state_spec]
        out_shape += [jax.ShapeDtypeStruct((n_seq * per_step, GLA_HEADS, GLA_DK, GLA_DV), F32)] * 2
    return pl.pallas_call(
        kern,
        grid=(n_seq,),
        in_specs=in_specs,
        out_specs=out_specs,
        out_shape=out_shape,
        scratch_shapes=[
            pltpu.VMEM((2 * per_step, GLA_HEADS // 2, 2 * GLA_DV, LANES), F32),
            pltpu.VMEM((GLA_SUB * LANES, 2 * GLA_TILE), BF16),
            pltpu.VMEM((seq_rows, B_V), F32),
        ],
        compiler_params=pltpu.CompilerParams(
            dimension_semantics=("parallel",), vmem_limit_bytes=VMEM_LIMIT_BYTES),
        name="gla_state" if want_state else "gla_latent",
    )(*args)


def _evout_kernel(x_ref, mod_ref, gpre_ref, gpost_ref, attn_p_ref, attn_s_ref, gla_p_ref, gla_s_ref, gb_ref,
                  gn_ref, w_ref, o_ref, w_bf, *, np_tiles, layer, group):
    @pl.when(pl.program_id(0) == 0)
    def _cast_weights():
        w_bf[...] = w_ref[0].astype(BF16)

    is_prompt = pl.program_id(0) < np_tiles
    mods = _mod_rows(mod_ref, group(pl.program_id(0)))
    g_post = _norm_row(gpost_ref, layer, 1)
    for rows in _sub_tiles(x_ref.shape[0]):
        attn = _pick((attn_p_ref, attn_s_ref), is_prompt, rows)
        gla = _pick((gla_p_ref, gla_s_ref), is_prompt, rows)
        parts = [attn]
        for hd in range(GLA_HEADS):
            sl = slice(hd * GLA_DV, (hd + 1) * GLA_DV)
            g = _rms(gla[:, sl].astype(F32), gn_ref[:, sl]) * _silu(gb_ref[rows, sl].astype(F32))
            parts.append(g.astype(BF16))
        y = jnp.dot(jnp.concatenate(parts, axis=1), w_bf[...], preferred_element_type=F32)
        o_ref[rows, :] = _sub_out(x_ref[rows, :], y, mods, g_post, 1, 1.0)


def _even_out(x, mods, norm_pre, norm_post, attn_ps, gla_ps, gb, gla_norm, w_out, *, layer, e, group,
              np_tiles):
    n_tok, d = x.shape
    tm = WIDE_TILE
    ev_out = w_out.shape[1]
    kern = functools.partial(_evout_kernel, np_tiles=np_tiles, layer=layer, group=group)
    return pl.pallas_call(
        kern,
        grid=(n_tok // tm,),
        in_specs=[pl.BlockSpec((tm, d), lambda i: (i, 0))] + _cond_specs(mods, norm_pre, norm_post, layer) + [
            *_tile_specs(tm, A_Q, np_tiles),
            *_tile_specs(tm, B_V, np_tiles),
            pl.BlockSpec((tm, B_V), lambda i: (i, 0)),
            pl.BlockSpec((1, B_V), lambda i: (e, 0)),
            pl.BlockSpec((1, ev_out, d), lambda i: (e, 0, 0), pipeline_mode=pl.Buffered(1)),
        ],
        out_specs=pl.BlockSpec((tm, d), lambda i: (i, 0)),
        out_shape=jax.ShapeDtypeStruct((n_tok, d), F32),
        scratch_shapes=[pltpu.VMEM((ev_out, d), BF16)],
        compiler_params=pltpu.CompilerParams(
            dimension_semantics=("arbitrary",), vmem_limit_bytes=VMEM_LIMIT_BYTES),
        name="even_out",
    )(x, mods, norm_pre, norm_post, *attn_ps, *gla_ps, gb, gla_norm, w_out)


def _gelu(x):
    return 0.5 * x * (1.0 + lax.erf(x * (2.0 ** -0.5)))


def _cm_kernel(x_ref, mod_ref, gpre_ref, gpost_ref, win_ref, vg_ref, vb_ref, ws_ref, bs_ref, wout_ref,
               o_ref, win_bf, wout_bf, *, layer, group):
    @pl.when(pl.program_id(0) == 0)
    def _cast_weights():
        win_bf[...] = win_ref[0].astype(BF16)
        wout_bf[...] = wout_ref[0].astype(BF16)

    width = wout_bf.shape[0]
    gw = width // CMLP_GROUPS
    ws = [ws_ref[0, g].astype(BF16) for g in range(CMLP_GROUPS)]
    mods = _mod_rows(mod_ref, group(pl.program_id(0)))
    g_pre = _norm_row(gpre_ref, layer, 1)
    g_post = _norm_row(gpost_ref, layer, 1)
    eye = (lax.broadcasted_iota(jnp.int32, (CHUNK, CHUNK), 0)
           == lax.broadcasted_iota(jnp.int32, (CHUNK, CHUNK), 1))
    bias_cols = [jnp.sum(jnp.where(eye, jnp.broadcast_to(bs_ref[0, g:g + 1, :], (CHUNK, CHUNK)), 0.0),
                         axis=1, keepdims=True) for g in range(CMLP_GROUPS)]

    def phase_in(rows):
        x = x_ref[rows, :]
        h = _sub_in(x, mods, g_pre, 1).astype(BF16)
        v = _gelu(jnp.dot(h, win_bf[:, width:2 * width], preferred_element_type=F32))
        u = _gelu(jnp.dot(h, win_bf[:, 0:width], preferred_element_type=F32))
        return x, u, v

    def phase_mix(x, u, v):
        mu = jnp.mean(v, axis=-1, keepdims=True)
        vc = v - mu
        var = jnp.mean(vc * vc, axis=-1, keepdims=True)
        vn = (vc * lax.rsqrt(var + EPS) * vg_ref[...] + vb_ref[...]).astype(BF16)
        chunk_rows = []
        for c in range(x.shape[0] // CHUNK):
            cols = []
            for g in range(CMLP_GROUPS):
                blk = vn[c * CHUNK:(c + 1) * CHUNK, g * gw:(g + 1) * gw]
                cols.append(jnp.dot(ws[g], blk, preferred_element_type=F32) + bias_cols[g])
            chunk_rows.append(jnp.concatenate(cols, axis=1))
        mixed = jnp.concatenate(chunk_rows, axis=0)
        return x, (u * mixed).astype(BF16)

    def phase_out(rows, x, m):
        y = jnp.dot(m, wout_bf[...], preferred_element_type=F32)
        o_ref[rows, :] = _sub_out(x, y, mods, g_post, 1, 1.0)

    tiles = _sub_tiles(x_ref.shape[0])
    pending = None
    for rows in tiles:
        cur = phase_in(rows)
        if pending is not None:
            phase_out(pending[0], *phase_mix(*pending[1]))
        pending = (rows, cur)
    phase_out(pending[0], *phase_mix(*pending[1]))


def _chunk_mlp(x, mods, norm_pre, norm_post, w_in, v_gain, v_bias, w_s, b_s, w_out, *, layer, o, group):
    n_tok, d = x.shape
    tm = WIDE_TILE
    width = w_out.shape[1]
    kern = functools.partial(_cm_kernel, layer=layer, group=group)
    return pl.pallas_call(
        kern,
        grid=(n_tok // tm,),
        in_specs=[pl.BlockSpec((tm, d), lambda i: (i, 0))] + _cond_specs(mods, norm_pre, norm_post, layer) + [
            pl.BlockSpec((1, d, 2 * width), lambda i: (o, 0, 0), pipeline_mode=pl.Buffered(1)),
            pl.BlockSpec((1, width), lambda i: (o, 0)),
            pl.BlockSpec((1, width), lambda i: (o, 0)),
            pl.BlockSpec((1, CMLP_GROUPS, CHUNK, CHUNK), lambda i: (o, 0, 0, 0)),
            pl.BlockSpec((1, CMLP_GROUPS, CHUNK), lambda i: (o, 0, 0)),
            pl.BlockSpec((1, width, d), lambda i: (o, 0, 0), pipeline_mode=pl.Buffered(1)),
        ],
        out_specs=pl.BlockSpec((tm, d), lambda i: (i, 0)),
        out_shape=jax.ShapeDtypeStruct((n_tok, d), F32),
        scratch_shapes=[pltpu.VMEM((d, 2 * width), BF16), pltpu.VMEM((width, d), BF16)],
        compiler_params=pltpu.CompilerParams(
            dimension_semantics=("arbitrary",), vmem_limit_bytes=VMEM_LIMIT_BYTES),
        name="chunk_mlp",
    )(x, mods, norm_pre, norm_post, w_in, v_gain, v_bias, w_s, b_s, w_out)


def kernel(x_prompt, x_sample, cache_k, cache_v, state_gla_fwd, state_gla_bwd, c, c_ctx, w_mod, b_mod, norm_pre, norm_post, ffn_w_gate, ffn_w_up, ffn_w_down, ev_w_in, ev_w_out, ev_sink, gla_wa_f, gla_ba_f, gla_wa_b, gla_ba_b, gla_norm, cm_w_in, cm_v_gain, cm_v_bias, cm_w_s, cm_b_s, cm_w_out):
    batch, seq, d = x_prompt.shape
    dec_batch, dec_seq, _ = x_sample.shape
    depth = w_mod.shape[0]
    n_prompt_tok = batch * seq
    n_sample_tok = dec_batch * dec_seq
    assert n_prompt_tok % dec_seq == 0 and dec_seq % TOKEN_TILE == 0
    group = _group_index_map(TOKEN_TILE, n_prompt_tok, dec_seq)
    np_tiles = n_prompt_tok // TOKEN_TILE
    ffn = functools.partial(_ffn, group=group, np_tiles=np_tiles)
    assert dec_seq % WIDE_TILE == 0
    group_w = _group_index_map(WIDE_TILE, n_prompt_tok, dec_seq)
    np_tiles_w = n_prompt_tok // WIDE_TILE

    xs = (x_prompt.reshape(n_prompt_tok, d), x_sample.reshape(n_sample_tok, d))
    ev_w_in_t = jnp.swapaxes(ev_w_in, 1, 2)

    mods = _adaln_mods(c_ctx, c, w_mod, b_mod)
    gains_pre = norm_pre.reshape(depth * N_SUB, d)
    gains_post = norm_post.reshape(depth * N_SUB, d)
    cond = (mods, gains_pre, gains_post)

    new_k, new_v, new_sf, new_sb = [], [], [], []
    for layer in range(depth):
        (x,) = ffn(xs, *cond, ffn_w_gate, ffn_w_up, ffn_w_down,
                   sub=0, layer=layer, half=0, split_out=False)
        if layer % 2 == 0:
            e = layer // 2
            qa, ka, va, qb, kb, vb, gb, ldf, ldb = _even_in(
                x, *cond, ev_w_in_t, gla_wa_f, gla_wa_b, gla_ba_f, gla_ba_b,
                layer=layer, e=e, group=group_w)
            sink = ev_sink[e]
            attn_p, k_t, v_t = _attn_context(sink, qa, ka, va, n_seq=batch, seq=seq)
            k_ctx = jnp.transpose(cache_k[:, e], (0, 2, 3, 1)).reshape(dec_batch, A_KV, -1)
            v_ctx = jnp.transpose(cache_v[:, e], (0, 2, 3, 1)).reshape(dec_batch, A_KV, -1)
            attn_s = _attn_latent(sink, qa, ka, va, k_ctx, v_ctx, n_seq=dec_batch, seq=dec_seq,
                                  row_block0=n_prompt_tok // dec_seq)
            gla_p, s_f, s_b = _gla(qb, kb, vb, ldf, ldb, None, None, n_seq=batch, seq=seq,
                                   row_block0=0, want_state=True, per_step=GLA_PROMPT_SEQS)
            (gla_s,) = _gla(qb, kb, vb, ldf, ldb, state_gla_fwd[:, e], state_gla_bwd[:, e],
                            n_seq=dec_batch, seq=dec_seq, row_block0=n_prompt_tok // dec_seq,
                            want_state=False, per_step=GLA_LATENT_SEQS)
            x = _even_out(x, *cond, (attn_p, attn_s), (gla_p, gla_s), gb, gla_norm, ev_w_out,
                          layer=layer, e=e, group=group_w, np_tiles=np_tiles_w)
            for cache_t, dst in ((k_t, new_k), (v_t, new_v)):
                dst.append(jnp.transpose(cache_t.reshape(batch, ATTN_KV_HEADS, HEAD_DIM, seq), (0, 3, 1, 2)))
            new_sf.append(s_f)
            new_sb.append(s_b)
        else:
            o = layer // 2
            x = _chunk_mlp(x, *cond, cm_w_in, cm_v_gain, cm_v_bias, cm_w_s, cm_b_s, cm_w_out,
                           layer=layer, o=o, group=group_w)
        xs = ffn((x,), *cond, ffn_w_gate, ffn_w_up, ffn_w_down,
                 sub=2, layer=layer, half=1, split_out=(layer == depth - 1))

    y_prompt = xs[0].reshape(batch, seq, d)
    y_sample = xs[1].reshape(dec_batch, dec_seq, d)
    return (y_prompt, y_sample, jnp.stack(new_k, axis=1), jnp.stack(new_v, axis=1),
            jnp.stack(new_sf, axis=1), jnp.stack(new_sb, axis=1))
```

```python
import functools
import math

import jax
import jax.numpy as jnp
from jax import lax
from jax.experimental import pallas as pl
from jax.experimental.pallas import tpu as pltpu

F32 = jnp.float32
BF16 = jnp.bfloat16

EPS = 1e-6
NEG_INF = -1e30
FFN_RESIDUAL = 0.5
N_SUB = 3

ATTN_HEADS = 8
ATTN_KV_HEADS = 2
HEAD_DIM = 64
ATTN_BLOCK = 128
WINDOW = 128
GRID_W = 64
GRID_W_LOG2 = 6
ROPE_BASE = 10000.0
GLA_HEADS = 4
GLA_DK = 64
GLA_DV = 128
GLA_RANK = 16
GLA_TAU = 16.0
CHUNK = 128
CMLP_GROUPS = 4

A_Q = ATTN_HEADS * HEAD_DIM
A_KV = ATTN_KV_HEADS * HEAD_DIM
B_QK = GLA_HEADS * GLA_DK
B_V = GLA_HEADS * GLA_DV
EV_MAIN = A_Q + 2 * A_KV + 2 * B_QK + 2 * B_V
EV_OFFS = (0, A_Q, A_Q + A_KV, A_Q + 2 * A_KV, A_Q + 2 * A_KV + B_QK,
           A_Q + 2 * A_KV + 2 * B_QK, A_Q + 2 * A_KV + 2 * B_QK + B_V, EV_MAIN)

LANES = 128
LANES_LOG2 = 7
SUBLANES = 8
VMEM_LIMIT_BYTES = 56 * 1024 * 1024

TOKEN_TILE = 512
WIDE_TILE = 1024
SUB_ROWS = 512
ADALN_BUFFERS = 4
FFN_CHUNK = 256
FFN_STAGE_SLOTS = 3
GLA_TILE = 128
GLA_SUB = 8
GLA_PROMPT_SEQS = 4
GLA_LATENT_SEQS = 2
ATTN_PROMPT_SEQS = 2
ATTN_LATENT_SEQS = 2


def _group_index_map(tile_rows, n_prompt_tok, seq_tok):
    def group(i):
        return jnp.maximum(i * tile_rows - (n_prompt_tok - seq_tok), 0) // seq_tok
    return group


def _sub_tiles(n_rows):
    assert n_rows % SUB_ROWS == 0
    return [slice(r, r + SUB_ROWS) for r in range(0, n_rows, SUB_ROWS)]


def _rms(x, g):
    ms = jnp.mean(x * x, axis=-1, keepdims=True)
    return x * lax.rsqrt(ms + EPS) * g


def _mod_rows(mod_ref, g):
    return [mod_ref[k, pl.ds(g, 1), :] for k in range(3 * N_SUB)]


def _sub_in(x, mods, g_pre, sub):
    shift, scale = mods[3 * sub + 0], mods[3 * sub + 1]
    return _rms(x, g_pre * (1.0 + scale)) + shift


def _sub_out(x, y, mods, g_post, sub, coef):
    return x + _rms(y, (coef * mods[3 * sub + 2]) * g_post)


def _norm_row(ref, layer, sub):
    r = layer * N_SUB + sub
    return ref[r:r + 1, :]


def _silu(x):
    return x * jax.nn.sigmoid(x)


def _mod_kernel(cctx_ref, c_ref, w_hbm, b_ref, o_ref, buf, sem):
    depth, n_kinds, rows, d = o_ref.shape
    r = lax.broadcasted_iota(jnp.int32, (rows, d), 0)
    cond = jnp.where(r == 0, jnp.broadcast_to(cctx_ref[...], (rows, d)), 0.0)
    for j in range(c_ref.shape[0]):
        cond = jnp.where(r == 1 + j, jnp.broadcast_to(c_ref[j:j + 1, :], (rows, d)), cond)
    s = _silu(cond).astype(BF16)
    chunks = [(l, q) for l in range(depth) for q in range(n_kinds)]
    n_buf = buf.shape[0]

    def copy(i):
        l, q = chunks[i]
        return pltpu.make_async_copy(w_hbm.at[l, :, pl.ds(q * d, d)], buf.at[i % n_buf], sem.at[i % n_buf])

    for i in range(min(n_buf, len(chunks))):
        copy(i).start()
    for i, (l, q) in enumerate(chunks):
        copy(i).wait()
        res = jnp.dot(s, buf[i % n_buf].astype(BF16), preferred_element_type=F32)
        o_ref[l, q] = res + b_ref[l:l + 1, q * d:(q + 1) * d]
        if i + n_buf < len(chunks):
            copy(i + n_buf).start()


def _adaln_mods(c_ctx, c, w_mod, b_mod):
    depth, d, n = w_mod.shape
    assert n % d == 0
    rows = -(-(1 + c.shape[0]) // SUBLANES) * SUBLANES
    return pl.pallas_call(
        _mod_kernel,
        grid=(1,),
        in_specs=[
            pl.BlockSpec((1, d), lambda i: (0, 0)),
            pl.BlockSpec(c.shape, lambda i: (0, 0)),
            pl.BlockSpec(memory_space=pl.ANY),
            pl.BlockSpec((depth, n), lambda i: (0, 0)),
        ],
        out_specs=pl.BlockSpec((depth, n // d, rows, d), lambda i: (0, 0, 0, 0)),
        out_shape=jax.ShapeDtypeStruct((depth, n // d, rows, d), F32),
        scratch_shapes=[pltpu.VMEM((ADALN_BUFFERS, d, d), F32), pltpu.SemaphoreType.DMA((ADALN_BUFFERS,))],
        compiler_params=pltpu.CompilerParams(
            dimension_semantics=("arbitrary",), vmem_limit_bytes=VMEM_LIMIT_BYTES),
        name="adaln_mods",
    )(c_ctx.reshape(1, d), c, w_mod, b_mod)


def _tile_specs(tm, width, np_tiles):
    return [pl.BlockSpec((tm, width), lambda i: (jnp.minimum(i, np_tiles - 1), 0)),
            pl.BlockSpec((tm, width), lambda i: (jnp.maximum(i - np_tiles, 0), 0))]


def _pick(refs, is_prompt, rows=slice(None)):
    if len(refs) == 1:
        return refs[0][rows, :]
    return jnp.where(is_prompt, refs[0][rows, :], refs[1][rows, :])


def _cond_specs(mods, norm_pre, norm_post, layer):
    return [pl.BlockSpec((None,) + mods.shape[1:], lambda i: (layer, 0, 0, 0)),
            pl.BlockSpec(norm_pre.shape, lambda i: (0, 0)),
            pl.BlockSpec(norm_post.shape, lambda i: (0, 0))]


def _ffn_kernel(*refs, sub, layer, half, n_chunks, n_x, n_o, np_tiles, group):
    x_refs = refs[:n_x]
    mod_ref, gpre_ref, gpost_ref, wg_hbm, wu_hbm, wd_hbm = refs[n_x:n_x + 6]
    o_refs = refs[n_x + 6:n_x + 6 + n_o]
    wg_bf, wu_bf, wd_bf, st_g, st_u, st_d, sem = refs[n_x + 6 + n_o:]
    fc = FFN_CHUNK
    step = pl.program_id(0)
    mods = _mod_rows(mod_ref, group(step))
    g_pre = _norm_row(gpre_ref, layer, sub)
    g_post = _norm_row(gpost_ref, layer, sub)

    def chunk_copies(c, slot):
        cols = pl.ds(c * fc, fc)
        return (
            pltpu.make_async_copy(wg_hbm.at[layer, half, :, cols], st_g.at[slot], sem.at[0, slot]),
            pltpu.make_async_copy(wu_hbm.at[layer, half, :, cols], st_u.at[slot], sem.at[1, slot]),
            pltpu.make_async_copy(wd_hbm.at[layer, half, cols, :], st_d.at[slot], sem.at[2, slot]),
        )

    def run(stage_weights, prompt):
        n_slots = st_g.shape[0]
        if stage_weights:
            for c in range(min(n_slots - 1, n_chunks)):
                for cp in chunk_copies(c, c):
                    cp.start()
        x = _pick(x_refs, step < np_tiles)
        h = _sub_in(x, mods, g_pre, sub).astype(BF16)
        y = jnp.zeros(x.shape, F32)
        for c in range(n_chunks):
            sl = slice(c * fc, (c + 1) * fc)
            if stage_weights:
                slot = c % n_slots
                ahead = c + n_slots - 1
                if ahead < n_chunks:
                    for cp in chunk_copies(ahead, ahead % n_slots):
                        cp.start()
                for cp in chunk_copies(c, slot):
                    cp.wait()
                wg_bf[:, sl] = st_g[slot].astype(BF16)
                wu_bf[:, sl] = st_u[slot].astype(BF16)
                wd_bf[sl, :] = st_d[slot].astype(BF16)
            g = jnp.dot(h, wg_bf[:, sl], preferred_element_type=F32)
            u = jnp.dot(h, wu_bf[:, sl], preferred_element_type=F32)
            a = (_silu(g) * u).astype(BF16)
            y = y + jnp.dot(a, wd_bf[sl, :], preferred_element_type=F32)
        o_refs[0 if (prompt or n_o == 1) else 1][...] = _sub_out(x, y, mods, g_post, sub, FFN_RESIDUAL)

    @pl.when(step == 0)
    def _first_tile():
        run(True, True)

    if n_o == 1:
        @pl.when(step > 0)
        def _other_tiles():
            run(False, True)
    else:
        @pl.when(jnp.logical_and(step > 0, step < np_tiles))
        def _prompt_tiles():
            run(False, True)

        @pl.when(step >= np_tiles)
        def _latent_tiles():
            run(False, False)


def _ffn(xs, mods, norm_pre, norm_post, wg, wu, wd, *, sub, layer, half, group, np_tiles, split_out):
    d = xs[0].shape[-1]
    n_tok = sum(x.shape[0] for x in xs)
    d_ff = wg.shape[-1]
    tm = TOKEN_TILE
    fc = FFN_CHUNK
    n_chunks = d_ff // fc
    assert n_chunks * fc == d_ff and n_tok % tm == 0
    n_x, n_o = len(xs), (2 if split_out else 1)
    kern = functools.partial(_ffn_kernel, sub=sub, layer=layer, half=half, n_chunks=n_chunks,
                             n_x=n_x, n_o=n_o, np_tiles=np_tiles, group=group)
    whole = pl.BlockSpec((tm, d), lambda i: (i, 0))
    x_specs = _tile_specs(tm, d, np_tiles) if n_x == 2 else [whole]
    n_p = np_tiles * tm
    if split_out:
        out_specs = _tile_specs(tm, d, np_tiles)
        out_shape = [jax.ShapeDtypeStruct((n_p, d), F32), jax.ShapeDtypeStruct((n_tok - n_p, d), F32)]
    else:
        out_specs = [whole]
        out_shape = [jax.ShapeDtypeStruct((n_tok, d), F32)]
    return pl.pallas_call(
        kern,
        grid=(n_tok // tm,),
        in_specs=x_specs + _cond_specs(mods, norm_pre, norm_post, layer) + [
            pl.BlockSpec(memory_space=pl.ANY),
            pl.BlockSpec(memory_space=pl.ANY),
            pl.BlockSpec(memory_space=pl.ANY),
        ],
        out_specs=out_specs,
        out_shape=out_shape,
        scratch_shapes=[
            pltpu.VMEM((d, d_ff), BF16),
            pltpu.VMEM((d, d_ff), BF16),
            pltpu.VMEM((d_ff, d), BF16),
            pltpu.VMEM((FFN_STAGE_SLOTS, d, fc), F32),
            pltpu.VMEM((FFN_STAGE_SLOTS, d, fc), F32),
            pltpu.VMEM((FFN_STAGE_SLOTS, fc, d), F32),
            pltpu.SemaphoreType.DMA((3, FFN_STAGE_SLOTS)),
        ],
        compiler_params=pltpu.CompilerParams(
            dimension_semantics=("arbitrary",), vmem_limit_bytes=VMEM_LIMIT_BYTES),
        name=f"ffn_l{layer}h{half}",
    )(*xs, mods, norm_pre, norm_post, wg, wu, wd)


def _log_sigmoid(x):
    return jnp.minimum(x, 0.0) - jnp.log(1.0 + jnp.exp(-jnp.abs(x)))


def _evin_kernel(x_ref, mod_ref, gpre_ref, gpost_ref, w_ref, waf_ref, wab_ref, baf_ref, bab_ref,
                 qa_ref, ka_ref, va_ref, qb_ref, kb_ref, vb_ref, gb_ref, ldf_ref, ldb_ref,
                 w_bf, wa_bf, *, layer, group):
    @pl.when(pl.program_id(0) == 0)
    def _cast_weights():
        w_bf[0:EV_MAIN, :] = w_ref[0, 0:EV_MAIN, :].astype(BF16)
        w_bf[EV_MAIN:EV_MAIN + LANES, :] = jnp.zeros((LANES, w_bf.shape[1]), BF16)
        w_bf[EV_MAIN:EV_MAIN + 2 * GLA_RANK, :] = w_ref[0, EV_MAIN:EV_MAIN + 2 * GLA_RANK, :].astype(BF16)
        wa_bf[...] = jnp.zeros(wa_bf.shape, BF16)
        wa_bf[0:GLA_RANK, 0:B_QK] = waf_ref[0].astype(BF16)
        wa_bf[GLA_RANK:2 * GLA_RANK, B_QK:2 * B_QK] = wab_ref[0].astype(BF16)

    o = EV_OFFS
    mods = _mod_rows(mod_ref, group(pl.program_id(0)))
    g_pre = _norm_row(gpre_ref, layer, 1)

    def project(rows):
        h = _sub_in(x_ref[rows, :], mods, g_pre, 1).astype(BF16)
        return _dot_nt(h, w_bf[...])

    def finish(rows, full):
        qa_ref[rows, :] = (full[:, o[0]:o[1]] * (HEAD_DIM ** -0.5)).astype(qa_ref.dtype)
        ka_ref[rows, :] = full[:, o[1]:o[2]]
        va_ref[rows, :] = full[:, o[2]:o[3]]
        qb_ref[rows, :] = full[:, o[3]:o[4]] * (GLA_DK ** -0.5)
        kb_ref[rows, :] = full[:, o[4]:o[5]]
        vb_ref[rows, :] = full[:, o[5]:o[6]].astype(vb_ref.dtype)
        gb_ref[rows, :] = full[:, o[6]:o[7]].astype(gb_ref.dtype)
        lr = full[:, EV_MAIN:EV_MAIN + LANES].astype(BF16)
        logits = jnp.dot(lr, wa_bf[...], preferred_element_type=F32)
        ldf_ref[rows, :] = _log_sigmoid(logits[:, 0:B_QK] + baf_ref[...]) * (1.0 / GLA_TAU)
        ldb_ref[rows, :] = _log_sigmoid(logits[:, B_QK:2 * B_QK] + bab_ref[...]) * (1.0 / GLA_TAU)

    pending = None
    for rows in _sub_tiles(x_ref.shape[0]):
        full = project(rows)
        if pending is not None:
            finish(*pending)
        pending = (rows, full)
    finish(*pending)


def _even_in(x, mods, norm_pre, norm_post, w_in_t, wa_f, wa_b, ba_f, ba_b, *, layer, e, group):
    n_tok, d = x.shape
    tm = WIDE_TILE
    ev_in = w_in_t.shape[1]
    rank = wa_f.shape[1]
    assert rank == GLA_RANK and ev_in == EV_MAIN + 2 * rank
    widths = (A_Q, A_KV, A_KV, B_QK, B_QK, B_V, B_V, B_QK, B_QK)
    dtypes = (BF16, F32, F32, F32, F32, BF16, BF16, F32, F32)
    kern = functools.partial(_evin_kernel, layer=layer, group=group)
    return pl.pallas_call(
        kern,
        grid=(n_tok // tm,),
        in_specs=[pl.BlockSpec((tm, d), lambda i: (i, 0))] + _cond_specs(mods, norm_pre, norm_post, layer) + [
            pl.BlockSpec((1, ev_in, d), lambda i: (e, 0, 0), pipeline_mode=pl.Buffered(1)),
            pl.BlockSpec((1, rank, B_QK), lambda i: (e, 0, 0)),
            pl.BlockSpec((1, rank, B_QK), lambda i: (e, 0, 0)),
            pl.BlockSpec((1, B_QK), lambda i: (e, 0)),
            pl.BlockSpec((1, B_QK), lambda i: (e, 0)),
        ],
        out_specs=[pl.BlockSpec((tm, w), lambda i: (i, 0)) for w in widths],
        out_shape=[jax.ShapeDtypeStruct((n_tok, w), t) for w, t in zip(widths, dtypes)],
        scratch_shapes=[pltpu.VMEM((EV_MAIN + LANES, d), BF16), pltpu.VMEM((LANES, 2 * B_QK), BF16)],
        compiler_params=pltpu.CompilerParams(
            dimension_semantics=("arbitrary",), vmem_limit_bytes=VMEM_LIMIT_BYTES),
        name="even_in",
    )(x, mods, norm_pre, norm_post, w_in_t, wa_f, wa_b, ba_f, ba_b)


def _lane_lt(shape, n):
    return lax.broadcasted_iota(jnp.int32, shape, len(shape) - 1) < n


def _dup_kv_head(x, g):
    sw = pltpu.roll(x, HEAD_DIM, 1)
    lo = _lane_lt(x.shape, HEAD_DIM)
    return jnp.where(lo, x, sw) if g == 0 else jnp.where(lo, sw, x)


def _dot_nt(a, b):
    return lax.dot_general(a, b, (((1,), (1,)), ((), ())), preferred_element_type=F32)


ATTN_GROUP = ATTN_HEADS // ATTN_KV_HEADS


def _sink_column(sink_ref, g, t):
    rows = lax.broadcasted_iota(jnp.int32, (ATTN_GROUP * t, 1), 0)
    col = jnp.full((ATTN_GROUP * t, 1), sink_ref[ATTN_GROUP * g], F32)
    for j in range(1, ATTN_GROUP):
        col = jnp.where(rows >= j * t, sink_ref[ATTN_GROUP * g + j], col)
    return col


def _attn_scores(q_pairs, keys):
    t = q_pairs[0].shape[0]
    lo = _lane_lt((t, LANES), HEAD_DIM)
    zero = jnp.zeros((t, LANES), BF16)
    q4 = jnp.concatenate([jnp.where(lo, q_pairs[0], zero), jnp.where(lo, zero, q_pairs[0]),
                          jnp.where(lo, q_pairs[1], zero), jnp.where(lo, zero, q_pairs[1])], axis=0)
    scores = []
    for k2, _, bias, feature_major in keys:
        s = jnp.dot(q4, k2, preferred_element_type=F32) if feature_major else _dot_nt(q4, k2)
        if bias is not None:
            s = s + bias[...]
        scores.append(s)
    return scores


def _attn_finish(scores, sink_col, keys):
    t = scores[0].shape[0] // ATTN_GROUP
    lo = _lane_lt((t, LANES), HEAD_DIM)
    mx = sink_col
    for s in scores:
        mx = jnp.maximum(mx, jnp.max(s, axis=-1, keepdims=True))
    den = jnp.exp(sink_col - mx)
    o = None
    for s, (_, v2, _, feature_major) in zip(scores, keys):
        p = jnp.exp(s - mx)
        den = den + jnp.sum(p, axis=-1, keepdims=True)
        pb = p.astype(BF16)
        pv = _dot_nt(pb, v2) if feature_major else jnp.dot(pb, v2, preferred_element_type=F32)
        o = pv if o is None else o + pv
    o = o / den
    return (jnp.where(lo, o[0:t], o[t:2 * t]), jnp.where(lo, o[2 * t:3 * t], o[3 * t:4 * t]))


def _attn_ctx_kernel(sink_ref, q_ref, k_ref, v_ref, o_ref, kt_ref, vt_ref):
    per_step, _, t = kt_ref.shape
    work = []
    for s in range(per_step):
        rows = slice(s * t, (s + 1) * t)
        k = k_ref[rows, :]
        v = v_ref[rows, :]
        kt_ref[s] = k.T
        vt_ref[s] = v.T
        for g in range(ATTN_KV_HEADS):
            keys = [(_dup_kv_head(k, g).astype(BF16), _dup_kv_head(v, g).astype(BF16), None, False)]
            cols = (2 * g, 2 * g + 1)
            scores = _attn_scores([q_ref[rows, m * LANES:(m + 1) * LANES] for m in cols], keys)
            work.append((rows, cols, scores, keys, g))
    for rows, cols, scores, keys, g in work:
        outs = _attn_finish(scores, _sink_column(sink_ref, g, t), keys)
        for m, o in zip(cols, outs):
            o_ref[rows, m * LANES:(m + 1) * LANES] = o.astype(o_ref.dtype)


def _attn_context(sink, qa, ka, va, *, n_seq, seq):
    per_step = ATTN_PROMPT_SEQS
    assert n_seq % per_step == 0
    n_seq, rows = n_seq // per_step, per_step * seq
    cache_spec = pl.BlockSpec((per_step, A_KV, seq), lambda b: (b, 0, 0))
    return pl.pallas_call(
        _attn_ctx_kernel,
        grid=(n_seq,),
        in_specs=[
            pl.BlockSpec(memory_space=pltpu.SMEM),
            pl.BlockSpec((rows, A_Q), lambda b: (b, 0)),
            pl.BlockSpec((rows, A_KV), lambda b: (b, 0)),
            pl.BlockSpec((rows, A_KV), lambda b: (b, 0)),
        ],
        out_specs=[pl.BlockSpec((rows, A_Q), lambda b: (b, 0)), cache_spec, cache_spec],
        out_shape=[jax.ShapeDtypeStruct((n_seq * rows, A_Q), BF16),
                   jax.ShapeDtypeStruct((n_seq * per_step, A_KV, seq), F32),
                   jax.ShapeDtypeStruct((n_seq * per_step, A_KV, seq), F32)],
        compiler_params=pltpu.CompilerParams(dimension_semantics=("parallel",)),
        name="attn_context",
    )(sink, qa, ka, va)


def _rope_tables(n_tok):
    n_rows = n_tok // GRID_W
    assert n_rows <= GRID_W
    shape = (GRID_W, LANES)
    pos = lax.broadcasted_iota(jnp.int32, shape, 0).astype(F32)
    lane = lax.broadcasted_iota(jnp.int32, shape, 1)
    half = HEAD_DIM // 2
    nf = half // 2
    within = lane & (HEAD_DIM - 1)
    is_col = within >= half
    second = (within & (half - 1)) >= nf
    f = (within & (nf - 1)).astype(F32)
    inv = jnp.exp(f * (-2.0 / half * math.log(ROPE_BASE)))
    ang = pos * inv
    cos_t = jnp.cos(ang)
    sin_t = jnp.sin(ang)
    sin_t = jnp.where(second, sin_t, -sin_t)

    def per_token(tab):
        return jnp.concatenate(
            [jnp.where(is_col, tab, jnp.broadcast_to(tab[r:r + 1, :], shape)) for r in range(n_rows)], axis=0)

    return per_token(cos_t), per_token(sin_t)


def _rope(x, cos, sin_signed):
    nf = HEAD_DIM // 4
    lane = lax.broadcasted_iota(jnp.int32, x.shape, 1)
    second = (lane & (2 * nf - 1)) >= nf
    up = pltpu.roll(x, nf, 1)
    dn = pltpu.roll(x, LANES - nf, 1)
    return x * cos + jnp.where(second, up, dn) * sin_signed


def _rope_bf16(x, cos, sin_signed):
    nf = HEAD_DIM // 4
    src = lax.broadcasted_iota(jnp.int32, (LANES, LANES), 0)
    lane = lax.broadcasted_iota(jnp.int32, (LANES, LANES), 1)
    second = (lane & (2 * nf - 1)) >= nf
    perm = jnp.where(src == jnp.where(second, lane - nf, lane + nf), 1.0, 0.0).astype(BF16)
    return x.astype(F32) * cos + jnp.dot(x, perm, preferred_element_type=F32) * sin_signed


def _attn_lat_kernel(sink_ref, q_ref, k_ref, v_ref, kc_ref, vc_ref, o_ref, qr_ref, kp_ref, vp_ref, bias_ref,
                     *, seq):
    per_step = kc_ref.shape[0]
    nb = seq // ATTN_BLOCK
    blk = ATTN_BLOCK
    cos, sin_signed = _rope_tables(seq)
    zpad = jnp.zeros((blk, LANES), BF16)
    ctx = []
    for s in range(per_step):
        rows = slice(s * seq, (s + 1) * seq)
        for m in range(A_Q // LANES):
            qr_ref[rows, m * LANES:(m + 1) * LANES] = _rope_bf16(
                q_ref[rows, m * LANES:(m + 1) * LANES], cos, sin_signed).astype(BF16)
        kr = _rope(k_ref[rows, :], cos, sin_signed)
        v = v_ref[rows, :]
        for g in range(ATTN_KV_HEADS):
            n = s * ATTN_KV_HEADS + g
            kp_ref[n, 0:blk, :] = zpad
            kp_ref[n, blk + seq:2 * blk + seq, :] = zpad
            vp_ref[n, 0:blk, :] = zpad
            vp_ref[n, blk + seq:2 * blk + seq, :] = zpad
            kp_ref[n, blk:blk + seq, :] = _dup_kv_head(kr, g).astype(BF16)
            vp_ref[n, blk:blk + seq, :] = _dup_kv_head(v, g).astype(BF16)
            hd = slice(g * HEAD_DIM, (g + 1) * HEAD_DIM)
            ctx.append((jnp.concatenate([kc_ref[s, hd, :]] * 2, axis=0).astype(BF16),
                        jnp.concatenate([vc_ref[s, hd, :]] * 2, axis=0).astype(BF16), None, True))
    sink_cols = [_sink_column(sink_ref, g, blk) for g in range(ATTN_KV_HEADS)]
    r = lax.broadcasted_iota(jnp.int32, (ATTN_GROUP * blk, 3 * blk), 0) & (blk - 1)
    c = lax.broadcasted_iota(jnp.int32, (ATTN_GROUP * blk, 3 * blk), 1)
    band = (c >= r) & (c <= r + 2 * WINDOW)
    bias_ref[0] = jnp.where(band & (c >= blk), 0.0, NEG_INF)
    bias_ref[1] = jnp.where(band, 0.0, NEG_INF)
    bias_ref[2] = jnp.where(band & (c < 2 * blk), 0.0, NEG_INF)

    def block_body(i, carry):
        row0 = pl.multiple_of(i * blk, blk)
        edge = jnp.where(i == 0, 0, jnp.where(i == nb - 1, 2, 1))
        work = []
        for s in range(per_step):
            q0 = pl.multiple_of(s * seq + i * blk, blk)
            for g in range(ATTN_KV_HEADS):
                n = s * ATTN_KV_HEADS + g
                cols = (2 * g, 2 * g + 1)
                window = (kp_ref[n, pl.ds(row0, 3 * blk), :], vp_ref[n, pl.ds(row0, 3 * blk), :],
                          bias_ref.at[edge], False)
                keys = [window, ctx[n]]
                scores = _attn_scores([qr_ref[pl.ds(q0, blk), m * LANES:(m + 1) * LANES] for m in cols], keys)
                work.append((q0, cols, scores, keys, g))
        for q0, cols, scores, keys, g in work:
            outs = _attn_finish(scores, sink_cols[g], keys)
            for m, o in zip(cols, outs):
                o_ref[pl.ds(q0, blk), m * LANES:(m + 1) * LANES] = o.astype(o_ref.dtype)
        return carry

    lax.fori_loop(0, nb, block_body, 0)


def _attn_latent(sink, qa, ka, va, k_ctx, v_ctx, *, n_seq, seq, row_block0):
    past = k_ctx.shape[2]
    per_step = ATTN_LATENT_SEQS
    assert n_seq % per_step == 0 and row_block0 % per_step == 0 and seq // ATTN_BLOCK >= 2
    rows = per_step * seq
    first = row_block0 // per_step
    kern = functools.partial(_attn_lat_kernel, seq=seq)
    return pl.pallas_call(
        kern,
        grid=(n_seq // per_step,),
        in_specs=[
            pl.BlockSpec(memory_space=pltpu.SMEM),
            pl.BlockSpec((rows, A_Q), lambda b: (first + b, 0)),
            pl.BlockSpec((rows, A_KV), lambda b: (first + b, 0)),
            pl.BlockSpec((rows, A_KV), lambda b: (first + b, 0)),
            pl.BlockSpec((per_step, A_KV, past), lambda b: (b, 0, 0)),
            pl.BlockSpec((per_step, A_KV, past), lambda b: (b, 0, 0)),
        ],
        out_specs=pl.BlockSpec((rows, A_Q), lambda b: (b, 0)),
        out_shape=jax.ShapeDtypeStruct((n_seq * seq, A_Q), BF16),
        scratch_shapes=[
            pltpu.VMEM((rows, A_Q), BF16),
            pltpu.VMEM((per_step * ATTN_KV_HEADS, seq + 2 * ATTN_BLOCK, LANES), BF16),
            pltpu.VMEM((per_step * ATTN_KV_HEADS, seq + 2 * ATTN_BLOCK, LANES), BF16),
            pltpu.VMEM((3, ATTN_GROUP * ATTN_BLOCK, 3 * ATTN_BLOCK), F32),
        ],
        compiler_params=pltpu.CompilerParams(dimension_semantics=("parallel",)),
        name="attn_latent",
    )(sink, qa, ka, va, k_ctx, v_ctx)


def _same_block(a, b, size):
    return (a & -size) == (b & -size)


def _split3(x):
    hi = x.astype(BF16)
    r1 = x - hi.astype(F32)
    mid = r1.astype(BF16)
    lo = (r1 - mid.astype(F32)).astype(BF16)
    return hi, mid, lo


def _interleave(generators):
    live = list(generators)
    while live:
        for gen in list(live):
            try:
                next(gen)
            except StopIteration:
                live.remove(gen)


def _gla_tile(q, k, v, ld, st_ref, sel_ref, emit, *, rev):
    t = GLA_TILE
    n_pair = GLA_HEADS // 2
    row = lax.broadcasted_iota(jnp.int32, (t, t), 0)
    col = lax.broadcasted_iota(jnp.int32, (t, t), 1)
    tri = (row <= col) if rev else (row >= col)
    tri = jnp.where(tri, 1.0, 0.0).astype(BF16)
    hi, mid, lo = _split3(ld * math.log2(math.e))
    b = (jnp.dot(tri, hi, preferred_element_type=F32) + jnp.dot(tri, mid, preferred_element_type=F32)
         + jnp.dot(tri, lo, preferred_element_type=F32))
    yield
    b_tot = b[0:1, :] if rev else b[t - 1:t, :]
    qs = (q * jnp.exp2(b)).astype(BF16)
    kd = (k * jnp.exp2(b_tot - b)).astype(BF16)

    rowv = lax.broadcasted_iota(jnp.int32, (t, B_QK), 0)
    levels = []
    half = GLA_SUB
    while 2 * half <= t:
        blk = 2 * half
        pieces = []
        for bs in range(0, t, blk):
            r = bs + half if rev else bs + half - 1
            pieces.append(jnp.broadcast_to(b[r:r + 1, :], (blk, B_QK)))
        ref = jnp.concatenate(pieces, axis=0) if len(pieces) > 1 else pieces[0]
        second = (rowv & (blk - 1)) >= half
        is_q = ~second if rev else second
        ql = (q * jnp.exp2(jnp.where(is_q, b - ref, -jnp.inf))).astype(BF16)
        kl = (k * jnp.exp2(jnp.where(is_q, -jnp.inf, ref - b))).astype(BF16)
        levels.append((blk, ql, kl))
        half = blk
    yield

    rr = lax.broadcasted_iota(jnp.int32, (t, 2 * t), 0)
    ss = lax.broadcasted_iota(jnp.int32, (t, 2 * t), 1) & (t - 1)
    sub_rows = lax.broadcasted_iota(jnp.int32, (GLA_SUB, LANES), 0)
    lane_lo = _lane_lt((t, LANES), GLA_DK)
    lane_lo2 = _lane_lt((2 * t, LANES), GLA_DK)
    par_rows = lax.broadcasted_iota(jnp.int32, (2 * t, LANES), 0) < t
    head_sel = par_rows == lane_lo2

    outs = []
    for p in range(n_pair):
        pl_ = slice(p * LANES, (p + 1) * LANES)
        qp, kp, bp = q[:, pl_], k[:, pl_], b[:, pl_]
        rows = []
        for sb in range(t // GLA_SUB):
            r0 = sb * GLA_SUB
            q_sub = qp[r0:r0 + GLA_SUB, :]
            b_sub = bp[r0:r0 + GLA_SUB, :]
            cols = []
            for j in range(GLA_SUB):
                kj = kp[r0 + j:r0 + j + 1, :]
                bj = bp[r0 + j:r0 + j + 1, :]
                valid = (sub_rows <= j) if rev else (sub_rows >= j)
                e = jnp.exp2(jnp.where(valid, b_sub - bj, -jnp.inf))
                cols.append(q_sub * kj * e)
            rows.append(jnp.concatenate(cols, axis=1))
        e_p = jnp.concatenate(rows, axis=0).astype(BF16)
        yield
        sub_scores = jnp.dot(e_p, sel_ref[...], preferred_element_type=F32)
        scores = jnp.zeros_like(sub_scores)
        for blk, ql, kl in reversed(levels):
            klp = kl[:, pl_]
            kstack = jnp.concatenate([klp, klp], axis=0)
            kstack = jnp.where(head_sel, kstack, jnp.zeros_like(kstack))
            s_l = _dot_nt(ql[:, pl_], kstack)
            scores = jnp.where(_same_block(rr, ss, blk), s_l, scores)
        scores = jnp.where(_same_block(rr, ss, GLA_SUB), sub_scores, scores)
        yield
        vp = v[:, p * 2 * GLA_DV:(p + 1) * 2 * GLA_DV]
        v_lo = _lane_lt(vp.shape, GLA_DV)
        vbd = jnp.concatenate([jnp.where(v_lo, vp, 0.0), jnp.where(v_lo, 0.0, vp)], axis=0).astype(BF16)
        st = st_ref[p]
        o_p = (jnp.dot(scores.astype(BF16), vbd, preferred_element_type=F32)
               + _dot_nt(qs[:, pl_], st.astype(BF16)))
        outs.append(o_p)
        upd = jnp.dot(vp.T.astype(BF16), kd[:, pl_], preferred_element_type=F32)
        st_rows_lo = lax.broadcasted_iota(jnp.int32, upd.shape, 0) < GLA_DV
        bd = st_rows_lo == _lane_lt(upd.shape, GLA_DK)
        st_ref[p] = st * jnp.exp2(b_tot[:, pl_]) + jnp.where(bd, upd, 0.0)
        yield
    emit(jnp.concatenate(outs, axis=1))


def _gla_kernel(*refs, seq, has_init, want_state, per_step):
    q_ref, k_ref, v_ref, ldf_ref, ldb_ref = refs[:5]
    pos = 5
    if has_init:
        s0f_ref, s0b_ref = refs[pos:pos + 2]
        pos += 2
    o_ref = refs[pos]
    pos += 1
    if want_state:
        sf_ref, sb_ref = refs[pos:pos + 2]
        pos += 2
    st_ref, sel_ref, acc_ref = refs[pos:pos + 3]
    t = GLA_TILE
    nt = seq // t
    n_pair = GLA_HEADS // 2

    kk = lax.broadcasted_iota(jnp.int32, (GLA_SUB * LANES, 2 * t), 0)
    nn = lax.broadcasted_iota(jnp.int32, (GLA_SUB * LANES, 2 * t), 1)
    sel = ((nn & (GLA_SUB - 1)) == (kk >> LANES_LOG2)) & (((kk & (LANES - 1)) >= GLA_DK) == (nn >= t))
    sel_ref[...] = jnp.where(sel, 1.0, 0.0).astype(BF16)

    def load_state(st, s0_ref, s):
        for p in range(n_pair):
            if s0_ref is None:
                st[p] = jnp.zeros(st.shape[1:], F32)
            else:
                x = jnp.concatenate([s0_ref[s, 2 * p], s0_ref[s, 2 * p + 1]], axis=0)
                xt = x.T
                lo = _lane_lt(xt.shape, GLA_DK)
                st[p] = jnp.concatenate([jnp.where(lo, xt, 0.0), jnp.where(lo, 0.0, xt)], axis=0)

    def store_state(st, out_ref, s):
        for p in range(n_pair):
            z = st[p]
            lo = _lane_lt((GLA_DV, LANES), GLA_DK)
            z = jnp.where(lo, z[0:GLA_DV, :], z[GLA_DV:2 * GLA_DV, :]).T
            out_ref[s, 2 * p] = z[0:GLA_DK, :]
            out_ref[s, 2 * p + 1] = z[GLA_DK:2 * GLA_DK, :]

    walks = [(s, rev, ldb_ref if rev else ldf_ref, st_ref.at[2 * s + int(rev)])
             for s in range(per_step) for rev in (False, True)]
    for s, rev, _, st in walks:
        load_state(st, (s0b_ref if rev else s0f_ref) if has_init else None, s)
    acc_ref[...] = jnp.zeros(acc_ref.shape, F32)

    def tile_body(ti, carry):
        def walk(s, rev, ld_ref, st):
            tile = (nt - 1 - ti) if rev else ti
            r0 = pl.multiple_of(s * seq + tile * t, t)
            rows = pl.ds(r0, t)

            def emit(o):
                acc_ref[rows, :] = acc_ref[rows, :] + o

            return _gla_tile(q_ref[rows, :], k_ref[rows, :], v_ref[rows, :].astype(F32), ld_ref[rows, :],
                             st, sel_ref, emit, rev=rev)

        _interleave([walk(*w) for w in walks])
        return carry

    lax.fori_loop(0, nt, tile_body, 0)
    o_ref[...] = acc_ref[...].astype(o_ref.dtype)
    if want_state:
        for s, rev, _, st in walks:
            store_state(st, sb_ref if rev else sf_ref, s)


def _gla(qb, kb, vb, ldf, ldb, s0f, s0b, *, n_seq, seq, row_block0, want_state, per_step):
    has_init = s0f is not None
    assert n_seq % per_step == 0 and row_block0 % per_step == 0
    kern = functools.partial(_gla_kernel, seq=seq, has_init=has_init, want_state=want_state,
                             per_step=per_step)
    n_seq, seq_rows, row_block0 = n_seq // per_step, per_step * seq, row_block0 // per_step
    tok = lambda w: pl.BlockSpec((seq_rows, w), lambda b: (row_block0 + b, 0))
    in_specs = [tok(B_QK), tok(B_QK), tok(B_V), tok(B_QK), tok(B_QK)]
    args = [qb, kb, vb, ldf, ldb]
    state_spec = pl.BlockSpec((per_step, GLA_HEADS, GLA_DK, GLA_DV), lambda b: (b, 0, 0, 0))
    if has_init:
        in_specs += [state_spec, state_spec]
        args += [s0f, s0b]
    out_specs = [pl.BlockSpec((seq_rows, B_V), lambda b: (b, 0))]
    out_shape = [jax.ShapeDtypeStruct((n_seq * seq_rows, B_V), BF16)]
    if want_state:
        out_specs += [state_spec, state_spec]
        out_shape += [jax.ShapeDtypeStruct((n_seq * per_step, GLA_HEADS, GLA_DK, GLA_DV), F32)] * 2
    return pl.pallas_call(
        kern,
        grid=(n_seq,),
        in_specs=in_specs,
        out_specs=out_specs,
        out_shape=out_shape,
        scratch_shapes=[
            pltpu.VMEM((2 * per_step, GLA_HEADS // 2, 2 * GLA_DV, LANES), F32),
            pltpu.VMEM((GLA_SUB * LANES, 2 * GLA_TILE), BF16),
            pltpu.VMEM((seq_rows, B_V), F32),
        ],
        compiler_params=pltpu.CompilerParams(
            dimension_semantics=("parallel",), vmem_limit_bytes=VMEM_LIMIT_BYTES),
        name="gla_state" if want_state else "gla_latent",
    )(*args)


def _evout_kernel(x_ref, mod_ref, gpre_ref, gpost_ref, attn_p_ref, attn_s_ref, gla_p_ref, gla_s_ref, gb_ref,
                  gn_ref, w_ref, o_ref, w_bf, *, np_tiles, layer, group):
    @pl.when(pl.program_id(0) == 0)
    def _cast_weights():
        w_bf[...] = w_ref[0].astype(BF16)

    is_prompt = pl.program_id(0) < np_tiles
    mods = _mod_rows(mod_ref, group(pl.program_id(0)))
    g_post = _norm_row(gpost_ref, layer, 1)
    for rows in _sub_tiles(x_ref.shape[0]):
        attn = _pick((attn_p_ref, attn_s_ref), is_prompt, rows)
        gla = _pick((gla_p_ref, gla_s_ref), is_prompt, rows)
        parts = [attn]
        for hd in range(GLA_HEADS):
            sl = slice(hd * GLA_DV, (hd + 1) * GLA_DV)
            g = _rms(gla[:, sl].astype(F32), gn_ref[:, sl]) * _silu(gb_ref[rows, sl].astype(F32))
            parts.append(g.astype(BF16))
        y = jnp.dot(jnp.concatenate(parts, axis=1), w_bf[...], preferred_element_type=F32)
        o_ref[rows, :] = _sub_out(x_ref[rows, :], y, mods, g_post, 1, 1.0)


def _even_out(x, mods, norm_pre, norm_post, attn_ps, gla_ps, gb, gla_norm, w_out, *, layer, e, group,
              np_tiles):
    n_tok, d = x.shape
    tm = WIDE_TILE
    ev_out = w_out.shape[1]
    kern = functools.partial(_evout_kernel, np_tiles=np_tiles, layer=layer, group=group)
    return pl.pallas_call(
        kern,
        grid=(n_tok // tm,),
        in_specs=[pl.BlockSpec((tm, d), lambda i: (i, 0))] + _cond_specs(mods, norm_pre, norm_post, layer) + [
            *_tile_specs(tm, A_Q, np_tiles),
            *_tile_specs(tm, B_V, np_tiles),
            pl.BlockSpec((tm, B_V), lambda i: (i, 0)),
            pl.BlockSpec((1, B_V), lambda i: (e, 0)),
            pl.BlockSpec((1, ev_out, d), lambda i: (e, 0, 0), pipeline_mode=pl.Buffered(1)),
        ],
        out_specs=pl.BlockSpec((tm, d), lambda i: (i, 0)),
        out_shape=jax.ShapeDtypeStruct((n_tok, d), F32),
        scratch_shapes=[pltpu.VMEM((ev_out, d), BF16)],
        compiler_params=pltpu.CompilerParams(
            dimension_semantics=("arbitrary",), vmem_limit_bytes=VMEM_LIMIT_BYTES),
        name="even_out",
    )(x, mods, norm_pre, norm_post, *attn_ps, *gla_ps, gb, gla_norm, w_out)


def _gelu(x):
    return 0.5 * x * (1.0 + lax.erf(x * (2.0 ** -0.5)))


def _cm_kernel(x_ref, mod_ref, gpre_ref, gpost_ref, win_ref, vg_ref, vb_ref, ws_ref, bs_ref, wout_ref,
               o_ref, win_bf, wout_bf, *, layer, group):
    @pl.when(pl.program_id(0) == 0)
    def _cast_weights():
        win_bf[...] = win_ref[0].astype(BF16)
        wout_bf[...] = wout_ref[0].astype(BF16)

    width = wout_bf.shape[0]
    gw = width // CMLP_GROUPS
    ws = [ws_ref[0, g].astype(BF16) for g in range(CMLP_GROUPS)]
    mods = _mod_rows(mod_ref, group(pl.program_id(0)))
    g_pre = _norm_row(gpre_ref, layer, 1)
    g_post = _norm_row(gpost_ref, layer, 1)
    eye = (lax.broadcasted_iota(jnp.int32, (CHUNK, CHUNK), 0)
           == lax.broadcasted_iota(jnp.int32, (CHUNK, CHUNK), 1))
    bias_cols = [jnp.sum(jnp.where(eye, jnp.broadcast_to(bs_ref[0, g:g + 1, :], (CHUNK, CHUNK)), 0.0),
                         axis=1, keepdims=True) for g in range(CMLP_GROUPS)]

    def phase_in(rows):
        x = x_ref[rows, :]
        h = _sub_in(x, mods, g_pre, 1).astype(BF16)
        v = _gelu(jnp.dot(h, win_bf[:, width:2 * width], preferred_element_type=F32))
        u = _gelu(jnp.dot(h, win_bf[:, 0:width], preferred_element_type=F32))
        return x, u, v

    def phase_mix(x, u, v):
        mu = jnp.mean(v, axis=-1, keepdims=True)
        vc = v - mu
        var = jnp.mean(vc * vc, axis=-1, keepdims=True)
        vn = (vc * lax.rsqrt(var + EPS) * vg_ref[...] + vb_ref[...]).astype(BF16)
        chunk_rows = []
        for c in range(x.shape[0] // CHUNK):
            cols = []
            for g in range(CMLP_GROUPS):
                blk = vn[c * CHUNK:(c + 1) * CHUNK, g * gw:(g + 1) * gw]
                cols.append(jnp.dot(ws[g], blk, preferred_element_type=F32) + bias_cols[g])
            chunk_rows.append(jnp.concatenate(cols, axis=1))
        mixed = jnp.concatenate(chunk_rows, axis=0)
        return x, (u * mixed).astype(BF16)

    def phase_out(rows, x, m):
        y = jnp.dot(m, wout_bf[...], preferred_element_type=F32)
        o_ref[rows, :] = _sub_out(x, y, mods, g_post, 1, 1.0)

    tiles = _sub_tiles(x_ref.shape[0])
    pending = None
    for rows in tiles:
        cur = phase_in(rows)
        if pending is not None:
            phase_out(pending[0], *phase_mix(*pending[1]))
        pending = (rows, cur)
    phase_out(pending[0], *phase_mix(*pending[1]))


def _chunk_mlp(x, mods, norm_pre, norm_post, w_in, v_gain, v_bias, w_s, b_s, w_out, *, layer, o, group):
    n_tok, d = x.shape
    tm = WIDE_TILE
    width = w_out.shape[1]
    kern = functools.partial(_cm_kernel, layer=layer, group=group)
    return pl.pallas_call(
        kern,
        grid=(n_tok // tm,),
        in_specs=[pl.BlockSpec((tm, d), lambda i: (i, 0))] + _cond_specs(mods, norm_pre, norm_post, layer) + [
            pl.BlockSpec((1, d, 2 * width), lambda i: (o, 0, 0), pipeline_mode=pl.Buffered(1)),
            pl.BlockSpec((1, width), lambda i: (o, 0)),
            pl.BlockSpec((1, width), lambda i: (o, 0)),
            pl.BlockSpec((1, CMLP_GROUPS, CHUNK, CHUNK), lambda i: (o, 0, 0, 0)),
            pl.BlockSpec((1, CMLP_GROUPS, CHUNK), lambda i: (o, 0, 0)),
            pl.BlockSpec((1, width, d), lambda i: (o, 0, 0), pipeline_mode=pl.Buffered(1)),
        ],
        out_specs=pl.BlockSpec((tm, d), lambda i: (i, 0)),
        out_shape=jax.ShapeDtypeStruct((n_tok, d), F32),
        scratch_shapes=[pltpu.VMEM((d, 2 * width), BF16), pltpu.VMEM((width, d), BF16)],
        compiler_params=pltpu.CompilerParams(
            dimension_semantics=("arbitrary",), vmem_limit_bytes=VMEM_LIMIT_BYTES),
        name="chunk_mlp",
    )(x, mods, norm_pre, norm_post, w_in, v_gain, v_bias, w_s, b_s, w_out)


def kernel(x_prompt, x_sample, cache_k, cache_v, state_gla_fwd, state_gla_bwd, c, c_ctx, w_mod, b_mod, norm_pre, norm_post, ffn_w_gate, ffn_w_up, ffn_w_down, ev_w_in, ev_w_out, ev_sink, gla_wa_f, gla_ba_f, gla_wa_b, gla_ba_b, gla_norm, cm_w_in, cm_v_gain, cm_v_bias, cm_w_s, cm_b_s, cm_w_out):
    batch, seq, d = x_prompt.shape
    dec_batch, dec_seq, _ = x_sample.shape
    depth = w_mod.shape[0]
    n_prompt_tok = batch * seq
    n_sample_tok = dec_batch * dec_seq
    assert n_prompt_tok % dec_seq == 0 and dec_seq % TOKEN_TILE == 0
    group = _group_index_map(TOKEN_TILE, n_prompt_tok, dec_seq)
    np_tiles = n_prompt_tok // TOKEN_TILE
    ffn = functools.partial(_ffn, group=group, np_tiles=np_tiles)
    assert dec_seq % WIDE_TILE == 0
    group_w = _group_index_map(WIDE_TILE, n_prompt_tok, dec_seq)
    np_tiles_w = n_prompt_tok // WIDE_TILE

    xs = (x_prompt.reshape(n_prompt_tok, d), x_sample.reshape(n_sample_tok, d))
    ev_w_in_t = jnp.swapaxes(ev_w_in, 1, 2)

    mods = _adaln_mods(c_ctx, c, w_mod, b_mod)
    gains_pre = norm_pre.reshape(depth * N_SUB, d)
    gains_post = norm_post.reshape(depth * N_SUB, d)
    cond = (mods, gains_pre, gains_post)

    new_k, new_v, new_sf, new_sb = [], [], [], []
    for layer in range(depth):
        (x,) = ffn(xs, *cond, ffn_w_gate, ffn_w_up, ffn_w_down,
                   sub=0, layer=layer, half=0, split_out=False)
        if layer % 2 == 0:
            e = layer // 2
            qa, ka, va, qb, kb, vb, gb, ldf, ldb = _even_in(
                x, *cond, ev_w_in_t, gla_wa_f, gla_wa_b, gla_ba_f, gla_ba_b,
                layer=layer, e=e, group=group_w)
            sink = ev_sink[e]
            attn_p, k_t, v_t = _attn_context(sink, qa, ka, va, n_seq=batch, seq=seq)
            k_ctx = jnp.transpose(cache_k[:, e], (0, 2, 3, 1)).reshape(dec_batch, A_KV, -1)
            v_ctx = jnp.transpose(cache_v[:, e], (0, 2, 3, 1)).reshape(dec_batch, A_KV, -1)
            attn_s = _attn_latent(sink, qa, ka, va, k_ctx, v_ctx, n_seq=dec_batch, seq=dec_seq,
                                  row_block0=n_prompt_tok // dec_seq)
            gla_p, s_f, s_b = _gla(qb, kb, vb, ldf, ldb, None, None, n_seq=batch, seq=seq,
                                   row_block0=0, want_state=True, per_step=GLA_PROMPT_SEQS)
            (gla_s,) = _gla(qb, kb, vb, ldf, ldb, state_gla_fwd[:, e], state_gla_bwd[:, e],
                            n_seq=dec_batch, seq=dec_seq, row_block0=n_prompt_tok // dec_seq,
                            want_state=False, per_step=GLA_LATENT_SEQS)
            x = _even_out(x, *cond, (attn_p, attn_s), (gla_p, gla_s), gb, gla_norm, ev_w_out,
                          layer=layer, e=e, group=group_w, np_tiles=np_tiles_w)
            for cache_t, dst in ((k_t, new_k), (v_t, new_v)):
                dst.append(jnp.transpose(cache_t.reshape(batch, ATTN_KV_HEADS, HEAD_DIM, seq), (0, 3, 1, 2)))
            new_sf.append(s_f)
            new_sb.append(s_b)
        else:
            o = layer // 2
            x = _chunk_mlp(x, *cond, cm_w_in, cm_v_gain, cm_v_bias, cm_w_s, cm_b_s, cm_w_out,
                           layer=layer, o=o, group=group_w)
        xs = ffn((x,), *cond, ffn_w_gate, ffn_w_up, ffn_w_down,
                 sub=2, layer=layer, half=1, split_out=(layer == depth - 1))

    y_prompt = xs[0].reshape(batch, seq, d)
    y_sample = xs[1].reshape(dec_batch, dec_seq, d)
    return (y_prompt, y_sample, jnp.stack(new_k, axis=1), jnp.stack(new_v, axis=1),
            jnp.stack(new_sf, axis=1), jnp.stack(new_sb, axis=1))
```

```python
import functools
import math

import jax
import jax.numpy as jnp
from jax import lax
from jax.experimental import pallas as pl
from jax.experimental.pallas import tpu as pltpu

F32 = jnp.float32
BF16 = jnp.bfloat16

EPS = 1e-6
NEG_INF = -1e30
FFN_RESIDUAL = 0.5
N_SUB = 3

ATTN_HEADS = 8
ATTN_KV_HEADS = 2
HEAD_DIM = 64
ATTN_BLOCK = 128
WINDOW = 128
GRID_W = 64
GRID_W_LOG2 = 6
ROPE_BASE = 10000.0
GLA_HEADS = 4
GLA_DK = 64
GLA_DV = 128
GLA_RANK = 16
GLA_TAU = 16.0
CHUNK = 128
CMLP_GROUPS = 4

A_Q = ATTN_HEADS * HEAD_DIM
A_KV = ATTN_KV_HEADS * HEAD_DIM
B_QK = GLA_HEADS * GLA_DK
B_V = GLA_HEADS * GLA_DV
EV_MAIN = A_Q + 2 * A_KV + 2 * B_QK + 2 * B_V
EV_OFFS = (0, A_Q, A_Q + A_KV, A_Q + 2 * A_KV, A_Q + 2 * A_KV + B_QK,
           A_Q + 2 * A_KV + 2 * B_QK, A_Q + 2 * A_KV + 2 * B_QK + B_V, EV_MAIN)

LANES = 128
LANES_LOG2 = 7
SUBLANES = 8
VMEM_LIMIT_BYTES = 56 * 1024 * 1024

TOKEN_TILE = 512
WIDE_TILE = 1024
SUB_ROWS = 512
ADALN_BUFFERS = 4
FFN_CHUNK = 256
FFN_STAGE_SLOTS = 3
GLA_TILE = 128
GLA_SUB = 8
GLA_PROMPT_SEQS = 4
GLA_LATENT_SEQS = 2
ATTN_PROMPT_SEQS = 2
ATTN_LATENT_SEQS = 2


def _group_index_map(tile_rows, n_prompt_tok, seq_tok):
    def group(i):
        return jnp.maximum(i * tile_rows - (n_prompt_tok - seq_tok), 0) // seq_tok
    return group


def _sub_tiles(n_rows):
    assert n_rows % SUB_ROWS == 0
    return [slice(r, r + SUB_ROWS) for r in range(0, n_rows, SUB_ROWS)]


def _rms(x, g):
    ms = jnp.mean(x * x, axis=-1, keepdims=True)
    return x * lax.rsqrt(ms + EPS) * g


def _mod_rows(mod_ref, g):
    return [mod_ref[k, pl.ds(g, 1), :] for k in range(3 * N_SUB)]


def _sub_in(x, mods, g_pre, sub):
    shift, scale = mods[3 * sub + 0], mods[3 * sub + 1]
    return _rms(x, g_pre * (1.0 + scale)) + shift


def _sub_out(x, y, mods, g_post, sub, coef):
    return x + _rms(y, (coef * mods[3 * sub + 2]) * g_post)


def _norm_row(ref, layer, sub):
    r = layer * N_SUB + sub
    return ref[r:r + 1, :]


def _silu(x):
    return x * jax.nn.sigmoid(x)


def _mod_kernel(cctx_ref, c_ref, w_hbm, b_ref, o_ref, buf, sem):
    depth, n_kinds, rows, d = o_ref.shape
    r = lax.broadcasted_iota(jnp.int32, (rows, d), 0)
    cond = jnp.where(r == 0, jnp.broadcast_to(cctx_ref[...], (rows, d)), 0.0)
    for j in range(c_ref.shape[0]):
        cond = jnp.where(r == 1 + j, jnp.broadcast_to(c_ref[j:j + 1, :], (rows, d)), cond)
    s = _silu(cond).astype(BF16)
    n_chunks = depth * n_kinds
    n_buf = buf.shape[0]

    def copy(i):
        l, q = i // n_kinds, i % n_kinds
        col0 = q * d if isinstance(i, int) else pl.multiple_of(q * d, LANES)
        return pltpu.make_async_copy(w_hbm.at[l, :, pl.ds(col0, d)], buf.at[i % n_buf], sem.at[i % n_buf])

    for i in range(min(n_buf, n_chunks)):
        copy(i).start()

    def chunk_body(i, carry):
        l, q = i // n_kinds, i % n_kinds
        copy(i).wait()
        res = jnp.dot(s, buf[i % n_buf].astype(BF16), preferred_element_type=F32)
        o_ref[l, q] = res + b_ref[pl.ds(l, 1), pl.ds(pl.multiple_of(q * d, LANES), d)]

        @pl.when(i + n_buf < n_chunks)
        def _refill():
            copy(i + n_buf).start()

        return carry

    lax.fori_loop(0, n_chunks, chunk_body, 0)


def _adaln_mods(c_ctx, c, w_mod, b_mod):
    depth, d, n = w_mod.shape
    assert n % d == 0
    rows = -(-(1 + c.shape[0]) // SUBLANES) * SUBLANES
    return pl.pallas_call(
        _mod_kernel,
        grid=(1,),
        in_specs=[
            pl.BlockSpec((1, d), lambda i: (0, 0)),
            pl.BlockSpec(c.shape, lambda i: (0, 0)),
            pl.BlockSpec(memory_space=pl.ANY),
            pl.BlockSpec((depth, n), lambda i: (0, 0)),
        ],
        out_specs=pl.BlockSpec((depth, n // d, rows, d), lambda i: (0, 0, 0, 0)),
        out_shape=jax.ShapeDtypeStruct((depth, n // d, rows, d), F32),
        scratch_shapes=[pltpu.VMEM((ADALN_BUFFERS, d, d), F32), pltpu.SemaphoreType.DMA((ADALN_BUFFERS,))],
        compiler_params=pltpu.CompilerParams(
            dimension_semantics=("arbitrary",), vmem_limit_bytes=VMEM_LIMIT_BYTES),
        name="adaln_mods",
    )(c_ctx.reshape(1, d), c, w_mod, b_mod)


def _tile_specs(tm, width, np_tiles):
    return [pl.BlockSpec((tm, width), lambda i: (jnp.minimum(i, np_tiles - 1), 0)),
            pl.BlockSpec((tm, width), lambda i: (jnp.maximum(i - np_tiles, 0), 0))]


def _pick(refs, is_prompt, rows=slice(None)):
    if len(refs) == 1:
        return refs[0][rows, :]
    return jnp.where(is_prompt, refs[0][rows, :], refs[1][rows, :])


def _cond_specs(mods, norm_pre, norm_post, layer):
    return [pl.BlockSpec((None,) + mods.shape[1:], lambda i: (layer, 0, 0, 0)),
            pl.BlockSpec(norm_pre.shape, lambda i: (0, 0)),
            pl.BlockSpec(norm_post.shape, lambda i: (0, 0))]


def _ffn_kernel(*refs, sub, layer, half, n_chunks, n_x, n_o, np_tiles, group):
    x_refs = refs[:n_x]
    mod_ref, gpre_ref, gpost_ref, wg_hbm, wu_hbm, wd_hbm = refs[n_x:n_x + 6]
    o_refs = refs[n_x + 6:n_x + 6 + n_o]
    wg_bf, wu_bf, wd_bf, st_g, st_u, st_d, sem = refs[n_x + 6 + n_o:]
    fc = FFN_CHUNK
    step = pl.program_id(0)
    mods = _mod_rows(mod_ref, group(step))
    g_pre = _norm_row(gpre_ref, layer, sub)
    g_post = _norm_row(gpost_ref, layer, sub)

    def chunk_copies(c, slot):
        cols = pl.ds(c * fc, fc)
        return (
            pltpu.make_async_copy(wg_hbm.at[layer, half, :, cols], st_g.at[slot], sem.at[0, slot]),
            pltpu.make_async_copy(wu_hbm.at[layer, half, :, cols], st_u.at[slot], sem.at[1, slot]),
            pltpu.make_async_copy(wd_hbm.at[layer, half, cols, :], st_d.at[slot], sem.at[2, slot]),
        )

    def run(stage_weights, prompt):
        n_slots = st_g.shape[0]
        if stage_weights:
            for c in range(min(n_slots - 1, n_chunks)):
                for cp in chunk_copies(c, c):
                    cp.start()
        x = _pick(x_refs, step < np_tiles)
        h = _sub_in(x, mods, g_pre, sub).astype(BF16)
        y = jnp.zeros(x.shape, F32)
        for c in range(n_chunks):
            sl = slice(c * fc, (c + 1) * fc)
            if stage_weights:
                slot = c % n_slots
                ahead = c + n_slots - 1
                if ahead < n_chunks:
                    for cp in chunk_copies(ahead, ahead % n_slots):
                        cp.start()
                for cp in chunk_copies(c, slot):
                    cp.wait()
                wg_bf[:, sl] = st_g[slot].astype(BF16)
                wu_bf[:, sl] = st_u[slot].astype(BF16)
                wd_bf[sl, :] = st_d[slot].astype(BF16)
            g = jnp.dot(h, wg_bf[:, sl], preferred_element_type=F32)
            u = jnp.dot(h, wu_bf[:, sl], preferred_element_type=F32)
            a = (_silu(g) * u).astype(BF16)
            y = y + jnp.dot(a, wd_bf[sl, :], preferred_element_type=F32)
        o_refs[0 if (prompt or n_o == 1) else 1][...] = _sub_out(x, y, mods, g_post, sub, FFN_RESIDUAL)

    @pl.when(step == 0)
    def _first_tile():
        run(True, True)

    if n_o == 1:
        @pl.when(step > 0)
        def _other_tiles():
            run(False, True)
    else:
        @pl.when(jnp.logical_and(step > 0, step < np_tiles))
        def _prompt_tiles():
            run(False, True)

        @pl.when(step >= np_tiles)
        def _latent_tiles():
            run(False, False)


def _ffn(xs, mods, norm_pre, norm_post, wg, wu, wd, *, sub, layer, half, group, np_tiles, split_out):
    d = xs[0].shape[-1]
    n_tok = sum(x.shape[0] for x in xs)
    d_ff = wg.shape[-1]
    tm = TOKEN_TILE
    fc = FFN_CHUNK
    n_chunks = d_ff // fc
    assert n_chunks * fc == d_ff and n_tok % tm == 0
    n_x, n_o = len(xs), (2 if split_out else 1)
    kern = functools.partial(_ffn_kernel, sub=sub, layer=layer, half=half, n_chunks=n_chunks,
                             n_x=n_x, n_o=n_o, np_tiles=np_tiles, group=group)
    whole = pl.BlockSpec((tm, d), lambda i: (i, 0))
    x_specs = _tile_specs(tm, d, np_tiles) if n_x == 2 else [whole]
    n_p = np_tiles * tm
    if split_out:
        out_specs = _tile_specs(tm, d, np_tiles)
        out_shape = [jax.ShapeDtypeStruct((n_p, d), F32), jax.ShapeDtypeStruct((n_tok - n_p, d), F32)]
    else:
        out_specs = [whole]
        out_shape = [jax.ShapeDtypeStruct((n_tok, d), F32)]
    return pl.pallas_call(
        kern,
        grid=(n_tok // tm,),
        in_specs=x_specs + _cond_specs(mods, norm_pre, norm_post, layer) + [
            pl.BlockSpec(memory_space=pl.ANY),
            pl.BlockSpec(memory_space=pl.ANY),
            pl.BlockSpec(memory_space=pl.ANY),
        ],
        out_specs=out_specs,
        out_shape=out_shape,
        scratch_shapes=[
            pltpu.VMEM((d, d_ff), BF16),
            pltpu.VMEM((d, d_ff), BF16),
            pltpu.VMEM((d_ff, d), BF16),
            pltpu.VMEM((FFN_STAGE_SLOTS, d, fc), F32),
            pltpu.VMEM((FFN_STAGE_SLOTS, d, fc), F32),
            pltpu.VMEM((FFN_STAGE_SLOTS, fc, d), F32),
            pltpu.SemaphoreType.DMA((3, FFN_STAGE_SLOTS)),
        ],
        compiler_params=pltpu.CompilerParams(
            dimension_semantics=("arbitrary",), vmem_limit_bytes=VMEM_LIMIT_BYTES),
        name=f"ffn_l{layer}h{half}",
    )(*xs, mods, norm_pre, norm_post, wg, wu, wd)


def _log_sigmoid(x):
    return jnp.minimum(x, 0.0) - jnp.log(1.0 + jnp.exp(-jnp.abs(x)))


def _evin_kernel(x_ref, mod_ref, gpre_ref, gpost_ref, w_ref, waf_ref, wab_ref, baf_ref, bab_ref,
                 qa_ref, ka_ref, va_ref, qb_ref, kb_ref, vb_ref, gb_ref, ldf_ref, ldb_ref,
                 w_bf, wa_bf, *, layer, group):
    @pl.when(pl.program_id(0) == 0)
    def _cast_weights():
        w_bf[0:EV_MAIN, :] = w_ref[0, 0:EV_MAIN, :].astype(BF16)
        w_bf[EV_MAIN:EV_MAIN + LANES, :] = jnp.zeros((LANES, w_bf.shape[1]), BF16)
        w_bf[EV_MAIN:EV_MAIN + 2 * GLA_RANK, :] = w_ref[0, EV_MAIN:EV_MAIN + 2 * GLA_RANK, :].astype(BF16)
        wa_bf[...] = jnp.zeros(wa_bf.shape, BF16)
        wa_bf[0:GLA_RANK, 0:B_QK] = waf_ref[0].astype(BF16)
        wa_bf[GLA_RANK:2 * GLA_RANK, B_QK:2 * B_QK] = wab_ref[0].astype(BF16)

    o = EV_OFFS
    mods = _mod_rows(mod_ref, group(pl.program_id(0)))
    g_pre = _norm_row(gpre_ref, layer, 1)

    def project(rows):
        h = _sub_in(x_ref[rows, :], mods, g_pre, 1).astype(BF16)
        return _dot_nt(h, w_bf[...])

    def finish(rows, full):
        qa_ref[rows, :] = (full[:, o[0]:o[1]] * (HEAD_DIM ** -0.5)).astype(qa_ref.dtype)
        ka_ref[rows, :] = full[:, o[1]:o[2]]
        va_ref[rows, :] = full[:, o[2]:o[3]]
        qb_ref[rows, :] = full[:, o[3]:o[4]] * (GLA_DK ** -0.5)
        kb_ref[rows, :] = full[:, o[4]:o[5]]
        vb_ref[rows, :] = full[:, o[5]:o[6]].astype(vb_ref.dtype)
        gb_ref[rows, :] = full[:, o[6]:o[7]].astype(gb_ref.dtype)
        lr = full[:, EV_MAIN:EV_MAIN + LANES].astype(BF16)
        logits = jnp.dot(lr, wa_bf[...], preferred_element_type=F32)
        ldf_ref[rows, :] = _log_sigmoid(logits[:, 0:B_QK] + baf_ref[...]) * (1.0 / GLA_TAU)
        ldb_ref[rows, :] = _log_sigmoid(logits[:, B_QK:2 * B_QK] + bab_ref[...]) * (1.0 / GLA_TAU)

    pending = None
    for rows in _sub_tiles(x_ref.shape[0]):
        full = project(rows)
        if pending is not None:
            finish(*pending)
        pending = (rows, full)
    finish(*pending)


def _even_in(x, mods, norm_pre, norm_post, w_in_t, wa_f, wa_b, ba_f, ba_b, *, layer, e, group):
    n_tok, d = x.shape
    tm = WIDE_TILE
    ev_in = w_in_t.shape[1]
    rank = wa_f.shape[1]
    assert rank == GLA_RANK and ev_in == EV_MAIN + 2 * rank
    widths = (A_Q, A_KV, A_KV, B_QK, B_QK, B_V, B_V, B_QK, B_QK)
    dtypes = (BF16, F32, F32, F32, F32, BF16, BF16, F32, F32)
    kern = functools.partial(_evin_kernel, layer=layer, group=group)
    return pl.pallas_call(
        kern,
        grid=(n_tok // tm,),
        in_specs=[pl.BlockSpec((tm, d), lambda i: (i, 0))] + _cond_specs(mods, norm_pre, norm_post, layer) + [
            pl.BlockSpec((1, ev_in, d), lambda i: (e, 0, 0), pipeline_mode=pl.Buffered(1)),
            pl.BlockSpec((1, rank, B_QK), lambda i: (e, 0, 0)),
            pl.BlockSpec((1, rank, B_QK), lambda i: (e, 0, 0)),
            pl.BlockSpec((1, B_QK), lambda i: (e, 0)),
            pl.BlockSpec((1, B_QK), lambda i: (e, 0)),
        ],
        out_specs=[pl.BlockSpec((tm, w), lambda i: (i, 0)) for w in widths],
        out_shape=[jax.ShapeDtypeStruct((n_tok, w), t) for w, t in zip(widths, dtypes)],
        scratch_shapes=[pltpu.VMEM((EV_MAIN + LANES, d), BF16), pltpu.VMEM((LANES, 2 * B_QK), BF16)],
        compiler_params=pltpu.CompilerParams(
            dimension_semantics=("arbitrary",), vmem_limit_bytes=VMEM_LIMIT_BYTES),
        name="even_in",
    )(x, mods, norm_pre, norm_post, w_in_t, wa_f, wa_b, ba_f, ba_b)


def _lane_lt(shape, n):
    return lax.broadcasted_iota(jnp.int32, shape, len(shape) - 1) < n


def _dup_kv_head(x, g):
    sw = pltpu.roll(x, HEAD_DIM, 1)
    lo = _lane_lt(x.shape, HEAD_DIM)
    return jnp.where(lo, x, sw) if g == 0 else jnp.where(lo, sw, x)


def _dot_nt(a, b):
    return lax.dot_general(a, b, (((1,), (1,)), ((), ())), preferred_element_type=F32)


ATTN_GROUP = ATTN_HEADS // ATTN_KV_HEADS


def _sink_column(sink_ref, g, t):
    rows = lax.broadcasted_iota(jnp.int32, (ATTN_GROUP * t, 1), 0)
    col = jnp.full((ATTN_GROUP * t, 1), sink_ref[ATTN_GROUP * g], F32)
    for j in range(1, ATTN_GROUP):
        col = jnp.where(rows >= j * t, sink_ref[ATTN_GROUP * g + j], col)
    return col


def _attn_scores(q_pairs, keys):
    t = q_pairs[0].shape[0]
    lo = _lane_lt((t, LANES), HEAD_DIM)
    zero = jnp.zeros((t, LANES), BF16)
    q4 = jnp.concatenate([jnp.where(lo, q_pairs[0], zero), jnp.where(lo, zero, q_pairs[0]),
                          jnp.where(lo, q_pairs[1], zero), jnp.where(lo, zero, q_pairs[1])], axis=0)
    scores = []
    for k2, _, bias, feature_major in keys:
        s = jnp.dot(q4, k2, preferred_element_type=F32) if feature_major else _dot_nt(q4, k2)
        if bias is not None:
            s = s + bias[...]
        scores.append(s)
    return scores


def _attn_finish(scores, sink_col, keys):
    t = scores[0].shape[0] // ATTN_GROUP
    lo = _lane_lt((t, LANES), HEAD_DIM)
    mx = sink_col
    for s in scores:
        mx = jnp.maximum(mx, jnp.max(s, axis=-1, keepdims=True))
    den = jnp.exp(sink_col - mx)
    o = None
    for s, (_, v2, _, feature_major) in zip(scores, keys):
        p = jnp.exp(s - mx)
        den = den + jnp.sum(p, axis=-1, keepdims=True)
        pb = p.astype(BF16)
        pv = _dot_nt(pb, v2) if feature_major else jnp.dot(pb, v2, preferred_element_type=F32)
        o = pv if o is None else o + pv
    o = o / den
    return (jnp.where(lo, o[0:t], o[t:2 * t]), jnp.where(lo, o[2 * t:3 * t], o[3 * t:4 * t]))


def _attn_ctx_kernel(sink_ref, q_ref, k_ref, v_ref, o_ref, kt_ref, vt_ref):
    per_step, _, t = kt_ref.shape
    work = []
    for s in range(per_step):
        rows = slice(s * t, (s + 1) * t)
        k = k_ref[rows, :]
        v = v_ref[rows, :]
        kt_ref[s] = k.T
        vt_ref[s] = v.T
        for g in range(ATTN_KV_HEADS):
            keys = [(_dup_kv_head(k, g).astype(BF16), _dup_kv_head(v, g).astype(BF16), None, False)]
            cols = (2 * g, 2 * g + 1)
            scores = _attn_scores([q_ref[rows, m * LANES:(m + 1) * LANES] for m in cols], keys)
            work.append((rows, cols, scores, keys, g))
    for rows, cols, scores, keys, g in work:
        outs = _attn_finish(scores, _sink_column(sink_ref, g, t), keys)
        for m, o in zip(cols, outs):
            o_ref[rows, m * LANES:(m + 1) * LANES] = o.astype(o_ref.dtype)


def _attn_context(sink, qa, ka, va, *, n_seq, seq):
    per_step = ATTN_PROMPT_SEQS
    assert n_seq % per_step == 0
    n_seq, rows = n_seq // per_step, per_step * seq
    cache_spec = pl.BlockSpec((per_step, A_KV, seq), lambda b: (b, 0, 0))
    return pl.pallas_call(
        _attn_ctx_kernel,
        grid=(n_seq,),
        in_specs=[
            pl.BlockSpec(memory_space=pltpu.SMEM),
            pl.BlockSpec((rows, A_Q), lambda b: (b, 0)),
            pl.BlockSpec((rows, A_KV), lambda b: (b, 0)),
            pl.BlockSpec((rows, A_KV), lambda b: (b, 0)),
        ],
        out_specs=[pl.BlockSpec((rows, A_Q), lambda b: (b, 0)), cache_spec, cache_spec],
        out_shape=[jax.ShapeDtypeStruct((n_seq * rows, A_Q), BF16),
                   jax.ShapeDtypeStruct((n_seq * per_step, A_KV, seq), F32),
                   jax.ShapeDtypeStruct((n_seq * per_step, A_KV, seq), F32)],
        compiler_params=pltpu.CompilerParams(dimension_semantics=("parallel",)),
        name="attn_context",
    )(sink, qa, ka, va)


def _rope_tables(n_tok):
    n_rows = n_tok // GRID_W
    assert n_rows <= GRID_W
    shape = (GRID_W, LANES)
    pos = lax.broadcasted_iota(jnp.int32, shape, 0).astype(F32)
    lane = lax.broadcasted_iota(jnp.int32, shape, 1)
    half = HEAD_DIM // 2
    nf = half // 2
    within = lane & (HEAD_DIM - 1)
    is_col = within >= half
    second = (within & (half - 1)) >= nf
    f = (within & (nf - 1)).astype(F32)
    inv = jnp.exp(f * (-2.0 / half * math.log(ROPE_BASE)))
    ang = pos * inv
    cos_t = jnp.cos(ang)
    sin_t = jnp.sin(ang)
    sin_t = jnp.where(second, sin_t, -sin_t)

    def per_token(tab):
        return jnp.concatenate(
            [jnp.where(is_col, tab, jnp.broadcast_to(tab[r:r + 1, :], shape)) for r in range(n_rows)], axis=0)

    return per_token(cos_t), per_token(sin_t)


def _rope(x, cos, sin_signed):
    nf = HEAD_DIM // 4
    lane = lax.broadcasted_iota(jnp.int32, x.shape, 1)
    second = (lane & (2 * nf - 1)) >= nf
    up = pltpu.roll(x, nf, 1)
    dn = pltpu.roll(x, LANES - nf, 1)
    return x * cos + jnp.where(second, up, dn) * sin_signed


def _rope_bf16(x, cos, sin_signed):
    nf = HEAD_DIM // 4
    src = lax.broadcasted_iota(jnp.int32, (LANES, LANES), 0)
    lane = lax.broadcasted_iota(jnp.int32, (LANES, LANES), 1)
    second = (lane & (2 * nf - 1)) >= nf
    perm = jnp.where(src == jnp.where(second, lane - nf, lane + nf), 1.0, 0.0).astype(BF16)
    return x.astype(F32) * cos + jnp.dot(x, perm, preferred_element_type=F32) * sin_signed


def _attn_lat_kernel(sink_ref, q_ref, k_ref, v_ref, kc_ref, vc_ref, o_ref, qr_ref, kp_ref, vp_ref, bias_ref,
                     *, seq):
    per_step = kc_ref.shape[0]
    nb = seq // ATTN_BLOCK
    blk = ATTN_BLOCK
    cos, sin_signed = _rope_tables(seq)
    zpad = jnp.zeros((blk, LANES), BF16)
    ctx = []
    for s in range(per_step):
        rows = slice(s * seq, (s + 1) * seq)
        for m in range(A_Q // LANES):
            qr_ref[rows, m * LANES:(m + 1) * LANES] = _rope_bf16(
                q_ref[rows, m * LANES:(m + 1) * LANES], cos, sin_signed).astype(BF16)
        kr = _rope(k_ref[rows, :], cos, sin_signed)
        v = v_ref[rows, :]
        for g in range(ATTN_KV_HEADS):
            n = s * ATTN_KV_HEADS + g
            kp_ref[n, 0:blk, :] = zpad
            kp_ref[n, blk + seq:2 * blk + seq, :] = zpad
            vp_ref[n, 0:blk, :] = zpad
            vp_ref[n, blk + seq:2 * blk + seq, :] = zpad
            kp_ref[n, blk:blk + seq, :] = _dup_kv_head(kr, g).astype(BF16)
            vp_ref[n, blk:blk + seq, :] = _dup_kv_head(v, g).astype(BF16)
            hd = slice(g * HEAD_DIM, (g + 1) * HEAD_DIM)
            ctx.append((jnp.concatenate([kc_ref[s, hd, :]] * 2, axis=0).astype(BF16),
                        jnp.concatenate([vc_ref[s, hd, :]] * 2, axis=0).astype(BF16), None, True))
    sink_cols = [_sink_column(sink_ref, g, blk) for g in range(ATTN_KV_HEADS)]
    r = lax.broadcasted_iota(jnp.int32, (ATTN_GROUP * blk, 3 * blk), 0) & (blk - 1)
    c = lax.broadcasted_iota(jnp.int32, (ATTN_GROUP * blk, 3 * blk), 1)
    band = (c >= r) & (c <= r + 2 * WINDOW)
    bias_ref[0] = jnp.where(band & (c >= blk), 0.0, NEG_INF)
    bias_ref[1] = jnp.where(band, 0.0, NEG_INF)
    bias_ref[2] = jnp.where(band & (c < 2 * blk), 0.0, NEG_INF)

    def block_body(i, carry):
        row0 = pl.multiple_of(i * blk, blk)
        edge = jnp.where(i == 0, 0, jnp.where(i == nb - 1, 2, 1))
        work = []
        for s in range(per_step):
            q0 = pl.multiple_of(s * seq + i * blk, blk)
            for g in range(ATTN_KV_HEADS):
                n = s * ATTN_KV_HEADS + g
                cols = (2 * g, 2 * g + 1)
                window = (kp_ref[n, pl.ds(row0, 3 * blk), :], vp_ref[n, pl.ds(row0, 3 * blk), :],
                          bias_ref.at[edge], False)
                keys = [window, ctx[n]]
                scores = _attn_scores([qr_ref[pl.ds(q0, blk), m * LANES:(m + 1) * LANES] for m in cols], keys)
                work.append((q0, cols, scores, keys, g))
        for q0, cols, scores, keys, g in work:
            outs = _attn_finish(scores, sink_cols[g], keys)
            for m, o in zip(cols, outs):
                o_ref[pl.ds(q0, blk), m * LANES:(m + 1) * LANES] = o.astype(o_ref.dtype)
        return carry

    lax.fori_loop(0, nb, block_body, 0)


def _attn_latent(sink, qa, ka, va, k_ctx, v_ctx, *, n_seq, seq, row_block0):
    past = k_ctx.shape[2]
    per_step = ATTN_LATENT_SEQS
    assert n_seq % per_step == 0 and row_block0 % per_step == 0 and seq // ATTN_BLOCK >= 2
    rows = per_step * seq
    first = row_block0 // per_step
    kern = functools.partial(_attn_lat_kernel, seq=seq)
    return pl.pallas_call(
        kern,
        grid=(n_seq // per_step,),
        in_specs=[
            pl.BlockSpec(memory_space=pltpu.SMEM),
            pl.BlockSpec((rows, A_Q), lambda b: (first + b, 0)),
            pl.BlockSpec((rows, A_KV), lambda b: (first + b, 0)),
            pl.BlockSpec((rows, A_KV), lambda b: (first + b, 0)),
            pl.BlockSpec((per_step, A_KV, past), lambda b: (b, 0, 0)),
            pl.BlockSpec((per_step, A_KV, past), lambda b: (b, 0, 0)),
        ],
        out_specs=pl.BlockSpec((rows, A_Q), lambda b: (b, 0)),
        out_shape=jax.ShapeDtypeStruct((n_seq * seq, A_Q), BF16),
        scratch_shapes=[
            pltpu.VMEM((rows, A_Q), BF16),
            pltpu.VMEM((per_step * ATTN_KV_HEADS, seq + 2 * ATTN_BLOCK, LANES), BF16),
            pltpu.VMEM((per_step * ATTN_KV_HEADS, seq + 2 * ATTN_BLOCK, LANES), BF16),
            pltpu.VMEM((3, ATTN_GROUP * ATTN_BLOCK, 3 * ATTN_BLOCK), F32),
        ],
        compiler_params=pltpu.CompilerParams(dimension_semantics=("parallel",)),
        name="attn_latent",
    )(sink, qa, ka, va, k_ctx, v_ctx)


def _same_block(a, b, size):
    return (a & -size) == (b & -size)


def _split3(x):
    hi = x.astype(BF16)
    r1 = x - hi.astype(F32)
    mid = r1.astype(BF16)
    lo = (r1 - mid.astype(F32)).astype(BF16)
    return hi, mid, lo


def _interleave(generators):
    live = list(generators)
    while live:
        for gen in list(live):
            try:
                next(gen)
            except StopIteration:
                live.remove(gen)


def _gla_tile(q, k, v, ld, st_ref, sel_ref, emit, *, rev):
    t = GLA_TILE
    n_pair = GLA_HEADS // 2
    row = lax.broadcasted_iota(jnp.int32, (t, t), 0)
    col = lax.broadcasted_iota(jnp.int32, (t, t), 1)
    tri = (row <= col) if rev else (row >= col)
    tri = jnp.where(tri, 1.0, 0.0).astype(BF16)
    hi, mid, lo = _split3(ld * math.log2(math.e))
    b = (jnp.dot(tri, hi, preferred_element_type=F32) + jnp.dot(tri, mid, preferred_element_type=F32)
         + jnp.dot(tri, lo, preferred_element_type=F32))
    yield
    b_tot = b[0:1, :] if rev else b[t - 1:t, :]
    qs = (q * jnp.exp2(b)).astype(BF16)
    kd = (k * jnp.exp2(b_tot - b)).astype(BF16)

    rowv = lax.broadcasted_iota(jnp.int32, (t, B_QK), 0)
    levels = []
    half = GLA_SUB
    while 2 * half <= t:
        blk = 2 * half
        pieces = []
        for bs in range(0, t, blk):
            r = bs + half if rev else bs + half - 1
            pieces.append(jnp.broadcast_to(b[r:r + 1, :], (blk, B_QK)))
        ref = jnp.concatenate(pieces, axis=0) if len(pieces) > 1 else pieces[0]
        second = (rowv & (blk - 1)) >= half
        is_q = ~second if rev else second
        ql = (q * jnp.exp2(jnp.where(is_q, b - ref, -jnp.inf))).astype(BF16)
        kl = (k * jnp.exp2(jnp.where(is_q, -jnp.inf, ref - b))).astype(BF16)
        levels.append((blk, ql, kl))
        half = blk
    yield

    rr = lax.broadcasted_iota(jnp.int32, (t, 2 * t), 0)
    ss = lax.broadcasted_iota(jnp.int32, (t, 2 * t), 1) & (t - 1)
    sub_rows = lax.broadcasted_iota(jnp.int32, (GLA_SUB, LANES), 0)
    lane_lo = _lane_lt((t, LANES), GLA_DK)
    lane_lo2 = _lane_lt((2 * t, LANES), GLA_DK)
    par_rows = lax.broadcasted_iota(jnp.int32, (2 * t, LANES), 0) < t
    head_sel = par_rows == lane_lo2

    outs = []
    for p in range(n_pair):
        pl_ = slice(p * LANES, (p + 1) * LANES)
        qp, kp, bp = q[:, pl_], k[:, pl_], b[:, pl_]
        rows = []
        for sb in range(t // GLA_SUB):
            r0 = sb * GLA_SUB
            q_sub = qp[r0:r0 + GLA_SUB, :]
            b_sub = bp[r0:r0 + GLA_SUB, :]
            cols = []
            for j in range(GLA_SUB):
                kj = kp[r0 + j:r0 + j + 1, :]
                bj = bp[r0 + j:r0 + j + 1, :]
                valid = (sub_rows <= j) if rev else (sub_rows >= j)
                e = jnp.exp2(jnp.where(valid, b_sub - bj, -jnp.inf))
                cols.append(q_sub * kj * e)
            rows.append(jnp.concatenate(cols, axis=1))
        e_p = jnp.concatenate(rows, axis=0).astype(BF16)
        yield
        sub_scores = jnp.dot(e_p, sel_ref[...], preferred_element_type=F32)
        scores = jnp.zeros_like(sub_scores)
        for blk, ql, kl in reversed(levels):
            klp = kl[:, pl_]
            kstack = jnp.concatenate([klp, klp], axis=0)
            kstack = jnp.where(head_sel, kstack, jnp.zeros_like(kstack))
            s_l = _dot_nt(ql[:, pl_], kstack)
            scores = jnp.where(_same_block(rr, ss, blk), s_l, scores)
        scores = jnp.where(_same_block(rr, ss, GLA_SUB), sub_scores, scores)
        yield
        vp = v[:, p * 2 * GLA_DV:(p + 1) * 2 * GLA_DV]
        v_lo = _lane_lt(vp.shape, GLA_DV)
        vbd = jnp.concatenate([jnp.where(v_lo, vp, 0.0), jnp.where(v_lo, 0.0, vp)], axis=0).astype(BF16)
        st = st_ref[p]
        o_p = (jnp.dot(scores.astype(BF16), vbd, preferred_element_type=F32)
               + _dot_nt(qs[:, pl_], st.astype(BF16)))
        outs.append(o_p)
        upd = jnp.dot(vp.T.astype(BF16), kd[:, pl_], preferred_element_type=F32)
        st_rows_lo = lax.broadcasted_iota(jnp.int32, upd.shape, 0) < GLA_DV
        bd = st_rows_lo == _lane_lt(upd.shape, GLA_DK)
        st_ref[p] = st * jnp.exp2(b_tot[:, pl_]) + jnp.where(bd, upd, 0.0)
        yield
    emit(jnp.concatenate(outs, axis=1))


def _gla_kernel(*refs, seq, has_init, want_state, per_step):
    q_ref, k_ref, v_ref, ldf_ref, ldb_ref = refs[:5]
    pos = 5
    if has_init:
        s0f_ref, s0b_ref = refs[pos:pos + 2]
        pos += 2
    o_ref = refs[pos]
    pos += 1
    if want_state:
        sf_ref, sb_ref = refs[pos:pos + 2]
        pos += 2
    st_ref, sel_ref, acc_ref = refs[pos:pos + 3]
    t = GLA_TILE
    nt = seq // t
    n_pair = GLA_HEADS // 2

    kk = lax.broadcasted_iota(jnp.int32, (GLA_SUB * LANES, 2 * t), 0)
    nn = lax.broadcasted_iota(jnp.int32, (GLA_SUB * LANES, 2 * t), 1)
    sel = ((nn & (GLA_SUB - 1)) == (kk >> LANES_LOG2)) & (((kk & (LANES - 1)) >= GLA_DK) == (nn >= t))
    sel_ref[...] = jnp.where(sel, 1.0, 0.0).astype(BF16)

    def load_state(st, s0_ref, s):
        for p in range(n_pair):
            if s0_ref is None:
                st[p] = jnp.zeros(st.shape[1:], F32)
            else:
                x = jnp.concatenate([s0_ref[s, 2 * p], s0_ref[s, 2 * p + 1]], axis=0)
                xt = x.T
                lo = _lane_lt(xt.shape, GLA_DK)
                st[p] = jnp.concatenate([jnp.where(lo, xt, 0.0), jnp.where(lo, 0.0, xt)], axis=0)

    def store_state(st, out_ref, s):
        for p in range(n_pair):
            z = st[p]
            lo = _lane_lt((GLA_DV, LANES), GLA_DK)
            z = jnp.where(lo, z[0:GLA_DV, :], z[GLA_DV:2 * GLA_DV, :]).T
            out_ref[s, 2 * p] = z[0:GLA_DK, :]
            out_ref[s, 2 * p + 1] = z[GLA_DK:2 * GLA_DK, :]

    walks = [(s, rev, ldb_ref if rev else ldf_ref, st_ref.at[2 * s + int(rev)])
             for s in range(per_step) for rev in (False, True)]
    for s, rev, _, st in walks:
        load_state(st, (s0b_ref if rev else s0f_ref) if has_init else None, s)
    assert nt % 2 == 0

    def make_tile_body(first_visit):
        def tile_body(ti, carry):
            def walk(s, rev, ld_ref, st):
                tile = (nt - 1 - ti) if rev else ti
                r0 = pl.multiple_of(s * seq + tile * t, t)
                rows = pl.ds(r0, t)

                def emit(o):
                    if first_visit:
                        acc_ref[rows, :] = o
                    else:
                        o_ref[rows, :] = (acc_ref[rows, :] + o).astype(o_ref.dtype)

                return _gla_tile(q_ref[rows, :], k_ref[rows, :], v_ref[rows, :].astype(F32), ld_ref[rows, :],
                                 st, sel_ref, emit, rev=rev)

            _interleave([walk(*w) for w in walks])
            return carry
        return tile_body

    lax.fori_loop(0, nt // 2, make_tile_body(True), 0)
    lax.fori_loop(nt // 2, nt, make_tile_body(False), 0)
    if want_state:
        for s, rev, _, st in walks:
            store_state(st, sb_ref if rev else sf_ref, s)


def _gla(qb, kb, vb, ldf, ldb, s0f, s0b, *, n_seq, seq, row_block0, want_state, per_step):
    has_init = s0f is not None
    assert n_seq % per_step == 0 and row_block0 % per_step == 0
    kern = functools.partial(_gla_kernel, seq=seq, has_init=has_init, want_state=want_state,
                             per_step=per_step)
    n_seq, seq_rows, row_block0 = n_seq // per_step, per_step * seq, row_block0 // per_step
    tok = lambda w: pl.BlockSpec((seq_rows, w), lambda b: (row_block0 + b, 0))
    in_specs = [tok(B_QK), tok(B_QK), tok(B_V), tok(B_QK), tok(B_QK)]
    args = [qb, kb, vb, ldf, ldb]
    state_spec = pl.BlockSpec((per_step, GLA_HEADS, GLA_DK, GLA_DV), lambda b: (b, 0, 0, 0))
    if has_init:
        in_specs += [state_spec, state_spec]
        args += [s0f, s0b]
    out_specs = [pl.BlockSpec((seq_rows, B_V), lambda b: (b, 0))]
    out_shape = [jax.ShapeDtypeStruct((n_seq * seq_rows, B_V), BF16)]
    if want_state:
        out_specs += [state_spec, state_spec]
        out_shape += [jax.ShapeDtypeStruct((n_seq * per_step, GLA_HEADS, GLA_DK, GLA_DV), F32)] * 2
    return pl.pallas_call(
        kern,
        grid=(n_seq,),
        in_specs=in_specs,
        out_specs=out_specs,
        out_shape=out_shape,
        scratch_shapes=[
            pltpu.VMEM((2 * per_step, GLA_HEADS // 2, 2 * GLA_DV, LANES), F32),
            pltpu.VMEM((GLA_SUB * LANES, 2 * GLA_TILE), BF16),
            pltpu.VMEM((seq_rows, B_V), F32),
        ],
        compiler_params=pltpu.CompilerParams(
            dimension_semantics=("parallel",), vmem_limit_bytes=VMEM_LIMIT_BYTES),
        name="gla_state" if want_state else "gla_latent",
    )(*args)


def _evout_kernel(x_ref, mod_ref, gpre_ref, gpost_ref, attn_p_ref, attn_s_ref, gla_p_ref, gla_s_ref, gb_ref,
                  gn_ref, w_ref, o_ref, w_bf, *, np_tiles, layer, group):
    @pl.when(pl.program_id(0) == 0)
    def _cast_weights():
        w_bf[...] = w_ref[0].astype(BF16)

    is_prompt = pl.program_id(0) < np_tiles
    mods = _mod_rows(mod_ref, group(pl.program_id(0)))
    g_post = _norm_row(gpost_ref, layer, 1)
    for rows in _sub_tiles(x_ref.shape[0]):
        attn = _pick((attn_p_ref, attn_s_ref), is_prompt, rows)
        gla = _pick((gla_p_ref, gla_s_ref), is_prompt, rows)
        parts = [attn]
        for hd in range(GLA_HEADS):
            sl = slice(hd * GLA_DV, (hd + 1) * GLA_DV)
            g = _rms(gla[:, sl].astype(F32), gn_ref[:, sl]) * _silu(gb_ref[rows, sl].astype(F32))
            parts.append(g.astype(BF16))
        y = jnp.dot(jnp.concatenate(parts, axis=1), w_bf[...], preferred_element_type=F32)
        o_ref[rows, :] = _sub_out(x_ref[rows, :], y, mods, g_post, 1, 1.0)


def _even_out(x, mods, norm_pre, norm_post, attn_ps, gla_ps, gb, gla_norm, w_out, *, layer, e, group,
              np_tiles):
    n_tok, d = x.shape
    tm = WIDE_TILE
    ev_out = w_out.shape[1]
    kern = functools.partial(_evout_kernel, np_tiles=np_tiles, layer=layer, group=group)
    return pl.pallas_call(
        kern,
        grid=(n_tok // tm,),
        in_specs=[pl.BlockSpec((tm, d), lambda i: (i, 0))] + _cond_specs(mods, norm_pre, norm_post, layer) + [
            *_tile_specs(tm, A_Q, np_tiles),
            *_tile_specs(tm, B_V, np_tiles),
            pl.BlockSpec((tm, B_V), lambda i: (i, 0)),
            pl.BlockSpec((1, B_V), lambda i: (e, 0)),
            pl.BlockSpec((1, ev_out, d), lambda i: (e, 0, 0), pipeline_mode=pl.Buffered(1)),
        ],
        out_specs=pl.BlockSpec((tm, d), lambda i: (i, 0)),
        out_shape=jax.ShapeDtypeStruct((n_tok, d), F32),
        scratch_shapes=[pltpu.VMEM((ev_out, d), BF16)],
        compiler_params=pltpu.CompilerParams(
            dimension_semantics=("arbitrary",), vmem_limit_bytes=VMEM_LIMIT_BYTES),
        name="even_out",
    )(x, mods, norm_pre, norm_post, *attn_ps, *gla_ps, gb, gla_norm, w_out)


def _gelu(x):
    return 0.5 * x * (1.0 + lax.erf(x * (2.0 ** -0.5)))


def _cm_kernel(x_ref, mod_ref, gpre_ref, gpost_ref, win_ref, vg_ref, vb_ref, ws_ref, bs_ref, wout_ref,
               o_ref, win_bf, wout_bf, *, layer, group):
    @pl.when(pl.program_id(0) == 0)
    def _cast_weights():
        win_bf[...] = win_ref[0].astype(BF16)
        wout_bf[...] = wout_ref[0].astype(BF16)

    width = wout_bf.shape[0]
    gw = width // CMLP_GROUPS
    ws = [ws_ref[0, g].astype(BF16) for g in range(CMLP_GROUPS)]
    mods = _mod_rows(mod_ref, group(pl.program_id(0)))
    g_pre = _norm_row(gpre_ref, layer, 1)
    g_post = _norm_row(gpost_ref, layer, 1)
    eye = (lax.broadcasted_iota(jnp.int32, (CHUNK, CHUNK), 0)
           == lax.broadcasted_iota(jnp.int32, (CHUNK, CHUNK), 1))
    bias_cols = [jnp.sum(jnp.where(eye, jnp.broadcast_to(bs_ref[0, g:g + 1, :], (CHUNK, CHUNK)), 0.0),
                         axis=1, keepdims=True) for g in range(CMLP_GROUPS)]

    def phase_in(rows):
        x = x_ref[rows, :]
        h = _sub_in(x, mods, g_pre, 1).astype(BF16)
        v = _gelu(jnp.dot(h, win_bf[:, width:2 * width], preferred_element_type=F32))
        u = _gelu(jnp.dot(h, win_bf[:, 0:width], preferred_element_type=F32))
        return x, u, v

    def phase_mix(x, u, v):
        mu = jnp.mean(v, axis=-1, keepdims=True)
        vc = v - mu
        var = jnp.mean(vc * vc, axis=-1, keepdims=True)
        vn = (vc * lax.rsqrt(var + EPS) * vg_ref[...] + vb_ref[...]).astype(BF16)
        chunk_rows = []
        for c in range(x.shape[0] // CHUNK):
            cols = []
            for g in range(CMLP_GROUPS):
                blk = vn[c * CHUNK:(c + 1) * CHUNK, g * gw:(g + 1) * gw]
                cols.append(jnp.dot(ws[g], blk, preferred_element_type=F32) + bias_cols[g])
            chunk_rows.append(jnp.concatenate(cols, axis=1))
        mixed = jnp.concatenate(chunk_rows, axis=0)
        return x, (u * mixed).astype(BF16)

    def phase_out(rows, x, m):
        y = jnp.dot(m, wout_bf[...], preferred_element_type=F32)
        o_ref[rows, :] = _sub_out(x, y, mods, g_post, 1, 1.0)

    tiles = _sub_tiles(x_ref.shape[0])
    pending = None
    for rows in tiles:
        cur = phase_in(rows)
        if pending is not None:
            phase_out(pending[0], *phase_mix(*pending[1]))
        pending = (rows, cur)
    phase_out(pending[0], *phase_mix(*pending[1]))


def _chunk_mlp(x, mods, norm_pre, norm_post, w_in, v_gain, v_bias, w_s, b_s, w_out, *, layer, o, group):
    n_tok, d = x.shape
    tm = WIDE_TILE
    width = w_out.shape[1]
    kern = functools.partial(_cm_kernel, layer=layer, group=group)
    return pl.pallas_call(
        kern,
        grid=(n_tok // tm,),
        in_specs=[pl.BlockSpec((tm, d), lambda i: (i, 0))] + _cond_specs(mods, norm_pre, norm_post, layer) + [
            pl.BlockSpec((1, d, 2 * width), lambda i: (o, 0, 0), pipeline_mode=pl.Buffered(1)),
            pl.BlockSpec((1, width), lambda i: (o, 0)),
            pl.BlockSpec((1, width), lambda i: (o, 0)),
            pl.BlockSpec((1, CMLP_GROUPS, CHUNK, CHUNK), lambda i: (o, 0, 0, 0)),
            pl.BlockSpec((1, CMLP_GROUPS, CHUNK), lambda i: (o, 0, 0)),
            pl.BlockSpec((1, width, d), lambda i: (o, 0, 0), pipeline_mode=pl.Buffered(1)),
        ],
        out_specs=pl.BlockSpec((tm, d), lambda i: (i, 0)),
        out_shape=jax.ShapeDtypeStruct((n_tok, d), F32),
        scratch_shapes=[pltpu.VMEM((d, 2 * width), BF16), pltpu.VMEM((width, d), BF16)],
        compiler_params=pltpu.CompilerParams(
            dimension_semantics=("arbitrary",), vmem_limit_bytes=VMEM_LIMIT_BYTES),
        name="chunk_mlp",
    )(x, mods, norm_pre, norm_post, w_in, v_gain, v_bias, w_s, b_s, w_out)


def kernel(x_prompt, x_sample, cache_k, cache_v, state_gla_fwd, state_gla_bwd, c, c_ctx, w_mod, b_mod, norm_pre, norm_post, ffn_w_gate, ffn_w_up, ffn_w_down, ev_w_in, ev_w_out, ev_sink, gla_wa_f, gla_ba_f, gla_wa_b, gla_ba_b, gla_norm, cm_w_in, cm_v_gain, cm_v_bias, cm_w_s, cm_b_s, cm_w_out):
    batch, seq, d = x_prompt.shape
    dec_batch, dec_seq, _ = x_sample.shape
    depth = w_mod.shape[0]
    n_prompt_tok = batch * seq
    n_sample_tok = dec_batch * dec_seq
    assert n_prompt_tok % dec_seq == 0 and dec_seq % TOKEN_TILE == 0
    group = _group_index_map(TOKEN_TILE, n_prompt_tok, dec_seq)
    np_tiles = n_prompt_tok // TOKEN_TILE
    ffn = functools.partial(_ffn, group=group, np_tiles=np_tiles)
    assert dec_seq % WIDE_TILE == 0
    group_w = _group_index_map(WIDE_TILE, n_prompt_tok, dec_seq)
    np_tiles_w = n_prompt_tok // WIDE_TILE

    xs = (x_prompt.reshape(n_prompt_tok, d), x_sample.reshape(n_sample_tok, d))
    ev_w_in_t = jnp.swapaxes(ev_w_in, 1, 2)

    mods = _adaln_mods(c_ctx, c, w_mod, b_mod)
    gains_pre = norm_pre.reshape(depth * N_SUB, d)
    gains_post = norm_post.reshape(depth * N_SUB, d)
    cond = (mods, gains_pre, gains_post)

    new_k, new_v, new_sf, new_sb = [], [], [], []
    for layer in range(depth):
        (x,) = ffn(xs, *cond, ffn_w_gate, ffn_w_up, ffn_w_down,
                   sub=0, layer=layer, half=0, split_out=False)
        if layer % 2 == 0:
            e = layer // 2
            qa, ka, va, qb, kb, vb, gb, ldf, ldb = _even_in(
                x, *cond, ev_w_in_t, gla_wa_f, gla_wa_b, gla_ba_f, gla_ba_b,
                layer=layer, e=e, group=group_w)
            sink = ev_sink[e]
            attn_p, k_t, v_t = _attn_context(sink, qa, ka, va, n_seq=batch, seq=seq)
            k_ctx = jnp.transpose(cache_k[:, e], (0, 2, 3, 1)).reshape(dec_batch, A_KV, -1)
            v_ctx = jnp.transpose(cache_v[:, e], (0, 2, 3, 1)).reshape(dec_batch, A_KV, -1)
            attn_s = _attn_latent(sink, qa, ka, va, k_ctx, v_ctx, n_seq=dec_batch, seq=dec_seq,
                                  row_block0=n_prompt_tok // dec_seq)
            gla_p, s_f, s_b = _gla(qb, kb, vb, ldf, ldb, None, None, n_seq=batch, seq=seq,
                                   row_block0=0, want_state=True, per_step=GLA_PROMPT_SEQS)
            (gla_s,) = _gla(qb, kb, vb, ldf, ldb, state_gla_fwd[:, e], state_gla_bwd[:, e],
                            n_seq=dec_batch, seq=dec_seq, row_block0=n_prompt_tok // dec_seq,
                            want_state=False, per_step=GLA_LATENT_SEQS)
            x = _even_out(x, *cond, (attn_p, attn_s), (gla_p, gla_s), gb, gla_norm, ev_w_out,
                          layer=layer, e=e, group=group_w, np_tiles=np_tiles_w)
            for cache_t, dst in ((k_t, new_k), (v_t, new_v)):
                dst.append(jnp.transpose(cache_t.reshape(batch, ATTN_KV_HEADS, HEAD_DIM, seq), (0, 3, 1, 2)))
            new_sf.append(s_f)
            new_sb.append(s_b)
        else:
            o = layer // 2
            x = _chunk_mlp(x, *cond, cm_w_in, cm_v_gain, cm_v_bias, cm_w_s, cm_b_s, cm_w_out,
                           layer=layer, o=o, group=group_w)
        xs = ffn((x,), *cond, ffn_w_gate, ffn_w_up, ffn_w_down,
                 sub=2, layer=layer, half=1, split_out=(layer == depth - 1))

    y_prompt = xs[0].reshape(batch, seq, d)
    y_sample = xs[1].reshape(dec_batch, dec_seq, d)
    return (y_prompt, y_sample, jnp.stack(new_k, axis=1), jnp.stack(new_v, axis=1),
            jnp.stack(new_sf, axis=1), jnp.stack(new_sb, axis=1))
```

```python
import functools
import math

import jax
import jax.numpy as jnp
from jax import lax
from jax.experimental import pallas as pl
from jax.experimental.pallas import tpu as pltpu

F32 = jnp.float32
BF16 = jnp.bfloat16

EPS = 1e-6
NEG_INF = -1e30
FFN_RESIDUAL = 0.5
N_SUB = 3

ATTN_HEADS = 8
ATTN_KV_HEADS = 2
HEAD_DIM = 64
ATTN_BLOCK = 128
WINDOW = 128
GRID_W = 64
GRID_W_LOG2 = 6
ROPE_BASE = 10000.0
GLA_HEADS = 4
GLA_DK = 64
GLA_DV = 128
GLA_RANK = 16
GLA_TAU = 16.0
CHUNK = 128
CMLP_GROUPS = 4

A_Q = ATTN_HEADS * HEAD_DIM
A_KV = ATTN_KV_HEADS * HEAD_DIM
B_QK = GLA_HEADS * GLA_DK
B_V = GLA_HEADS * GLA_DV
EV_MAIN = A_Q + 2 * A_KV + 2 * B_QK + 2 * B_V
EV_OFFS = (0, A_Q, A_Q + A_KV, A_Q + 2 * A_KV, A_Q + 2 * A_KV + B_QK,
           A_Q + 2 * A_KV + 2 * B_QK, A_Q + 2 * A_KV + 2 * B_QK + B_V, EV_MAIN)

LANES = 128
LANES_LOG2 = 7
SUBLANES = 8
VMEM_LIMIT_BYTES = 56 * 1024 * 1024

TOKEN_TILE = 512
WIDE_TILE = 1024
SUB_ROWS = 512
ADALN_BUFFERS = 4
FFN_CHUNK = 256
FFN_STAGE_SLOTS = 3
GLA_TILE = 128
GLA_SUB = 8
GLA_PROMPT_SEQS = 4
GLA_LATENT_SEQS = 2
ATTN_PROMPT_SEQS = 2
ATTN_LATENT_SEQS = 2


def _group_index_map(tile_rows, n_prompt_tok, seq_tok):
    def group(i):
        return jnp.maximum(i * tile_rows - (n_prompt_tok - seq_tok), 0) // seq_tok
    return group


def _sub_tiles(n_rows):
    assert n_rows % SUB_ROWS == 0
    return [slice(r, r + SUB_ROWS) for r in range(0, n_rows, SUB_ROWS)]


def _rms(x, g):
    ms = jnp.mean(x * x, axis=-1, keepdims=True)
    return x * lax.rsqrt(ms + EPS) * g


def _mod_rows(mod_ref, g):
    return [mod_ref[k, pl.ds(g, 1), :] for k in range(3 * N_SUB)]


def _sub_in(x, mods, g_pre, sub):
    shift, scale = mods[3 * sub + 0], mods[3 * sub + 1]
    return _rms(x, g_pre * (1.0 + scale)) + shift


def _sub_out(x, y, mods, g_post, sub, coef):
    return x + _rms(y, (coef * mods[3 * sub + 2]) * g_post)


def _norm_row(ref, layer, sub):
    r = layer * N_SUB + sub
    return ref[r:r + 1, :]


def _silu(x):
    return x * jax.nn.sigmoid(x)


def _mod_kernel(cctx_ref, c_ref, w_hbm, b_ref, o_ref, buf, sem):
    depth, n_kinds, rows, d = o_ref.shape
    r = lax.broadcasted_iota(jnp.int32, (rows, d), 0)
    cond = jnp.where(r == 0, jnp.broadcast_to(cctx_ref[...], (rows, d)), 0.0)
    for j in range(c_ref.shape[0]):
        cond = jnp.where(r == 1 + j, jnp.broadcast_to(c_ref[j:j + 1, :], (rows, d)), cond)
    s = _silu(cond).astype(BF16)
    n_chunks = depth * n_kinds
    n_buf = buf.shape[0]

    def copy(i):
        l, q = i // n_kinds, i % n_kinds
        col0 = q * d if isinstance(i, int) else pl.multiple_of(q * d, LANES)
        return pltpu.make_async_copy(w_hbm.at[l, :, pl.ds(col0, d)], buf.at[i % n_buf], sem.at[i % n_buf])

    for i in range(min(n_buf, n_chunks)):
        copy(i).start()

    def chunk_body(i, carry):
        l, q = i // n_kinds, i % n_kinds
        copy(i).wait()
        res = jnp.dot(s, buf[i % n_buf].astype(BF16), preferred_element_type=F32)
        o_ref[l, q] = res + b_ref[pl.ds(l, 1), pl.ds(pl.multiple_of(q * d, LANES), d)]

        @pl.when(i + n_buf < n_chunks)
        def _refill():
            copy(i + n_buf).start()

        return carry

    lax.fori_loop(0, n_chunks, chunk_body, 0)


def _adaln_mods(c_ctx, c, w_mod, b_mod):
    depth, d, n = w_mod.shape
    assert n % d == 0
    rows = -(-(1 + c.shape[0]) // SUBLANES) * SUBLANES
    return pl.pallas_call(
        _mod_kernel,
        grid=(1,),
        in_specs=[
            pl.BlockSpec((1, d), lambda i: (0, 0)),
            pl.BlockSpec(c.shape, lambda i: (0, 0)),
            pl.BlockSpec(memory_space=pl.ANY),
            pl.BlockSpec((depth, n), lambda i: (0, 0)),
        ],
        out_specs=pl.BlockSpec((depth, n // d, rows, d), lambda i: (0, 0, 0, 0)),
        out_shape=jax.ShapeDtypeStruct((depth, n // d, rows, d), F32),
        scratch_shapes=[pltpu.VMEM((ADALN_BUFFERS, d, d), F32), pltpu.SemaphoreType.DMA((ADALN_BUFFERS,))],
        compiler_params=pltpu.CompilerParams(
            dimension_semantics=("arbitrary",), vmem_limit_bytes=VMEM_LIMIT_BYTES),
        name="adaln_mods",
    )(c_ctx.reshape(1, d), c, w_mod, b_mod)


def _tile_specs(tm, width, np_tiles):
    return [pl.BlockSpec((tm, width), lambda i: (jnp.minimum(i, np_tiles - 1), 0)),
            pl.BlockSpec((tm, width), lambda i: (jnp.maximum(i - np_tiles, 0), 0))]


def _pick(refs, is_prompt, rows=slice(None)):
    if len(refs) == 1:
        return refs[0][rows, :]
    return jnp.where(is_prompt, refs[0][rows, :], refs[1][rows, :])


def _cond_specs(mods, norm_pre, norm_post, layer):
    return [pl.BlockSpec((None,) + mods.shape[1:], lambda i: (layer, 0, 0, 0)),
            pl.BlockSpec(norm_pre.shape, lambda i: (0, 0)),
            pl.BlockSpec(norm_post.shape, lambda i: (0, 0))]


def _ffn_kernel(*refs, sub, layer, half, n_chunks, n_x, n_o, np_tiles, group):
    x_refs = refs[:n_x]
    mod_ref, gpre_ref, gpost_ref, wg_hbm, wu_hbm, wd_hbm = refs[n_x:n_x + 6]
    o_refs = refs[n_x + 6:n_x + 6 + n_o]
    wg_bf, wu_bf, wd_bf, st_g, st_u, st_d, sem = refs[n_x + 6 + n_o:]
    fc = FFN_CHUNK
    step = pl.program_id(0)
    mods = _mod_rows(mod_ref, group(step))
    g_pre = _norm_row(gpre_ref, layer, sub)
    g_post = _norm_row(gpost_ref, layer, sub)

    def chunk_copies(c, slot):
        cols = pl.ds(c * fc, fc)
        return (
            pltpu.make_async_copy(wg_hbm.at[layer, half, :, cols], st_g.at[slot], sem.at[0, slot]),
            pltpu.make_async_copy(wu_hbm.at[layer, half, :, cols], st_u.at[slot], sem.at[1, slot]),
            pltpu.make_async_copy(wd_hbm.at[layer, half, cols, :], st_d.at[slot], sem.at[2, slot]),
        )

    def run(stage_weights, prompt):
        n_slots = st_g.shape[0]
        if stage_weights:
            for c in range(min(n_slots - 1, n_chunks)):
                for cp in chunk_copies(c, c):
                    cp.start()
        x = _pick(x_refs, step < np_tiles)
        h = _sub_in(x, mods, g_pre, sub).astype(BF16)
        y = jnp.zeros(x.shape, F32)
        for c in range(n_chunks):
            sl = slice(c * fc, (c + 1) * fc)
            if stage_weights:
                slot = c % n_slots
                ahead = c + n_slots - 1
                if ahead < n_chunks:
                    for cp in chunk_copies(ahead, ahead % n_slots):
                        cp.start()
                for cp in chunk_copies(c, slot):
                    cp.wait()
                wg_bf[:, sl] = st_g[slot].astype(BF16)
                wu_bf[:, sl] = st_u[slot].astype(BF16)
                wd_bf[sl, :] = st_d[slot].astype(BF16)
            g = jnp.dot(h, wg_bf[:, sl], preferred_element_type=F32)
            u = jnp.dot(h, wu_bf[:, sl], preferred_element_type=F32)
            a = (_silu(g) * u).astype(BF16)
            y = y + jnp.dot(a, wd_bf[sl, :], preferred_element_type=F32)
        o_refs[0 if (prompt or n_o == 1) else 1][...] = _sub_out(x, y, mods, g_post, sub, FFN_RESIDUAL)

    @pl.when(step == 0)
    def _first_tile():
        run(True, True)

    if n_o == 1:
        @pl.when(step > 0)
        def _other_tiles():
            run(False, True)
    else:
        @pl.when(jnp.logical_and(step > 0, step < np_tiles))
        def _prompt_tiles():
            run(False, True)

        @pl.when(step >= np_tiles)
        def _latent_tiles():
            run(False, False)


def _ffn(xs, mods, norm_pre, norm_post, wg, wu, wd, *, sub, layer, half, group, np_tiles, split_out):
    d = xs[0].shape[-1]
    n_tok = sum(x.shape[0] for x in xs)
    d_ff = wg.shape[-1]
    tm = TOKEN_TILE
    fc = FFN_CHUNK
    n_chunks = d_ff // fc
    assert n_chunks * fc == d_ff and n_tok % tm == 0
    n_x, n_o = len(xs), (2 if split_out else 1)
    kern = functools.partial(_ffn_kernel, sub=sub, layer=layer, half=half, n_chunks=n_chunks,
                             n_x=n_x, n_o=n_o, np_tiles=np_tiles, group=group)
    whole = pl.BlockSpec((tm, d), lambda i: (i, 0))
    x_specs = _tile_specs(tm, d, np_tiles) if n_x == 2 else [whole]
    n_p = np_tiles * tm
    if split_out:
        out_specs = _tile_specs(tm, d, np_tiles)
        out_shape = [jax.ShapeDtypeStruct((n_p, d), F32), jax.ShapeDtypeStruct((n_tok - n_p, d), F32)]
    else:
        out_specs = [whole]
        out_shape = [jax.ShapeDtypeStruct((n_tok, d), F32)]
    return pl.pallas_call(
        kern,
        grid=(n_tok // tm,),
        in_specs=x_specs + _cond_specs(mods, norm_pre, norm_post, layer) + [
            pl.BlockSpec(memory_space=pl.ANY),
            pl.BlockSpec(memory_space=pl.ANY),
            pl.BlockSpec(memory_space=pl.ANY),
        ],
        out_specs=out_specs,
        out_shape=out_shape,
        scratch_shapes=[
            pltpu.VMEM((d, d_ff), BF16),
            pltpu.VMEM((d, d_ff), BF16),
            pltpu.VMEM((d_ff, d), BF16),
            pltpu.VMEM((FFN_STAGE_SLOTS, d, fc), F32),
            pltpu.VMEM((FFN_STAGE_SLOTS, d, fc), F32),
            pltpu.VMEM((FFN_STAGE_SLOTS, fc, d), F32),
            pltpu.SemaphoreType.DMA((3, FFN_STAGE_SLOTS)),
        ],
        compiler_params=pltpu.CompilerParams(
            dimension_semantics=("arbitrary",), vmem_limit_bytes=VMEM_LIMIT_BYTES),
        name=f"ffn_l{layer}h{half}",
    )(*xs, mods, norm_pre, norm_post, wg, wu, wd)


def _log_sigmoid(x):
    return jnp.minimum(x, 0.0) - jnp.log(1.0 + jnp.exp(-jnp.abs(x)))


def _evin_kernel(x_ref, mod_ref, gpre_ref, gpost_ref, w_ref, waf_ref, wab_ref, baf_ref, bab_ref,
                 qa_ref, ka_ref, va_ref, qb_ref, kb_ref, vb_ref, gb_ref, ldf_ref, ldb_ref,
                 w_bf, wa_bf, *, layer, group):
    @pl.when(pl.program_id(0) == 0)
    def _cast_weights():
        w_bf[0:EV_MAIN, :] = w_ref[0, 0:EV_MAIN, :].astype(BF16)
        w_bf[EV_MAIN:EV_MAIN + LANES, :] = jnp.zeros((LANES, w_bf.shape[1]), BF16)
        w_bf[EV_MAIN:EV_MAIN + 2 * GLA_RANK, :] = w_ref[0, EV_MAIN:EV_MAIN + 2 * GLA_RANK, :].astype(BF16)
        wa_bf[...] = jnp.zeros(wa_bf.shape, BF16)
        wa_bf[0:GLA_RANK, 0:B_QK] = waf_ref[0].astype(BF16)
        wa_bf[GLA_RANK:2 * GLA_RANK, B_QK:2 * B_QK] = wab_ref[0].astype(BF16)

    o = EV_OFFS
    mods = _mod_rows(mod_ref, group(pl.program_id(0)))
    g_pre = _norm_row(gpre_ref, layer, 1)

    def project(rows):
        h = _sub_in(x_ref[rows, :], mods, g_pre, 1).astype(BF16)
        return _dot_nt(h, w_bf[...])

    def finish(rows, full):
        qa_ref[rows, :] = (full[:, o[0]:o[1]] * (HEAD_DIM ** -0.5)).astype(qa_ref.dtype)
        ka_ref[rows, :] = full[:, o[1]:o[2]]
        va_ref[rows, :] = full[:, o[2]:o[3]]
        qb_ref[rows, :] = full[:, o[3]:o[4]] * (GLA_DK ** -0.5)
        kb_ref[rows, :] = full[:, o[4]:o[5]]
        vb_ref[rows, :] = full[:, o[5]:o[6]].astype(vb_ref.dtype)
        gb_ref[rows, :] = full[:, o[6]:o[7]].astype(gb_ref.dtype)
        lr = full[:, EV_MAIN:EV_MAIN + LANES].astype(BF16)
        logits = jnp.dot(lr, wa_bf[...], preferred_element_type=F32)
        ldf_ref[rows, :] = _log_sigmoid(logits[:, 0:B_QK] + baf_ref[...]) * (1.0 / GLA_TAU)
        ldb_ref[rows, :] = _log_sigmoid(logits[:, B_QK:2 * B_QK] + bab_ref[...]) * (1.0 / GLA_TAU)

    pending = None
    for rows in _sub_tiles(x_ref.shape[0]):
        full = project(rows)
        if pending is not None:
            finish(*pending)
        pending = (rows, full)
    finish(*pending)


def _even_in(x, mods, norm_pre, norm_post, w_in_t, wa_f, wa_b, ba_f, ba_b, *, layer, e, group):
    n_tok, d = x.shape
    tm = WIDE_TILE
    ev_in = w_in_t.shape[1]
    rank = wa_f.shape[1]
    assert rank == GLA_RANK and ev_in == EV_MAIN + 2 * rank
    widths = (A_Q, A_KV, A_KV, B_QK, B_QK, B_V, B_V, B_QK, B_QK)
    dtypes = (BF16, F32, F32, F32, F32, BF16, BF16, F32, F32)
    kern = functools.partial(_evin_kernel, layer=layer, group=group)
    return pl.pallas_call(
        kern,
        grid=(n_tok // tm,),
        in_specs=[pl.BlockSpec((tm, d), lambda i: (i, 0))] + _cond_specs(mods, norm_pre, norm_post, layer) + [
            pl.BlockSpec((1, ev_in, d), lambda i: (e, 0, 0), pipeline_mode=pl.Buffered(1)),
            pl.BlockSpec((1, rank, B_QK), lambda i: (e, 0, 0)),
            pl.BlockSpec((1, rank, B_QK), lambda i: (e, 0, 0)),
            pl.BlockSpec((1, B_QK), lambda i: (e, 0)),
            pl.BlockSpec((1, B_QK), lambda i: (e, 0)),
        ],
        out_specs=[pl.BlockSpec((tm, w), lambda i: (i, 0)) for w in widths],
        out_shape=[jax.ShapeDtypeStruct((n_tok, w), t) for w, t in zip(widths, dtypes)],
        scratch_shapes=[pltpu.VMEM((EV_MAIN + LANES, d), BF16), pltpu.VMEM((LANES, 2 * B_QK), BF16)],
        compiler_params=pltpu.CompilerParams(
            dimension_semantics=("arbitrary",), vmem_limit_bytes=VMEM_LIMIT_BYTES),
        name="even_in",
    )(x, mods, norm_pre, norm_post, w_in_t, wa_f, wa_b, ba_f, ba_b)


def _lane_lt(shape, n):
    return lax.broadcasted_iota(jnp.int32, shape, len(shape) - 1) < n


def _dup_kv_head(x, g):
    sw = pltpu.roll(x, HEAD_DIM, 1)
    lo = _lane_lt(x.shape, HEAD_DIM)
    return jnp.where(lo, x, sw) if g == 0 else jnp.where(lo, sw, x)


def _dot_nt(a, b):
    return lax.dot_general(a, b, (((1,), (1,)), ((), ())), preferred_element_type=F32)


ATTN_GROUP = ATTN_HEADS // ATTN_KV_HEADS


def _sink_column(sink_ref, g, t):
    rows = lax.broadcasted_iota(jnp.int32, (ATTN_GROUP * t, 1), 0)
    col = jnp.full((ATTN_GROUP * t, 1), sink_ref[ATTN_GROUP * g], F32)
    for j in range(1, ATTN_GROUP):
        col = jnp.where(rows >= j * t, sink_ref[ATTN_GROUP * g + j], col)
    return col


def _attn_scores(q_pairs, keys):
    t = q_pairs[0].shape[0]
    lo = _lane_lt((t, LANES), HEAD_DIM)
    zero = jnp.zeros((t, LANES), BF16)
    q4 = jnp.concatenate([jnp.where(lo, q_pairs[0], zero), jnp.where(lo, zero, q_pairs[0]),
                          jnp.where(lo, q_pairs[1], zero), jnp.where(lo, zero, q_pairs[1])], axis=0)
    scores = []
    for k2, _, bias, feature_major in keys:
        s = jnp.dot(q4, k2, preferred_element_type=F32) if feature_major else _dot_nt(q4, k2)
        if bias is not None:
            s = s + bias[...]
        scores.append(s)
    return scores


def _attn_finish(scores, sink_col, keys):
    t = scores[0].shape[0] // ATTN_GROUP
    lo = _lane_lt((t, LANES), HEAD_DIM)
    mx = sink_col
    for s in scores:
        mx = jnp.maximum(mx, jnp.max(s, axis=-1, keepdims=True))
    den = jnp.exp(sink_col - mx)
    o = None
    for s, (_, v2, _, feature_major) in zip(scores, keys):
        p = jnp.exp(s - mx)
        den = den + jnp.sum(p, axis=-1, keepdims=True)
        pb = p.astype(BF16)
        pv = _dot_nt(pb, v2) if feature_major else jnp.dot(pb, v2, preferred_element_type=F32)
        o = pv if o is None else o + pv
    o = o / den
    return (jnp.where(lo, o[0:t], o[t:2 * t]), jnp.where(lo, o[2 * t:3 * t], o[3 * t:4 * t]))


def _attn_ctx_kernel(sink_ref, q_ref, k_ref, v_ref, o_ref, kt_ref, vt_ref):
    per_step, _, t = kt_ref.shape
    work = []
    for s in range(per_step):
        rows = slice(s * t, (s + 1) * t)
        k = k_ref[rows, :]
        v = v_ref[rows, :]
        kt_ref[s] = k.T
        vt_ref[s] = v.T
        for g in range(ATTN_KV_HEADS):
            keys = [(_dup_kv_head(k, g).astype(BF16), _dup_kv_head(v, g).astype(BF16), None, False)]
            cols = (2 * g, 2 * g + 1)
            scores = _attn_scores([q_ref[rows, m * LANES:(m + 1) * LANES] for m in cols], keys)
            work.append((rows, cols, scores, keys, g))
    for rows, cols, scores, keys, g in work:
        outs = _attn_finish(scores, _sink_column(sink_ref, g, t), keys)
        for m, o in zip(cols, outs):
            o_ref[rows, m * LANES:(m + 1) * LANES] = o.astype(o_ref.dtype)


def _attn_context(sink, qa, ka, va, *, n_seq, seq):
    per_step = ATTN_PROMPT_SEQS
    assert n_seq % per_step == 0
    n_seq, rows = n_seq // per_step, per_step * seq
    cache_spec = pl.BlockSpec((per_step, A_KV, seq), lambda b: (b, 0, 0))
    return pl.pallas_call(
        _attn_ctx_kernel,
        grid=(n_seq,),
        in_specs=[
            pl.BlockSpec(memory_space=pltpu.SMEM),
            pl.BlockSpec((rows, A_Q), lambda b: (b, 0)),
            pl.BlockSpec((rows, A_KV), lambda b: (b, 0)),
            pl.BlockSpec((rows, A_KV), lambda b: (b, 0)),
        ],
        out_specs=[pl.BlockSpec((rows, A_Q), lambda b: (b, 0)), cache_spec, cache_spec],
        out_shape=[jax.ShapeDtypeStruct((n_seq * rows, A_Q), BF16),
                   jax.ShapeDtypeStruct((n_seq * per_step, A_KV, seq), F32),
                   jax.ShapeDtypeStruct((n_seq * per_step, A_KV, seq), F32)],
        compiler_params=pltpu.CompilerParams(dimension_semantics=("parallel",)),
        name="attn_context",
    )(sink, qa, ka, va)


def _rope_tables(n_tok):
    n_rows = n_tok // GRID_W
    assert n_rows <= GRID_W
    shape = (GRID_W, LANES)
    pos = lax.broadcasted_iota(jnp.int32, shape, 0).astype(F32)
    lane = lax.broadcasted_iota(jnp.int32, shape, 1)
    half = HEAD_DIM // 2
    nf = half // 2
    within = lane & (HEAD_DIM - 1)
    is_col = within >= half
    second = (within & (half - 1)) >= nf
    f = (within & (nf - 1)).astype(F32)
    inv = jnp.exp(f * (-2.0 / half * math.log(ROPE_BASE)))
    ang = pos * inv
    cos_t = jnp.cos(ang)
    sin_t = jnp.sin(ang)
    sin_t = jnp.where(second, sin_t, -sin_t)

    def per_token(tab):
        return jnp.concatenate(
            [jnp.where(is_col, tab, jnp.broadcast_to(tab[r:r + 1, :], shape)) for r in range(n_rows)], axis=0)

    return per_token(cos_t), per_token(sin_t)


def _rope(x, cos, sin_signed):
    nf = HEAD_DIM // 4
    lane = lax.broadcasted_iota(jnp.int32, x.shape, 1)
    second = (lane & (2 * nf - 1)) >= nf
    up = pltpu.roll(x, nf, 1)
    dn = pltpu.roll(x, LANES - nf, 1)
    return x * cos + jnp.where(second, up, dn) * sin_signed


def _rope_bf16(x, cos, sin_signed):
    nf = HEAD_DIM // 4
    src = lax.broadcasted_iota(jnp.int32, (LANES, LANES), 0)
    lane = lax.broadcasted_iota(jnp.int32, (LANES, LANES), 1)
    second = (lane & (2 * nf - 1)) >= nf
    perm = jnp.where(src == jnp.where(second, lane - nf, lane + nf), 1.0, 0.0).astype(BF16)
    return x.astype(F32) * cos + jnp.dot(x, perm, preferred_element_type=F32) * sin_signed


def _attn_lat_kernel(sink_ref, q_ref, k_ref, v_ref, kc_ref, vc_ref, o_ref, qr_ref, kp_ref, vp_ref, bias_ref,
                     *, seq):
    per_step = kc_ref.shape[0]
    nb = seq // ATTN_BLOCK
    blk = ATTN_BLOCK
    cos, sin_signed = _rope_tables(seq)
    zpad = jnp.zeros((blk, LANES), BF16)
    ctx = []
    for s in range(per_step):
        rows = slice(s * seq, (s + 1) * seq)
        for m in range(A_Q // LANES):
            qr_ref[rows, m * LANES:(m + 1) * LANES] = _rope_bf16(
                q_ref[rows, m * LANES:(m + 1) * LANES], cos, sin_signed).astype(BF16)
        kr = _rope(k_ref[rows, :], cos, sin_signed)
        v = v_ref[rows, :]
        for g in range(ATTN_KV_HEADS):
            n = s * ATTN_KV_HEADS + g
            kp_ref[n, 0:blk, :] = zpad
            kp_ref[n, blk + seq:2 * blk + seq, :] = zpad
            vp_ref[n, 0:blk, :] = zpad
            vp_ref[n, blk + seq:2 * blk + seq, :] = zpad
            kp_ref[n, blk:blk + seq, :] = _dup_kv_head(kr, g).astype(BF16)
            vp_ref[n, blk:blk + seq, :] = _dup_kv_head(v, g).astype(BF16)
            hd = slice(g * HEAD_DIM, (g + 1) * HEAD_DIM)
            ctx.append((jnp.concatenate([kc_ref[s, hd, :]] * 2, axis=0).astype(BF16),
                        jnp.concatenate([vc_ref[s, hd, :]] * 2, axis=0).astype(BF16), None, True))
    sink_cols = [_sink_column(sink_ref, g, blk) for g in range(ATTN_KV_HEADS)]
    r = lax.broadcasted_iota(jnp.int32, (ATTN_GROUP * blk, 3 * blk), 0) & (blk - 1)
    c = lax.broadcasted_iota(jnp.int32, (ATTN_GROUP * blk, 3 * blk), 1)
    band = (c >= r) & (c <= r + 2 * WINDOW)
    bias_ref[0] = jnp.where(band & (c >= blk), 0.0, NEG_INF)
    bias_ref[1] = jnp.where(band, 0.0, NEG_INF)
    bias_ref[2] = jnp.where(band & (c < 2 * blk), 0.0, NEG_INF)

    def block_body(i, carry):
        row0 = pl.multiple_of(i * blk, blk)
        edge = jnp.where(i == 0, 0, jnp.where(i == nb - 1, 2, 1))
        work = []
        for s in range(per_step):
            q0 = pl.multiple_of(s * seq + i * blk, blk)
            for g in range(ATTN_KV_HEADS):
                n = s * ATTN_KV_HEADS + g
                cols = (2 * g, 2 * g + 1)
                window = (kp_ref[n, pl.ds(row0, 3 * blk), :], vp_ref[n, pl.ds(row0, 3 * blk), :],
                          bias_ref.at[edge], False)
                keys = [window, ctx[n]]
                scores = _attn_scores([qr_ref[pl.ds(q0, blk), m * LANES:(m + 1) * LANES] for m in cols], keys)
                work.append((q0, cols, scores, keys, g))
        for q0, cols, scores, keys, g in work:
            outs = _attn_finish(scores, sink_cols[g], keys)
            for m, o in zip(cols, outs):
                o_ref[pl.ds(q0, blk), m * LANES:(m + 1) * LANES] = o.astype(o_ref.dtype)
        return carry

    lax.fori_loop(0, nb, block_body, 0)


def _attn_latent(sink, qa, ka, va, k_ctx, v_ctx, *, n_seq, seq, row_block0):
    past = k_ctx.shape[2]
    per_step = ATTN_LATENT_SEQS
    assert n_seq % per_step == 0 and row_block0 % per_step == 0 and seq // ATTN_BLOCK >= 2
    rows = per_step * seq
    first = row_block0 // per_step
    kern = functools.partial(_attn_lat_kernel, seq=seq)
    return pl.pallas_call(
        kern,
        grid=(n_seq // per_step,),
        in_specs=[
            pl.BlockSpec(memory_space=pltpu.SMEM),
            pl.BlockSpec((rows, A_Q), lambda b: (first + b, 0)),
            pl.BlockSpec((rows, A_KV), lambda b: (first + b, 0)),
            pl.BlockSpec((rows, A_KV), lambda b: (first + b, 0)),
            pl.BlockSpec((per_step, A_KV, past), lambda b: (b, 0, 0)),
            pl.BlockSpec((per_step, A_KV, past), lambda b: (b, 0, 0)),
        ],
        out_specs=pl.BlockSpec((rows, A_Q), lambda b: (b, 0)),
        out_shape=jax.ShapeDtypeStruct((n_seq * seq, A_Q), BF16),
        scratch_shapes=[
            pltpu.VMEM((rows, A_Q), BF16),
            pltpu.VMEM((per_step * ATTN_KV_HEADS, seq + 2 * ATTN_BLOCK, LANES), BF16),
            pltpu.VMEM((per_step * ATTN_KV_HEADS, seq + 2 * ATTN_BLOCK, LANES), BF16),
            pltpu.VMEM((3, ATTN_GROUP * ATTN_BLOCK, 3 * ATTN_BLOCK), F32),
        ],
        compiler_params=pltpu.CompilerParams(dimension_semantics=("parallel",)),
        name="attn_latent",
    )(sink, qa, ka, va, k_ctx, v_ctx)


def _same_block(a, b, size):
    return (a & -size) == (b & -size)


def _split3(x):
    hi = x.astype(BF16)
    r1 = x - hi.astype(F32)
    mid = r1.astype(BF16)
    lo = (r1 - mid.astype(F32)).astype(BF16)
    return hi, mid, lo


def _interleave(generators):
    live = list(generators)
    while live:
        for gen in list(live):
            try:
                next(gen)
            except StopIteration:
                live.remove(gen)


def _gla_tile(q, k, v, ld, st_ref, sel_ref, emit, *, rev):
    t = GLA_TILE
    n_pair = GLA_HEADS // 2
    row = lax.broadcasted_iota(jnp.int32, (t, t), 0)
    col = lax.broadcasted_iota(jnp.int32, (t, t), 1)
    tri = (row <= col) if rev else (row >= col)
    tri = jnp.where(tri, 1.0, 0.0).astype(BF16)
    hi, mid, lo = _split3(ld * math.log2(math.e))
    b = (jnp.dot(tri, hi, preferred_element_type=F32) + jnp.dot(tri, mid, preferred_element_type=F32)
         + jnp.dot(tri, lo, preferred_element_type=F32))
    yield
    b_tot = b[0:1, :] if rev else b[t - 1:t, :]
    qs = (q * jnp.exp2(b)).astype(BF16)
    kd = (k * jnp.exp2(b_tot - b)).astype(BF16)

    rowv = lax.broadcasted_iota(jnp.int32, (t, B_QK), 0)
    levels = []
    half = GLA_SUB
    while 2 * half <= t:
        blk = 2 * half
        pieces = []
        for bs in range(0, t, blk):
            r = bs + half if rev else bs + half - 1
            pieces.append(jnp.broadcast_to(b[r:r + 1, :], (blk, B_QK)))
        ref = jnp.concatenate(pieces, axis=0) if len(pieces) > 1 else pieces[0]
        second = (rowv & (blk - 1)) >= half
        is_q = ~second if rev else second
        ql = (q * jnp.exp2(jnp.where(is_q, b - ref, -jnp.inf))).astype(BF16)
        kl = (k * jnp.exp2(jnp.where(is_q, -jnp.inf, ref - b))).astype(BF16)
        levels.append((blk, ql, kl))
        half = blk
    yield

    rr = lax.broadcasted_iota(jnp.int32, (t, 2 * t), 0)
    ss = lax.broadcasted_iota(jnp.int32, (t, 2 * t), 1) & (t - 1)
    sub_rows = lax.broadcasted_iota(jnp.int32, (GLA_SUB, LANES), 0)
    lane_lo = _lane_lt((t, LANES), GLA_DK)
    lane_lo2 = _lane_lt((2 * t, LANES), GLA_DK)
    par_rows = lax.broadcasted_iota(jnp.int32, (2 * t, LANES), 0) < t
    head_sel = par_rows == lane_lo2

    outs = []
    for p in range(n_pair):
        pl_ = slice(p * LANES, (p + 1) * LANES)
        qp, kp, bp = q[:, pl_], k[:, pl_], b[:, pl_]
        rows = []
        for sb in range(t // GLA_SUB):
            r0 = sb * GLA_SUB
            q_sub = qp[r0:r0 + GLA_SUB, :]
            b_sub = bp[r0:r0 + GLA_SUB, :]
            cols = []
            for j in range(GLA_SUB):
                kj = kp[r0 + j:r0 + j + 1, :]
                bj = bp[r0 + j:r0 + j + 1, :]
                valid = (sub_rows <= j) if rev else (sub_rows >= j)
                e = jnp.exp2(jnp.where(valid, b_sub - bj, -jnp.inf))
                cols.append(q_sub * kj * e)
            rows.append(jnp.concatenate(cols, axis=1))
        e_p = jnp.concatenate(rows, axis=0).astype(BF16)
        yield
        sub_scores = jnp.dot(e_p, sel_ref[...], preferred_element_type=F32)
        scores = jnp.zeros_like(sub_scores)
        for blk, ql, kl in reversed(levels):
            klp = kl[:, pl_]
            kstack = jnp.concatenate([klp, klp], axis=0)
            kstack = jnp.where(head_sel, kstack, jnp.zeros_like(kstack))
            s_l = _dot_nt(ql[:, pl_], kstack)
            scores = jnp.where(_same_block(rr, ss, blk), s_l, scores)
        scores = jnp.where(_same_block(rr, ss, GLA_SUB), sub_scores, scores)
        yield
        vp = v[:, p * 2 * GLA_DV:(p + 1) * 2 * GLA_DV]
        v_lo = _lane_lt(vp.shape, GLA_DV)
        vbd = jnp.concatenate([jnp.where(v_lo, vp, 0.0), jnp.where(v_lo, 0.0, vp)], axis=0).astype(BF16)
        st = st_ref[p]
        o_p = (jnp.dot(scores.astype(BF16), vbd, preferred_element_type=F32)
               + _dot_nt(qs[:, pl_], st.astype(BF16)))
        outs.append(o_p)
        upd = jnp.dot(vp.T.astype(BF16), kd[:, pl_], preferred_element_type=F32)
        st_rows_lo = lax.broadcasted_iota(jnp.int32, upd.shape, 0) < GLA_DV
        bd = st_rows_lo == _lane_lt(upd.shape, GLA_DK)
        st_ref[p] = st * jnp.exp2(b_tot[:, pl_]) + jnp.where(bd, upd, 0.0)
        yield
    emit(jnp.concatenate(outs, axis=1))


def _gla_kernel(*refs, seq, has_init, want_state, per_step):
    q_ref, k_ref, v_ref, ldf_ref, ldb_ref = refs[:5]
    pos = 5
    if has_init:
        s0f_ref, s0b_ref = refs[pos:pos + 2]
        pos += 2
    o_ref = refs[pos]
    pos += 1
    if want_state:
        sf_ref, sb_ref = refs[pos:pos + 2]
        pos += 2
    st_ref, sel_ref, acc_ref = refs[pos:pos + 3]
    t = GLA_TILE
    nt = seq // t
    n_pair = GLA_HEADS // 2

    kk = lax.broadcasted_iota(jnp.int32, (GLA_SUB * LANES, 2 * t), 0)
    nn = lax.broadcasted_iota(jnp.int32, (GLA_SUB * LANES, 2 * t), 1)
    sel = ((nn & (GLA_SUB - 1)) == (kk >> LANES_LOG2)) & (((kk & (LANES - 1)) >= GLA_DK) == (nn >= t))
    sel_ref[...] = jnp.where(sel, 1.0, 0.0).astype(BF16)

    def load_state(st, s0_ref, s):
        for p in range(n_pair):
            if s0_ref is None:
                st[p] = jnp.zeros(st.shape[1:], F32)
            else:
                x = jnp.concatenate([s0_ref[s, 2 * p], s0_ref[s, 2 * p + 1]], axis=0)
                xt = x.T
                lo = _lane_lt(xt.shape, GLA_DK)
                st[p] = jnp.concatenate([jnp.where(lo, xt, 0.0), jnp.where(lo, 0.0, xt)], axis=0)

    def store_state(st, out_ref, s):
        for p in range(n_pair):
            z = st[p]
            lo = _lane_lt((GLA_DV, LANES), GLA_DK)
            z = jnp.where(lo, z[0:GLA_DV, :], z[GLA_DV:2 * GLA_DV, :]).T
            out_ref[s, 2 * p] = z[0:GLA_DK, :]
            out_ref[s, 2 * p + 1] = z[GLA_DK:2 * GLA_DK, :]

    walks = [(s, rev, ldb_ref if rev else ldf_ref, st_ref.at[2 * s + int(rev)])
             for s in range(per_step) for rev in (False, True)]
    for s, rev, _, st in walks:
        load_state(st, (s0b_ref if rev else s0f_ref) if has_init else None, s)
    assert nt % 2 == 0

    def make_tile_body(first_visit):
        def tile_body(ti, carry):
            def walk(s, rev, ld_ref, st):
                tile = (nt - 1 - ti) if rev else ti
                r0 = pl.multiple_of(s * seq + tile * t, t)
                rows = pl.ds(r0, t)

                def emit(o):
                    if first_visit:
                        acc_ref[rows, :] = o
                    else:
                        o_ref[rows, :] = (acc_ref[rows, :] + o).astype(o_ref.dtype)

                return _gla_tile(q_ref[rows, :], k_ref[rows, :], v_ref[rows, :].astype(F32), ld_ref[rows, :],
                                 st, sel_ref, emit, rev=rev)

            _interleave([walk(*w) for w in walks])
            return carry
        return tile_body

    lax.fori_loop(0, nt // 2, make_tile_body(True), 0)
    lax.fori_loop(nt // 2, nt, make_tile_body(False), 0)
    if want_state:
        for s, rev, _, st in walks:
            store_state(st, sb_ref if rev else sf_ref, s)


def _gla(qb, kb, vb, ldf, ldb, s0f, s0b, *, n_seq, seq, row_block0, want_state, per_step):
    has_init = s0f is not None
    assert n_seq % per_step == 0 and row_block0 % per_step == 0
    kern = functools.partial(_gla_kernel, seq=seq, has_init=has_init, want_state=want_state,
                             per_step=per_step)
    n_seq, seq_rows, row_block0 = n_seq // per_step, per_step * seq, row_block0 // per_step
    tok = lambda w: pl.BlockSpec((seq_rows, w), lambda b: (row_block0 + b, 0))
    in_specs = [tok(B_QK), tok(B_QK), tok(B_V), tok(B_QK), tok(B_QK)]
    args = [qb, kb, vb, ldf, ldb]
    state_spec = pl.BlockSpec((per_step, GLA_HEADS, GLA_DK, GLA_DV), lambda b: (b, 0, 0, 0))
    if has_init:
        in_specs += [state_spec, state_spec]
        args += [s0f, s0b]
    out_specs = [pl.BlockSpec((seq_rows, B_V), lambda b: (b, 0))]
    out_shape = [jax.ShapeDtypeStruct((n_seq * seq_rows, B_V), BF16)]
    if want_state:
        out_specs += [state_spec, state_spec]
        out_shape += [jax.ShapeDtypeStruct((n_seq * per_step, GLA_HEADS, GLA_DK, GLA_DV), F32)] * 2
    return pl.pallas_call(
        kern,
        grid=(n_seq,),
        in_specs=in_specs,
        out_specs=out_specs,
        out_shape=out_shape,
        scratch_shapes=[
            pltpu.VMEM((2 * per_step, GLA_HEADS // 2, 2 * GLA_DV, LANES), F32),
            pltpu.VMEM((GLA_SUB * LANES, 2 * GLA_TILE), BF16),
            pltpu.VMEM((seq_rows, B_V), F32),
        ],
        compiler_params=pltpu.CompilerParams(
            dimension_semantics=("parallel",), vmem_limit_bytes=VMEM_LIMIT_BYTES),
        name="gla_state" if want_state else "gla_latent",
    )(*args)


def _evout_kernel(x_hbm, mod_ref, gpre_ref, gpost_ref, attn_p_ref, attn_s_ref, gla_p_ref, gla_s_ref, gb_ref,
                  gn_ref, w_ref, o_ref, w_bf, x_ring, x_sem, *, np_tiles, n_tiles, layer, group):
    step = pl.program_id(0)
    tm = o_ref.shape[0]
    n_slots = x_ring.shape[0]

    def x_copy(s):
        rows = pl.ds(s * tm if isinstance(s, int) else pl.multiple_of(s * tm, tm), tm)
        return pltpu.make_async_copy(x_hbm.at[rows, :], x_ring.at[s % n_slots], x_sem.at[s % n_slots])

    @pl.when(step == 0)
    def _first_step():
        for s in range(min(n_slots - 1, n_tiles)):
            x_copy(s).start()
        w_bf[...] = w_ref[0].astype(BF16)

    @pl.when(step + (n_slots - 1) < n_tiles)
    def _prefetch():
        x_copy(step + (n_slots - 1)).start()

    x_copy(step).wait()
    x_ref = x_ring.at[step % n_slots]
    is_prompt = step < np_tiles
    mods = _mod_rows(mod_ref, group(pl.program_id(0)))
    g_post = _norm_row(gpost_ref, layer, 1)
    for rows in _sub_tiles(x_ref.shape[0]):
        attn = _pick((attn_p_ref, attn_s_ref), is_prompt, rows)
        gla = _pick((gla_p_ref, gla_s_ref), is_prompt, rows)
        parts = [attn]
        for hd in range(GLA_HEADS):
            sl = slice(hd * GLA_DV, (hd + 1) * GLA_DV)
            g = _rms(gla[:, sl].astype(F32), gn_ref[:, sl]) * _silu(gb_ref[rows, sl].astype(F32))
            parts.append(g.astype(BF16))
        y = jnp.dot(jnp.concatenate(parts, axis=1), w_bf[...], preferred_element_type=F32)
        o_ref[rows, :] = _sub_out(x_ref[rows, :], y, mods, g_post, 1, 1.0)


def _even_out(x, mods, norm_pre, norm_post, attn_ps, gla_ps, gb, gla_norm, w_out, *, layer, e, group,
              np_tiles):
    n_tok, d = x.shape
    tm = WIDE_TILE
    ev_out = w_out.shape[1]
    kern = functools.partial(_evout_kernel, np_tiles=np_tiles, n_tiles=n_tok // tm, layer=layer, group=group)
    return pl.pallas_call(
        kern,
        grid=(n_tok // tm,),
        in_specs=[pl.BlockSpec(memory_space=pl.ANY)] + _cond_specs(mods, norm_pre, norm_post, layer) + [
            *_tile_specs(tm, A_Q, np_tiles),
            *_tile_specs(tm, B_V, np_tiles),
            pl.BlockSpec((tm, B_V), lambda i: (i, 0)),
            pl.BlockSpec((1, B_V), lambda i: (e, 0)),
            pl.BlockSpec((1, ev_out, d), lambda i: (e, 0, 0), pipeline_mode=pl.Buffered(1)),
        ],
        out_specs=pl.BlockSpec((tm, d), lambda i: (i, 0)),
        out_shape=jax.ShapeDtypeStruct((n_tok, d), F32),
        scratch_shapes=[pltpu.VMEM((ev_out, d), BF16), pltpu.VMEM((3, tm, d), F32),
                        pltpu.SemaphoreType.DMA((3,))],
        compiler_params=pltpu.CompilerParams(
            dimension_semantics=("arbitrary",), vmem_limit_bytes=VMEM_LIMIT_BYTES),
        name="even_out",
    )(x, mods, norm_pre, norm_post, *attn_ps, *gla_ps, gb, gla_norm, w_out)


def _gelu(x):
    return 0.5 * x * (1.0 + lax.erf(x * (2.0 ** -0.5)))


def _cm_kernel(x_ref, mod_ref, gpre_ref, gpost_ref, win_ref, vg_ref, vb_ref, ws_ref, bs_ref, wout_ref,
               o_ref, win_bf, wout_bf, *, layer, group):
    @pl.when(pl.program_id(0) == 0)
    def _cast_weights():
        win_bf[...] = win_ref[0].astype(BF16)
        wout_bf[...] = wout_ref[0].astype(BF16)

    width = wout_bf.shape[0]
    gw = width // CMLP_GROUPS
    ws = [ws_ref[0, g].astype(BF16) for g in range(CMLP_GROUPS)]
    mods = _mod_rows(mod_ref, group(pl.program_id(0)))
    g_pre = _norm_row(gpre_ref, layer, 1)
    g_post = _norm_row(gpost_ref, layer, 1)
    eye = (lax.broadcasted_iota(jnp.int32, (CHUNK, CHUNK), 0)
           == lax.broadcasted_iota(jnp.int32, (CHUNK, CHUNK), 1))
    bias_cols = [jnp.sum(jnp.where(eye, jnp.broadcast_to(bs_ref[0, g:g + 1, :], (CHUNK, CHUNK)), 0.0),
                         axis=1, keepdims=True) for g in range(CMLP_GROUPS)]

    def phase_in(rows):
        x = x_ref[rows, :]
        h = _sub_in(x, mods, g_pre, 1).astype(BF16)
        v = _gelu(jnp.dot(h, win_bf[:, width:2 * width], preferred_element_type=F32))
        u = _gelu(jnp.dot(h, win_bf[:, 0:width], preferred_element_type=F32))
        return x, u, v

    def phase_mix(x, u, v):
        mu = jnp.mean(v, axis=-1, keepdims=True)
        vc = v - mu
        var = jnp.mean(vc * vc, axis=-1, keepdims=True)
        vn = (vc * lax.rsqrt(var + EPS) * vg_ref[...] + vb_ref[...]).astype(BF16)
        chunk_rows = []
        for c in range(x.shape[0] // CHUNK):
            cols = []
            for g in range(CMLP_GROUPS):
                blk = vn[c * CHUNK:(c + 1) * CHUNK, g * gw:(g + 1) * gw]
                cols.append(jnp.dot(ws[g], blk, preferred_element_type=F32) + bias_cols[g])
            chunk_rows.append(jnp.concatenate(cols, axis=1))
        mixed = jnp.concatenate(chunk_rows, axis=0)
        return x, (u * mixed).astype(BF16)

    def phase_out(rows, x, m):
        y = jnp.dot(m, wout_bf[...], preferred_element_type=F32)
        o_ref[rows, :] = _sub_out(x, y, mods, g_post, 1, 1.0)

    tiles = _sub_tiles(x_ref.shape[0])
    pending = None
    for rows in tiles:
        cur = phase_in(rows)
        if pending is not None:
            phase_out(pending[0], *phase_mix(*pending[1]))
        pending = (rows, cur)
    phase_out(pending[0], *phase_mix(*pending[1]))


def _chunk_mlp(x, mods, norm_pre, norm_post, w_in, v_gain, v_bias, w_s, b_s, w_out, *, layer, o, group):
    n_tok, d = x.shape
    tm = WIDE_TILE
    width = w_out.shape[1]
    kern = functools.partial(_cm_kernel, layer=layer, group=group)
    return pl.pallas_call(
        kern,
        grid=(n_tok // tm,),
        in_specs=[pl.BlockSpec((tm, d), lambda i: (i, 0))] + _cond_specs(mods, norm_pre, norm_post, layer) + [
            pl.BlockSpec((1, d, 2 * width), lambda i: (o, 0, 0), pipeline_mode=pl.Buffered(1)),
            pl.BlockSpec((1, width), lambda i: (o, 0)),
            pl.BlockSpec((1, width), lambda i: (o, 0)),
            pl.BlockSpec((1, CMLP_GROUPS, CHUNK, CHUNK), lambda i: (o, 0, 0, 0)),
            pl.BlockSpec((1, CMLP_GROUPS, CHUNK), lambda i: (o, 0, 0)),
            pl.BlockSpec((1, width, d), lambda i: (o, 0, 0), pipeline_mode=pl.Buffered(1)),
        ],
        out_specs=pl.BlockSpec((tm, d), lambda i: (i, 0)),
        out_shape=jax.ShapeDtypeStruct((n_tok, d), F32),
        scratch_shapes=[pltpu.VMEM((d, 2 * width), BF16), pltpu.VMEM((width, d), BF16)],
        compiler_params=pltpu.CompilerParams(
            dimension_semantics=("arbitrary",), vmem_limit_bytes=VMEM_LIMIT_BYTES),
        name="chunk_mlp",
    )(x, mods, norm_pre, norm_post, w_in, v_gain, v_bias, w_s, b_s, w_out)


def kernel(x_prompt, x_sample, cache_k, cache_v, state_gla_fwd, state_gla_bwd, c, c_ctx, w_mod, b_mod, norm_pre, norm_post, ffn_w_gate, ffn_w_up, ffn_w_down, ev_w_in, ev_w_out, ev_sink, gla_wa_f, gla_ba_f, gla_wa_b, gla_ba_b, gla_norm, cm_w_in, cm_v_gain, cm_v_bias, cm_w_s, cm_b_s, cm_w_out):
    batch, seq, d = x_prompt.shape
    dec_batch, dec_seq, _ = x_sample.shape
    depth = w_mod.shape[0]
    n_prompt_tok = batch * seq
    n_sample_tok = dec_batch * dec_seq
    assert n_prompt_tok % dec_seq == 0 and dec_seq % TOKEN_TILE == 0
    group = _group_index_map(TOKEN_TILE, n_prompt_tok, dec_seq)
    np_tiles = n_prompt_tok // TOKEN_TILE
    ffn = functools.partial(_ffn, group=group, np_tiles=np_tiles)
    assert dec_seq % WIDE_TILE == 0
    group_w = _group_index_map(WIDE_TILE, n_prompt_tok, dec_seq)
    np_tiles_w = n_prompt_tok // WIDE_TILE

    xs = (x_prompt.reshape(n_prompt_tok, d), x_sample.reshape(n_sample_tok, d))
    ev_w_in_t = jnp.swapaxes(ev_w_in, 1, 2)

    mods = _adaln_mods(c_ctx, c, w_mod, b_mod)
    gains_pre = norm_pre.reshape(depth * N_SUB, d)
    gains_post = norm_post.reshape(depth * N_SUB, d)
    cond = (mods, gains_pre, gains_post)

    new_k, new_v, new_sf, new_sb = [], [], [], []
    for layer in range(depth):
        (x,) = ffn(xs, *cond, ffn_w_gate, ffn_w_up, ffn_w_down,
                   sub=0, layer=layer, half=0, split_out=False)
        if layer % 2 == 0:
            e = layer // 2
            qa, ka, va, qb, kb, vb, gb, ldf, ldb = _even_in(
                x, *cond, ev_w_in_t, gla_wa_f, gla_wa_b, gla_ba_f, gla_ba_b,
                layer=layer, e=e, group=group_w)
            sink = ev_sink[e]
            attn_p, k_t, v_t = _attn_context(sink, qa, ka, va, n_seq=batch, seq=seq)
            k_ctx = jnp.transpose(cache_k[:, e], (0, 2, 3, 1)).reshape(dec_batch, A_KV, -1)
            v_ctx = jnp.transpose(cache_v[:, e], (0, 2, 3, 1)).reshape(dec_batch, A_KV, -1)
            attn_s = _attn_latent(sink, qa, ka, va, k_ctx, v_ctx, n_seq=dec_batch, seq=dec_seq,
                                  row_block0=n_prompt_tok // dec_seq)
            gla_p, s_f, s_b = _gla(qb, kb, vb, ldf, ldb, None, None, n_seq=batch, seq=seq,
                                   row_block0=0, want_state=True, per_step=GLA_PROMPT_SEQS)
            (gla_s,) = _gla(qb, kb, vb, ldf, ldb, state_gla_fwd[:, e], state_gla_bwd[:, e],
                            n_seq=dec_batch, seq=dec_seq, row_block0=n_prompt_tok // dec_seq,
                            want_state=False, per_step=GLA_LATENT_SEQS)
            x = _even_out(x, *cond, (attn_p, attn_s), (gla_p, gla_s), gb, gla_norm, ev_w_out,
                          layer=layer, e=e, group=group_w, np_tiles=np_tiles_w)
            for cache_t, dst in ((k_t, new_k), (v_t, new_v)):
                dst.append(jnp.transpose(cache_t.reshape(batch, ATTN_KV_HEADS, HEAD_DIM, seq), (0, 3, 1, 2)))
            new_sf.append(s_f)
            new_sb.append(s_b)
        else:
            o = layer // 2
            x = _chunk_mlp(x, *cond, cm_w_in, cm_v_gain, cm_v_bias, cm_w_s, cm_b_s, cm_w_out,
                           layer=layer, o=o, group=group_w)
        xs = ffn((x,), *cond, ffn_w_gate, ffn_w_up, ffn_w_down,
                 sub=2, layer=layer, half=1, split_out=(layer == depth - 1))

    y_prompt = xs[0].reshape(batch, seq, d)
    y_sample = xs[1].reshape(dec_batch, dec_seq, d)
    return (y_prompt, y_sample, jnp.stack(new_k, axis=1), jnp.stack(new_v, axis=1),
            jnp.stack(new_sf, axis=1), jnp.stack(new_sb, axis=1))
```
